```python
import functools
import jax, jax.numpy as jnp
from jax import lax
import numpy as np

D_MODEL = 2048
BATCH = 16
SEQ = 2048
DEPTH = 4

GRID_W = 64
CTX_LEN = 256
N_MIXERS = 3
EPS = 1e-6

F_GROUPS = 4
M_HEADS = 8
M_DQK = D_MODEL // (2 * M_HEADS)
M_DV = D_MODEL // M_HEADS
M_QK = M_HEADS * M_DQK
M_V = M_HEADS * M_DV
M_CHUNK = 64
M_IN = 2 * M_QK + 3 * M_V + 4 * M_HEADS
A_HEADS = 16
A_KV = 4
A_HD = 128
A_GROUP = A_HEADS // A_KV
A_Q = A_HEADS * A_HD
A_KVW = A_KV * A_HD
A_IN = 2 * A_Q + 2 * A_KVW
A_BLOCK = 128
A_ROT_AXIS = A_HD // 2
ROPE_THETA = 10000.0

N_FNET = (DEPTH + 2) // 3
N_MLSTM = (DEPTH + 1) // 3
N_ATTN = DEPTH // 3

kernel_name = 'hybrid_fnet_mlstm_gqa_prefix_dit'


def _rmsnorm(x, g):
    x32 = x.astype(jnp.float32)
    y = x32 * lax.rsqrt(jnp.mean(x32 * x32, axis=-1, keepdims=True) + EPS)
    return (y * g.astype(jnp.float32)).astype(x.dtype)


def _fourier_mix(h):
    B, T, D = h.shape
    hg = h.astype(jnp.float32).reshape(B, T, F_GROUPS, D // F_GROUPS)
    y = jnp.fft.fft2(hg, axes=(1, 3), norm='ortho').real
    return y.reshape(B, T, D).astype(h.dtype)


def _fnet_branch(h, w_gate, w_out):
    return (_fourier_mix(h) * jax.nn.silu(h @ w_gate)) @ w_out


def _mlstm_chunk(carry, xs, emit):
    C, n, m = carry
    q, k, v, ig, lf = xs
    L = lf.shape[-1]
    b = jnp.cumsum(lf, axis=-1)
    b_end = b[..., -1]
    g = b_end[..., None] - b + ig
    m_new = jnp.maximum(b_end + m, jnp.max(g, axis=-1))
    w = jnp.exp(g - m_new[..., None])
    decay = jnp.exp(b_end + m - m_new)
    C_new = decay[..., None, None] * C + jnp.einsum('bhsd,bhse->bhde', k * w[..., None], v)
    n_new = decay[..., None] * n + jnp.einsum('bhs,bhsd->bhd', w, k)
    if not emit:
        return (C_new, n_new, m_new), None
    order = jnp.tril(jnp.ones((L, L), dtype=bool))
    dmat = jnp.where(order, b[..., :, None] - b[..., None, :] + ig[..., None, :], -jnp.inf)
    inter = b + m[..., None]
    m_t = jnp.maximum(inter, jnp.max(dmat, axis=-1))
    s = jnp.einsum('bhtd,bhsd->bhts', q, k) * jnp.exp(dmat - m_t[..., None])
    a = jnp.exp(inter - m_t)
    num = a[..., None] * jnp.einsum('bhtd,bhde->bhte', q, C) + jnp.einsum('bhts,bhse->bhte', s, v)
    den = a * jnp.einsum('bhtd,bhd->bht', q, n) + jnp.sum(s, axis=-1)
    h = num / jnp.maximum(jnp.abs(den), jnp.exp(-m_t))[..., None]
    return (C_new, n_new, m_new), h


def _mlstm_scan(q, k, v, ig, lf, state, emit):
    B, NH, T, _ = q.shape
    nc = T // M_CHUNK

    def to_chunks(a):
        a = a.reshape(a.shape[:2] + (nc, M_CHUNK) + a.shape[3:])
        return jnp.moveaxis(a, 2, 0)

    xs = (to_chunks(q), to_chunks(k), to_chunks(v), to_chunks(ig), to_chunks(lf))
    state, h = lax.scan(functools.partial(_mlstm_chunk, emit=emit), state, xs)
    if emit:
        h = jnp.moveaxis(h, 0, 2).reshape(B, NH, T, M_DV)
    return state, h


def _mlstm_bidir(q, k, v, ig_f, lf_f, ig_b, lf_b, st_f, st_b, emit):
    st_f, h_f = _mlstm_scan(q, k, v, ig_f, lf_f, st_f, emit)
    flip = lambda a: jnp.flip(a, axis=2)
    st_b, h_b = _mlstm_scan(flip(q), flip(k), flip(v), flip(ig_b), flip(lf_b), st_b, emit)
    h = (h_f + flip(h_b)) if emit else None
    return st_f, st_b, h


def _mlstm_branch(h_lat, h_ctx, w_in, b_gate, hn, w_out, ctx_out):
    f32 = jnp.float32
    idx = [M_QK, 2 * M_QK, 2 * M_QK + M_V, 2 * M_QK + 2 * M_V, 2 * M_QK + 2 * M_V + 4 * M_HEADS]

    def project(h):
        B, T, _ = h.shape
        q, k, v, o, g, z = jnp.split(h @ w_in, idx, axis=-1)
        heads = lambda a, d: a.reshape(B, T, M_HEADS, d).transpose(0, 2, 1, 3).astype(f32)
        q = heads(q, M_DQK) * (M_DQK ** -0.5)
        k = heads(k, M_DQK)
        v = heads(v, M_DV)
        g = (g.astype(f32) + b_gate.astype(f32)).reshape(B, T, 4, M_HEADS).transpose(2, 0, 3, 1)
        ig_f, fg_f, ig_b, fg_b = g[0], g[1], g[2], g[3]
        scan_in = (q, k, v, ig_f, jax.nn.log_sigmoid(fg_f), ig_b, jax.nn.log_sigmoid(fg_b))
        return scan_in, o, z

    def finish(h, o, z):
        B, _, T, _ = h.shape
        h = h.transpose(0, 2, 1, 3)
        y = jax.nn.sigmoid(o.astype(f32)).reshape(B, T, M_HEADS, M_DV) * h
        y = _rmsnorm(y, hn.reshape(M_HEADS, M_DV)).reshape(B, T, M_V).astype(z.dtype)
        return (y * jax.nn.silu(z)) @ w_out

    B = h_ctx.shape[0]
    zero = (jnp.zeros((B, M_HEADS, M_DQK, M_DV), f32), jnp.zeros((B, M_HEADS, M_DQK), f32),
            jnp.zeros((B, M_HEADS), f32))
    ctx_in, o_c, z_c = project(h_ctx)
    st_f, st_b, h_c = _mlstm_bidir(*ctx_in, zero, zero, emit=ctx_out)
    lat_in, o_x, z_x = project(h_lat)
    _, _, h_x = _mlstm_bidir(*lat_in, st_f, st_b, emit=True)
    y_lat = finish(h_x, o_x, z_x)
    y_ctx = finish(h_c, o_c, z_c) if ctx_out else None
    return y_lat, y_ctx


def _rope_tables(T):
    rows = T // GRID_W
    r = jnp.repeat(jnp.arange(rows), GRID_W).astype(jnp.float32)
    col = jnp.tile(jnp.arange(GRID_W), rows).astype(jnp.float32)
    freqs = ROPE_THETA ** (-jnp.arange(0, A_ROT_AXIS, 2, dtype=jnp.float32) / A_ROT_AXIS)
    ang = jnp.concatenate([r[:, None] * freqs, col[:, None] * freqs], axis=-1)
    return jnp.cos(ang), jnp.sin(ang)


def _rope(x, cos, sin):
    x32 = x.astype(jnp.float32).reshape(x.shape[:-1] + (A_HD // 2, 2))
    x0, x1 = x32[..., 0], x32[..., 1]
    c = cos[None, :, None, :]
    s = sin[None, :, None, :]
    out = jnp.stack([x0 * c - x1 * s, x0 * s + x1 * c], axis=-1).reshape(x.shape)
    return out.astype(x.dtype)


def _attend(q, k, v):
    B, Tq = q.shape[:2]
    nb = Tq // A_BLOCK
    qb = jnp.moveaxis(q.reshape((B, nb, A_BLOCK) + q.shape[2:]), 1, 0)

    def one(qblk):
        s = jnp.einsum('bqkgd,bskd->bkgqs', qblk, k).astype(jnp.float32) * (A_HD ** -0.5)
        p = jax.nn.softmax(s, axis=-1).astype(v.dtype)
        return jnp.einsum('bkgqs,bskd->bqkgd', p, v)

    o = lax.map(one, qb)
    return jnp.moveaxis(o, 0, 1).reshape(B, Tq, A_Q)


def _attn_branch(h_lat, h_ctx, w_in, qn, kn, w_out, ctx_out):
    def project(h):
        B, T, _ = h.shape
        q, k, v, z = jnp.split(h @ w_in, [A_Q, A_Q + A_KVW, A_Q + 2 * A_KVW], axis=-1)
        q = _rmsnorm(q.reshape(B, T, A_HEADS, A_HD), qn)
        k = _rmsnorm(k.reshape(B, T, A_KV, A_HD), kn)
        v = v.reshape(B, T, A_KV, A_HD)
        return q, k, v, z

    group = lambda q: q.reshape(q.shape[:2] + (A_KV, A_GROUP, A_HD))
    qc, kc, vc, zc = project(h_ctx)
    qx, kx, vx, zx = project(h_lat)
    cos, sin = _rope_tables(h_lat.shape[1])
    qx = _rope(qx, cos, sin)
    kx = _rope(kx, cos, sin)
    k_all = jnp.concatenate([kx, kc], axis=1)
    v_all = jnp.concatenate([vx, vc], axis=1)
    y_lat = (_attend(group(qx), k_all, v_all) * jax.nn.silu(zx)) @ w_out
    y_ctx = ((_attend(group(qc), kc, vc) * jax.nn.silu(zc)) @ w_out) if ctx_out else None
    return y_lat, y_ctx


def setup_inputs(seed: int = 0) -> dict:
    key = jax.random.key(seed)
    ks = jax.random.split(key, 20)
    D = D_MODEL
    nrm = lambda k, shape, s: jax.random.normal(k, shape, jnp.float32) * s
    f_bias = jnp.linspace(3.0, 6.0, M_HEADS, dtype=jnp.float32)
    base = jnp.stack([jnp.zeros_like(f_bias), f_bias, jnp.zeros_like(f_bias), f_bias])
    b_gate = (base[None] + nrm(ks[11], (N_MLSTM, 4, M_HEADS), 0.1)).reshape(N_MLSTM, 4 * M_HEADS)
    return {
        'x': nrm(ks[0], (BATCH, SEQ, D), 1.0),
        'c': nrm(ks[1], (BATCH, D), 1.0),
        'ctx': nrm(ks[2], (BATCH, CTX_LEN, D), 1.0),
        'c_ctx': nrm(ks[3], (D,), 1.0),
        'ada_w': nrm(ks[4], (DEPTH, D, 3 * D), 0.5 * D ** -0.5),
        'ada_b': nrm(ks[5], (DEPTH, 3 * D), 0.01),
        'norm_g': 1.0 + nrm(ks[6], (DEPTH, D), 0.02),
        'fnet_w_gate': nrm(ks[7], (N_FNET, D, D), D ** -0.5),
        'fnet_w_out': nrm(ks[8], (N_FNET, D, D), D ** -0.5),
        'mlstm_w_in': nrm(ks[9], (N_MLSTM, D, M_IN), D ** -0.5),
        'mlstm_b_gate': b_gate,
        'mlstm_hn': 1.0 + nrm(ks[12], (N_MLSTM, M_V), 0.02),
        'mlstm_w_out': nrm(ks[13], (N_MLSTM, M_V, D), M_V ** -0.5),
        'attn_w_in': nrm(ks[14], (N_ATTN, D, A_IN), D ** -0.5),
        'attn_qn': 1.0 + nrm(ks[15], (N_ATTN, A_HD), 0.02),
        'attn_kn': 1.0 + nrm(ks[16], (N_ATTN, A_HD), 0.02),
        'attn_w_out': nrm(ks[17], (N_ATTN, A_Q, D), A_Q ** -0.5),
        'final_g': 1.0 + nrm(ks[18], (D,), 0.02),
    }


def reference(x, c, ctx, c_ctx, ada_w, ada_b, norm_g, fnet_w_gate, fnet_w_out, mlstm_w_in,
              mlstm_b_gate, mlstm_hn, mlstm_w_out, attn_w_in, attn_qn, attn_kn, attn_w_out, final_g):
    sc = jax.nn.silu(c)
    sc_ctx = jax.nn.silu(c_ctx)
    for i in range(DEPTH):
        kind = i % N_MIXERS
        j = i // N_MIXERS
        ctx_out = i != DEPTH - 1
        shift, scale, gate = jnp.split(sc @ ada_w[i] + ada_b[i], 3, axis=-1)
        hx = _rmsnorm(x, norm_g[i]) * (1.0 + scale[:, None]) + shift[:, None]
        hc = None
        if ctx_out or kind != 0:
            shift_c, scale_c, gate_c = jnp.split(sc_ctx @ ada_w[i] + ada_b[i], 3, axis=-1)
            hc = _rmsnorm(ctx, norm_g[i]) * (1.0 + scale_c) + shift_c
        if kind == 0:
            y_x = _fnet_branch(hx, fnet_w_gate[j], fnet_w_out[j])
            y_c = _fnet_branch(hc, fnet_w_gate[j], fnet_w_out[j]) if ctx_out else None
        elif kind == 1:
            y_x, y_c = _mlstm_branch(hx, hc, mlstm_w_in[j], mlstm_b_gate[j], mlstm_hn[j],
                                     mlstm_w_out[j], ctx_out)
        else:
            y_x, y_c = _attn_branch(hx, hc, attn_w_in[j], attn_qn[j], attn_kn[j],
                                    attn_w_out[j], ctx_out)
        x = x + gate[:, None] * y_x
        if ctx_out:
            ctx = ctx + gate_c * y_c
    return _rmsnorm(x, final_g)
```

```python
import functools
import math

import jax
import jax.numpy as jnp
from jax import lax
from jax.experimental import pallas as pl
from jax.experimental.pallas import tpu as pltpu

F32 = jnp.float32
BF16 = jnp.bfloat16

D = 2048
BATCH = 16
SEQ = 2048
CTX = 256
DEPTH = 4
EPS = 1e-6
N_LAT = BATCH * SEQ
N_CTX = BATCH * CTX
N_ALL = N_LAT + N_CTX
MOD_ROWS = 24
CTX_MOD_ROW = BATCH

F_GROUPS = 4
F_GW = D // F_GROUPS

M_HEADS = 8
M_DQK = 128
M_DV = 256
M_QK = M_HEADS * M_DQK
M_V = M_HEADS * M_DV
M_L = 256
M_GATES = 128

A_HEADS = 16
A_KV = 4
A_HD = 128
A_Q = A_HEADS * A_HD
A_KVW = A_KV * A_HD
A_GW = A_Q // A_KV
A_TQ = 256
GRID_W = 64
ROPE_THETA = 10000.0

TM = 1024
TN = 512
MIB = 1024 * 1024


def _params(sem, vmem_mib):
    return pltpu.CompilerParams(dimension_semantics=sem, vmem_limit_bytes=vmem_mib * MIB)


def _sigmoid(x):
    return 1.0 / (1.0 + jnp.exp(-x))


def _silu(x):
    return x * _sigmoid(x)


def _log_sigmoid(x):
    return jnp.minimum(x, 0.0) - jnp.log1p(jnp.exp(-jnp.abs(x)))


def _modnorm(x, g, scale, shift):
    ms = jnp.mean(x * x, axis=-1, keepdims=True)
    y = x * lax.rsqrt(ms + EPS) * g
    return y * (1.0 + scale) + shift


def _store_modnorm(x_ref, g_ref, sc_ref, sh_ref, h_scr):
    rows = x_ref.shape[0]
    step = min(rows, 256)
    for r0 in range(0, rows, step):
        h_scr[r0:r0 + step, :] = _modnorm(x_ref[r0:r0 + step, :], g_ref[...], sc_ref[...],
                                          sh_ref[...]).astype(BF16)


def _mod_row(r, tm):
    return jnp.where(r < N_LAT // tm, r // (SEQ // tm), CTX_MOD_ROW)


def _modvec_kernel(c_ref, w_ref, b_ref, o_ref):
    s = _silu(c_ref[...])
    o_ref[...] = jnp.dot(s, w_ref[...], preferred_element_type=F32,
                         precision=lax.Precision.HIGHEST) + b_ref[...]


def _modvec(cc, ada_w, ada_b):
    tn = 1024
    return pl.pallas_call(
        _modvec_kernel,
        grid=(DEPTH, 3 * D // tn),
        in_specs=[
            pl.BlockSpec((MOD_ROWS, D), lambda i, j: (0, 0)),
            pl.BlockSpec((None, D, tn), lambda i, j: (i, 0, j)),
            pl.BlockSpec((None, 1, tn), lambda i, j: (i, 0, j)),
        ],
        out_specs=pl.BlockSpec((None, MOD_ROWS, tn), lambda i, j: (i, 0, j)),
        out_shape=jax.ShapeDtypeStruct((DEPTH, MOD_ROWS, 3 * D), F32),
        compiler_params=_params(("parallel", "parallel"), 40),
        name="modvec",
    )(cc, ada_w, ada_b.reshape(DEPTH, 1, 3 * D))


def _mod_specs(tm, row_of):
    return [
        pl.BlockSpec((1, D), lambda *ids: (0, 0)),
        pl.BlockSpec((None, 1, D), lambda *ids: (_mod_row(row_of(*ids), tm), 0, 0)),
        pl.BlockSpec((None, 1, D), lambda *ids: (_mod_row(row_of(*ids), tm), 0, 1)),
    ]


def _fnet_chan_kernel(x_ref, g_ref, sh_ref, sc_ref, cs_ref, a_ref, b_ref):
    h = _modnorm(x_ref[...], g_ref[...], sc_ref[...], sh_ref[...]).astype(BF16)
    for grp in range(F_GROUPS):
        sl = slice(grp * F_GW, (grp + 1) * F_GW)
        p = jnp.dot(h[:, sl], cs_ref[...], preferred_element_type=F32)
        a_ref[:, sl] = p[:, :F_GW].astype(BF16)
        b_ref[:, sl] = p[:, F_GW:].astype(BF16)


def _fnet_chan(xs, g, mods, cs_c, n_rows):
    tm = 512
    out = jax.ShapeDtypeStruct((N_ALL, D), BF16)
    return pl.pallas_call(
        _fnet_chan_kernel,
        grid=(n_rows // tm,),
        in_specs=[pl.BlockSpec((tm, D), lambda r: (r, 0))] + _mod_specs(tm, lambda r: r) + [
            pl.BlockSpec((F_GW, 2 * F_GW), lambda r: (0, 0)),
        ],
        out_specs=[pl.BlockSpec((tm, D), lambda r: (r, 0))] * 2,
        out_shape=[out, out],
        compiler_params=_params(("parallel",), 48),
        name="fnet_chan",
    )(xs, g, mods, mods, cs_c)


def _fnet_mix_kernel(x_ref, g_ref, sh_ref, sc_ref, ct_ref, st_ref, a_ref, b_ref, wg_ref, *rest):
    o_ref, h_scr = rest[-2], rest[-1]

    @pl.when(pl.program_id(2) == 0)
    def _():
        _store_modnorm(x_ref, g_ref, sc_ref, sh_ref, h_scr)

    y = jnp.dot(ct_ref[...], a_ref[...], preferred_element_type=F32)
    y = y - jnp.dot(st_ref[...], b_ref[...], preferred_element_type=F32)
    gate = jnp.dot(h_scr[...], wg_ref[...], preferred_element_type=F32)
    o_ref[...] = (y * _silu(gate)).astype(BF16)


def _fnet_mix(xs, g, mods, ct, st, a, b, wg, tseq, row0, mod_row, prev=None):
    tm = min(tseq, 512)
    tn = 512
    mt = tseq // tm
    rb0 = row0 // tm
    sb0 = row0 // tseq
    row_blk = lambda bi, m, n: rb0 + bi * mt + m
    mrow = (lambda bi: bi) if mod_row is None else (lambda bi: mod_row)
    in_specs = [
        pl.BlockSpec((tm, D), lambda bi, m, n: (row_blk(bi, m, n), 0)),
        pl.BlockSpec((1, D), lambda bi, m, n: (0, 0)),
        pl.BlockSpec((None, 1, D), lambda bi, m, n: (mrow(bi), 0, 0)),
        pl.BlockSpec((None, 1, D), lambda bi, m, n: (mrow(bi), 0, 1)),
        pl.BlockSpec((tm, tseq), lambda bi, m, n: (m, 0)),
        pl.BlockSpec((tm, tseq), lambda bi, m, n: (m, 0)),
        pl.BlockSpec((tseq, tn), lambda bi, m, n: (sb0 + bi, n)),
        pl.BlockSpec((tseq, tn), lambda bi, m, n: (sb0 + bi, n)),
        pl.BlockSpec((D, tn), lambda bi, m, n: (0, n)),
    ]
    args = [xs, g, mods, mods, ct, st, a, b, wg]
    aliases = {}
    if prev is not None:
        in_specs.append(pl.BlockSpec(memory_space=pl.ANY))
        args.append(prev)
        aliases = {len(args) - 1: 0}
    return pl.pallas_call(
        _fnet_mix_kernel,
        grid=(BATCH, mt, D // tn),
        in_specs=in_specs,
        out_specs=pl.BlockSpec((tm, tn), lambda bi, m, n: (row_blk(bi, m, n), n)),
        out_shape=jax.ShapeDtypeStruct((N_ALL, D), BF16),
        scratch_shapes=[pltpu.VMEM((tm, D), BF16)],
        input_output_aliases=aliases,
        compiler_params=_params(("parallel", "parallel", "arbitrary"), 48),
        name="fnet_mix_%d" % tseq,
    )(*args)


def _outproj_kernel(m_ref, w_ref, x_ref, gate_ref, o_ref):
    y = jnp.dot(m_ref[...], w_ref[...], preferred_element_type=F32)
    o_ref[...] = x_ref[...] + gate_ref[...] * y


def _outproj(mbuf, w, xs, mods, n_rows):
    return pl.pallas_call(
        _outproj_kernel,
        grid=(n_rows // TM, D // TN),
        in_specs=[
            pl.BlockSpec((TM, D), lambda r, n: (r, 0)),
            pl.BlockSpec((D, TN), lambda r, n: (0, n)),
            pl.BlockSpec((TM, TN), lambda r, n: (r, n)),
            pl.BlockSpec((None, 1, TN), lambda r, n: (_mod_row(r, TM), 0, 2 * (D // TN) + n)),
        ],
        out_specs=pl.BlockSpec((TM, TN), lambda r, n: (r, n)),
        out_shape=jax.ShapeDtypeStruct((N_ALL, D), F32),
        input_output_aliases={2: 0},
        compiler_params=_params(("parallel", "arbitrary"), 48),
        name="outproj",
    )(mbuf, w, xs, mods)


M_NQKV = (2 * M_QK + M_V) // TN
M_NQ = M_QK // TN


def _mlstm_proj_kernel(x_ref, g_ref, sh_ref, sc_ref, w_ref, wg_ref, bg_ref,
                       qkv_ref, oz_ref, gt_ref, h_scr):
    j = pl.program_id(1)

    @pl.when(j == 0)
    def _():
        _store_modnorm(x_ref, g_ref, sc_ref, sh_ref, h_scr)
        gt_ref[...] = jnp.dot(h_scr[...], wg_ref[...], preferred_element_type=F32) + bg_ref[...]

    acc = jnp.dot(h_scr[...], w_ref[...], preferred_element_type=F32)

    @pl.when(j < M_NQ)
    def _():
        qkv_ref[...] = (acc * (M_DQK ** -0.5)).astype(BF16)

    @pl.when(jnp.logical_and(j >= M_NQ, j < M_NQKV))
    def _():
        qkv_ref[...] = acc.astype(BF16)

    @pl.when(j >= M_NQKV)
    def _():
        oz_ref[...] = acc


def _mlstm_proj(xs, g, mods, w, wg, bg):
    ncol = w.shape[1] // TN
    return pl.pallas_call(
        _mlstm_proj_kernel,
        grid=(N_ALL // TM, ncol),
        in_specs=[pl.BlockSpec((TM, D), lambda r, j: (r, 0))] + _mod_specs(TM, lambda r, j: r) + [
            pl.BlockSpec((D, TN), lambda r, j: (0, j)),
            pl.BlockSpec((D, M_GATES), lambda r, j: (0, 0)),
            pl.BlockSpec((1, M_GATES), lambda r, j: (0, 0)),
        ],
        out_specs=[
            pl.BlockSpec((TM, TN), lambda r, j: (r, jnp.minimum(j, M_NQKV - 1))),
            pl.BlockSpec((TM, TN), lambda r, j: (r, jnp.maximum(j - M_NQKV, 0))),
            pl.BlockSpec((TM, M_GATES), lambda r, j: (r, 0)),
        ],
        out_shape=[
            jax.ShapeDtypeStruct((N_ALL, 2 * M_QK + M_V), BF16),
            jax.ShapeDtypeStruct((N_ALL, 2 * M_V), F32),
            jax.ShapeDtypeStruct((N_ALL, M_GATES), F32),
        ],
        scratch_shapes=[pltpu.VMEM((TM, D), BF16)],
        compiler_params=_params(("parallel", "arbitrary"), 48),
        name="mlstm_proj",
    )(xs, g, mods, mods, w, wg, bg)


def _split3(x):
    hi = x.astype(BF16)
    r1 = x - hi.astype(F32)
    mid = r1.astype(BF16)
    lo = (r1 - mid.astype(F32)).astype(BF16)
    return hi, mid, lo


def _mlstm_scan_kernel(q_ref, k_ref, v_ref, gt_ref, o_ref, c_scr, n_scr, m_scr, *, reverse):
    @pl.when(pl.program_id(1) == 0)
    def _():
        c_scr[...] = jnp.zeros_like(c_scr)
        n_scr[...] = jnp.zeros_like(n_scr)
        m_scr[...] = jnp.zeros_like(m_scr)

    L = M_L
    col0 = 2 * M_HEADS if reverse else 0
    gt = gt_ref[...]
    ls = _log_sigmoid(gt)
    row = lax.broadcasted_iota(jnp.int32, (L, L), 0)
    col = lax.broadcasted_iota(jnp.int32, (L, L), 1)
    order = (row <= col) if reverse else (row >= col)
    tri = jnp.where(order, 1.0, 0.0).astype(BF16)
    hi, mid, lo = _split3(ls)
    bc = (jnp.dot(tri, hi, preferred_element_type=F32)
          + jnp.dot(tri, mid, preferred_element_type=F32)
          + jnp.dot(tri, lo, preferred_element_type=F32))
    gt_t = gt.T
    bc_t = bc.T
    end = 0 if reverse else L - 1

    for h in range(M_HEADS):
        ci = col0 + h
        cf = col0 + M_HEADS + h
        ig_c = gt[:, ci:ci + 1]
        b_c = bc[:, cf:cf + 1]
        ig_r = gt_t[ci:ci + 1, :]
        b_r = bc_t[cf:cf + 1, :]
        b_end = bc[end:end + 1, cf:cf + 1]
        m_prev = m_scr[h:h + 1, 0:1]
        q = q_ref[:, h * M_DQK:(h + 1) * M_DQK]
        k = k_ref[:, h * M_DQK:(h + 1) * M_DQK]
        v = v_ref[:, h * M_DV:(h + 1) * M_DV]
        c_prev = c_scr[h]
        n_prev = n_scr[h:h + 1, :]

        dmat = jnp.where(order, b_c + (ig_r - b_r), -jnp.inf)
        inter = b_c + m_prev
        m_t = jnp.maximum(inter, jnp.max(dmat, axis=-1, keepdims=True))
        qk = lax.dot_general(q, k, (((1,), (1,)), ((), ())), preferred_element_type=F32)
        s = qk * jnp.exp(dmat - m_t)
        a = jnp.exp(inter - m_t)
        num = (a * jnp.dot(q, c_prev.astype(BF16), preferred_element_type=F32)
               + jnp.dot(s.astype(BF16), v, preferred_element_type=F32))
        den = (a * jnp.sum(q.astype(F32) * n_prev, axis=-1, keepdims=True)
               + jnp.sum(s, axis=-1, keepdims=True))
        o_ref[:, h * M_DV:(h + 1) * M_DV] = num / jnp.maximum(jnp.abs(den), jnp.exp(-m_t))

        gl = b_end - b_c + ig_c
        m_new = jnp.maximum(b_end + m_prev, jnp.max(gl, axis=0, keepdims=True))
        w = jnp.exp(gl - m_new)
        decay = jnp.exp(b_end + m_prev - m_new)
        kw = k.astype(F32) * w
        c_scr[h] = decay * c_prev + lax.dot_general(
            kw.astype(BF16), v, (((0,), (0,)), ((), ())), preferred_element_type=F32)
        n_scr[h:h + 1, :] = decay * n_prev + jnp.sum(kw, axis=0, keepdims=True)
        m_scr[h:h + 1, :] = jnp.broadcast_to(m_new, (1, 128))


def _mlstm_scan(qkv, gates, reverse):
    nlc = SEQ // M_L
    ctx_blk0 = N_LAT // M_L

    def blk(b, i):
        lat = (nlc - i) if reverse else (i - 1)
        return jnp.where(i == 0, ctx_blk0 + b, b * nlc + lat)

    return pl.pallas_call(
        functools.partial(_mlstm_scan_kernel, reverse=reverse),
        grid=(BATCH, nlc + 1),
        in_specs=[
            pl.BlockSpec((M_L, M_QK), lambda b, i: (blk(b, i), 0)),
            pl.BlockSpec((M_L, M_QK), lambda b, i: (blk(b, i), 1)),
            pl.BlockSpec((M_L, M_V), lambda b, i: (blk(b, i), 1)),
            pl.BlockSpec((M_L, M_GATES), lambda b, i: (blk(b, i), 0)),
        ],
        out_specs=pl.BlockSpec((M_L, M_V), lambda b, i: (blk(b, i), 0)),
        out_shape=jax.ShapeDtypeStruct((N_ALL, M_V), F32),
        scratch_shapes=[
            pltpu.VMEM((M_HEADS, M_DQK, M_DV), F32),
            pltpu.VMEM((M_HEADS, M_DQK), F32),
            pltpu.VMEM((M_HEADS, 128), F32),
        ],
        compiler_params=_params(("parallel", "arbitrary"), 48),
        name="mlstm_scan_bwd" if reverse else "mlstm_scan_fwd",
    )(qkv, qkv, qkv, gates)


def _mlstm_finish_kernel(o_ref, z_ref, hf_ref, hb_ref, hn_ref, m_ref):
    for h in range(M_HEADS):
        sl = slice(h * M_DV, (h + 1) * M_DV)
        y = _sigmoid(o_ref[:, sl]) * (hf_ref[:, sl] + hb_ref[:, sl])
        ms = jnp.mean(y * y, axis=-1, keepdims=True)
        y = y * lax.rsqrt(ms + EPS) * hn_ref[:, sl]
        m_ref[:, sl] = (y * _silu(z_ref[:, sl])).astype(BF16)


def _mlstm_finish(oz, hf, hb, hn):
    tm = 256
    return pl.pallas_call(
        _mlstm_finish_kernel,
        grid=(N_ALL // tm,),
        in_specs=[
            pl.BlockSpec((tm, M_V), lambda r: (r, 0)),
            pl.BlockSpec((tm, M_V), lambda r: (r, 1)),
            pl.BlockSpec((tm, M_V), lambda r: (r, 0)),
            pl.BlockSpec((tm, M_V), lambda r: (r, 0)),
            pl.BlockSpec((1, M_V), lambda r: (0, 0)),
        ],
        out_specs=pl.BlockSpec((tm, M_V), lambda r: (r, 0)),
        out_shape=jax.ShapeDtypeStruct((N_ALL, M_V), BF16),
        compiler_params=_params(("parallel",), 48),
        name="mlstm_finish",
    )(oz, oz, hf, hb, hn)


A_NQ = A_Q // TN
A_JK = A_NQ
A_JV = A_NQ + 1
A_JZ = A_NQ + 2


def _headnorm_rope(acc, gain, cos, sin_a, sin_b, scale):
    outs = []
    for hh in range(TN // A_HD):
        a = acc[:, hh * A_HD:(hh + 1) * A_HD]
        ms = jnp.mean(a * a, axis=-1, keepdims=True)
        a = a * lax.rsqrt(ms + EPS) * gain
        a = a * cos + pltpu.roll(a, A_HD - 1, 1) * sin_a + pltpu.roll(a, 1, 1) * sin_b
        outs.append((a * scale).astype(BF16))
    return jnp.concatenate(outs, axis=-1)


def _attn_proj_kernel(x_ref, g_ref, sh_ref, sc_ref, w_ref, qn_ref, kn_ref, cos_ref, sa_ref, sb_ref,
                      q_ref, k_ref, v_ref, z_ref, h_scr):
    j = pl.program_id(1)

    @pl.when(j == 0)
    def _():
        _store_modnorm(x_ref, g_ref, sc_ref, sh_ref, h_scr)

    acc = jnp.dot(h_scr[...], w_ref[...], preferred_element_type=F32)

    @pl.when(j < A_NQ)
    def _():
        q_ref[...] = _headnorm_rope(acc, qn_ref[...], cos_ref[...], sa_ref[...], sb_ref[...],
                                    A_HD ** -0.5)

    @pl.when(j == A_JK)
    def _():
        k_ref[...] = _headnorm_rope(acc, kn_ref[...], cos_ref[...], sa_ref[...], sb_ref[...], 1.0)

    @pl.when(j == A_JV)
    def _():
        v_ref[...] = acc.astype(BF16)

    @pl.when(j >= A_JZ)
    def _():
        z_ref[...] = acc


def _attn_proj(xs, g, mods, w, qn, kn, cos, sin_a, sin_b):
    ncol = w.shape[1] // TN
    lat_tiles = N_LAT // TM
    rope_blk = lambda r, j: (jnp.where(r < lat_tiles, r % (SEQ // TM), SEQ // TM), 0)
    return pl.pallas_call(
        _attn_proj_kernel,
        grid=(N_ALL // TM, ncol),
        in_specs=[pl.BlockSpec((TM, D), lambda r, j: (r, 0))] + _mod_specs(TM, lambda r, j: r) + [
            pl.BlockSpec((D, TN), lambda r, j: (0, j)),
            pl.BlockSpec((1, A_HD), lambda r, j: (0, 0)),
            pl.BlockSpec((1, A_HD), lambda r, j: (0, 0)),
            pl.BlockSpec((TM, A_HD), rope_blk),
            pl.BlockSpec((TM, A_HD), rope_blk),
            pl.BlockSpec((TM, A_HD), rope_blk),
        ],
        out_specs=[
            pl.BlockSpec((TM, TN), lambda r, j: (r, jnp.minimum(j, A_NQ - 1))),
            pl.BlockSpec((TM, TN), lambda r, j: (r, 0)),
            pl.BlockSpec((TM, TN), lambda r, j: (r, 0)),
            pl.BlockSpec((TM, TN), lambda r, j: (r, jnp.maximum(j - A_JZ, 0))),
        ],
        out_shape=[
            jax.ShapeDtypeStruct((N_ALL, A_Q), BF16),
            jax.ShapeDtypeStruct((N_ALL, A_KVW), BF16),
            jax.ShapeDtypeStruct((N_ALL, A_KVW), BF16),
            jax.ShapeDtypeStruct((N_ALL, A_Q), F32),
        ],
        scratch_shapes=[pltpu.VMEM((TM, D), BF16)],
        compiler_params=_params(("parallel", "arbitrary"), 48),
        name="attn_proj",
    )(xs, g, mods, mods, w, qn, kn, cos, sin_a, sin_b)


A_LAT_TILES = SEQ // A_TQ


def _attn_kernel(q_ref, kl_ref, kc_ref, vl_ref, vc_ref, z_ref, o_ref):
    t = pl.program_id(2)
    nt = (((1,), (1,)), ((), ()))

    def run(with_latent_keys):
        for hh in range(A_GW // A_HD):
            sl = slice(hh * A_HD, (hh + 1) * A_HD)
            q = q_ref[:, sl]
            s_c = lax.dot_general(q, kc_ref[...], nt, preferred_element_type=F32)
            m = jnp.max(s_c, axis=-1, keepdims=True)
            if with_latent_keys:
                s_l = lax.dot_general(q, kl_ref[...], nt, preferred_element_type=F32)
                m = jnp.maximum(m, jnp.max(s_l, axis=-1, keepdims=True))
            p_c = jnp.exp(s_c - m)
            den = jnp.sum(p_c, axis=-1, keepdims=True)
            acc = jnp.dot(p_c.astype(BF16), vc_ref[...], preferred_element_type=F32)
            if with_latent_keys:
                p_l = jnp.exp(s_l - m)
                den = den + jnp.sum(p_l, axis=-1, keepdims=True)
                acc = acc + jnp.dot(p_l.astype(BF16), vl_ref[...], preferred_element_type=F32)
            o_ref[:, sl] = (acc / den * _silu(z_ref[:, sl])).astype(BF16)

    @pl.when(t < A_LAT_TILES)
    def _():
        run(True)

    @pl.when(t >= A_LAT_TILES)
    def _():
        run(False)


def _attn(q, k, v, z):
    ctx_blk0 = N_LAT // A_TQ
    qrow = lambda b, g, t: jnp.where(t < A_LAT_TILES, b * A_LAT_TILES + t, ctx_blk0 + b)
    return pl.pallas_call(
        _attn_kernel,
        grid=(BATCH, A_KV, A_LAT_TILES + 1),
        in_specs=[
            pl.BlockSpec((A_TQ, A_GW), lambda b, g, t: (qrow(b, g, t), g)),
            pl.BlockSpec((SEQ, A_HD), lambda b, g, t: (b, g)),
            pl.BlockSpec((CTX, A_HD), lambda b, g, t: (N_LAT // CTX + b, g)),
            pl.BlockSpec((SEQ, A_HD), lambda b, g, t: (b, g)),
            pl.BlockSpec((CTX, A_HD), lambda b, g, t: (N_LAT // CTX + b, g)),
            pl.BlockSpec((A_TQ, A_GW), lambda b, g, t: (qrow(b, g, t), g)),
        ],
        out_specs=pl.BlockSpec((A_TQ, A_GW), lambda b, g, t: (qrow(b, g, t), g)),
        out_shape=jax.ShapeDtypeStruct((N_ALL, A_Q), BF16),
        compiler_params=_params(("parallel", "parallel", "arbitrary"), 48),
        name="attn",
    )(q, k, k, v, v, z)


def _final_norm_kernel(x_ref, g_ref, o_ref):
    x = x_ref[...]
    ms = jnp.mean(x * x, axis=-1, keepdims=True)
    o_ref[...] = x * lax.rsqrt(ms + EPS) * g_ref[...]


def _final_norm(xs, g):
    tm = 512
    return pl.pallas_call(
        _final_norm_kernel,
        grid=(N_LAT // tm,),
        in_specs=[pl.BlockSpec((tm, D), lambda r: (r, 0)), pl.BlockSpec((1, D), lambda r: (0, 0))],
        out_specs=pl.BlockSpec((tm, D), lambda r: (r, 0)),
        out_shape=jax.ShapeDtypeStruct((N_LAT, D), F32),
        compiler_params=_params(("parallel",), 32),
        name="final_norm",
    )(xs, g)


def _dft_tables(n, scale):
    idx = jnp.arange(n, dtype=jnp.int32)
    ang = ((idx[:, None] * idx[None, :]) % n).astype(F32) * (2.0 * math.pi / n)
    return (jnp.cos(ang) * scale).astype(BF16), (jnp.sin(ang) * scale).astype(BF16)


def _rope_tables():
    t = jnp.arange(SEQ)
    half = A_HD // 4
    freqs = ROPE_THETA ** (-jnp.arange(0, A_HD // 2, 2, dtype=F32) / (A_HD // 2))
    ang = jnp.concatenate([(t // GRID_W).astype(F32)[:, None] * freqs,
                           (t % GRID_W).astype(F32)[:, None] * freqs], axis=-1)
    assert ang.shape == (SEQ, 2 * half)
    cos = jnp.repeat(jnp.cos(ang), 2, axis=-1)
    sin = jnp.repeat(jnp.sin(ang), 2, axis=-1)
    even = (jnp.arange(A_HD) % 2 == 0)[None, :]
    sin_a = jnp.where(even, -sin, 0.0)
    sin_b = jnp.where(even, 0.0, sin)
    pad = lambda a, fill: jnp.concatenate([a, jnp.full((TM, A_HD), fill, F32)], axis=0)
    return pad(cos, 1.0), pad(sin_a, 0.0), pad(sin_b, 0.0)


def kernel(x, c, ctx, c_ctx, ada_w, ada_b, norm_g, fnet_w_gate, fnet_w_out, mlstm_w_in, mlstm_b_gate,
           mlstm_hn, mlstm_w_out, attn_w_in, attn_qn, attn_kn, attn_w_out, final_g):
    xs = jnp.concatenate([x.reshape(N_LAT, D), ctx.reshape(N_CTX, D)], axis=0)
    cc = jnp.concatenate([c, c_ctx[None, :], jnp.zeros((MOD_ROWS - BATCH - 1, D), F32)], axis=0)
    mods_all = _modvec(cc, ada_w, ada_b).reshape(DEPTH, MOD_ROWS, 1, 3 * D)

    cc_c, sc_c = _dft_tables(F_GW, F_GW ** -0.5)
    cs_chan = jnp.concatenate([cc_c, sc_c], axis=1)
    ct_lat, st_lat = _dft_tables(SEQ, SEQ ** -0.5)
    ct_ctx, st_ctx = _dft_tables(CTX, CTX ** -0.5)

    for i in range(DEPTH):
        kind, j = i % 3, i // 3
        ctx_out = i != DEPTH - 1
        mods = mods_all[i]
        g = norm_g[i].reshape(1, D)
        n_rows = N_ALL if (ctx_out or kind != 0) else N_LAT
        if kind == 0:
            wg = fnet_w_gate[j].astype(BF16)
            a, b = _fnet_chan(xs, g, mods, cs_chan, n_rows)
            mbuf = _fnet_mix(xs, g, mods, ct_lat, st_lat, a, b, wg, SEQ, 0, None)
            if ctx_out:
                mbuf = _fnet_mix(xs, g, mods, ct_ctx, st_ctx, a, b, wg, CTX, N_LAT, CTX_MOD_ROW,
                                 prev=mbuf)
            w_out = fnet_w_out[j]
        elif kind == 1:
            w_in = mlstm_w_in[j]
            n_main = 2 * M_QK + 2 * M_V
            w_main = jnp.concatenate([w_in[:, :n_main], w_in[:, n_main + 4 * M_HEADS:]],
                                     axis=1).astype(BF16)
            w_gates = jnp.pad(w_in[:, n_main:n_main + 4 * M_HEADS],
                              ((0, 0), (0, M_GATES - 4 * M_HEADS))).astype(BF16)
            b_gates = jnp.pad(mlstm_b_gate[j], (0, M_GATES - 4 * M_HEADS)).reshape(1, M_GATES)
            qkv, oz, gates = _mlstm_proj(xs, g, mods, w_main, w_gates, b_gates)
            hf = _mlstm_scan(qkv, gates, reverse=False)
            hb = _mlstm_scan(qkv, gates, reverse=True)
            mbuf = _mlstm_finish(oz, hf, hb, mlstm_hn[j].reshape(1, M_V))
            w_out = mlstm_w_out[j]
        else:
            cos, sin_a, sin_b = _rope_tables()
            q, k, v, z = _attn_proj(xs, g, mods, attn_w_in[j].astype(BF16),
                                    attn_qn[j].reshape(1, A_HD), attn_kn[j].reshape(1, A_HD),
                                    cos, sin_a, sin_b)
            mbuf = _attn(q, k, v, z)
            w_out = attn_w_out[j]
        xs = _outproj(mbuf, w_out.astype(BF16), xs, mods, n_rows)

    return _final_norm(xs, final_g.reshape(1, D)).reshape(BATCH, SEQ, D)
```

```python
import functools
import math

import jax
import jax.numpy as jnp
from jax import lax
from jax.experimental import pallas as pl
from jax.experimental.pallas import tpu as pltpu

F32 = jnp.float32
BF16 = jnp.bfloat16

D = 2048
BATCH = 16
SEQ = 2048
CTX = 256
DEPTH = 4
EPS = 1e-6
N_LAT = BATCH * SEQ
N_CTX = BATCH * CTX
N_ALL = N_LAT + N_CTX
MOD_ROWS = 24
CTX_MOD_ROW = BATCH

F_GROUPS = 4
F_GW = D // F_GROUPS

M_HEADS = 8
M_DQK = 128
M_DV = 256
M_QK = M_HEADS * M_DQK
M_V = M_HEADS * M_DV
M_L = 256
M_GATES = 128
M_HALF = 2 * M_QK + M_V

A_HEADS = 16
A_KV = 4
A_HD = 128
A_Q = A_HEADS * A_HD
A_KVW = A_KV * A_HD
A_GW = A_Q // A_KV
A_TQ = 256
GRID_W = 64
ROPE_THETA = 10000.0

TM = 512
TN = 512
MIB = 1024 * 1024


def _params(sem, vmem_mib):
    return pltpu.CompilerParams(dimension_semantics=sem, vmem_limit_bytes=vmem_mib * MIB)


def _resident(shape, index_map):
    return pl.BlockSpec(shape, index_map, pipeline_mode=pl.Buffered(1))


def _sigmoid(x):
    return 1.0 / (1.0 + jnp.exp(-x))


def _silu(x):
    return x * _sigmoid(x)


def _log_sigmoid(x):
    return jnp.minimum(x, 0.0) - jnp.log1p(jnp.exp(-jnp.abs(x)))


def _modnorm(x, g, scale, shift):
    ms = jnp.mean(x * x, axis=-1, keepdims=True)
    y = x * lax.rsqrt(ms + EPS) * g
    return y * (1.0 + scale) + shift


def _store_modnorm(x_ref, g_ref, sc_ref, sh_ref, h_scr):
    rows = x_ref.shape[0]
    step = min(rows, 256)
    for r0 in range(0, rows, step):
        h_scr[r0:r0 + step, :] = _modnorm(x_ref[r0:r0 + step, :], g_ref[...], sc_ref[...],
                                          sh_ref[...]).astype(BF16)


def _mod_row(r, tm):
    return jnp.where(r < N_LAT // tm, r // (SEQ // tm), CTX_MOD_ROW)


def _modvec_kernel(c_ref, w_ref, b_ref, o_ref):
    s = _silu(c_ref[...])
    o_ref[...] = jnp.dot(s, w_ref[...], preferred_element_type=F32,
                         precision=lax.Precision.HIGHEST) + b_ref[...]


def _modvec(cc, ada_w, ada_b):
    tn = 1024
    return pl.pallas_call(
        _modvec_kernel,
        grid=(DEPTH, 3 * D // tn),
        in_specs=[
            pl.BlockSpec((MOD_ROWS, D), lambda i, j: (0, 0)),
            pl.BlockSpec((None, D, tn), lambda i, j: (i, 0, j)),
            pl.BlockSpec((None, 1, tn), lambda i, j: (i, 0, j)),
        ],
        out_specs=pl.BlockSpec((None, MOD_ROWS, tn), lambda i, j: (i, 0, j)),
        out_shape=jax.ShapeDtypeStruct((DEPTH, MOD_ROWS, 3 * D), F32),
        compiler_params=_params(("parallel", "parallel"), 40),
        name="modvec",
    )(cc, ada_w, ada_b.reshape(DEPTH, 1, 3 * D))


def _mod_specs(tm, row_of):
    return [
        pl.BlockSpec((1, D), lambda *ids: (0, 0)),
        pl.BlockSpec((None, 1, D), lambda *ids: (_mod_row(row_of(*ids), tm), 0, 0)),
        pl.BlockSpec((None, 1, D), lambda *ids: (_mod_row(row_of(*ids), tm), 0, 1)),
    ]


def _fnet_chan_kernel(x_ref, g_ref, sh_ref, sc_ref, cs_ref, a_ref, b_ref):
    h = _modnorm(x_ref[...], g_ref[...], sc_ref[...], sh_ref[...]).astype(BF16)
    for grp in range(F_GROUPS):
        sl = slice(grp * F_GW, (grp + 1) * F_GW)
        p = jnp.dot(h[:, sl], cs_ref[...], preferred_element_type=F32)
        a_ref[:, sl] = p[:, :F_GW].astype(BF16)
        b_ref[:, sl] = p[:, F_GW:].astype(BF16)


def _fnet_chan(xs, g, mods, cs_c, n_rows):
    tm = 512
    out = jax.ShapeDtypeStruct((N_ALL, D), BF16)
    return pl.pallas_call(
        _fnet_chan_kernel,
        grid=(n_rows // tm,),
        in_specs=[pl.BlockSpec((tm, D), lambda r: (r, 0))] + _mod_specs(tm, lambda r: r) + [
            pl.BlockSpec((F_GW, 2 * F_GW), lambda r: (0, 0)),
        ],
        out_specs=[pl.BlockSpec((tm, D), lambda r: (r, 0))] * 2,
        out_shape=[out, out],
        compiler_params=_params(("parallel",), 48),
        name="fnet_chan",
    )(xs, g, mods, mods, cs_c)


def _fnet_mix_kernel(x_ref, g_ref, sh_ref, sc_ref, ct_ref, st_ref, a_ref, b_ref, wg_ref, *rest):
    o_ref, h_scr = rest[-2], rest[-1]

    @pl.when(pl.program_id(2) == 0)
    def _():
        _store_modnorm(x_ref, g_ref, sc_ref, sh_ref, h_scr)

    y = jnp.dot(ct_ref[...], a_ref[...], preferred_element_type=F32)
    y = y - jnp.dot(st_ref[...], b_ref[...], preferred_element_type=F32)
    gate = jnp.dot(h_scr[...], wg_ref[...], preferred_element_type=F32)
    o_ref[...] = (y * _silu(gate)).astype(BF16)


def _fnet_mix(xs, g, mods, ct, st, a, b, wg, tseq, row0, mod_row, prev=None):
    tm = min(tseq, 512)
    tn = 512
    mt = tseq // tm
    rb0 = row0 // tm
    sb0 = row0 // tseq
    row_blk = lambda bi, m, n: rb0 + bi * mt + m
    mrow = (lambda bi: bi) if mod_row is None else (lambda bi: mod_row)
    in_specs = [
        pl.BlockSpec((tm, D), lambda bi, m, n: (row_blk(bi, m, n), 0)),
        pl.BlockSpec((1, D), lambda bi, m, n: (0, 0)),
        pl.BlockSpec((None, 1, D), lambda bi, m, n: (mrow(bi), 0, 0)),
        pl.BlockSpec((None, 1, D), lambda bi, m, n: (mrow(bi), 0, 1)),
        pl.BlockSpec((tm, tseq), lambda bi, m, n: (m, 0)),
        pl.BlockSpec((tm, tseq), lambda bi, m, n: (m, 0)),
        pl.BlockSpec((tseq, tn), lambda bi, m, n: (sb0 + bi, n)),
        pl.BlockSpec((tseq, tn), lambda bi, m, n: (sb0 + bi, n)),
        pl.BlockSpec((D, tn), lambda bi, m, n: (0, n)),
    ]
    args = [xs, g, mods, mods, ct, st, a, b, wg]
    aliases = {}
    if prev is not None:
        in_specs.append(pl.BlockSpec(memory_space=pl.ANY))
        args.append(prev)
        aliases = {len(args) - 1: 0}
    return pl.pallas_call(
        _fnet_mix_kernel,
        grid=(BATCH, mt, D // tn),
        in_specs=in_specs,
        out_specs=pl.BlockSpec((tm, tn), lambda bi, m, n: (row_blk(bi, m, n), n)),
        out_shape=jax.ShapeDtypeStruct((N_ALL, D), BF16),
        scratch_shapes=[pltpu.VMEM((tm, D), BF16)],
        input_output_aliases=aliases,
        compiler_params=_params(("parallel", "parallel", "arbitrary"), 48),
        name="fnet_mix_%d" % tseq,
    )(*args)


def _outproj_kernel(m_ref, w_ref, x_ref, gate_ref, *rest, final):
    o_ref = rest[-1]
    for c in range(D // TN):
        sl = slice(c * TN, (c + 1) * TN)
        y = jnp.dot(m_ref[...], w_ref[:, sl], preferred_element_type=F32)
        o_ref[:, sl] = x_ref[:, sl] + gate_ref[:, sl] * y
    if final:
        fg_ref = rest[0]
        x = o_ref[...]
        ms = jnp.mean(x * x, axis=-1, keepdims=True)
        o_ref[...] = x * lax.rsqrt(ms + EPS) * fg_ref[...]


def _outproj(mbuf, w, xs, mods, n_rows, final_g=None):
    final = final_g is not None
    in_specs = [
        pl.BlockSpec((TM, D), lambda r: (r, 0)),
        _resident((D, D), lambda r: (0, 0)),
        pl.BlockSpec((TM, D), lambda r: (r, 0)),
        pl.BlockSpec((None, 1, D), lambda r: (_mod_row(r, TM), 0, 2)),
    ]
    args = [mbuf, w, xs, mods]
    if final:
        in_specs.append(pl.BlockSpec((1, D), lambda r: (0, 0)))
        args.append(final_g)
    return pl.pallas_call(
        functools.partial(_outproj_kernel, final=final),
        grid=(n_rows // TM,),
        in_specs=in_specs,
        out_specs=pl.BlockSpec((TM, D), lambda r: (r, 0)),
        out_shape=jax.ShapeDtypeStruct((n_rows if final else N_ALL, D), F32),
        input_output_aliases={} if final else {2: 0},
        compiler_params=_params(("parallel",), 48),
        name="outproj_final" if final else "outproj",
    )(*args)


def _mlstm_proj_kernel(x_ref, g_ref, sh_ref, sc_ref, w_ref, wg_ref, bg_ref, out_ref, gt_ref, h_scr):
    half = pl.program_id(0)
    _store_modnorm(x_ref, g_ref, sc_ref, sh_ref, h_scr)

    def chunks(post):
        for c in range(M_HALF // TN):
            sl = slice(c * TN, (c + 1) * TN)
            acc = jnp.dot(h_scr[...], w_ref[:, sl], preferred_element_type=F32)
            out_ref[:, sl] = post(c, acc).astype(BF16)

    @pl.when(half == 0)
    def _():
        gt_ref[...] = jnp.dot(h_scr[...], wg_ref[...], preferred_element_type=F32) + bg_ref[...]
        chunks(lambda c, acc: acc * (M_DQK ** -0.5) if c < M_QK // TN else acc)

    @pl.when(half == 1)
    def _():
        chunks(lambda c, acc: _sigmoid(acc) if c < M_V // TN else _silu(acc))


def _mlstm_proj(xs, g, mods, w, wg, bg):
    nr = N_ALL // TM
    return pl.pallas_call(
        _mlstm_proj_kernel,
        grid=(2, nr),
        in_specs=[pl.BlockSpec((TM, D), lambda hf, r: (r, 0))] + _mod_specs(TM, lambda hf, r: r) + [
            _resident((D, M_HALF), lambda hf, r: (0, hf)),
            pl.BlockSpec((D, M_GATES), lambda hf, r: (0, 0)),
            pl.BlockSpec((1, M_GATES), lambda hf, r: (0, 0)),
        ],
        out_specs=[
            pl.BlockSpec((TM, M_HALF), lambda hf, r: (r, hf)),
            pl.BlockSpec((TM, M_GATES), lambda hf, r: (jnp.where(hf == 0, r, nr - 1), 0)),
        ],
        out_shape=[
            jax.ShapeDtypeStruct((N_ALL, 2 * M_HALF), BF16),
            jax.ShapeDtypeStruct((N_ALL, M_GATES), F32),
        ],
        scratch_shapes=[pltpu.VMEM((TM, D), BF16)],
        compiler_params=_params(("arbitrary", "arbitrary"), 56),
        name="mlstm_proj",
    )(xs, g, mods, mods, w, wg, bg)


def _split3(x):
    hi = x.astype(BF16)
    r1 = x - hi.astype(F32)
    mid = r1.astype(BF16)
    lo = (r1 - mid.astype(F32)).astype(BF16)
    return hi, mid, lo


def _mlstm_scan_kernel(q_ref, k_ref, v_ref, gt_ref, o_ref, c_scr, n_scr, m_scr, *, reverse):
    @pl.when(pl.program_id(1) == 0)
    def _():
        c_scr[...] = jnp.zeros_like(c_scr)
        n_scr[...] = jnp.zeros_like(n_scr)
        m_scr[...] = jnp.zeros_like(m_scr)

    L = M_L
    col0 = 2 * M_HEADS if reverse else 0
    gt = gt_ref[...]
    ls = _log_sigmoid(gt)
    row = lax.broadcasted_iota(jnp.int32, (L, L), 0)
    col = lax.broadcasted_iota(jnp.int32, (L, L), 1)
    order = (row <= col) if reverse else (row >= col)
    tri = jnp.where(order, 1.0, 0.0).astype(BF16)
    hi, mid, lo = _split3(ls)
    bc = (jnp.dot(tri, hi, preferred_element_type=F32)
          + jnp.dot(tri, mid, preferred_element_type=F32)
          + jnp.dot(tri, lo, preferred_element_type=F32))
    gt_t = gt.T
    bc_t = bc.T
    end = 0 if reverse else L - 1

    for h in range(M_HEADS):
        ci = col0 + h
        cf = col0 + M_HEADS + h
        ig_c = gt[:, ci:ci + 1]
        b_c = bc[:, cf:cf + 1]
        ig_r = gt_t[ci:ci + 1, :]
        b_r = bc_t[cf:cf + 1, :]
        b_end = bc[end:end + 1, cf:cf + 1]
        m_prev = m_scr[h:h + 1, 0:1]
        q = q_ref[:, h * M_DQK:(h + 1) * M_DQK]
        k = k_ref[:, h * M_DQK:(h + 1) * M_DQK]
        v = v_ref[:, h * M_DV:(h + 1) * M_DV]
        c_prev = c_scr[h]
        n_prev = n_scr[h:h + 1, :]

        dmat = jnp.where(order, b_c + (ig_r - b_r), -jnp.inf)
        inter = b_c + m_prev
        m_t = jnp.maximum(inter, jnp.max(dmat, axis=-1, keepdims=True))
        qk = lax.dot_general(q, k, (((1,), (1,)), ((), ())), preferred_element_type=F32)
        s = qk * jnp.exp(dmat - m_t)
        a = jnp.exp(inter - m_t)
        num = (a * jnp.dot(q, c_prev.astype(BF16), preferred_element_type=F32)
               + jnp.dot(s.astype(BF16), v, preferred_element_type=F32))
        den = (a * jnp.sum(q.astype(F32) * n_prev, axis=-1, keepdims=True)
               + jnp.sum(s, axis=-1, keepdims=True))
        o_ref[:, h * M_DV:(h + 1) * M_DV] = num / jnp.maximum(jnp.abs(den), jnp.exp(-m_t))

        gl = b_end - b_c + ig_c
        m_new = jnp.maximum(b_end + m_prev, jnp.max(gl, axis=0, keepdims=True))
        w = jnp.exp(gl - m_new)
        decay = jnp.exp(b_end + m_prev - m_new)
        kw = k.astype(F32) * w
        c_scr[h] = decay * c_prev + lax.dot_general(
            kw.astype(BF16), v, (((0,), (0,)), ((), ())), preferred_element_type=F32)
        n_scr[h:h + 1, :] = decay * n_prev + jnp.sum(kw, axis=0, keepdims=True)
        m_scr[h:h + 1, :] = jnp.broadcast_to(m_new, (1, 128))


def _mlstm_scan(proj, gates, reverse):
    nlc = SEQ // M_L
    ctx_blk0 = N_LAT // M_L

    def blk(b, i):
        lat = (nlc - i) if reverse else (i - 1)
        return jnp.where(i == 0, ctx_blk0 + b, b * nlc + lat)

    return pl.pallas_call(
        functools.partial(_mlstm_scan_kernel, reverse=reverse),
        grid=(BATCH, nlc + 1),
        in_specs=[
            pl.BlockSpec((M_L, M_QK), lambda b, i: (blk(b, i), 0)),
            pl.BlockSpec((M_L, M_QK), lambda b, i: (blk(b, i), 1)),
            pl.BlockSpec((M_L, M_V), lambda b, i: (blk(b, i), 1)),
            pl.BlockSpec((M_L, M_GATES), lambda b, i: (blk(b, i), 0)),
        ],
        out_specs=pl.BlockSpec((M_L, M_V), lambda b, i: (blk(b, i), 0)),
        out_shape=jax.ShapeDtypeStruct((N_ALL, M_V), F32),
        scratch_shapes=[
            pltpu.VMEM((M_HEADS, M_DQK, M_DV), F32),
            pltpu.VMEM((M_HEADS, M_DQK), F32),
            pltpu.VMEM((M_HEADS, 128), F32),
        ],
        compiler_params=_params(("parallel", "arbitrary"), 48),
        name="mlstm_scan_bwd" if reverse else "mlstm_scan_fwd",
    )(proj, proj, proj, gates)


def _mlstm_finish_kernel(so_ref, sz_ref, hf_ref, hb_ref, hn_ref, m_ref):
    for h in range(M_HEADS):
        sl = slice(h * M_DV, (h + 1) * M_DV)
        y = so_ref[:, sl].astype(F32) * (hf_ref[:, sl] + hb_ref[:, sl])
        ms = jnp.mean(y * y, axis=-1, keepdims=True)
        y = y * lax.rsqrt(ms + EPS) * hn_ref[:, sl]
        m_ref[:, sl] = (y * sz_ref[:, sl].astype(F32)).astype(BF16)


def _mlstm_finish(proj, hf, hb, hn):
    tm = 256
    return pl.pallas_call(
        _mlstm_finish_kernel,
        grid=(N_ALL // tm,),
        in_specs=[
            pl.BlockSpec((tm, M_V), lambda r: (r, 2)),
            pl.BlockSpec((tm, M_V), lambda r: (r, 3)),
            pl.BlockSpec((tm, M_V), lambda r: (r, 0)),
            pl.BlockSpec((tm, M_V), lambda r: (r, 0)),
            pl.BlockSpec((1, M_V), lambda r: (0, 0)),
        ],
        out_specs=pl.BlockSpec((tm, M_V), lambda r: (r, 0)),
        out_shape=jax.ShapeDtypeStruct((N_ALL, M_V), BF16),
        compiler_params=_params(("parallel",), 48),
        name="mlstm_finish",
    )(proj, proj, hf, hb, hn)


A_QKW = A_Q + A_KVW


def _attn_qk_kernel(x_ref, g_ref, sh_ref, sc_ref, w_ref, gain_ref, cos_ref, sin_ref, q_ref, k_ref, h_scr):
    _store_modnorm(x_ref, g_ref, sc_ref, sh_ref, h_scr)
    mean_mat = jnp.full((A_HD, A_HD), 1.0 / A_HD, BF16)
    cos = cos_ref[...]
    sin = sin_ref[...]
    for c in range(A_QKW // TN):
        acc = jnp.dot(h_scr[...], w_ref[:, c * TN:(c + 1) * TN], preferred_element_type=F32)
        for hh in range(TN // A_HD):
            lo = c * TN + hh * A_HD
            a = acc[:, hh * A_HD:(hh + 1) * A_HD]
            ms = jnp.dot((a * a).astype(BF16), mean_mat, preferred_element_type=F32)
            a = a * lax.rsqrt(ms + EPS) * gain_ref[:, lo:lo + A_HD]
            a = (a * cos + pltpu.roll(a, A_HD // 2, 1) * sin).astype(BF16)
            if lo < A_Q:
                q_ref[:, lo:lo + A_HD] = a
            else:
                k_ref[:, lo - A_Q:lo - A_Q + A_HD] = a


def _attn_qk(xs, g, mods, w, gain, cos, sin):
    lat_tiles = N_LAT // TM
    rope_blk = lambda r: (jnp.where(r < lat_tiles, r % (SEQ // TM), SEQ // TM), 0)
    return pl.pallas_call(
        _attn_qk_kernel,
        grid=(N_ALL // TM,),
        in_specs=[pl.BlockSpec((TM, D), lambda r: (r, 0))] + _mod_specs(TM, lambda r: r) + [
            _resident((D, A_QKW), lambda r: (0, 0)),
            pl.BlockSpec((1, A_QKW), lambda r: (0, 0)),
            pl.BlockSpec((TM, A_HD), rope_blk),
            pl.BlockSpec((TM, A_HD), rope_blk),
        ],
        out_specs=[
            pl.BlockSpec((TM, A_Q), lambda r: (r, 0)),
            pl.BlockSpec((TM, A_KVW), lambda r: (r, 0)),
        ],
        out_shape=[
            jax.ShapeDtypeStruct((N_ALL, A_Q), BF16),
            jax.ShapeDtypeStruct((N_ALL, A_KVW), BF16),
        ],
        scratch_shapes=[pltpu.VMEM((TM, D), BF16)],
        compiler_params=_params(("parallel",), 48),
        name="attn_qk",
    )(xs, g, mods, mods, w, gain, cos, sin)


def _attn_vz_kernel(x_ref, g_ref, sh_ref, sc_ref, w_ref, v_ref, zg_ref, h_scr):
    _store_modnorm(x_ref, g_ref, sc_ref, sh_ref, h_scr)
    v_ref[...] = jnp.dot(h_scr[...], w_ref[:, :A_KVW], preferred_element_type=F32).astype(BF16)
    for c in range(A_Q // TN):
        lo = A_KVW + c * TN
        z = jnp.dot(h_scr[...], w_ref[:, lo:lo + TN], preferred_element_type=F32)
        zg_ref[:, c * TN:(c + 1) * TN] = _silu(z).astype(BF16)


def _attn_vz(xs, g, mods, w):
    return pl.pallas_call(
        _attn_vz_kernel,
        grid=(N_ALL // TM,),
        in_specs=[pl.BlockSpec((TM, D), lambda r: (r, 0))] + _mod_specs(TM, lambda r: r) + [
            _resident((D, A_KVW + A_Q), lambda r: (0, 0)),
        ],
        out_specs=[
            pl.BlockSpec((TM, A_KVW), lambda r: (r, 0)),
            pl.BlockSpec((TM, A_Q), lambda r: (r, 0)),
        ],
        out_shape=[
            jax.ShapeDtypeStruct((N_ALL, A_KVW), BF16),
            jax.ShapeDtypeStruct((N_ALL, A_Q), BF16),
        ],
        scratch_shapes=[pltpu.VMEM((TM, D), BF16)],
        compiler_params=_params(("parallel",), 48),
        name="attn_vz",
    )(xs, g, mods, mods, w)


A_LAT_TILES = SEQ // A_TQ


def _attn_kernel(q_ref, kl_ref, kc_ref, vl_ref, vc_ref, zg_ref, o_ref):
    t = pl.program_id(2)
    nt = (((1,), (1,)), ((), ()))

    def run(with_latent_keys):
        for hh in range(A_GW // A_HD):
            sl = slice(hh * A_HD, (hh + 1) * A_HD)
            q = q_ref[:, sl]
            s_c = lax.dot_general(q, kc_ref[...], nt, preferred_element_type=F32)
            m = jnp.max(s_c, axis=-1, keepdims=True)
            if with_latent_keys:
                s_l = lax.dot_general(q, kl_ref[...], nt, preferred_element_type=F32)
                m = jnp.maximum(m, jnp.max(s_l, axis=-1, keepdims=True))
            p_c = jnp.exp(s_c - m)
            den = jnp.sum(p_c, axis=-1, keepdims=True)
            acc = jnp.dot(p_c.astype(BF16), vc_ref[...], preferred_element_type=F32)
            if with_latent_keys:
                p_l = jnp.exp(s_l - m)
                den = den + jnp.sum(p_l, axis=-1, keepdims=True)
                acc = acc + jnp.dot(p_l.astype(BF16), vl_ref[...], preferred_element_type=F32)
            o_ref[:, sl] = (acc / den * zg_ref[:, sl].astype(F32)).astype(BF16)

    @pl.when(t < A_LAT_TILES)
    def _():
        run(True)

    @pl.when(t >= A_LAT_TILES)
    def _():
        run(False)


def _attn(q, k, v, zg):
    ctx_blk0 = N_LAT // A_TQ
    qrow = lambda b, g, t: jnp.where(t < A_LAT_TILES, b * A_LAT_TILES + t, ctx_blk0 + b)
    return pl.pallas_call(
        _attn_kernel,
        grid=(BATCH, A_KV, A_LAT_TILES + 1),
        in_specs=[
            pl.BlockSpec((A_TQ, A_GW), lambda b, g, t: (qrow(b, g, t), g)),
            pl.BlockSpec((SEQ, A_HD), lambda b, g, t: (b, g)),
            pl.BlockSpec((CTX, A_HD), lambda b, g, t: (N_LAT // CTX + b, g)),
            pl.BlockSpec((SEQ, A_HD), lambda b, g, t: (b, g)),
            pl.BlockSpec((CTX, A_HD), lambda b, g, t: (N_LAT // CTX + b, g)),
            pl.BlockSpec((A_TQ, A_GW), lambda b, g, t: (qrow(b, g, t), g)),
        ],
        out_specs=pl.BlockSpec((A_TQ, A_GW), lambda b, g, t: (qrow(b, g, t), g)),
        out_shape=jax.ShapeDtypeStruct((N_ALL, A_Q), BF16),
        compiler_params=_params(("parallel", "parallel", "arbitrary"), 48),
        name="attn",
    )(q, k, k, v, v, zg)


def _dft_tables(n, scale):
    idx = jnp.arange(n, dtype=jnp.int32)
    ang = ((idx[:, None] * idx[None, :]) % n).astype(F32) * (2.0 * math.pi / n)
    return (jnp.cos(ang) * scale).astype(BF16), (jnp.sin(ang) * scale).astype(BF16)


def _rope_tables():
    t = jnp.arange(SEQ)
    freqs = ROPE_THETA ** (-jnp.arange(0, A_HD // 2, 2, dtype=F32) / (A_HD // 2))
    ang = jnp.concatenate([(t // GRID_W).astype(F32)[:, None] * freqs,
                           (t % GRID_W).astype(F32)[:, None] * freqs], axis=-1)
    cos = jnp.concatenate([jnp.cos(ang), jnp.cos(ang)], axis=-1)
    sin = jnp.concatenate([-jnp.sin(ang), jnp.sin(ang)], axis=-1)
    pad = lambda a, fill: jnp.concatenate([a, jnp.full((TM, A_HD), fill, F32)], axis=0)
    return pad(cos, 1.0), pad(sin, 0.0)


def _split_heads_even_odd(w, heads):
    perm = jnp.concatenate([jnp.arange(0, A_HD, 2), jnp.arange(1, A_HD, 2)])
    lead = w.shape[:-1]
    return w.reshape(lead + (heads, A_HD))[..., perm].reshape(lead + (heads * A_HD,))


def kernel(x, c, ctx, c_ctx, ada_w, ada_b, norm_g, fnet_w_gate, fnet_w_out, mlstm_w_in, mlstm_b_gate,
           mlstm_hn, mlstm_w_out, attn_w_in, attn_qn, attn_kn, attn_w_out, final_g):
    xs = jnp.concatenate([x.reshape(N_LAT, D), ctx.reshape(N_CTX, D)], axis=0)
    cc = jnp.concatenate([c, c_ctx[None, :], jnp.zeros((MOD_ROWS - BATCH - 1, D), F32)], axis=0)
    mods_all = _modvec(cc, ada_w, ada_b).reshape(DEPTH, MOD_ROWS, 1, 3 * D)

    cc_c, sc_c = _dft_tables(F_GW, F_GW ** -0.5)
    cs_chan = jnp.concatenate([cc_c, sc_c], axis=1)
    ct_lat, st_lat = _dft_tables(SEQ, SEQ ** -0.5)
    ct_ctx, st_ctx = _dft_tables(CTX, CTX ** -0.5)

    for i in range(DEPTH):
        kind, j = i % 3, i // 3
        last = i == DEPTH - 1
        ctx_out = not last
        mods = mods_all[i]
        g = norm_g[i].reshape(1, D)
        n_rows = N_ALL if (ctx_out or kind != 0) else N_LAT
        if kind == 0:
            wg = fnet_w_gate[j].astype(BF16)
            a, b = _fnet_chan(xs, g, mods, cs_chan, n_rows)
            mbuf = _fnet_mix(xs, g, mods, ct_lat, st_lat, a, b, wg, SEQ, 0, None)
            if ctx_out:
                mbuf = _fnet_mix(xs, g, mods, ct_ctx, st_ctx, a, b, wg, CTX, N_LAT, CTX_MOD_ROW,
                                 prev=mbuf)
            w_out = fnet_w_out[j]
        elif kind == 1:
            w_in = mlstm_w_in[j]
            n_main = 2 * M_QK + 2 * M_V
            w_main = jnp.concatenate([w_in[:, :n_main], w_in[:, n_main + 4 * M_HEADS:]],
                                     axis=1).astype(BF16)
            w_gates = jnp.pad(w_in[:, n_main:n_main + 4 * M_HEADS],
                              ((0, 0), (0, M_GATES - 4 * M_HEADS))).astype(BF16)
            b_gates = jnp.pad(mlstm_b_gate[j], (0, M_GATES - 4 * M_HEADS)).reshape(1, M_GATES)
            proj, gates = _mlstm_proj(xs, g, mods, w_main, w_gates, b_gates)
            hf = _mlstm_scan(proj, gates, reverse=False)
            hb = _mlstm_scan(proj, gates, reverse=True)
            mbuf = _mlstm_finish(proj, hf, hb, mlstm_hn[j].reshape(1, M_V))
            w_out = mlstm_w_out[j]
        else:
            w_in = attn_w_in[j]
            w_qk = jnp.concatenate([_split_heads_even_odd(w_in[:, :A_Q], A_HEADS),
                                    _split_heads_even_odd(w_in[:, A_Q:A_QKW], A_KV)], axis=1)
            qn = _split_heads_even_odd(attn_qn[j], 1) * (A_HD ** -0.5)
            kn = _split_heads_even_odd(attn_kn[j], 1)
            gain = jnp.concatenate([jnp.tile(qn, A_HEADS), jnp.tile(kn, A_KV)]).reshape(1, A_QKW)
            cos, sin = _rope_tables()
            q, k = _attn_qk(xs, g, mods, w_qk.astype(BF16), gain, cos, sin)
            v, zg = _attn_vz(xs, g, mods, w_in[:, A_QKW:].astype(BF16))
            mbuf = _attn(q, k, v, zg)
            w_out = attn_w_out[j]
        if last:
            out = _outproj(mbuf, w_out.astype(BF16), xs, mods, n_rows, final_g.reshape(1, D))
        else:
            xs = _outproj(mbuf, w_out.astype(BF16), xs, mods, n_rows)

    return out.reshape(BATCH, SEQ, D)
```

```python
import functools
import math

import jax
import jax.numpy as jnp
from jax import lax
from jax.experimental import pallas as pl
from jax.experimental.pallas import tpu as pltpu

F32 = jnp.float32
BF16 = jnp.bfloat16

D = 2048
BATCH = 16
SEQ = 2048
CTX = 256
DEPTH = 4
EPS = 1e-6
N_LAT = BATCH * SEQ
N_CTX = BATCH * CTX
N_ALL = N_LAT + N_CTX
MOD_ROWS = 24
CTX_MOD_ROW = BATCH

F_GROUPS = 4
F_GW = D // F_GROUPS

M_HEADS = 8
M_DQK = 128
M_DV = 256
M_QK = M_HEADS * M_DQK
M_V = M_HEADS * M_DV
M_L = 256
M_GATES = 128
M_HALF = 2 * M_QK + M_V

A_HEADS = 16
A_KV = 4
A_HD = 128
A_Q = A_HEADS * A_HD
A_KVW = A_KV * A_HD
A_GW = A_Q // A_KV
A_TQ = 256
GRID_W = 64
ROPE_THETA = 10000.0

TM = 512
TN = 512
MIB = 1024 * 1024


def _params(sem, vmem_mib):
    return pltpu.CompilerParams(dimension_semantics=sem, vmem_limit_bytes=vmem_mib * MIB)


def _resident(shape, index_map):
    return pl.BlockSpec(shape, index_map, pipeline_mode=pl.Buffered(1))


def _sigmoid(x):
    return 1.0 / (1.0 + jnp.exp(-x))


def _silu(x):
    return x * _sigmoid(x)


def _log_sigmoid(x):
    return jnp.minimum(x, 0.0) - jnp.log1p(jnp.exp(-jnp.abs(x)))


def _modnorm(x, g, scale, shift):
    ms = jnp.mean(x * x, axis=-1, keepdims=True)
    y = x * lax.rsqrt(ms + EPS) * g
    return y * (1.0 + scale) + shift


def _store_modnorm(x_ref, g_ref, sc_ref, sh_ref, h_scr):
    rows = x_ref.shape[0]
    step = min(rows, 256)
    for r0 in range(0, rows, step):
        h_scr[r0:r0 + step, :] = _modnorm(x_ref[r0:r0 + step, :], g_ref[...], sc_ref[...],
                                          sh_ref[...]).astype(BF16)


def _mod_row(r, tm):
    return jnp.where(r < N_LAT // tm, r // (SEQ // tm), CTX_MOD_ROW)


def _modvec_kernel(c_ref, w_ref, b_ref, o_ref):
    s = _silu(c_ref[...])
    o_ref[...] = jnp.dot(s, w_ref[...], preferred_element_type=F32,
                         precision=lax.Precision.HIGHEST) + b_ref[...]


def _modvec(cc, ada_w, ada_b):
    tn = 1024
    return pl.pallas_call(
        _modvec_kernel,
        grid=(DEPTH, 3 * D // tn),
        in_specs=[
            pl.BlockSpec((MOD_ROWS, D), lambda i, j: (0, 0)),
            pl.BlockSpec((None, D, tn), lambda i, j: (i, 0, j)),
            pl.BlockSpec((None, 1, tn), lambda i, j: (i, 0, j)),
        ],
        out_specs=pl.BlockSpec((None, MOD_ROWS, tn), lambda i, j: (i, 0, j)),
        out_shape=jax.ShapeDtypeStruct((DEPTH, MOD_ROWS, 3 * D), F32),
        compiler_params=_params(("parallel", "parallel"), 40),
        name="modvec",
    )(cc, ada_w, ada_b.reshape(DEPTH, 1, 3 * D))


def _mod_specs(tm, row_of):
    return [
        pl.BlockSpec((1, D), lambda *ids: (0, 0)),
        pl.BlockSpec((None, 1, D), lambda *ids: (_mod_row(row_of(*ids), tm), 0, 0)),
        pl.BlockSpec((None, 1, D), lambda *ids: (_mod_row(row_of(*ids), tm), 0, 1)),
    ]


def _fnet_chan_kernel(x_ref, g_ref, sh_ref, sc_ref, cs_ref, a_ref, b_ref):
    h = _modnorm(x_ref[...], g_ref[...], sc_ref[...], sh_ref[...]).astype(BF16)
    for grp in range(F_GROUPS):
        sl = slice(grp * F_GW, (grp + 1) * F_GW)
        p = jnp.dot(h[:, sl], cs_ref[...], preferred_element_type=F32)
        a_ref[:, sl] = p[:, :F_GW].astype(BF16)
        b_ref[:, sl] = p[:, F_GW:].astype(BF16)


def _fnet_chan_ctx(xc, row_blk0, g, mods, cs_c):
    tm = 512
    out = jax.ShapeDtypeStruct((N_CTX, D), BF16)
    mod = lambda chunk: pl.BlockSpec((None, 1, D), lambda r: (CTX_MOD_ROW, 0, chunk))
    return pl.pallas_call(
        _fnet_chan_kernel,
        grid=(N_CTX // tm,),
        in_specs=[pl.BlockSpec((tm, D), lambda r: (row_blk0 + r, 0)),
                  pl.BlockSpec((1, D), lambda r: (0, 0)), mod(0), mod(1),
                  pl.BlockSpec((F_GW, 2 * F_GW), lambda r: (0, 0))],
        out_specs=[pl.BlockSpec((tm, D), lambda r: (r, 0))] * 2,
        out_shape=[out, out],
        compiler_params=_params(("parallel",), 48),
        name="fnet_chan_ctx",
    )(xc, g, mods, mods, cs_c)


F_R = 4
F_M = SEQ // F_R
F_TC = 256


def _fnet_chan_dif_kernel(x0_ref, x1_ref, x2_ref, x3_ref, g_ref, sh_ref, sc_ref, cs_ref, p_ref, q_ref):
    hs = [_modnorm(x[...], g_ref[...], sc_ref[...], sh_ref[...]).astype(BF16)
          for x in (x0_ref, x1_ref, x2_ref, x3_ref)]
    for grp in range(F_GROUPS):
        sl = slice(grp * F_GW, (grp + 1) * F_GW)
        ab = [jnp.dot(h[:, sl], cs_ref[...], preferred_element_type=F32) for h in hs]
        a = [t[:, :F_GW] for t in ab]
        b = [t[:, F_GW:] for t in ab]
        sa02, da02, sa13, da13 = a[0] + a[2], a[0] - a[2], a[1] + a[3], a[1] - a[3]
        sb02, db02, sb13, db13 = b[0] + b[2], b[0] - b[2], b[1] + b[3], b[1] - b[3]
        re = (sa02 + sa13, da02 - db13, sa02 - sa13, da02 + db13)
        im = (sb02 + sb13, db02 + da13, sb02 - sb13, db02 - da13)
        for r in range(F_R):
            p_ref[r, :, sl] = re[r].astype(BF16)
            q_ref[r, :, sl] = im[r].astype(BF16)


def _fnet_chan_dif(xl, g, mods, cs_c):
    nt = F_M // F_TC
    xspec = lambda q: pl.BlockSpec((F_TC, D), lambda b, i: (b * (SEQ // F_TC) + q * nt + i, 0))
    mod = lambda chunk: pl.BlockSpec((None, 1, D), lambda b, i: (b, 0, chunk))
    out = jax.ShapeDtypeStruct((BATCH, F_R, F_M, D), BF16)
    ospec = pl.BlockSpec((None, F_R, F_TC, D), lambda b, i: (b, 0, i, 0))
    return pl.pallas_call(
        _fnet_chan_dif_kernel,
        grid=(BATCH, nt),
        in_specs=[xspec(0), xspec(1), xspec(2), xspec(3),
                  pl.BlockSpec((1, D), lambda b, i: (0, 0)), mod(0), mod(1),
                  pl.BlockSpec((F_GW, 2 * F_GW), lambda b, i: (0, 0))],
        out_specs=[ospec, ospec],
        out_shape=[out, out],
        compiler_params=_params(("parallel", "parallel"), 56),
        name="fnet_chan_dif",
    )(xl, xl, xl, xl, g, mods, mods, cs_c)


def _fnet_mix_kernel(x_ref, g_ref, sh_ref, sc_ref, c_ref, s_ref, p_ref, q_ref, wg_ref, o_ref, h_scr):
    _store_modnorm(x_ref, g_ref, sc_ref, sh_ref, h_scr)
    for c in range(D // TN):
        sl = slice(c * TN, (c + 1) * TN)
        y = jnp.dot(c_ref[...], p_ref[:, sl], preferred_element_type=F32)
        y = y - jnp.dot(s_ref[...], q_ref[:, sl], preferred_element_type=F32)
        gate = jnp.dot(h_scr[...], wg_ref[:, sl], preferred_element_type=F32)
        o_ref[:, sl] = (y * _silu(gate)).astype(BF16)


def _fnet_mix_lat(xl4, g, mods, cr, sr, p, q, wg):
    mod = lambda chunk: pl.BlockSpec((None, 1, D), lambda b, r: (b, 0, chunk))
    tab = pl.BlockSpec((None, F_M, F_M), lambda b, r: (r, 0, 0))
    pq = pl.BlockSpec((None, None, F_M, D), lambda b, r: (b, r, 0, 0))
    out = pl.pallas_call(
        _fnet_mix_kernel,
        grid=(BATCH, F_R),
        in_specs=[pl.BlockSpec((F_M, D), lambda b, r: (b, r)),
                  pl.BlockSpec((1, D), lambda b, r: (0, 0)), mod(0), mod(1),
                  tab, tab, pq, pq, _resident((D, D), lambda b, r: (0, 0))],
        out_specs=pl.BlockSpec((F_M, D), lambda b, r: (b, r)),
        out_shape=jax.ShapeDtypeStruct((N_LAT // F_R, F_R * D), BF16),
        scratch_shapes=[pltpu.VMEM((F_M, D), BF16)],
        compiler_params=_params(("parallel", "parallel"), 48),
        name="fnet_mix_lat",
    )(xl4, g, mods, mods, cr, sr, p, q, wg)
    return out.reshape(N_LAT, D)


def _fnet_mix_ctx(xc, row_blk0, g, mods, ct, st, a, b, wg):
    mod = lambda chunk: pl.BlockSpec((None, 1, D), lambda bi: (CTX_MOD_ROW, 0, chunk))
    tab = pl.BlockSpec((CTX, CTX), lambda bi: (0, 0))
    ab = pl.BlockSpec((CTX, D), lambda bi: (bi, 0))
    return pl.pallas_call(
        _fnet_mix_kernel,
        grid=(BATCH,),
        in_specs=[pl.BlockSpec((CTX, D), lambda bi: (row_blk0 + bi, 0)),
                  pl.BlockSpec((1, D), lambda bi: (0, 0)), mod(0), mod(1),
                  tab, tab, ab, ab, _resident((D, D), lambda bi: (0, 0))],
        out_specs=pl.BlockSpec((CTX, D), lambda bi: (bi, 0)),
        out_shape=jax.ShapeDtypeStruct((N_CTX, D), BF16),
        scratch_shapes=[pltpu.VMEM((CTX, D), BF16)],
        compiler_params=_params(("parallel",), 48),
        name="fnet_mix_ctx",
    )(xc, g, mods, mods, ct, st, a, b, wg)


def _outproj_kernel(m_ref, w_ref, x_ref, gate_ref, *rest, final):
    o_ref = rest[-1]
    for c in range(D // TN):
        sl = slice(c * TN, (c + 1) * TN)
        y = jnp.dot(m_ref[...], w_ref[:, sl], preferred_element_type=F32)
        o_ref[:, sl] = x_ref[:, sl] + gate_ref[:, sl] * y
    if final:
        fg_ref = rest[0]
        x = o_ref[...]
        ms = jnp.mean(x * x, axis=-1, keepdims=True)
        o_ref[...] = x * lax.rsqrt(ms + EPS) * fg_ref[...]


def _outproj(mbuf, w, xs, mods, n_rows, final_g=None):
    final = final_g is not None
    in_specs = [
        pl.BlockSpec((TM, D), lambda r: (r, 0)),
        _resident((D, D), lambda r: (0, 0)),
        pl.BlockSpec((TM, D), lambda r: (r, 0)),
        pl.BlockSpec((None, 1, D), lambda r: (_mod_row(r, TM), 0, 2)),
    ]
    args = [mbuf, w, xs, mods]
    if final:
        in_specs.append(pl.BlockSpec((1, D), lambda r: (0, 0)))
        args.append(final_g)
    return pl.pallas_call(
        functools.partial(_outproj_kernel, final=final),
        grid=(n_rows // TM,),
        in_specs=in_specs,
        out_specs=pl.BlockSpec((TM, D), lambda r: (r, 0)),
        out_shape=jax.ShapeDtypeStruct((n_rows if final else N_ALL, D), F32),
        input_output_aliases={} if final else {2: 0},
        compiler_params=_params(("parallel",), 48),
        name="outproj_final" if final else "outproj",
    )(*args)


def _outproj_join_kernel(ml_ref, mc_ref, w_ref, xl_ref, xc_ref, gate_ref, o_ref):
    def update(m_ref, x_ref):
        for c in range(D // TN):
            sl = slice(c * TN, (c + 1) * TN)
            y = jnp.dot(m_ref[...], w_ref[:, sl], preferred_element_type=F32)
            o_ref[:, sl] = x_ref[:, sl] + gate_ref[:, sl] * y

    is_lat = pl.program_id(0) < N_LAT // TM

    @pl.when(is_lat)
    def _():
        update(ml_ref, xl_ref)

    @pl.when(jnp.logical_not(is_lat))
    def _():
        update(mc_ref, xc_ref)


def _outproj_join(m_lat, m_ctx, w, x_lat, x_ctx, mods):
    nl = N_LAT // TM
    lat = pl.BlockSpec((TM, D), lambda r: (jnp.minimum(r, nl - 1), 0))
    cxt = pl.BlockSpec((TM, D), lambda r: (jnp.maximum(r - nl, 0), 0))
    return pl.pallas_call(
        _outproj_join_kernel,
        grid=(N_ALL // TM,),
        in_specs=[lat, cxt, _resident((D, D), lambda r: (0, 0)), lat, cxt,
                  pl.BlockSpec((None, 1, D), lambda r: (_mod_row(r, TM), 0, 2))],
        out_specs=pl.BlockSpec((TM, D), lambda r: (r, 0)),
        out_shape=jax.ShapeDtypeStruct((N_ALL, D), F32),
        compiler_params=_params(("parallel",), 48),
        name="outproj_join",
    )(m_lat, m_ctx, w, x_lat, x_ctx, mods)


def _mlstm_proj_kernel(x_ref, g_ref, sh_ref, sc_ref, w_ref, wg_ref, bg_ref, out_ref, gt_ref, h_scr):
    half = pl.program_id(0)
    _store_modnorm(x_ref, g_ref, sc_ref, sh_ref, h_scr)

    def chunks(post):
        for c in range(M_HALF // TN):
            sl = slice(c * TN, (c + 1) * TN)
            acc = jnp.dot(h_scr[...], w_ref[:, sl], preferred_element_type=F32)
            out_ref[:, sl] = post(c, acc).astype(BF16)

    @pl.when(half == 0)
    def _():
        gt_ref[...] = jnp.dot(h_scr[...], wg_ref[...], preferred_element_type=F32) + bg_ref[...]
        chunks(lambda c, acc: acc * (M_DQK ** -0.5) if c < M_QK // TN else acc)

    @pl.when(half == 1)
    def _():
        chunks(lambda c, acc: _sigmoid(acc) if c < M_V // TN else _silu(acc))


def _mlstm_proj(xs, g, mods, w, wg, bg):
    nr = N_ALL // TM
    return pl.pallas_call(
        _mlstm_proj_kernel,
        grid=(2, nr),
        in_specs=[pl.BlockSpec((TM, D), lambda hf, r: (r, 0))] + _mod_specs(TM, lambda hf, r: r) + [
            _resident((D, M_HALF), lambda hf, r: (0, hf)),
            pl.BlockSpec((D, M_GATES), lambda hf, r: (0, 0)),
            pl.BlockSpec((1, M_GATES), lambda hf, r: (0, 0)),
        ],
        out_specs=[
            pl.BlockSpec((TM, M_HALF), lambda hf, r: (r, hf)),
            pl.BlockSpec((TM, M_GATES), lambda hf, r: (jnp.where(hf == 0, r, nr - 1), 0)),
        ],
        out_shape=[
            jax.ShapeDtypeStruct((N_ALL, 2 * M_HALF), BF16),
            jax.ShapeDtypeStruct((N_ALL, M_GATES), F32),
        ],
        scratch_shapes=[pltpu.VMEM((TM, D), BF16)],
        compiler_params=_params(("arbitrary", "arbitrary"), 56),
        name="mlstm_proj",
    )(xs, g, mods, mods, w, wg, bg)


def _split3(x):
    hi = x.astype(BF16)
    r1 = x - hi.astype(F32)
    mid = r1.astype(BF16)
    lo = (r1 - mid.astype(F32)).astype(BF16)
    return hi, mid, lo


def _mlstm_scan_kernel(q_ref, k_ref, v_ref, gt_ref, o_ref, c_scr, n_scr, m_scr, *, reverse):
    @pl.when(pl.program_id(1) == 0)
    def _():
        c_scr[...] = jnp.zeros_like(c_scr)
        n_scr[...] = jnp.zeros_like(n_scr)
        m_scr[...] = jnp.zeros_like(m_scr)

    L = M_L
    col0 = 2 * M_HEADS if reverse else 0
    gt = gt_ref[...]
    ls = _log_sigmoid(gt)
    row = lax.broadcasted_iota(jnp.int32, (L, L), 0)
    col = lax.broadcasted_iota(jnp.int32, (L, L), 1)
    order = (row <= col) if reverse else (row >= col)
    tri = jnp.where(order, 1.0, 0.0).astype(BF16)
    hi, mid, lo = _split3(ls)
    bc = (jnp.dot(tri, hi, preferred_element_type=F32)
          + jnp.dot(tri, mid, preferred_element_type=F32)
          + jnp.dot(tri, lo, preferred_element_type=F32))
    gt_t = gt.T
    bc_t = bc.T
    end = 0 if reverse else L - 1

    for h in range(M_HEADS):
        ci = col0 + h
        cf = col0 + M_HEADS + h
        ig_c = gt[:, ci:ci + 1]
        b_c = bc[:, cf:cf + 1]
        ig_r = gt_t[ci:ci + 1, :]
        b_r = bc_t[cf:cf + 1, :]
        b_end = bc[end:end + 1, cf:cf + 1]
        m_prev = m_scr[h:h + 1, 0:1]
        q = q_ref[:, h * M_DQK:(h + 1) * M_DQK]
        k = k_ref[:, h * M_DQK:(h + 1) * M_DQK]
        v = v_ref[:, h * M_DV:(h + 1) * M_DV]
        c_prev = c_scr[h]
        n_prev = n_scr[h:h + 1, :]

        dmat = jnp.where(order, b_c + (ig_r - b_r), -jnp.inf)
        inter = b_c + m_prev
        m_t = jnp.maximum(inter, jnp.max(dmat, axis=-1, keepdims=True))
        qk = lax.dot_general(q, k, (((1,), (1,)), ((), ())), preferred_element_type=F32)
        s = qk * jnp.exp(dmat - m_t)
        a = jnp.exp(inter - m_t)
        num = (a * jnp.dot(q, c_prev.astype(BF16), preferred_element_type=F32)
               + jnp.dot(s.astype(BF16), v, preferred_element_type=F32))
        den = (a * jnp.sum(q.astype(F32) * n_prev, axis=-1, keepdims=True)
               + jnp.sum(s, axis=-1, keepdims=True))
        o_ref[:, h * M_DV:(h + 1) * M_DV] = num / jnp.maximum(jnp.abs(den), jnp.exp(-m_t))

        gl = b_end - b_c + ig_c
        m_new = jnp.maximum(b_end + m_prev, jnp.max(gl, axis=0, keepdims=True))
        w = jnp.exp(gl - m_new)
        decay = jnp.exp(b_end + m_prev - m_new)
        kw = k.astype(F32) * w
        c_scr[h] = decay * c_prev + lax.dot_general(
            kw.astype(BF16), v, (((0,), (0,)), ((), ())), preferred_element_type=F32)
        n_scr[h:h + 1, :] = decay * n_prev + jnp.sum(kw, axis=0, keepdims=True)
        m_scr[h:h + 1, :] = jnp.broadcast_to(m_new, (1, 128))


def _mlstm_scan(proj, gates, reverse):
    nlc = SEQ // M_L
    ctx_blk0 = N_LAT // M_L

    def blk(b, i):
        lat = (nlc - i) if reverse else (i - 1)
        return jnp.where(i == 0, ctx_blk0 + b, b * nlc + lat)

    return pl.pallas_call(
        functools.partial(_mlstm_scan_kernel, reverse=reverse),
        grid=(BATCH, nlc + 1),
        in_specs=[
            pl.BlockSpec((M_L, M_QK), lambda b, i: (blk(b, i), 0)),
            pl.BlockSpec((M_L, M_QK), lambda b, i: (blk(b, i), 1)),
            pl.BlockSpec((M_L, M_V), lambda b, i: (blk(b, i), 1)),
            pl.BlockSpec((M_L, M_GATES), lambda b, i: (blk(b, i), 0)),
        ],
        out_specs=pl.BlockSpec((M_L, M_V), lambda b, i: (blk(b, i), 0)),
        out_shape=jax.ShapeDtypeStruct((N_ALL, M_V), F32),
        scratch_shapes=[
            pltpu.VMEM((M_HEADS, M_DQK, M_DV), F32),
            pltpu.VMEM((M_HEADS, M_DQK), F32),
            pltpu.VMEM((M_HEADS, 128), F32),
        ],
        compiler_params=_params(("parallel", "arbitrary"), 48),
        name="mlstm_scan_bwd" if reverse else "mlstm_scan_fwd",
    )(proj, proj, proj, gates)


def _mlstm_finish_kernel(so_ref, sz_ref, hf_ref, hb_ref, hn_ref, m_ref):
    for h in range(M_HEADS):
        sl = slice(h * M_DV, (h + 1) * M_DV)
        y = so_ref[:, sl].astype(F32) * (hf_ref[:, sl] + hb_ref[:, sl])
        ms = jnp.mean(y * y, axis=-1, keepdims=True)
        y = y * lax.rsqrt(ms + EPS) * hn_ref[:, sl]
        m_ref[:, sl] = (y * sz_ref[:, sl].astype(F32)).astype(BF16)


def _mlstm_finish(proj, hf, hb, hn):
    tm = 256
    return pl.pallas_call(
        _mlstm_finish_kernel,
        grid=(N_ALL // tm,),
        in_specs=[
            pl.BlockSpec((tm, M_V), lambda r: (r, 2)),
            pl.BlockSpec((tm, M_V), lambda r: (r, 3)),
            pl.BlockSpec((tm, M_V), lambda r: (r, 0)),
            pl.BlockSpec((tm, M_V), lambda r: (r, 0)),
            pl.BlockSpec((1, M_V), lambda r: (0, 0)),
        ],
        out_specs=pl.BlockSpec((tm, M_V), lambda r: (r, 0)),
        out_shape=jax.ShapeDtypeStruct((N_ALL, M_V), BF16),
        compiler_params=_params(("parallel",), 48),
        name="mlstm_finish",
    )(proj, proj, hf, hb, hn)


A_QKW = A_Q + A_KVW


def _attn_qk_kernel(x_ref, g_ref, sh_ref, sc_ref, w_ref, gain_ref, cos_ref, sin_ref, q_ref, k_ref, h_scr):
    _store_modnorm(x_ref, g_ref, sc_ref, sh_ref, h_scr)
    mean_mat = jnp.full((A_HD, A_HD), 1.0 / A_HD, BF16)
    cos = cos_ref[...]
    sin = sin_ref[...]
    for c in range(A_QKW // TN):
        acc = jnp.dot(h_scr[...], w_ref[:, c * TN:(c + 1) * TN], preferred_element_type=F32)
        for hh in range(TN // A_HD):
            lo = c * TN + hh * A_HD
            a = acc[:, hh * A_HD:(hh + 1) * A_HD]
            ms = jnp.dot((a * a).astype(BF16), mean_mat, preferred_element_type=F32)
            a = a * lax.rsqrt(ms + EPS) * gain_ref[:, lo:lo + A_HD]
            a = (a * cos + pltpu.roll(a, A_HD // 2, 1) * sin).astype(BF16)
            if lo < A_Q:
                q_ref[:, lo:lo + A_HD] = a
            else:
                k_ref[:, lo - A_Q:lo - A_Q + A_HD] = a


def _attn_qk(xs, g, mods, w, gain, cos, sin):
    lat_tiles = N_LAT // TM
    rope_blk = lambda r: (jnp.where(r < lat_tiles, r % (SEQ // TM), SEQ // TM), 0)
    return pl.pallas_call(
        _attn_qk_kernel,
        grid=(N_ALL // TM,),
        in_specs=[pl.BlockSpec((TM, D), lambda r: (r, 0))] + _mod_specs(TM, lambda r: r) + [
            _resident((D, A_QKW), lambda r: (0, 0)),
            pl.BlockSpec((1, A_QKW), lambda r: (0, 0)),
            pl.BlockSpec((TM, A_HD), rope_blk),
            pl.BlockSpec((TM, A_HD), rope_blk),
        ],
        out_specs=[
            pl.BlockSpec((TM, A_Q), lambda r: (r, 0)),
            pl.BlockSpec((TM, A_KVW), lambda r: (r, 0)),
        ],
        out_shape=[
            jax.ShapeDtypeStruct((N_ALL, A_Q), BF16),
            jax.ShapeDtypeStruct((N_ALL, A_KVW), BF16),
        ],
        scratch_shapes=[pltpu.VMEM((TM, D), BF16)],
        compiler_params=_params(("parallel",), 48),
        name="attn_qk",
    )(xs, g, mods, mods, w, gain, cos, sin)


def _attn_vz_kernel(x_ref, g_ref, sh_ref, sc_ref, w_ref, v_ref, zg_ref, h_scr):
    _store_modnorm(x_ref, g_ref, sc_ref, sh_ref, h_scr)
    v_ref[...] = jnp.dot(h_scr[...], w_ref[:, :A_KVW], preferred_element_type=F32).astype(BF16)
    for c in range(A_Q // TN):
        lo = A_KVW + c * TN
        z = jnp.dot(h_scr[...], w_ref[:, lo:lo + TN], preferred_element_type=F32)
        zg_ref[:, c * TN:(c + 1) * TN] = _silu(z).astype(BF16)


def _attn_vz(xs, g, mods, w):
    return pl.pallas_call(
        _attn_vz_kernel,
        grid=(N_ALL // TM,),
        in_specs=[pl.BlockSpec((TM, D), lambda r: (r, 0))] + _mod_specs(TM, lambda r: r) + [
            _resident((D, A_KVW + A_Q), lambda r: (0, 0)),
        ],
        out_specs=[
            pl.BlockSpec((TM, A_KVW), lambda r: (r, 0)),
            pl.BlockSpec((TM, A_Q), lambda r: (r, 0)),
        ],
        out_shape=[
            jax.ShapeDtypeStruct((N_ALL, A_KVW), BF16),
            jax.ShapeDtypeStruct((N_ALL, A_Q), BF16),
        ],
        scratch_shapes=[pltpu.VMEM((TM, D), BF16)],
        compiler_params=_params(("parallel",), 48),
        name="attn_vz",
    )(xs, g, mods, mods, w)


A_LAT_TILES = SEQ // A_TQ


def _attn_kernel(q_ref, kl_ref, kc_ref, vl_ref, vc_ref, zg_ref, o_ref):
    t = pl.program_id(2)
    nt = (((1,), (1,)), ((), ()))

    def run(with_latent_keys):
        for hh in range(A_GW // A_HD):
            sl = slice(hh * A_HD, (hh + 1) * A_HD)
            q = q_ref[:, sl]
            s_c = lax.dot_general(q, kc_ref[...], nt, preferred_element_type=F32)
            m = jnp.max(s_c, axis=-1, keepdims=True)
            if with_latent_keys:
                s_l = lax.dot_general(q, kl_ref[...], nt, preferred_element_type=F32)
                m = jnp.maximum(m, jnp.max(s_l, axis=-1, keepdims=True))
            p_c = jnp.exp(s_c - m)
            den = jnp.sum(p_c, axis=-1, keepdims=True)
            acc = jnp.dot(p_c.astype(BF16), vc_ref[...], preferred_element_type=F32)
            if with_latent_keys:
                p_l = jnp.exp(s_l - m)
                den = den + jnp.sum(p_l, axis=-1, keepdims=True)
                acc = acc + jnp.dot(p_l.astype(BF16), vl_ref[...], preferred_element_type=F32)
            o_ref[:, sl] = (acc / den * zg_ref[:, sl].astype(F32)).astype(BF16)

    @pl.when(t < A_LAT_TILES)
    def _():
        run(True)

    @pl.when(t >= A_LAT_TILES)
    def _():
        run(False)


def _attn(q, k, v, zg):
    ctx_blk0 = N_LAT // A_TQ
    qrow = lambda b, g, t: jnp.where(t < A_LAT_TILES, b * A_LAT_TILES + t, ctx_blk0 + b)
    return pl.pallas_call(
        _attn_kernel,
        grid=(BATCH, A_KV, A_LAT_TILES + 1),
        in_specs=[
            pl.BlockSpec((A_TQ, A_GW), lambda b, g, t: (qrow(b, g, t), g)),
            pl.BlockSpec((SEQ, A_HD), lambda b, g, t: (b, g)),
            pl.BlockSpec((CTX, A_HD), lambda b, g, t: (N_LAT // CTX + b, g)),
            pl.BlockSpec((SEQ, A_HD), lambda b, g, t: (b, g)),
            pl.BlockSpec((CTX, A_HD), lambda b, g, t: (N_LAT // CTX + b, g)),
            pl.BlockSpec((A_TQ, A_GW), lambda b, g, t: (qrow(b, g, t), g)),
        ],
        out_specs=pl.BlockSpec((A_TQ, A_GW), lambda b, g, t: (qrow(b, g, t), g)),
        out_shape=jax.ShapeDtypeStruct((N_ALL, A_Q), BF16),
        compiler_params=_params(("parallel", "parallel", "arbitrary"), 48),
        name="attn",
    )(q, k, k, v, v, zg)


def _dft_tables(n, scale):
    idx = jnp.arange(n, dtype=jnp.int32)
    ang = ((idx[:, None] * idx[None, :]) % n).astype(F32) * (2.0 * math.pi / n)
    return (jnp.cos(ang) * scale).astype(BF16), (jnp.sin(ang) * scale).astype(BF16)


def _dif_tables():
    jt = jnp.arange(F_M, dtype=jnp.int32)
    r = jnp.arange(F_R, dtype=jnp.int32)
    k = ((F_R * jt[None, :, None] + r[:, None, None]) * jt[None, None, :]) % SEQ
    ang = k.astype(F32) * (2.0 * math.pi / SEQ)
    return (jnp.cos(ang) * SEQ ** -0.5).astype(BF16), (jnp.sin(ang) * SEQ ** -0.5).astype(BF16)


def _rope_tables():
    t = jnp.arange(SEQ)
    freqs = ROPE_THETA ** (-jnp.arange(0, A_HD // 2, 2, dtype=F32) / (A_HD // 2))
    ang = jnp.concatenate([(t // GRID_W).astype(F32)[:, None] * freqs,
                           (t % GRID_W).astype(F32)[:, None] * freqs], axis=-1)
    cos = jnp.concatenate([jnp.cos(ang), jnp.cos(ang)], axis=-1)
    sin = jnp.concatenate([-jnp.sin(ang), jnp.sin(ang)], axis=-1)
    pad = lambda a, fill: jnp.concatenate([a, jnp.full((TM, A_HD), fill, F32)], axis=0)
    return pad(cos, 1.0), pad(sin, 0.0)


def _split_heads_even_odd(w, heads):
    perm = jnp.concatenate([jnp.arange(0, A_HD, 2), jnp.arange(1, A_HD, 2)])
    lead = w.shape[:-1]
    return w.reshape(lead + (heads, A_HD))[..., perm].reshape(lead + (heads * A_HD,))


def kernel(x, c, ctx, c_ctx, ada_w, ada_b, norm_g, fnet_w_gate, fnet_w_out, mlstm_w_in, mlstm_b_gate,
           mlstm_hn, mlstm_w_out, attn_w_in, attn_qn, attn_kn, attn_w_out, final_g):
    cc = jnp.concatenate([c, c_ctx[None, :], jnp.zeros((MOD_ROWS - BATCH - 1, D), F32)], axis=0)
    mods_all = _modvec(cc, ada_w, ada_b).reshape(DEPTH, MOD_ROWS, 1, 3 * D)

    cc_c, sc_c = _dft_tables(F_GW, F_GW ** -0.5)
    cs_chan = jnp.concatenate([cc_c, sc_c], axis=1)
    cr_lat, sr_lat = _dif_tables()
    ct_ctx, st_ctx = _dft_tables(CTX, CTX ** -0.5)

    xs = None
    for i in range(DEPTH):
        kind, j = i % 3, i // 3
        last = i == DEPTH - 1
        mods = mods_all[i]
        g = norm_g[i].reshape(1, D)
        n_rows = N_LAT if last else N_ALL
        if kind == 0:
            assert i == 0 or last
            wg = fnet_w_gate[j].astype(BF16)
            w_out = fnet_w_out[j].astype(BF16)
            x_lat = x.reshape(N_LAT, D) if i == 0 else xs
            p, q = _fnet_chan_dif(x_lat, g, mods, cs_chan)
            m_lat = _fnet_mix_lat(x_lat.reshape(x_lat.shape[0] // F_R, F_R * D), g, mods,
                                  cr_lat, sr_lat, p, q, wg)
            if i == 0:
                x_ctx = ctx.reshape(N_CTX, D)
                a, b = _fnet_chan_ctx(x_ctx, 0, g, mods, cs_chan)
                m_ctx = _fnet_mix_ctx(x_ctx, 0, g, mods, ct_ctx, st_ctx, a, b, wg)
                xs = _outproj_join(m_lat, m_ctx, w_out, x_lat, x_ctx, mods)
            else:
                out = _outproj(m_lat, w_out, xs, mods, n_rows, final_g.reshape(1, D))
            continue
        if kind == 1:
            w_in = mlstm_w_in[j]
            n_main = 2 * M_QK + 2 * M_V
            w_main = jnp.concatenate([w_in[:, :n_main], w_in[:, n_main + 4 * M_HEADS:]],
                                     axis=1).astype(BF16)
            w_gates = jnp.pad(w_in[:, n_main:n_main + 4 * M_HEADS],
                              ((0, 0), (0, M_GATES - 4 * M_HEADS))).astype(BF16)
            b_gates = jnp.pad(mlstm_b_gate[j], (0, M_GATES - 4 * M_HEADS)).reshape(1, M_GATES)
            proj, gates = _mlstm_proj(xs, g, mods, w_main, w_gates, b_gates)
            hf = _mlstm_scan(proj, gates, reverse=False)
            hb = _mlstm_scan(proj, gates, reverse=True)
            mbuf = _mlstm_finish(proj, hf, hb, mlstm_hn[j].reshape(1, M_V))
            w_out = mlstm_w_out[j]
        else:
            w_in = attn_w_in[j]
            w_qk = jnp.concatenate([_split_heads_even_odd(w_in[:, :A_Q], A_HEADS),
                                    _split_heads_even_odd(w_in[:, A_Q:A_QKW], A_KV)], axis=1)
            qn = _split_heads_even_odd(attn_qn[j], 1) * (A_HD ** -0.5)
            kn = _split_heads_even_odd(attn_kn[j], 1)
            gain = jnp.concatenate([jnp.tile(qn, A_HEADS), jnp.tile(kn, A_KV)]).reshape(1, A_QKW)
            cos, sin = _rope_tables()
            q, k = _attn_qk(xs, g, mods, w_qk.astype(BF16), gain, cos, sin)
            v, zg = _attn_vz(xs, g, mods, w_in[:, A_QKW:].astype(BF16))
            mbuf = _attn(q, k, v, zg)
            w_out = attn_w_out[j]
        if last:
            out = _outproj(mbuf, w_out.astype(BF16), xs, mods, n_rows, final_g.reshape(1, D))
        else:
            xs = _outproj(mbuf, w_out.astype(BF16), xs, mods, n_rows)

    return out.reshape(BATCH, SEQ, D)
```

```python
import functools
import math

import jax
import jax.numpy as jnp
from jax import lax
from jax.experimental import pallas as pl
from jax.experimental.pallas import tpu as pltpu

F32 = jnp.float32
BF16 = jnp.bfloat16

D = 2048
BATCH = 16
SEQ = 2048
CTX = 256
DEPTH = 4
EPS = 1e-6
N_LAT = BATCH * SEQ
N_CTX = BATCH * CTX
N_ALL = N_LAT + N_CTX
MOD_ROWS = 24
CTX_MOD_ROW = BATCH

F_GROUPS = 4
F_GW = D // F_GROUPS

M_HEADS = 8
M_DQK = 128
M_DV = 256
M_QK = M_HEADS * M_DQK
M_V = M_HEADS * M_DV
M_L = 256
M_GATES = 128
M_HALF = 2 * M_QK + M_V

A_HEADS = 16
A_KV = 4
A_HD = 128
A_Q = A_HEADS * A_HD
A_KVW = A_KV * A_HD
A_GW = A_Q // A_KV
A_TQ = 256
GRID_W = 64
ROPE_THETA = 10000.0

TM = 512
TN = 512
MIB = 1024 * 1024


def _params(sem, vmem_mib):
    return pltpu.CompilerParams(dimension_semantics=sem, vmem_limit_bytes=vmem_mib * MIB)


def _resident(shape, index_map):
    return pl.BlockSpec(shape, index_map, pipeline_mode=pl.Buffered(1))


def _sigmoid(x):
    return 1.0 / (1.0 + jnp.exp(-x))


def _silu(x):
    return x * _sigmoid(x)


def _log_sigmoid(x):
    return jnp.minimum(x, 0.0) - jnp.log1p(jnp.exp(-jnp.abs(x)))


def _modnorm(x, g, scale, shift):
    ms = jnp.mean(x * x, axis=-1, keepdims=True)
    y = x * lax.rsqrt(ms + EPS) * g
    return y * (1.0 + scale) + shift


def _store_modnorm(x_ref, g_ref, sc_ref, sh_ref, h_scr):
    rows = x_ref.shape[0]
    step = min(rows, 256)
    for r0 in range(0, rows, step):
        h_scr[r0:r0 + step, :] = _modnorm(x_ref[r0:r0 + step, :], g_ref[...], sc_ref[...],
                                          sh_ref[...]).astype(BF16)


def _mod_row(r, tm):
    return jnp.where(r < N_LAT // tm, r // (SEQ // tm), CTX_MOD_ROW)


def _modvec_kernel(c_ref, w_ref, b_ref, o_ref):
    s = _silu(c_ref[...])
    o_ref[...] = jnp.dot(s, w_ref[...], preferred_element_type=F32,
                         precision=lax.Precision.HIGHEST) + b_ref[...]


def _modvec(cc, ada_w, ada_b):
    tn = 1024
    return pl.pallas_call(
        _modvec_kernel,
        grid=(DEPTH, 3 * D // tn),
        in_specs=[
            pl.BlockSpec((MOD_ROWS, D), lambda i, j: (0, 0)),
            pl.BlockSpec((None, D, tn), lambda i, j: (i, 0, j)),
            pl.BlockSpec((None, 1, tn), lambda i, j: (i, 0, j)),
        ],
        out_specs=pl.BlockSpec((None, MOD_ROWS, tn), lambda i, j: (i, 0, j)),
        out_shape=jax.ShapeDtypeStruct((DEPTH, MOD_ROWS, 3 * D), F32),
        compiler_params=_params(("parallel", "parallel"), 40),
        name="modvec",
    )(cc, ada_w, ada_b.reshape(DEPTH, 1, 3 * D))


def _mod_specs(tm, row_of):
    return [
        pl.BlockSpec((1, D), lambda *ids: (0, 0)),
        pl.BlockSpec((None, 1, D), lambda *ids: (_mod_row(row_of(*ids), tm), 0, 0)),
        pl.BlockSpec((None, 1, D), lambda *ids: (_mod_row(row_of(*ids), tm), 0, 1)),
    ]


def _fnet_chan_kernel(x_ref, g_ref, sh_ref, sc_ref, cs_ref, a_ref, b_ref):
    h = _modnorm(x_ref[...], g_ref[...], sc_ref[...], sh_ref[...]).astype(BF16)
    for grp in range(F_GROUPS):
        sl = slice(grp * F_GW, (grp + 1) * F_GW)
        p = jnp.dot(h[:, sl], cs_ref[...], preferred_element_type=F32)
        a_ref[:, sl] = p[:, :F_GW].astype(BF16)
        b_ref[:, sl] = p[:, F_GW:].astype(BF16)


def _fnet_chan_ctx(xc, row_blk0, g, mods, cs_c):
    tm = 512
    out = jax.ShapeDtypeStruct((N_CTX, D), BF16)
    mod = lambda chunk: pl.BlockSpec((None, 1, D), lambda r: (CTX_MOD_ROW, 0, chunk))
    return pl.pallas_call(
        _fnet_chan_kernel,
        grid=(N_CTX // tm,),
        in_specs=[pl.BlockSpec((tm, D), lambda r: (row_blk0 + r, 0)),
                  pl.BlockSpec((1, D), lambda r: (0, 0)), mod(0), mod(1),
                  pl.BlockSpec((F_GW, 2 * F_GW), lambda r: (0, 0))],
        out_specs=[pl.BlockSpec((tm, D), lambda r: (r, 0))] * 2,
        out_shape=[out, out],
        compiler_params=_params(("parallel",), 48),
        name="fnet_chan_ctx",
    )(xc, g, mods, mods, cs_c)


F_R = 4
F_M = SEQ // F_R
F_TC = 256
F_PR = F_TC // F_R


def _fnet_chan_dif_kernel(x0_ref, x1_ref, x2_ref, x3_ref, g_ref, sh_ref, sc_ref, cs_ref, perm_ref,
                          p_ref, q_ref, hp_ref):
    hs = [_modnorm(x[...], g_ref[...], sc_ref[...], sh_ref[...]).astype(BF16)
          for x in (x0_ref, x1_ref, x2_ref, x3_ref)]
    for qi, h in enumerate(hs):
        hp = jnp.dot(perm_ref[...], h, preferred_element_type=F32).astype(BF16)
        for r in range(F_R):
            hp_ref[qi, r] = hp[r * F_PR:(r + 1) * F_PR, :]
    for grp in range(F_GROUPS):
        sl = slice(grp * F_GW, (grp + 1) * F_GW)
        ab = [jnp.dot(h[:, sl], cs_ref[...], preferred_element_type=F32) for h in hs]
        a = [t[:, :F_GW] for t in ab]
        b = [t[:, F_GW:] for t in ab]
        sa02, da02, sa13, da13 = a[0] + a[2], a[0] - a[2], a[1] + a[3], a[1] - a[3]
        sb02, db02, sb13, db13 = b[0] + b[2], b[0] - b[2], b[1] + b[3], b[1] - b[3]
        re = (sa02 + sa13, da02 - db13, sa02 - sa13, da02 + db13)
        im = (sb02 + sb13, db02 + da13, sb02 - sb13, db02 - da13)
        for r in range(F_R):
            p_ref[r, :, sl] = re[r].astype(BF16)
            q_ref[r, :, sl] = im[r].astype(BF16)


def _fnet_chan_dif(xl, g, mods, cs_c, perm):
    nt = F_M // F_TC
    xspec = lambda q: pl.BlockSpec((F_TC, D), lambda b, i: (b * (SEQ // F_TC) + q * nt + i, 0))
    mod = lambda chunk: pl.BlockSpec((None, 1, D), lambda b, i: (b, 0, chunk))
    out = jax.ShapeDtypeStruct((BATCH, F_R, F_M, D), BF16)
    ospec = pl.BlockSpec((None, F_R, F_TC, D), lambda b, i: (b, 0, i, 0))
    p, q, hp = pl.pallas_call(
        _fnet_chan_dif_kernel,
        grid=(BATCH, nt),
        in_specs=[xspec(0), xspec(1), xspec(2), xspec(3),
                  pl.BlockSpec((1, D), lambda b, i: (0, 0)), mod(0), mod(1),
                  pl.BlockSpec((F_GW, 2 * F_GW), lambda b, i: (0, 0)),
                  pl.BlockSpec((F_TC, F_TC), lambda b, i: (0, 0))],
        out_specs=[ospec, ospec,
                   pl.BlockSpec((None, F_R, None, F_R, F_PR, D), lambda b, i: (b, 0, i, 0, 0, 0))],
        out_shape=[out, out, jax.ShapeDtypeStruct((BATCH, F_R, nt, F_R, F_PR, D), BF16)],
        compiler_params=_params(("parallel", "parallel"), 56),
        name="fnet_chan_dif",
    )(xl, xl, xl, xl, g, mods, mods, cs_c, perm)
    return p, q, hp.reshape(BATCH, SEQ // F_TC, F_R, F_PR, D)


def _fnet_mix_kernel(x_ref, g_ref, sh_ref, sc_ref, c_ref, s_ref, p_ref, q_ref, wg_ref, o_ref, h_scr):
    _store_modnorm(x_ref, g_ref, sc_ref, sh_ref, h_scr)
    for c in range(D // TN):
        sl = slice(c * TN, (c + 1) * TN)
        y = jnp.dot(c_ref[...], p_ref[:, sl], preferred_element_type=F32)
        y = y - jnp.dot(s_ref[...], q_ref[:, sl], preferred_element_type=F32)
        gate = jnp.dot(h_scr[...], wg_ref[:, sl], preferred_element_type=F32)
        o_ref[:, sl] = (y * _silu(gate)).astype(BF16)


def _fnet_mix_lat_kernel(h_ref, c_ref, s_ref, p_ref, q_ref, wg_ref, o_ref):
    nt = SEQ // F_TC
    h = h_ref[...].reshape(F_M, D)
    for c in range(D // TN):
        sl = slice(c * TN, (c + 1) * TN)
        y = jnp.dot(c_ref[...], p_ref[:, sl], preferred_element_type=F32)
        y = y - jnp.dot(s_ref[...], q_ref[:, sl], preferred_element_type=F32)
        gate = jnp.dot(h, wg_ref[:, sl], preferred_element_type=F32)
        val = (y * _silu(gate)).astype(BF16)
        for t in range(nt):
            o_ref[t, :, sl] = val[t * F_PR:(t + 1) * F_PR, :]


def _fnet_mix_lat(hp, cr, sr, p, q, wg):
    nt = SEQ // F_TC
    tab = pl.BlockSpec((None, F_M, F_M), lambda b, r: (r, 0, 0))
    pq = pl.BlockSpec((None, None, F_M, D), lambda b, r: (b, r, 0, 0))
    tiles = pl.BlockSpec((None, nt, None, F_PR, D), lambda b, r: (b, 0, r, 0, 0))
    out = pl.pallas_call(
        _fnet_mix_lat_kernel,
        grid=(BATCH, F_R),
        in_specs=[tiles, tab, tab, pq, pq, _resident((D, D), lambda b, r: (0, 0))],
        out_specs=tiles,
        out_shape=jax.ShapeDtypeStruct((BATCH, nt, F_R, F_PR, D), BF16),
        compiler_params=_params(("parallel", "parallel"), 48),
        name="fnet_mix_lat",
    )(hp, cr, sr, p, q, wg)
    return out.reshape(N_LAT, D)


def _fnet_mix_ctx(xc, row_blk0, g, mods, ct, st, a, b, wg):
    mod = lambda chunk: pl.BlockSpec((None, 1, D), lambda bi: (CTX_MOD_ROW, 0, chunk))
    tab = pl.BlockSpec((CTX, CTX), lambda bi: (0, 0))
    ab = pl.BlockSpec((CTX, D), lambda bi: (bi, 0))
    return pl.pallas_call(
        _fnet_mix_kernel,
        grid=(BATCH,),
        in_specs=[pl.BlockSpec((CTX, D), lambda bi: (row_blk0 + bi, 0)),
                  pl.BlockSpec((1, D), lambda bi: (0, 0)), mod(0), mod(1),
                  tab, tab, ab, ab, _resident((D, D), lambda bi: (0, 0))],
        out_specs=pl.BlockSpec((CTX, D), lambda bi: (bi, 0)),
        out_shape=jax.ShapeDtypeStruct((N_CTX, D), BF16),
        scratch_shapes=[pltpu.VMEM((CTX, D), BF16)],
        compiler_params=_params(("parallel",), 48),
        name="fnet_mix_ctx",
    )(xc, g, mods, mods, ct, st, a, b, wg)


def _residual_update(m_ref, w_ref, x_ref, gate_ref, o_ref, unperm_ref, m_scr):
    if unperm_ref is not None:
        for t in range(TM // F_TC):
            rows = slice(t * F_TC, (t + 1) * F_TC)
            m_scr[rows, :] = jnp.dot(unperm_ref[...], m_ref[rows, :],
                                     preferred_element_type=F32).astype(BF16)
        m_ref = m_scr
    for c in range(D // TN):
        sl = slice(c * TN, (c + 1) * TN)
        y = jnp.dot(m_ref[...], w_ref[:, sl], preferred_element_type=F32)
        o_ref[:, sl] = x_ref[:, sl] + gate_ref[:, sl] * y


def _outproj_kernel(m_ref, w_ref, x_ref, gate_ref, *rest, final):
    if final:
        unperm_ref, fg_ref, o_ref, m_scr = rest
        _residual_update(m_ref, w_ref, x_ref, gate_ref, o_ref, unperm_ref, m_scr)
        x = o_ref[...]
        ms = jnp.mean(x * x, axis=-1, keepdims=True)
        o_ref[...] = x * lax.rsqrt(ms + EPS) * fg_ref[...]
    else:
        _residual_update(m_ref, w_ref, x_ref, gate_ref, rest[0], None, None)


def _outproj(mbuf, w, xs, mods, n_rows, final=None):
    in_specs = [
        pl.BlockSpec((TM, D), lambda r: (r, 0)),
        _resident((D, D), lambda r: (0, 0)),
        pl.BlockSpec((TM, D), lambda r: (r, 0)),
        pl.BlockSpec((None, 1, D), lambda r: (_mod_row(r, TM), 0, 2)),
    ]
    args = [mbuf, w, xs, mods]
    scratch = []
    if final is not None:
        in_specs += [pl.BlockSpec((F_TC, F_TC), lambda r: (0, 0)), pl.BlockSpec((1, D), lambda r: (0, 0))]
        args += list(final)
        scratch = [pltpu.VMEM((TM, D), BF16)]
    return pl.pallas_call(
        functools.partial(_outproj_kernel, final=final is not None),
        grid=(n_rows // TM,),
        in_specs=in_specs,
        out_specs=pl.BlockSpec((TM, D), lambda r: (r, 0)),
        out_shape=jax.ShapeDtypeStruct((N_ALL if final is None else n_rows, D), F32),
        scratch_shapes=scratch,
        input_output_aliases={2: 0} if final is None else {},
        compiler_params=_params(("parallel",), 48),
        name="outproj" if final is None else "outproj_final",
    )(*args)


def _outproj_join_kernel(ml_ref, mc_ref, w_ref, xl_ref, xc_ref, gate_ref, unperm_ref, o_ref, m_scr):
    is_lat = pl.program_id(0) < N_LAT // TM

    @pl.when(is_lat)
    def _():
        _residual_update(ml_ref, w_ref, xl_ref, gate_ref, o_ref, unperm_ref, m_scr)

    @pl.when(jnp.logical_not(is_lat))
    def _():
        _residual_update(mc_ref, w_ref, xc_ref, gate_ref, o_ref, None, None)


def _outproj_join(m_lat, m_ctx, w, x_lat, x_ctx, mods, unperm):
    nl = N_LAT // TM
    lat = pl.BlockSpec((TM, D), lambda r: (jnp.minimum(r, nl - 1), 0))
    cxt = pl.BlockSpec((TM, D), lambda r: (jnp.maximum(r - nl, 0), 0))
    return pl.pallas_call(
        _outproj_join_kernel,
        grid=(N_ALL // TM,),
        in_specs=[lat, cxt, _resident((D, D), lambda r: (0, 0)), lat, cxt,
                  pl.BlockSpec((None, 1, D), lambda r: (_mod_row(r, TM), 0, 2)),
                  pl.BlockSpec((F_TC, F_TC), lambda r: (0, 0))],
        out_specs=pl.BlockSpec((TM, D), lambda r: (r, 0)),
        out_shape=jax.ShapeDtypeStruct((N_ALL, D), F32),
        scratch_shapes=[pltpu.VMEM((TM, D), BF16)],
        compiler_params=_params(("parallel",), 48),
        name="outproj_join",
    )(m_lat, m_ctx, w, x_lat, x_ctx, mods, unperm)


def _mlstm_proj_kernel(x_ref, g_ref, sh_ref, sc_ref, w_ref, wg_ref, bg_ref, out_ref, gt_ref, h_scr):
    half = pl.program_id(0)
    _store_modnorm(x_ref, g_ref, sc_ref, sh_ref, h_scr)

    def chunks(post):
        for c in range(M_HALF // TN):
            sl = slice(c * TN, (c + 1) * TN)
            acc = jnp.dot(h_scr[...], w_ref[:, sl], preferred_element_type=F32)
            out_ref[:, sl] = post(c, acc).astype(BF16)

    @pl.when(half == 0)
    def _():
        gt_ref[...] = jnp.dot(h_scr[...], wg_ref[...], preferred_element_type=F32) + bg_ref[...]
        chunks(lambda c, acc: acc * (M_DQK ** -0.5) if c < M_QK // TN else acc)

    @pl.when(half == 1)
    def _():
        chunks(lambda c, acc: _sigmoid(acc) if c < M_V // TN else _silu(acc))


def _mlstm_proj(xs, g, mods, w, wg, bg):
    nr = N_ALL // TM
    return pl.pallas_call(
        _mlstm_proj_kernel,
        grid=(2, nr),
        in_specs=[pl.BlockSpec((TM, D), lambda hf, r: (r, 0))] + _mod_specs(TM, lambda hf, r: r) + [
            _resident((D, M_HALF), lambda hf, r: (0, hf)),
            pl.BlockSpec((D, M_GATES), lambda hf, r: (0, 0)),
            pl.BlockSpec((1, M_GATES), lambda hf, r: (0, 0)),
        ],
        out_specs=[
            pl.BlockSpec((TM, M_HALF), lambda hf, r: (r, hf)),
            pl.BlockSpec((TM, M_GATES), lambda hf, r: (jnp.where(hf == 0, r, nr - 1), 0)),
        ],
        out_shape=[
            jax.ShapeDtypeStruct((N_ALL, 2 * M_HALF), BF16),
            jax.ShapeDtypeStruct((N_ALL, M_GATES), F32),
        ],
        scratch_shapes=[pltpu.VMEM((TM, D), BF16)],
        compiler_params=_params(("arbitrary", "arbitrary"), 56),
        name="mlstm_proj",
    )(xs, g, mods, mods, w, wg, bg)


def _split3(x):
    hi = x.astype(BF16)
    r1 = x - hi.astype(F32)
    mid = r1.astype(BF16)
    lo = (r1 - mid.astype(F32)).astype(BF16)
    return hi, mid, lo


def _mlstm_scan_kernel(q_ref, k_ref, v_ref, gt_ref, o_ref, c_scr, n_scr, m_scr, *, reverse):
    @pl.when(pl.program_id(1) == 0)
    def _():
        c_scr[...] = jnp.zeros_like(c_scr)
        n_scr[...] = jnp.zeros_like(n_scr)
        m_scr[...] = jnp.zeros_like(m_scr)

    L = M_L
    col0 = 2 * M_HEADS if reverse else 0
    gt = gt_ref[...]
    ls = _log_sigmoid(gt)
    row = lax.broadcasted_iota(jnp.int32, (L, L), 0)
    col = lax.broadcasted_iota(jnp.int32, (L, L), 1)
    order = (row <= col) if reverse else (row >= col)
    tri = jnp.where(order, 1.0, 0.0).astype(BF16)
    hi, mid, lo = _split3(ls)
    bc = (jnp.dot(tri, hi, preferred_element_type=F32)
          + jnp.dot(tri, mid, preferred_element_type=F32)
          + jnp.dot(tri, lo, preferred_element_type=F32))
    gt_t = gt.T
    bc_t = bc.T
    end = 0 if reverse else L - 1

    for h in range(M_HEADS):
        ci = col0 + h
        cf = col0 + M_HEADS + h
        ig_c = gt[:, ci:ci + 1]
        b_c = bc[:, cf:cf + 1]
        ig_r = gt_t[ci:ci + 1, :]
        b_r = bc_t[cf:cf + 1, :]
        b_end = bc[end:end + 1, cf:cf + 1]
        m_prev = m_scr[h:h + 1, 0:1]
        q = q_ref[:, h * M_DQK:(h + 1) * M_DQK]
        k = k_ref[:, h * M_DQK:(h + 1) * M_DQK]
        v = v_ref[:, h * M_DV:(h + 1) * M_DV]
        c_prev = c_scr[h]
        n_prev = n_scr[h:h + 1, :]

        dmat = jnp.where(order, b_c + (ig_r - b_r), -jnp.inf)
        inter = b_c + m_prev
        m_t = jnp.maximum(inter, jnp.max(dmat, axis=-1, keepdims=True))
        qk = lax.dot_general(q, k, (((1,), (1,)), ((), ())), preferred_element_type=F32)
        s = qk * jnp.exp(dmat - m_t)
        a = jnp.exp(inter - m_t)
        num = (a * jnp.dot(q, c_prev.astype(BF16), preferred_element_type=F32)
               + jnp.dot(s.astype(BF16), v, preferred_element_type=F32))
        den = (a * jnp.sum(q.astype(F32) * n_prev, axis=-1, keepdims=True)
               + jnp.sum(s, axis=-1, keepdims=True))
        o_ref[:, h * M_DV:(h + 1) * M_DV] = num / jnp.maximum(jnp.abs(den), jnp.exp(-m_t))

        gl = b_end - b_c + ig_c
        m_new = jnp.maximum(b_end + m_prev, jnp.max(gl, axis=0, keepdims=True))
        w = jnp.exp(gl - m_new)
        decay = jnp.exp(b_end + m_prev - m_new)
        kw = k.astype(F32) * w
        c_scr[h] = decay * c_prev + lax.dot_general(
            kw.astype(BF16), v, (((0,), (0,)), ((), ())), preferred_element_type=F32)
        n_scr[h:h + 1, :] = decay * n_prev + jnp.sum(kw, axis=0, keepdims=True)
        m_scr[h:h + 1, :] = jnp.broadcast_to(m_new, (1, 128))


def _mlstm_scan(proj, gates, reverse):
    nlc = SEQ // M_L
    ctx_blk0 = N_LAT // M_L

    def blk(b, i):
        lat = (nlc - i) if reverse else (i - 1)
        return jnp.where(i == 0, ctx_blk0 + b, b * nlc + lat)

    return pl.pallas_call(
        functools.partial(_mlstm_scan_kernel, reverse=reverse),
        grid=(BATCH, nlc + 1),
        in_specs=[
            pl.BlockSpec((M_L, M_QK), lambda b, i: (blk(b, i), 0)),
            pl.BlockSpec((M_L, M_QK), lambda b, i: (blk(b, i), 1)),
            pl.BlockSpec((M_L, M_V), lambda b, i: (blk(b, i), 1)),
            pl.BlockSpec((M_L, M_GATES), lambda b, i: (blk(b, i), 0)),
        ],
        out_specs=pl.BlockSpec((M_L, M_V), lambda b, i: (blk(b, i), 0)),
        out_shape=jax.ShapeDtypeStruct((N_ALL, M_V), F32),
        scratch_shapes=[
            pltpu.VMEM((M_HEADS, M_DQK, M_DV), F32),
            pltpu.VMEM((M_HEADS, M_DQK), F32),
            pltpu.VMEM((M_HEADS, 128), F32),
        ],
        compiler_params=_params(("parallel", "arbitrary"), 48),
        name="mlstm_scan_bwd" if reverse else "mlstm_scan_fwd",
    )(proj, proj, proj, gates)


def _mlstm_finish_kernel(so_ref, sz_ref, hf_ref, hb_ref, hn_ref, m_ref):
    for h in range(M_HEADS):
        sl = slice(h * M_DV, (h + 1) * M_DV)
        y = so_ref[:, sl].astype(F32) * (hf_ref[:, sl] + hb_ref[:, sl])
        ms = jnp.mean(y * y, axis=-1, keepdims=True)
        y = y * lax.rsqrt(ms + EPS) * hn_ref[:, sl]
        m_ref[:, sl] = (y * sz_ref[:, sl].astype(F32)).astype(BF16)


def _mlstm_finish(proj, hf, hb, hn):
    tm = 256
    return pl.pallas_call(
        _mlstm_finish_kernel,
        grid=(N_ALL // tm,),
        in_specs=[
            pl.BlockSpec((tm, M_V), lambda r: (r, 2)),
            pl.BlockSpec((tm, M_V), lambda r: (r, 3)),
            pl.BlockSpec((tm, M_V), lambda r: (r, 0)),
            pl.BlockSpec((tm, M_V), lambda r: (r, 0)),
            pl.BlockSpec((1, M_V), lambda r: (0, 0)),
        ],
        out_specs=pl.BlockSpec((tm, M_V), lambda r: (r, 0)),
        out_shape=jax.ShapeDtypeStruct((N_ALL, M_V), BF16),
        compiler_params=_params(("parallel",), 48),
        name="mlstm_finish",
    )(proj, proj, hf, hb, hn)


A_QKW = A_Q + A_KVW


def _attn_qk_kernel(x_ref, g_ref, sh_ref, sc_ref, w_ref, gain_ref, cos_ref, sin_ref, q_ref, k_ref, h_scr):
    _store_modnorm(x_ref, g_ref, sc_ref, sh_ref, h_scr)
    mean_mat = jnp.full((A_HD, A_HD), 1.0 / A_HD, BF16)
    cos = cos_ref[...]
    sin = sin_ref[...]
    for c in range(A_QKW // TN):
        acc = jnp.dot(h_scr[...], w_ref[:, c * TN:(c + 1) * TN], preferred_element_type=F32)
        for hh in range(TN // A_HD):
            lo = c * TN + hh * A_HD
            a = acc[:, hh * A_HD:(hh + 1) * A_HD]
            ms = jnp.dot((a * a).astype(BF16), mean_mat, preferred_element_type=F32)
            a = a * lax.rsqrt(ms + EPS) * gain_ref[:, lo:lo + A_HD]
            a = (a * cos + pltpu.roll(a, A_HD // 2, 1) * sin).astype(BF16)
            if lo < A_Q:
                q_ref[:, lo:lo + A_HD] = a
            else:
                k_ref[:, lo - A_Q:lo - A_Q + A_HD] = a


def _attn_qk(xs, g, mods, w, gain, cos, sin):
    lat_tiles = N_LAT // TM
    rope_blk = lambda r: (jnp.where(r < lat_tiles, r % (SEQ // TM), SEQ // TM), 0)
    return pl.pallas_call(
        _attn_qk_kernel,
        grid=(N_ALL // TM,),
        in_specs=[pl.BlockSpec((TM, D), lambda r: (r, 0))] + _mod_specs(TM, lambda r: r) + [
            _resident((D, A_QKW), lambda r: (0, 0)),
            pl.BlockSpec((1, A_QKW), lambda r: (0, 0)),
            pl.BlockSpec((TM, A_HD), rope_blk),
            pl.BlockSpec((TM, A_HD), rope_blk),
        ],
        out_specs=[
            pl.BlockSpec((TM, A_Q), lambda r: (r, 0)),
            pl.BlockSpec((TM, A_KVW), lambda r: (r, 0)),
        ],
        out_shape=[
            jax.ShapeDtypeStruct((N_ALL, A_Q), BF16),
            jax.ShapeDtypeStruct((N_ALL, A_KVW), BF16),
        ],
        scratch_shapes=[pltpu.VMEM((TM, D), BF16)],
        compiler_params=_params(("parallel",), 48),
        name="attn_qk",
    )(xs, g, mods, mods, w, gain, cos, sin)


def _attn_vz_kernel(x_ref, g_ref, sh_ref, sc_ref, w_ref, v_ref, zg_ref, h_scr):
    _store_modnorm(x_ref, g_ref, sc_ref, sh_ref, h_scr)
    v_ref[...] = jnp.dot(h_scr[...], w_ref[:, :A_KVW], preferred_element_type=F32).astype(BF16)
    for c in range(A_Q // TN):
        lo = A_KVW + c * TN
        z = jnp.dot(h_scr[...], w_ref[:, lo:lo + TN], preferred_element_type=F32)
        zg_ref[:, c * TN:(c + 1) * TN] = _silu(z).astype(BF16)


def _attn_vz(xs, g, mods, w):
    return pl.pallas_call(
        _attn_vz_kernel,
        grid=(N_ALL // TM,),
        in_specs=[pl.BlockSpec((TM, D), lambda r: (r, 0))] + _mod_specs(TM, lambda r: r) + [
            _resident((D, A_KVW + A_Q), lambda r: (0, 0)),
        ],
        out_specs=[
            pl.BlockSpec((TM, A_KVW), lambda r: (r, 0)),
            pl.BlockSpec((TM, A_Q), lambda r: (r, 0)),
        ],
        out_shape=[
            jax.ShapeDtypeStruct((N_ALL, A_KVW), BF16),
            jax.ShapeDtypeStruct((N_ALL, A_Q), BF16),
        ],
        scratch_shapes=[pltpu.VMEM((TM, D), BF16)],
        compiler_params=_params(("parallel",), 48),
        name="attn_vz",
    )(xs, g, mods, mods, w)


A_LAT_TILES = SEQ // A_TQ


def _attn_kernel(q_ref, kl_ref, kc_ref, vl_ref, vc_ref, zg_ref, o_ref):
    t = pl.program_id(2)
    nt = (((1,), (1,)), ((), ()))

    def run(with_latent_keys):
        for hh in range(A_GW // A_HD):
            sl = slice(hh * A_HD, (hh + 1) * A_HD)
            q = q_ref[:, sl]
            s_c = lax.dot_general(q, kc_ref[...], nt, preferred_element_type=F32)
            m = jnp.max(s_c, axis=-1, keepdims=True)
            if with_latent_keys:
                s_l = lax.dot_general(q, kl_ref[...], nt, preferred_element_type=F32)
                m = jnp.maximum(m, jnp.max(s_l, axis=-1, keepdims=True))
            p_c = jnp.exp(s_c - m)
            den = jnp.sum(p_c, axis=-1, keepdims=True)
            acc = jnp.dot(p_c.astype(BF16), vc_ref[...], preferred_element_type=F32)
            if with_latent_keys:
                p_l = jnp.exp(s_l - m)
                den = den + jnp.sum(p_l, axis=-1, keepdims=True)
                acc = acc + jnp.dot(p_l.astype(BF16), vl_ref[...], preferred_element_type=F32)
            o_ref[:, sl] = (acc / den * zg_ref[:, sl].astype(F32)).astype(BF16)

    @pl.when(t < A_LAT_TILES)
    def _():
        run(True)

    @pl.when(t >= A_LAT_TILES)
    def _():
        run(False)


def _attn(q, k, v, zg):
    ctx_blk0 = N_LAT // A_TQ
    qrow = lambda b, g, t: jnp.where(t < A_LAT_TILES, b * A_LAT_TILES + t, ctx_blk0 + b)
    return pl.pallas_call(
        _attn_kernel,
        grid=(BATCH, A_KV, A_LAT_TILES + 1),
        in_specs=[
            pl.BlockSpec((A_TQ, A_GW), lambda b, g, t: (qrow(b, g, t), g)),
            pl.BlockSpec((SEQ, A_HD), lambda b, g, t: (b, g)),
            pl.BlockSpec((CTX, A_HD), lambda b, g, t: (N_LAT // CTX + b, g)),
            pl.BlockSpec((SEQ, A_HD), lambda b, g, t: (b, g)),
            pl.BlockSpec((CTX, A_HD), lambda b, g, t: (N_LAT // CTX + b, g)),
            pl.BlockSpec((A_TQ, A_GW), lambda b, g, t: (qrow(b, g, t), g)),
        ],
        out_specs=pl.BlockSpec((A_TQ, A_GW), lambda b, g, t: (qrow(b, g, t), g)),
        out_shape=jax.ShapeDtypeStruct((N_ALL, A_Q), BF16),
        compiler_params=_params(("parallel", "parallel", "arbitrary"), 48),
        name="attn",
    )(q, k, k, v, v, zg)


def _dft_tables(n, scale):
    idx = jnp.arange(n, dtype=jnp.int32)
    ang = ((idx[:, None] * idx[None, :]) % n).astype(F32) * (2.0 * math.pi / n)
    return (jnp.cos(ang) * scale).astype(BF16), (jnp.sin(ang) * scale).astype(BF16)


def _dif_tables():
    jt = jnp.arange(F_M, dtype=jnp.int32)
    r = jnp.arange(F_R, dtype=jnp.int32)
    k = ((F_R * jt[None, :, None] + r[:, None, None]) * jt[None, None, :]) % SEQ
    ang = k.astype(F32) * (2.0 * math.pi / SEQ)
    return (jnp.cos(ang) * SEQ ** -0.5).astype(BF16), (jnp.sin(ang) * SEQ ** -0.5).astype(BF16)


def _perm_tables():
    n = jnp.arange(F_TC)
    src = F_R * (n % F_PR) + n // F_PR
    perm = (src[:, None] == n[None, :]).astype(BF16)
    return perm, perm.T


def _rope_tables():
    t = jnp.arange(SEQ)
    freqs = ROPE_THETA ** (-jnp.arange(0, A_HD // 2, 2, dtype=F32) / (A_HD // 2))
    ang = jnp.concatenate([(t // GRID_W).astype(F32)[:, None] * freqs,
                           (t % GRID_W).astype(F32)[:, None] * freqs], axis=-1)
    cos = jnp.concatenate([jnp.cos(ang), jnp.cos(ang)], axis=-1)
    sin = jnp.concatenate([-jnp.sin(ang), jnp.sin(ang)], axis=-1)
    pad = lambda a, fill: jnp.concatenate([a, jnp.full((TM, A_HD), fill, F32)], axis=0)
    return pad(cos, 1.0), pad(sin, 0.0)


def _split_heads_even_odd(w, heads):
    perm = jnp.concatenate([jnp.arange(0, A_HD, 2), jnp.arange(1, A_HD, 2)])
    lead = w.shape[:-1]
    return w.reshape(lead + (heads, A_HD))[..., perm].reshape(lead + (heads * A_HD,))


def kernel(x, c, ctx, c_ctx, ada_w, ada_b, norm_g, fnet_w_gate, fnet_w_out, mlstm_w_in, mlstm_b_gate,
           mlstm_hn, mlstm_w_out, attn_w_in, attn_qn, attn_kn, attn_w_out, final_g):
    cc = jnp.concatenate([c, c_ctx[None, :], jnp.zeros((MOD_ROWS - BATCH - 1, D), F32)], axis=0)
    mods_all = _modvec(cc, ada_w, ada_b).reshape(DEPTH, MOD_ROWS, 1, 3 * D)

    cc_c, sc_c = _dft_tables(F_GW, F_GW ** -0.5)
    cs_chan = jnp.concatenate([cc_c, sc_c], axis=1)
    cr_lat, sr_lat = _dif_tables()
    ct_ctx, st_ctx = _dft_tables(CTX, CTX ** -0.5)
    perm, unperm = _perm_tables()

    xs = None
    for i in range(DEPTH):
        kind, j = i % 3, i // 3
        last = i == DEPTH - 1
        mods = mods_all[i]
        g = norm_g[i].reshape(1, D)
        n_rows = N_LAT if last else N_ALL
        if kind == 0:
            assert i == 0 or last
            wg = fnet_w_gate[j].astype(BF16)
            w_out = fnet_w_out[j].astype(BF16)
            x_lat = x.reshape(N_LAT, D) if i == 0 else xs
            p, q, hp = _fnet_chan_dif(x_lat, g, mods, cs_chan, perm)
            m_lat = _fnet_mix_lat(hp, cr_lat, sr_lat, p, q, wg)
            if i == 0:
                x_ctx = ctx.reshape(N_CTX, D)
                a, b = _fnet_chan_ctx(x_ctx, 0, g, mods, cs_chan)
                m_ctx = _fnet_mix_ctx(x_ctx, 0, g, mods, ct_ctx, st_ctx, a, b, wg)
                xs = _outproj_join(m_lat, m_ctx, w_out, x_lat, x_ctx, mods, unperm)
            else:
                out = _outproj(m_lat, w_out, xs, mods, n_rows, (unperm, final_g.reshape(1, D)))
            continue
        if kind == 1:
            w_in = mlstm_w_in[j]
            n_main = 2 * M_QK + 2 * M_V
            w_main = jnp.concatenate([w_in[:, :n_main], w_in[:, n_main + 4 * M_HEADS:]],
                                     axis=1).astype(BF16)
            w_gates = jnp.pad(w_in[:, n_main:n_main + 4 * M_HEADS],
                              ((0, 0), (0, M_GATES - 4 * M_HEADS))).astype(BF16)
            b_gates = jnp.pad(mlstm_b_gate[j], (0, M_GATES - 4 * M_HEADS)).reshape(1, M_GATES)
            proj, gates = _mlstm_proj(xs, g, mods, w_main, w_gates, b_gates)
            hf = _mlstm_scan(proj, gates, reverse=False)
            hb = _mlstm_scan(proj, gates, reverse=True)
            mbuf = _mlstm_finish(proj, hf, hb, mlstm_hn[j].reshape(1, M_V))
            w_out = mlstm_w_out[j]
        else:
            w_in = attn_w_in[j]
            w_qk = jnp.concatenate([_split_heads_even_odd(w_in[:, :A_Q], A_HEADS),
                                    _split_heads_even_odd(w_in[:, A_Q:A_QKW], A_KV)], axis=1)
            qn = _split_heads_even_odd(attn_qn[j], 1) * (A_HD ** -0.5)
            kn = _split_heads_even_odd(attn_kn[j], 1)
            gain = jnp.concatenate([jnp.tile(qn, A_HEADS), jnp.tile(kn, A_KV)]).reshape(1, A_QKW)
            cos, sin = _rope_tables()
            q, k = _attn_qk(xs, g, mods, w_qk.astype(BF16), gain, cos, sin)
            v, zg = _attn_vz(xs, g, mods, w_in[:, A_QKW:].astype(BF16))
            mbuf = _attn(q, k, v, zg)
            w_out = attn_w_out[j]
        assert not last
        xs = _outproj(mbuf, w_out.astype(BF16), xs, mods, n_rows)

    return out.reshape(BATCH, SEQ, D)
```

```python
import functools
import math

import jax
import jax.numpy as jnp
from jax import lax
from jax.experimental import pallas as pl
from jax.experimental.pallas import tpu as pltpu

F32 = jnp.float32
BF16 = jnp.bfloat16

D = 2048
BATCH = 16
SEQ = 2048
CTX = 256
DEPTH = 4
EPS = 1e-6
N_LAT = BATCH * SEQ
N_CTX = BATCH * CTX
N_ALL = N_LAT + N_CTX
MOD_ROWS = 24
CTX_MOD_ROW = BATCH

F_GROUPS = 4
F_GW = D // F_GROUPS

M_HEADS = 8
M_DQK = 128
M_DV = 256
M_QK = M_HEADS * M_DQK
M_V = M_HEADS * M_DV
M_L = 256
M_GATES = 256
M_HALF = 2 * M_QK + M_V

A_HEADS = 16
A_KV = 4
A_HD = 128
A_Q = A_HEADS * A_HD
A_KVW = A_KV * A_HD
A_GW = A_Q // A_KV
A_TQ = 256
GRID_W = 64
ROPE_THETA = 10000.0

TM = 512
TN = 512
MIB = 1024 * 1024


def _params(sem, vmem_mib):
    return pltpu.CompilerParams(dimension_semantics=sem, vmem_limit_bytes=vmem_mib * MIB)


def _resident(shape, index_map):
    return pl.BlockSpec(shape, index_map, pipeline_mode=pl.Buffered(1))


def _sigmoid(x):
    return 1.0 / (1.0 + jnp.exp(-x))


def _silu(x):
    return x * _sigmoid(x)


def _log_sigmoid(x):
    return jnp.minimum(x, 0.0) - jnp.log1p(jnp.exp(-jnp.abs(x)))


def _modnorm(x, g, scale, shift):
    ms = jnp.mean(x * x, axis=-1, keepdims=True)
    y = x * lax.rsqrt(ms + EPS) * g
    return y * (1.0 + scale) + shift


def _store_modnorm(x_ref, g_ref, sc_ref, sh_ref, h_scr):
    rows = x_ref.shape[0]
    step = min(rows, 256)
    for r0 in range(0, rows, step):
        h_scr[r0:r0 + step, :] = _modnorm(x_ref[r0:r0 + step, :], g_ref[...], sc_ref[...],
                                          sh_ref[...]).astype(BF16)


def _mod_row(r, tm):
    return jnp.where(r < N_LAT // tm, r // (SEQ // tm), CTX_MOD_ROW)


def _modvec_kernel(c_ref, w_ref, b_ref, o_ref):
    s = _silu(c_ref[...])
    o_ref[...] = jnp.dot(s, w_ref[...], preferred_element_type=F32,
                         precision=lax.Precision.HIGHEST) + b_ref[...]


def _modvec(cc, ada_w, ada_b):
    tn = 1024
    return pl.pallas_call(
        _modvec_kernel,
        grid=(DEPTH, 3 * D // tn),
        in_specs=[
            pl.BlockSpec((MOD_ROWS, D), lambda i, j: (0, 0)),
            pl.BlockSpec((None, D, tn), lambda i, j: (i, 0, j)),
            pl.BlockSpec((None, 1, tn), lambda i, j: (i, 0, j)),
        ],
        out_specs=pl.BlockSpec((None, MOD_ROWS, tn), lambda i, j: (i, 0, j)),
        out_shape=jax.ShapeDtypeStruct((DEPTH, MOD_ROWS, 3 * D), F32),
        compiler_params=_params(("parallel", "parallel"), 40),
        name="modvec",
    )(cc, ada_w, ada_b.reshape(DEPTH, 1, 3 * D))


def _mod_specs(tm, row_of):
    return [
        pl.BlockSpec((1, D), lambda *ids: (0, 0)),
        pl.BlockSpec((None, 1, D), lambda *ids: (_mod_row(row_of(*ids), tm), 0, 0)),
        pl.BlockSpec((None, 1, D), lambda *ids: (_mod_row(row_of(*ids), tm), 0, 1)),
    ]


def _fnet_chan_kernel(x_ref, g_ref, sh_ref, sc_ref, cs_ref, a_ref, b_ref):
    h = _modnorm(x_ref[...], g_ref[...], sc_ref[...], sh_ref[...]).astype(BF16)
    for grp in range(F_GROUPS):
        sl = slice(grp * F_GW, (grp + 1) * F_GW)
        p = jnp.dot(h[:, sl], cs_ref[...], preferred_element_type=F32)
        a_ref[:, sl] = p[:, :F_GW].astype(BF16)
        b_ref[:, sl] = p[:, F_GW:].astype(BF16)


def _fnet_chan_ctx(xc, row_blk0, g, mods, cs_c):
    tm = 512
    out = jax.ShapeDtypeStruct((N_CTX, D), BF16)
    mod = lambda chunk: pl.BlockSpec((None, 1, D), lambda r: (CTX_MOD_ROW, 0, chunk))
    return pl.pallas_call(
        _fnet_chan_kernel,
        grid=(N_CTX // tm,),
        in_specs=[pl.BlockSpec((tm, D), lambda r: (row_blk0 + r, 0)),
                  pl.BlockSpec((1, D), lambda r: (0, 0)), mod(0), mod(1),
                  pl.BlockSpec((F_GW, 2 * F_GW), lambda r: (0, 0))],
        out_specs=[pl.BlockSpec((tm, D), lambda r: (r, 0))] * 2,
        out_shape=[out, out],
        compiler_params=_params(("parallel",), 48),
        name="fnet_chan_ctx",
    )(xc, g, mods, mods, cs_c)


F_R = 4
F_M = SEQ // F_R
F_TC = 256
F_PR = F_TC // F_R


def _fnet_chan_dif_kernel(x0_ref, x1_ref, x2_ref, x3_ref, g_ref, sh_ref, sc_ref, cs_ref, perm_ref,
                          p_ref, q_ref, hp_ref):
    hs = [_modnorm(x[...], g_ref[...], sc_ref[...], sh_ref[...]).astype(BF16)
          for x in (x0_ref, x1_ref, x2_ref, x3_ref)]
    for qi, h in enumerate(hs):
        hp = jnp.dot(perm_ref[...], h, preferred_element_type=F32).astype(BF16)
        for r in range(F_R):
            hp_ref[qi, r] = hp[r * F_PR:(r + 1) * F_PR, :]
    for grp in range(F_GROUPS):
        sl = slice(grp * F_GW, (grp + 1) * F_GW)
        ab = [jnp.dot(h[:, sl], cs_ref[...], preferred_element_type=F32) for h in hs]
        a = [t[:, :F_GW] for t in ab]
        b = [t[:, F_GW:] for t in ab]
        sa02, da02, sa13, da13 = a[0] + a[2], a[0] - a[2], a[1] + a[3], a[1] - a[3]
        sb02, db02, sb13, db13 = b[0] + b[2], b[0] - b[2], b[1] + b[3], b[1] - b[3]
        re = (sa02 + sa13, da02 - db13, sa02 - sa13, da02 + db13)
        im = (sb02 + sb13, db02 + da13, sb02 - sb13, db02 - da13)
        for r in range(F_R):
            p_ref[r, :, sl] = re[r].astype(BF16)
            q_ref[r, :, sl] = im[r].astype(BF16)


def _fnet_chan_dif(xl, g, mods, cs_c, perm):
    nt = F_M // F_TC
    xspec = lambda q: pl.BlockSpec((F_TC, D), lambda b, i: (b * (SEQ // F_TC) + q * nt + i, 0))
    mod = lambda chunk: pl.BlockSpec((None, 1, D), lambda b, i: (b, 0, chunk))
    out = jax.ShapeDtypeStruct((BATCH, F_R, F_M, D), BF16)
    ospec = pl.BlockSpec((None, F_R, F_TC, D), lambda b, i: (b, 0, i, 0))
    p, q, hp = pl.pallas_call(
        _fnet_chan_dif_kernel,
        grid=(BATCH, nt),
        in_specs=[xspec(0), xspec(1), xspec(2), xspec(3),
                  pl.BlockSpec((1, D), lambda b, i: (0, 0)), mod(0), mod(1),
                  pl.BlockSpec((F_GW, 2 * F_GW), lambda b, i: (0, 0)),
                  pl.BlockSpec((F_TC, F_TC), lambda b, i: (0, 0))],
        out_specs=[ospec, ospec,
                   pl.BlockSpec((None, F_R, None, F_R, F_PR, D), lambda b, i: (b, 0, i, 0, 0, 0))],
        out_shape=[out, out, jax.ShapeDtypeStruct((BATCH, F_R, nt, F_R, F_PR, D), BF16)],
        compiler_params=_params(("parallel", "parallel"), 56),
        name="fnet_chan_dif",
    )(xl, xl, xl, xl, g, mods, mods, cs_c, perm)
    return p, q, hp.reshape(BATCH, SEQ // F_TC, F_R, F_PR, D)


def _fnet_mix_kernel(x_ref, g_ref, sh_ref, sc_ref, c_ref, s_ref, p_ref, q_ref, wg_ref, o_ref, h_scr):
    _store_modnorm(x_ref, g_ref, sc_ref, sh_ref, h_scr)
    for c in range(D // TN):
        sl = slice(c * TN, (c + 1) * TN)
        y = jnp.dot(c_ref[...], p_ref[:, sl], preferred_element_type=F32)
        y = y - jnp.dot(s_ref[...], q_ref[:, sl], preferred_element_type=F32)
        gate = jnp.dot(h_scr[...], wg_ref[:, sl], preferred_element_type=F32)
        o_ref[:, sl] = (y * _silu(gate)).astype(BF16)


def _fnet_mix_lat_kernel(h_ref, c_ref, s_ref, p_ref, q_ref, wg_ref, o_ref):
    nt = SEQ // F_TC
    h = h_ref[...].reshape(F_M, D)
    for c in range(D // TN):
        sl = slice(c * TN, (c + 1) * TN)
        y = jnp.dot(c_ref[...], p_ref[:, sl], preferred_element_type=F32)
        y = y - jnp.dot(s_ref[...], q_ref[:, sl], preferred_element_type=F32)
        gate = jnp.dot(h, wg_ref[:, sl], preferred_element_type=F32)
        val = (y * _silu(gate)).astype(BF16)
        for t in range(nt):
            o_ref[t, :, sl] = val[t * F_PR:(t + 1) * F_PR, :]


def _fnet_mix_lat(hp, cr, sr, p, q, wg):
    nt = SEQ // F_TC
    tab = pl.BlockSpec((None, F_M, F_M), lambda b, r: (r, 0, 0))
    pq = pl.BlockSpec((None, None, F_M, D), lambda b, r: (b, r, 0, 0))
    tiles = pl.BlockSpec((None, nt, None, F_PR, D), lambda b, r: (b, 0, r, 0, 0))
    out = pl.pallas_call(
        _fnet_mix_lat_kernel,
        grid=(BATCH, F_R),
        in_specs=[tiles, tab, tab, pq, pq, _resident((D, D), lambda b, r: (0, 0))],
        out_specs=tiles,
        out_shape=jax.ShapeDtypeStruct((BATCH, nt, F_R, F_PR, D), BF16),
        compiler_params=_params(("parallel", "parallel"), 48),
        name="fnet_mix_lat",
    )(hp, cr, sr, p, q, wg)
    return out.reshape(N_LAT, D)


def _fnet_mix_ctx(xc, row_blk0, g, mods, ct, st, a, b, wg):
    mod = lambda chunk: pl.BlockSpec((None, 1, D), lambda bi: (CTX_MOD_ROW, 0, chunk))
    tab = pl.BlockSpec((CTX, CTX), lambda bi: (0, 0))
    ab = pl.BlockSpec((CTX, D), lambda bi: (bi, 0))
    return pl.pallas_call(
        _fnet_mix_kernel,
        grid=(BATCH,),
        in_specs=[pl.BlockSpec((CTX, D), lambda bi: (row_blk0 + bi, 0)),
                  pl.BlockSpec((1, D), lambda bi: (0, 0)), mod(0), mod(1),
                  tab, tab, ab, ab, _resident((D, D), lambda bi: (0, 0))],
        out_specs=pl.BlockSpec((CTX, D), lambda bi: (bi, 0)),
        out_shape=jax.ShapeDtypeStruct((N_CTX, D), BF16),
        scratch_shapes=[pltpu.VMEM((CTX, D), BF16)],
        compiler_params=_params(("parallel",), 48),
        name="fnet_mix_ctx",
    )(xc, g, mods, mods, ct, st, a, b, wg)


def _residual_update(m_ref, w_ref, x_ref, gate_ref, o_ref, unperm_ref, m_scr):
    if unperm_ref is not None:
        for t in range(TM // F_TC):
            rows = slice(t * F_TC, (t + 1) * F_TC)
            m_scr[rows, :] = jnp.dot(unperm_ref[...], m_ref[rows, :],
                                     preferred_element_type=F32).astype(BF16)
        m_ref = m_scr
    for c in range(D // TN):
        sl = slice(c * TN, (c + 1) * TN)
        y = jnp.dot(m_ref[...], w_ref[:, sl], preferred_element_type=F32)
        o_ref[:, sl] = x_ref[:, sl] + gate_ref[:, sl] * y


def _outproj_kernel(m_ref, w_ref, x_ref, gate_ref, *rest, final):
    if final:
        unperm_ref, fg_ref, o_ref, m_scr = rest
        _residual_update(m_ref, w_ref, x_ref, gate_ref, o_ref, unperm_ref, m_scr)
        x = o_ref[...]
        ms = jnp.mean(x * x, axis=-1, keepdims=True)
        o_ref[...] = x * lax.rsqrt(ms + EPS) * fg_ref[...]
    else:
        _residual_update(m_ref, w_ref, x_ref, gate_ref, rest[0], None, None)


def _outproj(mbuf, w, xs, mods, n_rows, final=None):
    in_specs = [
        pl.BlockSpec((TM, D), lambda r: (r, 0)),
        _resident((D, D), lambda r: (0, 0)),
        pl.BlockSpec((TM, D), lambda r: (r, 0)),
        pl.BlockSpec((None, 1, D), lambda r: (_mod_row(r, TM), 0, 2)),
    ]
    args = [mbuf, w, xs, mods]
    scratch = []
    if final is not None:
        in_specs += [pl.BlockSpec((F_TC, F_TC), lambda r: (0, 0)), pl.BlockSpec((1, D), lambda r: (0, 0))]
        args += list(final)
        scratch = [pltpu.VMEM((TM, D), BF16)]
    return pl.pallas_call(
        functools.partial(_outproj_kernel, final=final is not None),
        grid=(n_rows // TM,),
        in_specs=in_specs,
        out_specs=pl.BlockSpec((TM, D), lambda r: (r, 0)),
        out_shape=jax.ShapeDtypeStruct((N_ALL if final is None else n_rows, D), F32),
        scratch_shapes=scratch,
        input_output_aliases={2: 0} if final is None else {},
        compiler_params=_params(("parallel",), 48),
        name="outproj" if final is None else "outproj_final",
    )(*args)


def _outproj_join_kernel(ml_ref, mc_ref, w_ref, xl_ref, xc_ref, gate_ref, unperm_ref, o_ref, m_scr):
    is_lat = pl.program_id(0) < N_LAT // TM

    @pl.when(is_lat)
    def _():
        _residual_update(ml_ref, w_ref, xl_ref, gate_ref, o_ref, unperm_ref, m_scr)

    @pl.when(jnp.logical_not(is_lat))
    def _():
        _residual_update(mc_ref, w_ref, xc_ref, gate_ref, o_ref, None, None)


def _outproj_join(m_lat, m_ctx, w, x_lat, x_ctx, mods, unperm):
    nl = N_LAT // TM
    lat = pl.BlockSpec((TM, D), lambda r: (jnp.minimum(r, nl - 1), 0))
    cxt = pl.BlockSpec((TM, D), lambda r: (jnp.maximum(r - nl, 0), 0))
    return pl.pallas_call(
        _outproj_join_kernel,
        grid=(N_ALL // TM,),
        in_specs=[lat, cxt, _resident((D, D), lambda r: (0, 0)), lat, cxt,
                  pl.BlockSpec((None, 1, D), lambda r: (_mod_row(r, TM), 0, 2)),
                  pl.BlockSpec((F_TC, F_TC), lambda r: (0, 0))],
        out_specs=pl.BlockSpec((TM, D), lambda r: (r, 0)),
        out_shape=jax.ShapeDtypeStruct((N_ALL, D), F32),
        scratch_shapes=[pltpu.VMEM((TM, D), BF16)],
        compiler_params=_params(("parallel",), 48),
        name="outproj_join",
    )(m_lat, m_ctx, w, x_lat, x_ctx, mods, unperm)


def _mlstm_proj_kernel(x_ref, g_ref, sh_ref, sc_ref, w_ref, wg_ref, bg_ref, out_ref, gt_ref, h_scr):
    half = pl.program_id(0)
    _store_modnorm(x_ref, g_ref, sc_ref, sh_ref, h_scr)

    def chunks(post):
        for c in range(M_HALF // TN):
            sl = slice(c * TN, (c + 1) * TN)
            acc = jnp.dot(h_scr[...], w_ref[:, sl], preferred_element_type=F32)
            out_ref[:, sl] = post(c, acc).astype(BF16)

    @pl.when(half == 0)
    def _():
        gt_ref[...] = jnp.dot(h_scr[...], wg_ref[...], preferred_element_type=F32) + bg_ref[...]
        chunks(lambda c, acc: acc * (M_DQK ** -0.5) if c < M_QK // TN else acc)

    @pl.when(half == 1)
    def _():
        chunks(lambda c, acc: _sigmoid(acc) if c < M_V // TN else _silu(acc))


def _mlstm_proj(xs, g, mods, w, wg, bg):
    nr = N_ALL // TM
    return pl.pallas_call(
        _mlstm_proj_kernel,
        grid=(2, nr),
        in_specs=[pl.BlockSpec((TM, D), lambda hf, r: (r, 0))] + _mod_specs(TM, lambda hf, r: r) + [
            _resident((D, M_HALF), lambda hf, r: (0, hf)),
            pl.BlockSpec((D, M_GATES), lambda hf, r: (0, 0)),
            pl.BlockSpec((1, M_GATES), lambda hf, r: (0, 0)),
        ],
        out_specs=[
            pl.BlockSpec((TM, M_HALF), lambda hf, r: (r, hf)),
            pl.BlockSpec((TM, M_GATES), lambda hf, r: (jnp.where(hf == 0, r, nr - 1), 0)),
        ],
        out_shape=[
            jax.ShapeDtypeStruct((N_ALL, 2 * M_HALF), BF16),
            jax.ShapeDtypeStruct((N_ALL, M_GATES), F32),
        ],
        scratch_shapes=[pltpu.VMEM((TM, D), BF16)],
        compiler_params=_params(("arbitrary", "arbitrary"), 56),
        name="mlstm_proj",
    )(xs, g, mods, mods, w, wg, bg)


def _split3(x):
    hi = x.astype(BF16)
    r1 = x - hi.astype(F32)
    mid = r1.astype(BF16)
    lo = (r1 - mid.astype(F32)).astype(BF16)
    return hi, mid, lo


def _cummax_rows(x, reverse):
    rows = x.shape[0]
    row = lax.broadcasted_iota(jnp.int32, x.shape, 0)
    sh = 1
    while sh < rows:
        if reverse:
            x = jnp.where(row < rows - sh, jnp.maximum(x, pltpu.roll(x, rows - sh, 0)), x)
        else:
            x = jnp.where(row >= sh, jnp.maximum(x, pltpu.roll(x, sh, 0)), x)
        sh *= 2
    return x


def _lanes(col, width):
    return jnp.broadcast_to(col, (col.shape[0], width))


def _mlstm_scan_kernel(q_ref, k_ref, v_ref, gt_ref, o_ref, cn_scr, m_scr, *, reverse):
    @pl.when(pl.program_id(1) == 0)
    def _():
        cn_scr[...] = jnp.zeros_like(cn_scr)
        m_scr[...] = jnp.zeros_like(m_scr)

    L = M_L
    lane0 = M_HEADS if reverse else 0
    ig = gt_ref[:, :128]
    ls = _log_sigmoid(gt_ref[:, 128:])
    row = lax.broadcasted_iota(jnp.int32, (L, L), 0)
    col = lax.broadcasted_iota(jnp.int32, (L, L), 1)
    order = (row <= col) if reverse else (row >= col)
    tri = jnp.where(order, 1.0, 0.0).astype(BF16)
    hi, mid, lo = _split3(ls)
    b = (jnp.dot(tri, hi, preferred_element_type=F32)
         + jnp.dot(tri, mid, preferred_element_type=F32)
         + jnp.dot(tri, lo, preferred_element_type=F32))
    end = 0 if reverse else L - 1

    m_prev = m_scr[...]
    r = ig - b
    inter = b + m_prev
    m_t = jnp.maximum(inter, b + _cummax_rows(r, reverse))
    a_all = jnp.exp(inter - m_t)
    u_all = b - m_t
    en_all = jnp.exp(-m_t)
    b_end = b[end:end + 1, :]
    gl = b_end - b + ig
    m_new = jnp.maximum(b_end + m_prev, jnp.max(gl, axis=0, keepdims=True))
    w_all = jnp.exp(gl - m_new)
    decay_all = jnp.exp(b_end + m_prev - m_new)
    r_t = r.T
    m_scr[...] = m_new

    ones_bf = jnp.ones((L, 128), BF16)
    nt = (((1,), (1,)), ((), ()))
    tn = (((0,), (0,)), ((), ()))

    def stage_a(h):
        l = lane0 + h
        q = q_ref[:, h * M_DQK:(h + 1) * M_DQK]
        k = k_ref[:, h * M_DQK:(h + 1) * M_DQK]
        qk = lax.dot_general(q, k, nt, preferred_element_type=F32)
        qcn = jnp.dot(q, cn_scr[h].astype(BF16), preferred_element_type=F32)
        p = jnp.exp(jnp.where(order, u_all[:, l:l + 1] + r_t[l:l + 1, :], -jnp.inf))
        return qk, qcn, p

    def stage_b(h, qk, p):
        s = qk * p
        s_hi = s.astype(BF16)
        s_lo = (s - s_hi.astype(F32)).astype(BF16)
        vo = jnp.concatenate([v_ref[:, h * M_DV:(h + 1) * M_DV], ones_bf], axis=1)
        sv = jnp.dot(s_hi, vo, preferred_element_type=F32)
        return sv, jnp.dot(s_lo, ones_bf, preferred_element_type=F32)

    def stage_c(h, qcn, sv, rs_lo):
        l = lane0 + h
        a = _lanes(a_all[:, l:l + 1], 128)
        den = a * qcn[:, M_DV:] + (sv[:, M_DV:] + rs_lo)
        inv = 1.0 / jnp.maximum(jnp.abs(den), _lanes(en_all[:, l:l + 1], 128))
        for half in range(M_DV // 128):
            c0 = half * 128
            o_ref[:, h * M_DV + c0:h * M_DV + c0 + 128] = (a * qcn[:, c0:c0 + 128] + sv[:, c0:c0 + 128]) * inv

    def stage_d(h):
        l = lane0 + h
        k = k_ref[:, h * M_DQK:(h + 1) * M_DQK]
        kw = (k.astype(F32) * _lanes(w_all[:, l:l + 1], M_DQK)).astype(BF16)
        vo = jnp.concatenate([v_ref[:, h * M_DV:(h + 1) * M_DV], ones_bf], axis=1)
        upd = lax.dot_general(kw, vo, tn, preferred_element_type=F32)
        cn_scr[h] = decay_all[:, l:l + 1] * cn_scr[h] + upd

    sa = {}
    sb = {}
    for step in range(M_HEADS + 2):
        if step < M_HEADS:
            sa[step] = stage_a(step)
        if 1 <= step <= M_HEADS:
            h = step - 1
            sb[h] = stage_b(h, sa[h][0], sa[h][2])
        if step >= 2:
            h = step - 2
            stage_c(h, sa[h][1], *sb[h])
            stage_d(h)


def _mlstm_scan(proj, gates, reverse):
    nlc = SEQ // M_L
    ctx_blk0 = N_LAT // M_L

    def blk(b, i):
        lat = (nlc - i) if reverse else (i - 1)
        return jnp.where(i == 0, ctx_blk0 + b, b * nlc + lat)

    return pl.pallas_call(
        functools.partial(_mlstm_scan_kernel, reverse=reverse),
        grid=(BATCH, nlc + 1),
        in_specs=[
            pl.BlockSpec((M_L, M_QK), lambda b, i: (blk(b, i), 0)),
            pl.BlockSpec((M_L, M_QK), lambda b, i: (blk(b, i), 1)),
            pl.BlockSpec((M_L, M_V), lambda b, i: (blk(b, i), 1)),
            pl.BlockSpec((M_L, M_GATES), lambda b, i: (blk(b, i), 0)),
        ],
        out_specs=pl.BlockSpec((M_L, M_V), lambda b, i: (blk(b, i), 0)),
        out_shape=jax.ShapeDtypeStruct((N_ALL, M_V), F32),
        scratch_shapes=[
            pltpu.VMEM((M_HEADS, M_DQK, M_DV + 128), F32),
            pltpu.VMEM((1, 128), F32),
        ],
        compiler_params=_params(("parallel", "arbitrary"), 48),
        name="mlstm_scan_bwd" if reverse else "mlstm_scan_fwd",
    )(proj, proj, proj, gates)


def _mlstm_finish_kernel(so_ref, sz_ref, hf_ref, hb_ref, hn_ref, m_ref):
    for h in range(M_HEADS):
        sl = slice(h * M_DV, (h + 1) * M_DV)
        y = so_ref[:, sl].astype(F32) * (hf_ref[:, sl] + hb_ref[:, sl])
        ms = jnp.mean(y * y, axis=-1, keepdims=True)
        y = y * lax.rsqrt(ms + EPS) * hn_ref[:, sl]
        m_ref[:, sl] = (y * sz_ref[:, sl].astype(F32)).astype(BF16)


def _mlstm_finish(proj, hf, hb, hn):
    tm = 256
    return pl.pallas_call(
        _mlstm_finish_kernel,
        grid=(N_ALL // tm,),
        in_specs=[
            pl.BlockSpec((tm, M_V), lambda r: (r, 2)),
            pl.BlockSpec((tm, M_V), lambda r: (r, 3)),
            pl.BlockSpec((tm, M_V), lambda r: (r, 0)),
            pl.BlockSpec((tm, M_V), lambda r: (r, 0)),
            pl.BlockSpec((1, M_V), lambda r: (0, 0)),
        ],
        out_specs=pl.BlockSpec((tm, M_V), lambda r: (r, 0)),
        out_shape=jax.ShapeDtypeStruct((N_ALL, M_V), BF16),
        compiler_params=_params(("parallel",), 48),
        name="mlstm_finish",
    )(proj, proj, hf, hb, hn)


A_QKW = A_Q + A_KVW


def _attn_qk_kernel(x_ref, g_ref, sh_ref, sc_ref, w_ref, gain_ref, cos_ref, sin_ref, q_ref, k_ref, h_scr):
    _store_modnorm(x_ref, g_ref, sc_ref, sh_ref, h_scr)
    mean_mat = jnp.full((A_HD, A_HD), 1.0 / A_HD, BF16)
    cos = cos_ref[...]
    sin = sin_ref[...]
    for c in range(A_QKW // TN):
        acc = jnp.dot(h_scr[...], w_ref[:, c * TN:(c + 1) * TN], preferred_element_type=F32)
        for hh in range(TN // A_HD):
            lo = c * TN + hh * A_HD
            a = acc[:, hh * A_HD:(hh + 1) * A_HD]
            ms = jnp.dot((a * a).astype(BF16), mean_mat, preferred_element_type=F32)
            a = a * lax.rsqrt(ms + EPS) * gain_ref[:, lo:lo + A_HD]
            a = (a * cos + pltpu.roll(a, A_HD // 2, 1) * sin).astype(BF16)
            if lo < A_Q:
                q_ref[:, lo:lo + A_HD] = a
            else:
                k_ref[:, lo - A_Q:lo - A_Q + A_HD] = a


def _attn_qk(xs, g, mods, w, gain, cos, sin):
    lat_tiles = N_LAT // TM
    rope_blk = lambda r: (jnp.where(r < lat_tiles, r % (SEQ // TM), SEQ // TM), 0)
    return pl.pallas_call(
        _attn_qk_kernel,
        grid=(N_ALL // TM,),
        in_specs=[pl.BlockSpec((TM, D), lambda r: (r, 0))] + _mod_specs(TM, lambda r: r) + [
            _resident((D, A_QKW), lambda r: (0, 0)),
            pl.BlockSpec((1, A_QKW), lambda r: (0, 0)),
            pl.BlockSpec((TM, A_HD), rope_blk),
            pl.BlockSpec((TM, A_HD), rope_blk),
        ],
        out_specs=[
            pl.BlockSpec((TM, A_Q), lambda r: (r, 0)),
            pl.BlockSpec((TM, A_KVW), lambda r: (r, 0)),
        ],
        out_shape=[
            jax.ShapeDtypeStruct((N_ALL, A_Q), BF16),
            jax.ShapeDtypeStruct((N_ALL, A_KVW), BF16),
        ],
        scratch_shapes=[pltpu.VMEM((TM, D), BF16)],
        compiler_params=_params(("parallel",), 48),
        name="attn_qk",
    )(xs, g, mods, mods, w, gain, cos, sin)


def _attn_vz_kernel(x_ref, g_ref, sh_ref, sc_ref, w_ref, v_ref, zg_ref, h_scr):
    _store_modnorm(x_ref, g_ref, sc_ref, sh_ref, h_scr)
    v_ref[...] = jnp.dot(h_scr[...], w_ref[:, :A_KVW], preferred_element_type=F32).astype(BF16)
    for c in range(A_Q // TN):
        lo = A_KVW + c * TN
        z = jnp.dot(h_scr[...], w_ref[:, lo:lo + TN], preferred_element_type=F32)
        zg_ref[:, c * TN:(c + 1) * TN] = _silu(z).astype(BF16)


def _attn_vz(xs, g, mods, w):
    return pl.pallas_call(
        _attn_vz_kernel,
        grid=(N_ALL // TM,),
        in_specs=[pl.BlockSpec((TM, D), lambda r: (r, 0))] + _mod_specs(TM, lambda r: r) + [
            _resident((D, A_KVW + A_Q), lambda r: (0, 0)),
        ],
        out_specs=[
            pl.BlockSpec((TM, A_KVW), lambda r: (r, 0)),
            pl.BlockSpec((TM, A_Q), lambda r: (r, 0)),
        ],
        out_shape=[
            jax.ShapeDtypeStruct((N_ALL, A_KVW), BF16),
            jax.ShapeDtypeStruct((N_ALL, A_Q), BF16),
        ],
        scratch_shapes=[pltpu.VMEM((TM, D), BF16)],
        compiler_params=_params(("parallel",), 48),
        name="attn_vz",
    )(xs, g, mods, mods, w)


A_LAT_TILES = SEQ // A_TQ


def _attn_kernel(q_ref, kl_ref, kc_ref, vl_ref, vc_ref, zg_ref, o_ref):
    t = pl.program_id(2)
    nt = (((1,), (1,)), ((), ()))

    def run(with_latent_keys):
        for hh in range(A_GW // A_HD):
            sl = slice(hh * A_HD, (hh + 1) * A_HD)
            q = q_ref[:, sl]
            s_c = lax.dot_general(q, kc_ref[...], nt, preferred_element_type=F32)
            m = jnp.max(s_c, axis=-1, keepdims=True)
            if with_latent_keys:
                s_l = lax.dot_general(q, kl_ref[...], nt, preferred_element_type=F32)
                m = jnp.maximum(m, jnp.max(s_l, axis=-1, keepdims=True))
            p_c = jnp.exp(s_c - m)
            den = jnp.sum(p_c, axis=-1, keepdims=True)
            acc = jnp.dot(p_c.astype(BF16), vc_ref[...], preferred_element_type=F32)
            if with_latent_keys:
                p_l = jnp.exp(s_l - m)
                den = den + jnp.sum(p_l, axis=-1, keepdims=True)
                acc = acc + jnp.dot(p_l.astype(BF16), vl_ref[...], preferred_element_type=F32)
            o_ref[:, sl] = (acc / den * zg_ref[:, sl].astype(F32)).astype(BF16)

    @pl.when(t < A_LAT_TILES)
    def _():
        run(True)

    @pl.when(t >= A_LAT_TILES)
    def _():
        run(False)


def _attn(q, k, v, zg):
    ctx_blk0 = N_LAT // A_TQ
    qrow = lambda b, g, t: jnp.where(t < A_LAT_TILES, b * A_LAT_TILES + t, ctx_blk0 + b)
    return pl.pallas_call(
        _attn_kernel,
        grid=(BATCH, A_KV, A_LAT_TILES + 1),
        in_specs=[
            pl.BlockSpec((A_TQ, A_GW), lambda b, g, t: (qrow(b, g, t), g)),
            pl.BlockSpec((SEQ, A_HD), lambda b, g, t: (b, g)),
            pl.BlockSpec((CTX, A_HD), lambda b, g, t: (N_LAT // CTX + b, g)),
            pl.BlockSpec((SEQ, A_HD), lambda b, g, t: (b, g)),
            pl.BlockSpec((CTX, A_HD), lambda b, g, t: (N_LAT // CTX + b, g)),
            pl.BlockSpec((A_TQ, A_GW), lambda b, g, t: (qrow(b, g, t), g)),
        ],
        out_specs=pl.BlockSpec((A_TQ, A_GW), lambda b, g, t: (qrow(b, g, t), g)),
        out_shape=jax.ShapeDtypeStruct((N_ALL, A_Q), BF16),
        compiler_params=_params(("parallel", "parallel", "arbitrary"), 48),
        name="attn",
    )(q, k, k, v, v, zg)


def _dft_tables(n, scale):
    idx = jnp.arange(n, dtype=jnp.int32)
    ang = ((idx[:, None] * idx[None, :]) % n).astype(F32) * (2.0 * math.pi / n)
    return (jnp.cos(ang) * scale).astype(BF16), (jnp.sin(ang) * scale).astype(BF16)


def _dif_tables():
    jt = jnp.arange(F_M, dtype=jnp.int32)
    r = jnp.arange(F_R, dtype=jnp.int32)
    k = ((F_R * jt[None, :, None] + r[:, None, None]) * jt[None, None, :]) % SEQ
    ang = k.astype(F32) * (2.0 * math.pi / SEQ)
    return (jnp.cos(ang) * SEQ ** -0.5).astype(BF16), (jnp.sin(ang) * SEQ ** -0.5).astype(BF16)


def _perm_tables():
    n = jnp.arange(F_TC)
    src = F_R * (n % F_PR) + n // F_PR
    perm = (src[:, None] == n[None, :]).astype(BF16)
    return perm, perm.T


def _rope_tables():
    t = jnp.arange(SEQ)
    freqs = ROPE_THETA ** (-jnp.arange(0, A_HD // 2, 2, dtype=F32) / (A_HD // 2))
    ang = jnp.concatenate([(t // GRID_W).astype(F32)[:, None] * freqs,
                           (t % GRID_W).astype(F32)[:, None] * freqs], axis=-1)
    cos = jnp.concatenate([jnp.cos(ang), jnp.cos(ang)], axis=-1)
    sin = jnp.concatenate([-jnp.sin(ang), jnp.sin(ang)], axis=-1)
    pad = lambda a, fill: jnp.concatenate([a, jnp.full((TM, A_HD), fill, F32)], axis=0)
    return pad(cos, 1.0), pad(sin, 0.0)


def _split_heads_even_odd(w, heads):
    perm = jnp.concatenate([jnp.arange(0, A_HD, 2), jnp.arange(1, A_HD, 2)])
    lead = w.shape[:-1]
    return w.reshape(lead + (heads, A_HD))[..., perm].reshape(lead + (heads * A_HD,))


def kernel(x, c, ctx, c_ctx, ada_w, ada_b, norm_g, fnet_w_gate, fnet_w_out, mlstm_w_in, mlstm_b_gate,
           mlstm_hn, mlstm_w_out, attn_w_in, attn_qn, attn_kn, attn_w_out, final_g):
    cc = jnp.concatenate([c, c_ctx[None, :], jnp.zeros((MOD_ROWS - BATCH - 1, D), F32)], axis=0)
    mods_all = _modvec(cc, ada_w, ada_b).reshape(DEPTH, MOD_ROWS, 1, 3 * D)

    cc_c, sc_c = _dft_tables(F_GW, F_GW ** -0.5)
    cs_chan = jnp.concatenate([cc_c, sc_c], axis=1)
    cr_lat, sr_lat = _dif_tables()
    ct_ctx, st_ctx = _dft_tables(CTX, CTX ** -0.5)
    perm, unperm = _perm_tables()

    xs = None
    for i in range(DEPTH):
        kind, j = i % 3, i // 3
        last = i == DEPTH - 1
        mods = mods_all[i]
        g = norm_g[i].reshape(1, D)
        n_rows = N_LAT if last else N_ALL
        if kind == 0:
            assert i == 0 or last
            wg = fnet_w_gate[j].astype(BF16)
            w_out = fnet_w_out[j].astype(BF16)
            x_lat = x.reshape(N_LAT, D) if i == 0 else xs
            p, q, hp = _fnet_chan_dif(x_lat, g, mods, cs_chan, perm)
            m_lat = _fnet_mix_lat(hp, cr_lat, sr_lat, p, q, wg)
            if i == 0:
                x_ctx = ctx.reshape(N_CTX, D)
                a, b = _fnet_chan_ctx(x_ctx, 0, g, mods, cs_chan)
                m_ctx = _fnet_mix_ctx(x_ctx, 0, g, mods, ct_ctx, st_ctx, a, b, wg)
                xs = _outproj_join(m_lat, m_ctx, w_out, x_lat, x_ctx, mods, unperm)
            else:
                out = _outproj(m_lat, w_out, xs, mods, n_rows, (unperm, final_g.reshape(1, D)))
            continue
        if kind == 1:
            w_in = mlstm_w_in[j]
            n_main = 2 * M_QK + 2 * M_V
            w_main = jnp.concatenate([w_in[:, :n_main], w_in[:, n_main + 4 * M_HEADS:]],
                                     axis=1).astype(BF16)
            gate_tiles = lambda gcols: jnp.pad(
                jnp.concatenate([gcols[..., 0:8], gcols[..., 16:24]], axis=-1),
                [(0, 0)] * (gcols.ndim - 1) + [(0, 128 - 2 * M_HEADS)])
            gate_layout = lambda gcols: jnp.concatenate(
                [gate_tiles(gcols), gate_tiles(gcols[..., M_HEADS:])], axis=-1)
            w_gates = gate_layout(w_in[:, n_main:n_main + 4 * M_HEADS]).astype(BF16)
            b_gates = gate_layout(mlstm_b_gate[j][None, :])
            proj, gates = _mlstm_proj(xs, g, mods, w_main, w_gates, b_gates)
            hf = _mlstm_scan(proj, gates, reverse=False)
            hb = _mlstm_scan(proj, gates, reverse=True)
            mbuf = _mlstm_finish(proj, hf, hb, mlstm_hn[j].reshape(1, M_V))
            w_out = mlstm_w_out[j]
        else:
            w_in = attn_w_in[j]
            w_qk = jnp.concatenate([_split_heads_even_odd(w_in[:, :A_Q], A_HEADS),
                                    _split_heads_even_odd(w_in[:, A_Q:A_QKW], A_KV)], axis=1)
            qn = _split_heads_even_odd(attn_qn[j], 1) * (A_HD ** -0.5)
            kn = _split_heads_even_odd(attn_kn[j], 1)
            gain = jnp.concatenate([jnp.tile(qn, A_HEADS), jnp.tile(kn, A_KV)]).reshape(1, A_QKW)
            cos, sin = _rope_tables()
            q, k = _attn_qk(xs, g, mods, w_qk.astype(BF16), gain, cos, sin)
            v, zg = _attn_vz(xs, g, mods, w_in[:, A_QKW:].astype(BF16))
            mbuf = _attn(q, k, v, zg)
            w_out = attn_w_out[j]
        assert not last
        xs = _outproj(mbuf, w_out.astype(BF16), xs, mods, n_rows)

    return out.reshape(BATCH, SEQ, D)
```

```python
import functools
import math

import jax
import jax.numpy as jnp
from jax import lax
from jax.experimental import pallas as pl
from jax.experimental.pallas import tpu as pltpu

F32 = jnp.float32
BF16 = jnp.bfloat16

D = 2048
BATCH = 16
SEQ = 2048
CTX = 256
DEPTH = 4
EPS = 1e-6
N_LAT = BATCH * SEQ
N_CTX = BATCH * CTX
N_ALL = N_LAT + N_CTX
MOD_ROWS = 24
CTX_MOD_ROW = BATCH

F_GROUPS = 4
F_GW = D // F_GROUPS

M_HEADS = 8
M_DQK = 128
M_DV = 256
M_QK = M_HEADS * M_DQK
M_V = M_HEADS * M_DV
M_L = 256
M_GATES = 256
M_HALF = 2 * M_QK + M_V

A_HEADS = 16
A_KV = 4
A_HD = 128
A_Q = A_HEADS * A_HD
A_KVW = A_KV * A_HD
A_GW = A_Q // A_KV
A_TQ = 256
GRID_W = 64
ROPE_THETA = 10000.0

TM = 512
TN = 512
MIB = 1024 * 1024


def _params(sem, vmem_mib):
    return pltpu.CompilerParams(dimension_semantics=sem, vmem_limit_bytes=vmem_mib * MIB)


def _resident(shape, index_map):
    return pl.BlockSpec(shape, index_map, pipeline_mode=pl.Buffered(1))


def _sigmoid(x):
    return 1.0 / (1.0 + jnp.exp(-x))


def _silu(x):
    return x * _sigmoid(x)


def _log_sigmoid(x):
    return jnp.minimum(x, 0.0) - jnp.log1p(jnp.exp(-jnp.abs(x)))


def _modnorm(x, g, scale, shift):
    ms = jnp.mean(x * x, axis=-1, keepdims=True)
    y = x * lax.rsqrt(ms + EPS) * g
    return y * (1.0 + scale) + shift


def _store_modnorm(x_ref, g_ref, sc_ref, sh_ref, h_scr):
    rows = x_ref.shape[0]
    step = min(rows, 256)
    for r0 in range(0, rows, step):
        h_scr[r0:r0 + step, :] = _modnorm(x_ref[r0:r0 + step, :], g_ref[...], sc_ref[...],
                                          sh_ref[...]).astype(BF16)


def _mod_row(r, tm):
    return jnp.where(r < N_LAT // tm, r // (SEQ // tm), CTX_MOD_ROW)


def _modvec_kernel(c_ref, w_ref, b_ref, o_ref):
    s = _silu(c_ref[...])
    o_ref[...] = jnp.dot(s, w_ref[...], preferred_element_type=F32,
                         precision=lax.Precision.HIGHEST) + b_ref[...]


def _modvec(cc, ada_w, ada_b):
    tn = 1024
    return pl.pallas_call(
        _modvec_kernel,
        grid=(DEPTH, 3 * D // tn),
        in_specs=[
            pl.BlockSpec((MOD_ROWS, D), lambda i, j: (0, 0)),
            pl.BlockSpec((None, D, tn), lambda i, j: (i, 0, j)),
            pl.BlockSpec((None, 1, tn), lambda i, j: (i, 0, j)),
        ],
        out_specs=pl.BlockSpec((None, MOD_ROWS, tn), lambda i, j: (i, 0, j)),
        out_shape=jax.ShapeDtypeStruct((DEPTH, MOD_ROWS, 3 * D), F32),
        compiler_params=_params(("parallel", "parallel"), 40),
        name="modvec",
    )(cc, ada_w, ada_b.reshape(DEPTH, 1, 3 * D))


def _mod_specs(tm, row_of):
    return [
        pl.BlockSpec((1, D), lambda *ids: (0, 0)),
        pl.BlockSpec((None, 1, D), lambda *ids: (_mod_row(row_of(*ids), tm), 0, 0)),
        pl.BlockSpec((None, 1, D), lambda *ids: (_mod_row(row_of(*ids), tm), 0, 1)),
    ]


def _fnet_chan_kernel(x_ref, g_ref, sh_ref, sc_ref, cs_ref, a_ref, b_ref):
    h = _modnorm(x_ref[...], g_ref[...], sc_ref[...], sh_ref[...]).astype(BF16)
    for grp in range(F_GROUPS):
        sl = slice(grp * F_GW, (grp + 1) * F_GW)
        p = jnp.dot(h[:, sl], cs_ref[...], preferred_element_type=F32)
        a_ref[:, sl] = p[:, :F_GW].astype(BF16)
        b_ref[:, sl] = p[:, F_GW:].astype(BF16)


def _fnet_chan_ctx(xc, row_blk0, g, mods, cs_c):
    tm = 512
    out = jax.ShapeDtypeStruct((N_CTX, D), BF16)
    mod = lambda chunk: pl.BlockSpec((None, 1, D), lambda r: (CTX_MOD_ROW, 0, chunk))
    return pl.pallas_call(
        _fnet_chan_kernel,
        grid=(N_CTX // tm,),
        in_specs=[pl.BlockSpec((tm, D), lambda r: (row_blk0 + r, 0)),
                  pl.BlockSpec((1, D), lambda r: (0, 0)), mod(0), mod(1),
                  pl.BlockSpec((F_GW, 2 * F_GW), lambda r: (0, 0))],
        out_specs=[pl.BlockSpec((tm, D), lambda r: (r, 0))] * 2,
        out_shape=[out, out],
        compiler_params=_params(("parallel",), 48),
        name="fnet_chan_ctx",
    )(xc, g, mods, mods, cs_c)


F_R = 4
F_M = SEQ // F_R
F_TC = 256
F_PR = F_TC // F_R


def _fnet_chan_dif_kernel(x0_ref, x1_ref, x2_ref, x3_ref, g_ref, sh_ref, sc_ref, cs_ref, perm_ref,
                          p_ref, q_ref, hp_ref):
    hs = [_modnorm(x[...], g_ref[...], sc_ref[...], sh_ref[...]).astype(BF16)
          for x in (x0_ref, x1_ref, x2_ref, x3_ref)]
    for qi, h in enumerate(hs):
        hp = jnp.dot(perm_ref[...], h, preferred_element_type=F32).astype(BF16)
        for r in range(F_R):
            hp_ref[qi, r] = hp[r * F_PR:(r + 1) * F_PR, :]
    for grp in range(F_GROUPS):
        sl = slice(grp * F_GW, (grp + 1) * F_GW)
        ab = [jnp.dot(h[:, sl], cs_ref[...], preferred_element_type=F32) for h in hs]
        a = [t[:, :F_GW] for t in ab]
        b = [t[:, F_GW:] for t in ab]
        sa02, da02, sa13, da13 = a[0] + a[2], a[0] - a[2], a[1] + a[3], a[1] - a[3]
        sb02, db02, sb13, db13 = b[0] + b[2], b[0] - b[2], b[1] + b[3], b[1] - b[3]
        re = (sa02 + sa13, da02 - db13, sa02 - sa13, da02 + db13)
        im = (sb02 + sb13, db02 + da13, sb02 - sb13, db02 - da13)
        for r in range(F_R):
            p_ref[r, :, sl] = re[r].astype(BF16)
            q_ref[r, :, sl] = im[r].astype(BF16)


def _fnet_chan_dif(xl, g, mods, cs_c, perm):
    nt = F_M // F_TC
    xspec = lambda q: pl.BlockSpec((F_TC, D), lambda b, i: (b * (SEQ // F_TC) + q * nt + i, 0))
    mod = lambda chunk: pl.BlockSpec((None, 1, D), lambda b, i: (b, 0, chunk))
    out = jax.ShapeDtypeStruct((BATCH, F_R, F_M, D), BF16)
    ospec = pl.BlockSpec((None, F_R, F_TC, D), lambda b, i: (b, 0, i, 0))
    p, q, hp = pl.pallas_call(
        _fnet_chan_dif_kernel,
        grid=(BATCH, nt),
        in_specs=[xspec(0), xspec(1), xspec(2), xspec(3),
                  pl.BlockSpec((1, D), lambda b, i: (0, 0)), mod(0), mod(1),
                  pl.BlockSpec((F_GW, 2 * F_GW), lambda b, i: (0, 0)),
                  pl.BlockSpec((F_TC, F_TC), lambda b, i: (0, 0))],
        out_specs=[ospec, ospec,
                   pl.BlockSpec((None, F_R, None, F_R, F_PR, D), lambda b, i: (b, 0, i, 0, 0, 0))],
        out_shape=[out, out, jax.ShapeDtypeStruct((BATCH, F_R, nt, F_R, F_PR, D), BF16)],
        compiler_params=_params(("parallel", "parallel"), 56),
        name="fnet_chan_dif",
    )(xl, xl, xl, xl, g, mods, mods, cs_c, perm)
    return p, q, hp.reshape(BATCH, SEQ // F_TC, F_R, F_PR, D)


def _fnet_mix_kernel(x_ref, g_ref, sh_ref, sc_ref, c_ref, s_ref, p_ref, q_ref, wg_ref, o_ref, h_scr):
    _store_modnorm(x_ref, g_ref, sc_ref, sh_ref, h_scr)
    for c in range(D // TN):
        sl = slice(c * TN, (c + 1) * TN)
        y = jnp.dot(c_ref[...], p_ref[:, sl], preferred_element_type=F32)
        y = y - jnp.dot(s_ref[...], q_ref[:, sl], preferred_element_type=F32)
        gate = jnp.dot(h_scr[...], wg_ref[:, sl], preferred_element_type=F32)
        o_ref[:, sl] = (y * _silu(gate)).astype(BF16)


def _fnet_mix_lat_kernel(h_ref, c_ref, s_ref, p_ref, q_ref, wg_ref, o_ref):
    nt = SEQ // F_TC
    h = h_ref[...].reshape(F_M, D)
    for c in range(D // TN):
        sl = slice(c * TN, (c + 1) * TN)
        y = jnp.dot(c_ref[...], p_ref[:, sl], preferred_element_type=F32)
        y = y - jnp.dot(s_ref[...], q_ref[:, sl], preferred_element_type=F32)
        gate = jnp.dot(h, wg_ref[:, sl], preferred_element_type=F32)
        val = (y * _silu(gate)).astype(BF16)
        for t in range(nt):
            o_ref[t, :, sl] = val[t * F_PR:(t + 1) * F_PR, :]


def _fnet_mix_lat(hp, cr, sr, p, q, wg):
    nt = SEQ // F_TC
    tab = pl.BlockSpec((None, F_M, F_M), lambda b, r: (r, 0, 0))
    pq = pl.BlockSpec((None, None, F_M, D), lambda b, r: (b, r, 0, 0))
    tiles = pl.BlockSpec((None, nt, None, F_PR, D), lambda b, r: (b, 0, r, 0, 0))
    out = pl.pallas_call(
        _fnet_mix_lat_kernel,
        grid=(BATCH, F_R),
        in_specs=[tiles, tab, tab, pq, pq, _resident((D, D), lambda b, r: (0, 0))],
        out_specs=tiles,
        out_shape=jax.ShapeDtypeStruct((BATCH, nt, F_R, F_PR, D), BF16),
        compiler_params=_params(("parallel", "parallel"), 48),
        name="fnet_mix_lat",
    )(hp, cr, sr, p, q, wg)
    return out.reshape(N_LAT, D)


def _fnet_mix_ctx(xc, row_blk0, g, mods, ct, st, a, b, wg):
    mod = lambda chunk: pl.BlockSpec((None, 1, D), lambda bi: (CTX_MOD_ROW, 0, chunk))
    tab = pl.BlockSpec((CTX, CTX), lambda bi: (0, 0))
    ab = pl.BlockSpec((CTX, D), lambda bi: (bi, 0))
    return pl.pallas_call(
        _fnet_mix_kernel,
        grid=(BATCH,),
        in_specs=[pl.BlockSpec((CTX, D), lambda bi: (row_blk0 + bi, 0)),
                  pl.BlockSpec((1, D), lambda bi: (0, 0)), mod(0), mod(1),
                  tab, tab, ab, ab, _resident((D, D), lambda bi: (0, 0))],
        out_specs=pl.BlockSpec((CTX, D), lambda bi: (bi, 0)),
        out_shape=jax.ShapeDtypeStruct((N_CTX, D), BF16),
        scratch_shapes=[pltpu.VMEM((CTX, D), BF16)],
        compiler_params=_params(("parallel",), 48),
        name="fnet_mix_ctx",
    )(xc, g, mods, mods, ct, st, a, b, wg)


def _residual_update(m_ref, w_ref, x_ref, gate_ref, o_ref, unperm_ref, m_scr):
    if unperm_ref is not None:
        for t in range(TM // F_TC):
            rows = slice(t * F_TC, (t + 1) * F_TC)
            m_scr[rows, :] = jnp.dot(unperm_ref[...], m_ref[rows, :],
                                     preferred_element_type=F32).astype(BF16)
        m_ref = m_scr
    for c in range(D // TN):
        sl = slice(c * TN, (c + 1) * TN)
        y = jnp.dot(m_ref[...], w_ref[:, sl], preferred_element_type=F32)
        o_ref[:, sl] = x_ref[:, sl] + gate_ref[:, sl] * y


def _outproj_kernel(m_ref, w_ref, x_ref, gate_ref, *rest, final):
    if final:
        unperm_ref, fg_ref, o_ref, m_scr = rest
        _residual_update(m_ref, w_ref, x_ref, gate_ref, o_ref, unperm_ref, m_scr)
        x = o_ref[...]
        ms = jnp.mean(x * x, axis=-1, keepdims=True)
        o_ref[...] = x * lax.rsqrt(ms + EPS) * fg_ref[...]
    else:
        _residual_update(m_ref, w_ref, x_ref, gate_ref, rest[0], None, None)


def _outproj(mbuf, w, xs, mods, n_rows, final=None):
    in_specs = [
        pl.BlockSpec((TM, D), lambda r: (r, 0)),
        _resident((D, D), lambda r: (0, 0)),
        pl.BlockSpec((TM, D), lambda r: (r, 0)),
        pl.BlockSpec((None, 1, D), lambda r: (_mod_row(r, TM), 0, 2)),
    ]
    args = [mbuf, w, xs, mods]
    scratch = []
    if final is not None:
        in_specs += [pl.BlockSpec((F_TC, F_TC), lambda r: (0, 0)), pl.BlockSpec((1, D), lambda r: (0, 0))]
        args += list(final)
        scratch = [pltpu.VMEM((TM, D), BF16)]
    return pl.pallas_call(
        functools.partial(_outproj_kernel, final=final is not None),
        grid=(n_rows // TM,),
        in_specs=in_specs,
        out_specs=pl.BlockSpec((TM, D), lambda r: (r, 0)),
        out_shape=jax.ShapeDtypeStruct((N_ALL if final is None else n_rows, D), F32),
        scratch_shapes=scratch,
        input_output_aliases={2: 0} if final is None else {},
        compiler_params=_params(("parallel",), 48),
        name="outproj" if final is None else "outproj_final",
    )(*args)


def _outproj_join_kernel(ml_ref, mc_ref, w_ref, xl_ref, xc_ref, gate_ref, unperm_ref, o_ref, m_scr):
    is_lat = pl.program_id(0) < N_LAT // TM

    @pl.when(is_lat)
    def _():
        _residual_update(ml_ref, w_ref, xl_ref, gate_ref, o_ref, unperm_ref, m_scr)

    @pl.when(jnp.logical_not(is_lat))
    def _():
        _residual_update(mc_ref, w_ref, xc_ref, gate_ref, o_ref, None, None)


def _outproj_join(m_lat, m_ctx, w, x_lat, x_ctx, mods, unperm):
    nl = N_LAT // TM
    lat = pl.BlockSpec((TM, D), lambda r: (jnp.minimum(r, nl - 1), 0))
    cxt = pl.BlockSpec((TM, D), lambda r: (jnp.maximum(r - nl, 0), 0))
    return pl.pallas_call(
        _outproj_join_kernel,
        grid=(N_ALL // TM,),
        in_specs=[lat, cxt, _resident((D, D), lambda r: (0, 0)), lat, cxt,
                  pl.BlockSpec((None, 1, D), lambda r: (_mod_row(r, TM), 0, 2)),
                  pl.BlockSpec((F_TC, F_TC), lambda r: (0, 0))],
        out_specs=pl.BlockSpec((TM, D), lambda r: (r, 0)),
        out_shape=jax.ShapeDtypeStruct((N_ALL, D), F32),
        scratch_shapes=[pltpu.VMEM((TM, D), BF16)],
        compiler_params=_params(("parallel",), 48),
        name="outproj_join",
    )(m_lat, m_ctx, w, x_lat, x_ctx, mods, unperm)


def _mlstm_proj_kernel(x_ref, g_ref, sh_ref, sc_ref, w_ref, wg_ref, bg_ref, out_ref, gt_ref, h_scr):
    half = pl.program_id(0)
    _store_modnorm(x_ref, g_ref, sc_ref, sh_ref, h_scr)

    def chunks(post):
        for c in range(M_HALF // TN):
            sl = slice(c * TN, (c + 1) * TN)
            acc = jnp.dot(h_scr[...], w_ref[:, sl], preferred_element_type=F32)
            out_ref[:, sl] = post(c, acc).astype(BF16)

    @pl.when(half == 0)
    def _():
        gt_ref[...] = jnp.dot(h_scr[...], wg_ref[...], preferred_element_type=F32) + bg_ref[...]
        chunks(lambda c, acc: acc * (M_DQK ** -0.5) if c < M_QK // TN else acc)

    @pl.when(half == 1)
    def _():
        chunks(lambda c, acc: _sigmoid(acc) if c < M_V // TN else _silu(acc))


def _mlstm_proj(xs, g, mods, w, wg, bg):
    nr = N_ALL // TM
    return pl.pallas_call(
        _mlstm_proj_kernel,
        grid=(2, nr),
        in_specs=[pl.BlockSpec((TM, D), lambda hf, r: (r, 0))] + _mod_specs(TM, lambda hf, r: r) + [
            _resident((D, M_HALF), lambda hf, r: (0, hf)),
            pl.BlockSpec((D, M_GATES), lambda hf, r: (0, 0)),
            pl.BlockSpec((1, M_GATES), lambda hf, r: (0, 0)),
        ],
        out_specs=[
            pl.BlockSpec((TM, M_HALF), lambda hf, r: (r, hf)),
            pl.BlockSpec((TM, M_GATES), lambda hf, r: (jnp.where(hf == 0, r, nr - 1), 0)),
        ],
        out_shape=[
            jax.ShapeDtypeStruct((N_ALL, 2 * M_HALF), BF16),
            jax.ShapeDtypeStruct((N_ALL, M_GATES), F32),
        ],
        scratch_shapes=[pltpu.VMEM((TM, D), BF16)],
        compiler_params=_params(("arbitrary", "arbitrary"), 56),
        name="mlstm_proj",
    )(xs, g, mods, mods, w, wg, bg)


def _split3(x):
    hi = x.astype(BF16)
    r1 = x - hi.astype(F32)
    mid = r1.astype(BF16)
    lo = (r1 - mid.astype(F32)).astype(BF16)
    return hi, mid, lo


def _cummax_rows(x, reverse):
    rows = x.shape[0]
    row = lax.broadcasted_iota(jnp.int32, x.shape, 0)
    sh = 1
    while sh < rows:
        if reverse:
            x = jnp.where(row < rows - sh, jnp.maximum(x, pltpu.roll(x, rows - sh, 0)), x)
        else:
            x = jnp.where(row >= sh, jnp.maximum(x, pltpu.roll(x, sh, 0)), x)
        sh *= 2
    return x


def _lanes(col, width):
    return jnp.broadcast_to(col, (col.shape[0], width))


def _mlstm_scan_kernel(q_ref, k_ref, v_ref, gt_ref, *rest, reverse):
    if reverse:
        o_ref, cn_scr, m_scr = rest
    else:
        hb_ref, so_ref, sz_ref, hn_ref, o_ref, cn_scr, m_scr = rest

    @pl.when(pl.program_id(1) == 0)
    def _():
        cn_scr[...] = jnp.zeros_like(cn_scr)
        m_scr[...] = jnp.zeros_like(m_scr)

    L = M_L
    lane0 = M_HEADS if reverse else 0
    ig = gt_ref[:, :128]
    ls = _log_sigmoid(gt_ref[:, 128:])
    row = lax.broadcasted_iota(jnp.int32, (L, L), 0)
    col = lax.broadcasted_iota(jnp.int32, (L, L), 1)
    order = (row <= col) if reverse else (row >= col)
    tri = jnp.where(order, 1.0, 0.0).astype(BF16)
    hi, mid, lo = _split3(ls)
    b = (jnp.dot(tri, hi, preferred_element_type=F32)
         + jnp.dot(tri, mid, preferred_element_type=F32)
         + jnp.dot(tri, lo, preferred_element_type=F32))
    end = 0 if reverse else L - 1

    m_prev = m_scr[...]
    r = ig - b
    inter = b + m_prev
    m_t = jnp.maximum(inter, b + _cummax_rows(r, reverse))
    a_all = jnp.exp(inter - m_t)
    u_all = b - m_t
    en_all = jnp.exp(-m_t)
    b_end = b[end:end + 1, :]
    gl = b_end - b + ig
    m_new = jnp.maximum(b_end + m_prev, jnp.max(gl, axis=0, keepdims=True))
    w_all = jnp.exp(gl - m_new)
    decay_all = jnp.exp(b_end + m_prev - m_new)
    r_t = r.T
    m_scr[...] = m_new

    ones_bf = jnp.ones((L, 128), BF16)
    mean_dv = jnp.full((M_DV, 128), 1.0 / M_DV, BF16)
    nt = (((1,), (1,)), ((), ()))
    tn = (((0,), (0,)), ((), ()))

    def stage_a(h):
        l = lane0 + h
        q = q_ref[:, h * M_DQK:(h + 1) * M_DQK]
        k = k_ref[:, h * M_DQK:(h + 1) * M_DQK]
        qk = lax.dot_general(q, k, nt, preferred_element_type=F32)
        qcn = jnp.dot(q, cn_scr[h].astype(BF16), preferred_element_type=F32)
        p = jnp.exp(jnp.where(order, u_all[:, l:l + 1] + r_t[l:l + 1, :], -jnp.inf))
        return qk, qcn, p

    def stage_b(h, qk, p):
        s = qk * p
        s_hi = s.astype(BF16)
        s_lo = (s - s_hi.astype(F32)).astype(BF16)
        vo = jnp.concatenate([v_ref[:, h * M_DV:(h + 1) * M_DV], ones_bf], axis=1)
        sv = jnp.dot(s_hi, vo, preferred_element_type=F32)
        return sv, jnp.dot(s_lo, ones_bf, preferred_element_type=F32)

    def stage_c(h, qcn, sv, rs_lo):
        l = lane0 + h
        a = _lanes(a_all[:, l:l + 1], 128)
        den = a * qcn[:, M_DV:] + (sv[:, M_DV:] + rs_lo)
        inv = 1.0 / jnp.maximum(jnp.abs(den), _lanes(en_all[:, l:l + 1], 128))
        cols = [slice(h * M_DV + c0, h * M_DV + c0 + 128) for c0 in range(0, M_DV, 128)]
        hid = [(a * qcn[:, c0:c0 + 128] + sv[:, c0:c0 + 128]) * inv for c0 in range(0, M_DV, 128)]
        if reverse:
            for sl, hv in zip(cols, hid):
                o_ref[:, sl] = hv
            return
        ys = [so_ref[:, sl].astype(F32) * (hv + hb_ref[:, sl]) for sl, hv in zip(cols, hid)]
        sq = jnp.concatenate([(y * y).astype(BF16) for y in ys], axis=1)
        ms = jnp.dot(sq, mean_dv, preferred_element_type=F32)
        scale = lax.rsqrt(ms + EPS)
        for sl, y in zip(cols, ys):
            o_ref[:, sl] = (y * scale * hn_ref[:, sl] * sz_ref[:, sl].astype(F32)).astype(BF16)

    def stage_d(h):
        l = lane0 + h
        k = k_ref[:, h * M_DQK:(h + 1) * M_DQK]
        kw = (k.astype(F32) * _lanes(w_all[:, l:l + 1], M_DQK)).astype(BF16)
        vo = jnp.concatenate([v_ref[:, h * M_DV:(h + 1) * M_DV], ones_bf], axis=1)
        upd = lax.dot_general(kw, vo, tn, preferred_element_type=F32)
        cn_scr[h] = decay_all[:, l:l + 1] * cn_scr[h] + upd

    sa = {}
    sb = {}
    for step in range(M_HEADS + 2):
        if step < M_HEADS:
            sa[step] = stage_a(step)
        if 1 <= step <= M_HEADS:
            h = step - 1
            sb[h] = stage_b(h, sa[h][0], sa[h][2])
        if step >= 2:
            h = step - 2
            stage_c(h, sa[h][1], *sb[h])
            stage_d(h)


def _mlstm_scan(proj, gates, hb=None, hn=None):
    reverse = hb is None
    nlc = SEQ // M_L
    ctx_blk0 = N_LAT // M_L

    def blk(b, i):
        lat = (nlc - i) if reverse else (i - 1)
        return jnp.where(i == 0, ctx_blk0 + b, b * nlc + lat)

    wide = lambda cb: pl.BlockSpec((M_L, M_V), lambda b, i: (blk(b, i), cb))
    in_specs = [
        pl.BlockSpec((M_L, M_QK), lambda b, i: (blk(b, i), 0)),
        pl.BlockSpec((M_L, M_QK), lambda b, i: (blk(b, i), 1)),
        wide(1),
        pl.BlockSpec((M_L, M_GATES), lambda b, i: (blk(b, i), 0)),
    ]
    args = [proj, proj, proj, gates]
    if not reverse:
        in_specs += [wide(0), wide(2), wide(3), pl.BlockSpec((1, M_V), lambda b, i: (0, 0))]
        args += [hb, proj, proj, hn]
    return pl.pallas_call(
        functools.partial(_mlstm_scan_kernel, reverse=reverse),
        grid=(BATCH, nlc + 1),
        in_specs=in_specs,
        out_specs=wide(0),
        out_shape=jax.ShapeDtypeStruct((N_ALL, M_V), F32 if reverse else BF16),
        scratch_shapes=[
            pltpu.VMEM((M_HEADS, M_DQK, M_DV + 128), F32),
            pltpu.VMEM((1, 128), F32),
        ],
        compiler_params=_params(("parallel", "arbitrary"), 48),
        name="mlstm_scan_bwd" if reverse else "mlstm_scan_fwd",
    )(*args)


A_QKW = A_Q + A_KVW


def _attn_qk_kernel(x_ref, g_ref, sh_ref, sc_ref, w_ref, gain_ref, cos_ref, sin_ref, q_ref, k_ref, h_scr):
    _store_modnorm(x_ref, g_ref, sc_ref, sh_ref, h_scr)
    mean_mat = jnp.full((A_HD, A_HD), 1.0 / A_HD, BF16)
    cos = cos_ref[...]
    sin = sin_ref[...]
    for c in range(A_QKW // TN):
        acc = jnp.dot(h_scr[...], w_ref[:, c * TN:(c + 1) * TN], preferred_element_type=F32)
        for hh in range(TN // A_HD):
            lo = c * TN + hh * A_HD
            a = acc[:, hh * A_HD:(hh + 1) * A_HD]
            ms = jnp.dot((a * a).astype(BF16), mean_mat, preferred_element_type=F32)
            a = a * lax.rsqrt(ms + EPS) * gain_ref[:, lo:lo + A_HD]
            a = (a * cos + pltpu.roll(a, A_HD // 2, 1) * sin).astype(BF16)
            if lo < A_Q:
                q_ref[:, lo:lo + A_HD] = a
            else:
                k_ref[:, lo - A_Q:lo - A_Q + A_HD] = a


def _attn_qk(xs, g, mods, w, gain, cos, sin):
    lat_tiles = N_LAT // TM
    rope_blk = lambda r: (jnp.where(r < lat_tiles, r % (SEQ // TM), SEQ // TM), 0)
    return pl.pallas_call(
        _attn_qk_kernel,
        grid=(N_ALL // TM,),
        in_specs=[pl.BlockSpec((TM, D), lambda r: (r, 0))] + _mod_specs(TM, lambda r: r) + [
            _resident((D, A_QKW), lambda r: (0, 0)),
            pl.BlockSpec((1, A_QKW), lambda r: (0, 0)),
            pl.BlockSpec((TM, A_HD), rope_blk),
            pl.BlockSpec((TM, A_HD), rope_blk),
        ],
        out_specs=[
            pl.BlockSpec((TM, A_Q), lambda r: (r, 0)),
            pl.BlockSpec((TM, A_KVW), lambda r: (r, 0)),
        ],
        out_shape=[
            jax.ShapeDtypeStruct((N_ALL, A_Q), BF16),
            jax.ShapeDtypeStruct((N_ALL, A_KVW), BF16),
        ],
        scratch_shapes=[pltpu.VMEM((TM, D), BF16)],
        compiler_params=_params(("parallel",), 48),
        name="attn_qk",
    )(xs, g, mods, mods, w, gain, cos, sin)


def _attn_vz_kernel(x_ref, g_ref, sh_ref, sc_ref, w_ref, v_ref, zg_ref, h_scr):
    _store_modnorm(x_ref, g_ref, sc_ref, sh_ref, h_scr)
    v_ref[...] = jnp.dot(h_scr[...], w_ref[:, :A_KVW], preferred_element_type=F32).astype(BF16)
    for c in range(A_Q // TN):
        lo = A_KVW + c * TN
        z = jnp.dot(h_scr[...], w_ref[:, lo:lo + TN], preferred_element_type=F32)
        zg_ref[:, c * TN:(c + 1) * TN] = _silu(z).astype(BF16)


def _attn_vz(xs, g, mods, w):
    return pl.pallas_call(
        _attn_vz_kernel,
        grid=(N_ALL // TM,),
        in_specs=[pl.BlockSpec((TM, D), lambda r: (r, 0))] + _mod_specs(TM, lambda r: r) + [
            _resident((D, A_KVW + A_Q), lambda r: (0, 0)),
        ],
        out_specs=[
            pl.BlockSpec((TM, A_KVW), lambda r: (r, 0)),
            pl.BlockSpec((TM, A_Q), lambda r: (r, 0)),
        ],
        out_shape=[
            jax.ShapeDtypeStruct((N_ALL, A_KVW), BF16),
            jax.ShapeDtypeStruct((N_ALL, A_Q), BF16),
        ],
        scratch_shapes=[pltpu.VMEM((TM, D), BF16)],
        compiler_params=_params(("parallel",), 48),
        name="attn_vz",
    )(xs, g, mods, mods, w)


A_LAT_TILES = SEQ // A_TQ


def _attn_kernel(q_ref, kl_ref, kc_ref, vl_ref, vc_ref, zg_ref, o_ref, vol_scr, voc_scr):
    t = pl.program_id(2)
    nt = (((1,), (1,)), ((), ()))
    heads = A_GW // A_HD

    @pl.when(t == 0)
    def _():
        vol_scr[:, :A_HD] = vl_ref[...]
        vol_scr[:, A_HD:] = jnp.ones((SEQ, A_HD), BF16)
        voc_scr[:, :A_HD] = vc_ref[...]
        voc_scr[:, A_HD:] = jnp.ones((CTX, A_HD), BF16)

    def run(with_latent_keys):
        def scores(hh):
            q = q_ref[:, hh * A_HD:(hh + 1) * A_HD]
            s_c = lax.dot_general(q, kc_ref[...], nt, preferred_element_type=F32)
            s_l = (lax.dot_general(q, kl_ref[...], nt, preferred_element_type=F32)
                   if with_latent_keys else None)
            return s_c, s_l

        def probs(s_c, s_l):
            m = jnp.max(s_c, axis=-1, keepdims=True)
            if with_latent_keys:
                m = jnp.maximum(m, jnp.max(s_l, axis=-1, keepdims=True))
            p_c = jnp.exp(s_c - m).astype(BF16)
            p_l = jnp.exp(s_l - m).astype(BF16) if with_latent_keys else None
            return p_c, p_l

        def output(hh, p_c, p_l):
            sl = slice(hh * A_HD, (hh + 1) * A_HD)
            acc = jnp.dot(p_c, voc_scr[...], preferred_element_type=F32)
            if with_latent_keys:
                acc = acc + jnp.dot(p_l, vol_scr[...], preferred_element_type=F32)
            o_ref[:, sl] = (acc[:, :A_HD] / acc[:, A_HD:] * zg_ref[:, sl].astype(F32)).astype(BF16)

        s, p = {}, {}
        for step in range(heads + 2):
            if step < heads:
                s[step] = scores(step)
            if 1 <= step <= heads:
                p[step - 1] = probs(*s[step - 1])
            if step >= 2:
                output(step - 2, *p[step - 2])

    @pl.when(t < A_LAT_TILES)
    def _():
        run(True)

    @pl.when(t >= A_LAT_TILES)
    def _():
        run(False)


def _attn(q, k, v, zg):
    ctx_blk0 = N_LAT // A_TQ
    qrow = lambda b, g, t: jnp.where(t < A_LAT_TILES, b * A_LAT_TILES + t, ctx_blk0 + b)
    return pl.pallas_call(
        _attn_kernel,
        grid=(BATCH, A_KV, A_LAT_TILES + 1),
        in_specs=[
            pl.BlockSpec((A_TQ, A_GW), lambda b, g, t: (qrow(b, g, t), g)),
            pl.BlockSpec((SEQ, A_HD), lambda b, g, t: (b, g)),
            pl.BlockSpec((CTX, A_HD), lambda b, g, t: (N_LAT // CTX + b, g)),
            pl.BlockSpec((SEQ, A_HD), lambda b, g, t: (b, g)),
            pl.BlockSpec((CTX, A_HD), lambda b, g, t: (N_LAT // CTX + b, g)),
            pl.BlockSpec((A_TQ, A_GW), lambda b, g, t: (qrow(b, g, t), g)),
        ],
        out_specs=pl.BlockSpec((A_TQ, A_GW), lambda b, g, t: (qrow(b, g, t), g)),
        out_shape=jax.ShapeDtypeStruct((N_ALL, A_Q), BF16),
        scratch_shapes=[pltpu.VMEM((SEQ, 2 * A_HD), BF16), pltpu.VMEM((CTX, 2 * A_HD), BF16)],
        compiler_params=_params(("parallel", "parallel", "arbitrary"), 48),
        name="attn",
    )(q, k, k, v, v, zg)


def _dft_tables(n, scale):
    idx = jnp.arange(n, dtype=jnp.int32)
    ang = ((idx[:, None] * idx[None, :]) % n).astype(F32) * (2.0 * math.pi / n)
    return (jnp.cos(ang) * scale).astype(BF16), (jnp.sin(ang) * scale).astype(BF16)


def _dif_tables():
    jt = jnp.arange(F_M, dtype=jnp.int32)
    r = jnp.arange(F_R, dtype=jnp.int32)
    k = ((F_R * jt[None, :, None] + r[:, None, None]) * jt[None, None, :]) % SEQ
    ang = k.astype(F32) * (2.0 * math.pi / SEQ)
    return (jnp.cos(ang) * SEQ ** -0.5).astype(BF16), (jnp.sin(ang) * SEQ ** -0.5).astype(BF16)


def _perm_tables():
    n = jnp.arange(F_TC)
    src = F_R * (n % F_PR) + n // F_PR
    perm = (src[:, None] == n[None, :]).astype(BF16)
    return perm, perm.T


def _rope_tables():
    t = jnp.arange(SEQ)
    freqs = ROPE_THETA ** (-jnp.arange(0, A_HD // 2, 2, dtype=F32) / (A_HD // 2))
    ang = jnp.concatenate([(t // GRID_W).astype(F32)[:, None] * freqs,
                           (t % GRID_W).astype(F32)[:, None] * freqs], axis=-1)
    cos = jnp.concatenate([jnp.cos(ang), jnp.cos(ang)], axis=-1)
    sin = jnp.concatenate([-jnp.sin(ang), jnp.sin(ang)], axis=-1)
    pad = lambda a, fill: jnp.concatenate([a, jnp.full((TM, A_HD), fill, F32)], axis=0)
    return pad(cos, 1.0), pad(sin, 0.0)


def _split_heads_even_odd(w, heads):
    perm = jnp.concatenate([jnp.arange(0, A_HD, 2), jnp.arange(1, A_HD, 2)])
    lead = w.shape[:-1]
    return w.reshape(lead + (heads, A_HD))[..., perm].reshape(lead + (heads * A_HD,))


def kernel(x, c, ctx, c_ctx, ada_w, ada_b, norm_g, fnet_w_gate, fnet_w_out, mlstm_w_in, mlstm_b_gate,
           mlstm_hn, mlstm_w_out, attn_w_in, attn_qn, attn_kn, attn_w_out, final_g):
    cc = jnp.concatenate([c, c_ctx[None, :], jnp.zeros((MOD_ROWS - BATCH - 1, D), F32)], axis=0)
    mods_all = _modvec(cc, ada_w, ada_b).reshape(DEPTH, MOD_ROWS, 1, 3 * D)

    cc_c, sc_c = _dft_tables(F_GW, F_GW ** -0.5)
    cs_chan = jnp.concatenate([cc_c, sc_c], axis=1)
    cr_lat, sr_lat = _dif_tables()
    ct_ctx, st_ctx = _dft_tables(CTX, CTX ** -0.5)
    perm, unperm = _perm_tables()

    xs = None
    for i in range(DEPTH):
        kind, j = i % 3, i // 3
        last = i == DEPTH - 1
        mods = mods_all[i]
        g = norm_g[i].reshape(1, D)
        n_rows = N_LAT if last else N_ALL
        if kind == 0:
            assert i == 0 or last
            wg = fnet_w_gate[j].astype(BF16)
            w_out = fnet_w_out[j].astype(BF16)
            x_lat = x.reshape(N_LAT, D) if i == 0 else xs
            p, q, hp = _fnet_chan_dif(x_lat, g, mods, cs_chan, perm)
            m_lat = _fnet_mix_lat(hp, cr_lat, sr_lat, p, q, wg)
            if i == 0:
                x_ctx = ctx.reshape(N_CTX, D)
                a, b = _fnet_chan_ctx(x_ctx, 0, g, mods, cs_chan)
                m_ctx = _fnet_mix_ctx(x_ctx, 0, g, mods, ct_ctx, st_ctx, a, b, wg)
                xs = _outproj_join(m_lat, m_ctx, w_out, x_lat, x_ctx, mods, unperm)
            else:
                out = _outproj(m_lat, w_out, xs, mods, n_rows, (unperm, final_g.reshape(1, D)))
            continue
        if kind == 1:
            w_in = mlstm_w_in[j]
            n_main = 2 * M_QK + 2 * M_V
            w_main = jnp.concatenate([w_in[:, :n_main], w_in[:, n_main + 4 * M_HEADS:]],
                                     axis=1).astype(BF16)
            gate_tiles = lambda gcols: jnp.pad(
                jnp.concatenate([gcols[..., 0:8], gcols[..., 16:24]], axis=-1),
                [(0, 0)] * (gcols.ndim - 1) + [(0, 128 - 2 * M_HEADS)])
            gate_layout = lambda gcols: jnp.concatenate(
                [gate_tiles(gcols), gate_tiles(gcols[..., M_HEADS:])], axis=-1)
            w_gates = gate_layout(w_in[:, n_main:n_main + 4 * M_HEADS]).astype(BF16)
            b_gates = gate_layout(mlstm_b_gate[j][None, :])
            proj, gates = _mlstm_proj(xs, g, mods, w_main, w_gates, b_gates)
            hb = _mlstm_scan(proj, gates)
            mbuf = _mlstm_scan(proj, gates, hb, mlstm_hn[j].reshape(1, M_V))
            w_out = mlstm_w_out[j]
        else:
            w_in = attn_w_in[j]
            w_qk = jnp.concatenate([_split_heads_even_odd(w_in[:, :A_Q], A_HEADS),
                                    _split_heads_even_odd(w_in[:, A_Q:A_QKW], A_KV)], axis=1)
            qn = _split_heads_even_odd(attn_qn[j], 1) * (A_HD ** -0.5)
            kn = _split_heads_even_odd(attn_kn[j], 1)
            gain = jnp.concatenate([jnp.tile(qn, A_HEADS), jnp.tile(kn, A_KV)]).reshape(1, A_QKW)
            cos, sin = _rope_tables()
            q, k = _attn_qk(xs, g, mods, w_qk.astype(BF16), gain, cos, sin)
            v, zg = _attn_vz(xs, g, mods, w_in[:, A_QKW:].astype(BF16))
            mbuf = _attn(q, k, v, zg)
            w_out = attn_w_out[j]
        assert not last
        xs = _outproj(mbuf, w_out.astype(BF16), xs, mods, n_rows)

    return out.reshape(BATCH, SEQ, D)
```

```python
import functools
import math

import jax
import jax.numpy as jnp
from jax import lax
from jax.experimental import pallas as pl
from jax.experimental.pallas import tpu as pltpu

F32 = jnp.float32
BF16 = jnp.bfloat16

D = 2048
BATCH = 16
SEQ = 2048
CTX = 256
DEPTH = 4
EPS = 1e-6
N_LAT = BATCH * SEQ
N_CTX = BATCH * CTX
N_ALL = N_LAT + N_CTX
MOD_ROWS = 24
CTX_MOD_ROW = BATCH

F_GROUPS = 4
F_GW = D // F_GROUPS

M_HEADS = 8
M_DQK = 128
M_DV = 256
M_QK = M_HEADS * M_DQK
M_V = M_HEADS * M_DV
M_L = 256
M_GATES = 256
M_HALF = 2 * M_QK + M_V

A_HEADS = 16
A_KV = 4
A_HD = 128
A_Q = A_HEADS * A_HD
A_KVW = A_KV * A_HD
A_GW = A_Q // A_KV
A_TQ = 256
GRID_W = 64
ROPE_THETA = 10000.0

TM = 512
TN = 512
MIB = 1024 * 1024


def _params(sem, vmem_mib):
    return pltpu.CompilerParams(dimension_semantics=sem, vmem_limit_bytes=vmem_mib * MIB)


def _resident(shape, index_map):
    return pl.BlockSpec(shape, index_map, pipeline_mode=pl.Buffered(1))


def _sigmoid(x):
    return 1.0 / (1.0 + jnp.exp(-x))


def _silu(x):
    return x * _sigmoid(x)


def _log_sigmoid(x):
    return jnp.minimum(x, 0.0) - jnp.log1p(jnp.exp(-jnp.abs(x)))


def _modnorm(x, g, scale, shift):
    ms = jnp.mean(x * x, axis=-1, keepdims=True)
    y = x * lax.rsqrt(ms + EPS) * g
    return y * (1.0 + scale) + shift


def _store_modnorm(x_ref, g_ref, sc_ref, sh_ref, h_scr):
    rows = x_ref.shape[0]
    step = min(rows, 256)
    for r0 in range(0, rows, step):
        h_scr[r0:r0 + step, :] = _modnorm(x_ref[r0:r0 + step, :], g_ref[...], sc_ref[...],
                                          sh_ref[...]).astype(BF16)


def _mod_row(r, tm):
    return jnp.where(r < N_LAT // tm, r // (SEQ // tm), CTX_MOD_ROW)


def _modvec_kernel(c_ref, w_ref, b_ref, o_ref):
    s = _silu(c_ref[...])
    o_ref[...] = jnp.dot(s, w_ref[...], preferred_element_type=F32,
                         precision=lax.Precision.HIGHEST) + b_ref[...]


def _modvec(cc, ada_w, ada_b):
    tn = 1024
    return pl.pallas_call(
        _modvec_kernel,
        grid=(DEPTH, 3 * D // tn),
        in_specs=[
            pl.BlockSpec((MOD_ROWS, D), lambda i, j: (0, 0)),
            pl.BlockSpec((None, D, tn), lambda i, j: (i, 0, j)),
            pl.BlockSpec((None, 1, tn), lambda i, j: (i, 0, j)),
        ],
        out_specs=pl.BlockSpec((None, MOD_ROWS, tn), lambda i, j: (i, 0, j)),
        out_shape=jax.ShapeDtypeStruct((DEPTH, MOD_ROWS, 3 * D), F32),
        compiler_params=_params(("parallel", "parallel"), 40),
        name="modvec",
    )(cc, ada_w, ada_b.reshape(DEPTH, 1, 3 * D))


def _fnet_chan_kernel(x_ref, g_ref, sh_ref, sc_ref, cs_ref, a_ref, b_ref):
    h = _modnorm(x_ref[...], g_ref[...], sc_ref[...], sh_ref[...]).astype(BF16)
    for grp in range(F_GROUPS):
        sl = slice(grp * F_GW, (grp + 1) * F_GW)
        p = jnp.dot(h[:, sl], cs_ref[...], preferred_element_type=F32)
        a_ref[:, sl] = p[:, :F_GW].astype(BF16)
        b_ref[:, sl] = p[:, F_GW:].astype(BF16)


def _fnet_chan_ctx(xc, row_blk0, g, mods, cs_c):
    tm = 512
    out = jax.ShapeDtypeStruct((N_CTX, D), BF16)
    mod = lambda chunk: pl.BlockSpec((None, 1, D), lambda r: (CTX_MOD_ROW, 0, chunk))
    return pl.pallas_call(
        _fnet_chan_kernel,
        grid=(N_CTX // tm,),
        in_specs=[pl.BlockSpec((tm, D), lambda r: (row_blk0 + r, 0)),
                  pl.BlockSpec((1, D), lambda r: (0, 0)), mod(0), mod(1),
                  pl.BlockSpec((F_GW, 2 * F_GW), lambda r: (0, 0))],
        out_specs=[pl.BlockSpec((tm, D), lambda r: (r, 0))] * 2,
        out_shape=[out, out],
        compiler_params=_params(("parallel",), 48),
        name="fnet_chan_ctx",
    )(xc, g, mods, mods, cs_c)


F_R = 4
F_M = SEQ // F_R
F_TC = 256
F_PR = F_TC // F_R


def _fnet_chan_dif_kernel(x0_ref, x1_ref, x2_ref, x3_ref, *rest, prenormed):
    if prenormed:
        cs_ref, perm_ref, p_ref, q_ref, hp_ref = rest
        hs = [x[...] for x in (x0_ref, x1_ref, x2_ref, x3_ref)]
    else:
        g_ref, sh_ref, sc_ref, cs_ref, perm_ref, p_ref, q_ref, hp_ref = rest
        hs = [_modnorm(x[...], g_ref[...], sc_ref[...], sh_ref[...]).astype(BF16)
              for x in (x0_ref, x1_ref, x2_ref, x3_ref)]
    for qi, h in enumerate(hs):
        hp = jnp.dot(perm_ref[...], h, preferred_element_type=F32).astype(BF16)
        for r in range(F_R):
            hp_ref[qi, r] = hp[r * F_PR:(r + 1) * F_PR, :]
    for grp in range(F_GROUPS):
        sl = slice(grp * F_GW, (grp + 1) * F_GW)
        ab = [jnp.dot(h[:, sl], cs_ref[...], preferred_element_type=F32) for h in hs]
        a = [t[:, :F_GW] for t in ab]
        b = [t[:, F_GW:] for t in ab]
        sa02, da02, sa13, da13 = a[0] + a[2], a[0] - a[2], a[1] + a[3], a[1] - a[3]
        sb02, db02, sb13, db13 = b[0] + b[2], b[0] - b[2], b[1] + b[3], b[1] - b[3]
        re = (sa02 + sa13, da02 - db13, sa02 - sa13, da02 + db13)
        im = (sb02 + sb13, db02 + da13, sb02 - sb13, db02 - da13)
        for r in range(F_R):
            p_ref[r, :, sl] = re[r].astype(BF16)
            q_ref[r, :, sl] = im[r].astype(BF16)


def _fnet_chan_dif(xl, norm, cs_c, perm):
    nt = F_M // F_TC
    xspec = lambda q: pl.BlockSpec((F_TC, D), lambda b, i: (b * (SEQ // F_TC) + q * nt + i, 0))
    mod = lambda chunk: pl.BlockSpec((None, 1, D), lambda b, i: (b, 0, chunk))
    out = jax.ShapeDtypeStruct((BATCH, F_R, F_M, D), BF16)
    ospec = pl.BlockSpec((None, F_R, F_TC, D), lambda b, i: (b, 0, i, 0))
    norm_specs = [] if norm is None else [pl.BlockSpec((1, D), lambda b, i: (0, 0)), mod(0), mod(1)]
    norm_args = [] if norm is None else [norm[0], norm[1], norm[1]]
    p, q, hp = pl.pallas_call(
        functools.partial(_fnet_chan_dif_kernel, prenormed=norm is None),
        grid=(BATCH, nt),
        in_specs=[xspec(0), xspec(1), xspec(2), xspec(3)] + norm_specs + [
            pl.BlockSpec((F_GW, 2 * F_GW), lambda b, i: (0, 0)),
            pl.BlockSpec((F_TC, F_TC), lambda b, i: (0, 0))],
        out_specs=[ospec, ospec,
                   pl.BlockSpec((None, F_R, None, F_R, F_PR, D), lambda b, i: (b, 0, i, 0, 0, 0))],
        out_shape=[out, out, jax.ShapeDtypeStruct((BATCH, F_R, nt, F_R, F_PR, D), BF16)],
        compiler_params=_params(("parallel", "parallel"), 56),
        name="fnet_chan_dif",
    )(xl, xl, xl, xl, *norm_args, cs_c, perm)
    return p, q, hp.reshape(BATCH, SEQ // F_TC, F_R, F_PR, D)


def _fnet_mix_kernel(x_ref, g_ref, sh_ref, sc_ref, c_ref, s_ref, p_ref, q_ref, wg_ref, o_ref, h_scr):
    _store_modnorm(x_ref, g_ref, sc_ref, sh_ref, h_scr)
    for c in range(D // TN):
        sl = slice(c * TN, (c + 1) * TN)
        y = jnp.dot(c_ref[...], p_ref[:, sl], preferred_element_type=F32)
        y = y - jnp.dot(s_ref[...], q_ref[:, sl], preferred_element_type=F32)
        gate = jnp.dot(h_scr[...], wg_ref[:, sl], preferred_element_type=F32)
        o_ref[:, sl] = (y * _silu(gate)).astype(BF16)


def _fnet_mix_lat_kernel(h_ref, c_ref, s_ref, p_ref, q_ref, wg_ref, o_ref):
    nt = SEQ // F_TC
    h = h_ref[...].reshape(F_M, D)
    for c in range(D // TN):
        sl = slice(c * TN, (c + 1) * TN)
        y = jnp.dot(c_ref[...], p_ref[:, sl], preferred_element_type=F32)
        y = y - jnp.dot(s_ref[...], q_ref[:, sl], preferred_element_type=F32)
        gate = jnp.dot(h, wg_ref[:, sl], preferred_element_type=F32)
        val = (y * _silu(gate)).astype(BF16)
        for t in range(nt):
            o_ref[t, :, sl] = val[t * F_PR:(t + 1) * F_PR, :]


def _fnet_mix_lat(hp, cr, sr, p, q, wg):
    nt = SEQ // F_TC
    tab = pl.BlockSpec((None, F_M, F_M), lambda b, r: (r, 0, 0))
    pq = pl.BlockSpec((None, None, F_M, D), lambda b, r: (b, r, 0, 0))
    tiles = pl.BlockSpec((None, nt, None, F_PR, D), lambda b, r: (b, 0, r, 0, 0))
    out = pl.pallas_call(
        _fnet_mix_lat_kernel,
        grid=(BATCH, F_R),
        in_specs=[tiles, tab, tab, pq, pq, _resident((D, D), lambda b, r: (0, 0))],
        out_specs=tiles,
        out_shape=jax.ShapeDtypeStruct((BATCH, nt, F_R, F_PR, D), BF16),
        compiler_params=_params(("parallel", "parallel"), 48),
        name="fnet_mix_lat",
    )(hp, cr, sr, p, q, wg)
    return out.reshape(N_LAT, D)


def _fnet_mix_ctx(xc, row_blk0, g, mods, ct, st, a, b, wg):
    mod = lambda chunk: pl.BlockSpec((None, 1, D), lambda bi: (CTX_MOD_ROW, 0, chunk))
    tab = pl.BlockSpec((CTX, CTX), lambda bi: (0, 0))
    ab = pl.BlockSpec((CTX, D), lambda bi: (bi, 0))
    return pl.pallas_call(
        _fnet_mix_kernel,
        grid=(BATCH,),
        in_specs=[pl.BlockSpec((CTX, D), lambda bi: (row_blk0 + bi, 0)),
                  pl.BlockSpec((1, D), lambda bi: (0, 0)), mod(0), mod(1),
                  tab, tab, ab, ab, _resident((D, D), lambda bi: (0, 0))],
        out_specs=pl.BlockSpec((CTX, D), lambda bi: (bi, 0)),
        out_shape=jax.ShapeDtypeStruct((N_CTX, D), BF16),
        scratch_shapes=[pltpu.VMEM((CTX, D), BF16)],
        compiler_params=_params(("parallel",), 48),
        name="fnet_mix_ctx",
    )(xc, g, mods, mods, ct, st, a, b, wg)


def _residual_update(m_ref, w_ref, x_ref, gate_ref, o_ref, unperm_ref, m_scr, nxt):
    if unperm_ref is not None:
        for t in range(TM // F_TC):
            rows = slice(t * F_TC, (t + 1) * F_TC)
            m_scr[rows, :] = jnp.dot(unperm_ref[...], m_ref[rows, :],
                                     preferred_element_type=F32).astype(BF16)
        m_ref = m_scr
    ss = 0.0
    for c in range(D // TN):
        sl = slice(c * TN, (c + 1) * TN)
        y = jnp.dot(m_ref[...], w_ref[:, sl], preferred_element_type=F32)
        o = x_ref[:, sl] + gate_ref[:, sl] * y
        o_ref[:, sl] = o
        if nxt is not None:
            ss = ss + jnp.sum(o * o, axis=-1, keepdims=True)
    if nxt is not None:
        gn_ref, shn_ref, scn_ref, h_ref = nxt
        rinv = lax.rsqrt(ss * (1.0 / D) + EPS)
        for c in range(D // TN):
            sl = slice(c * TN, (c + 1) * TN)
            gs = gn_ref[:, sl] * (1.0 + scn_ref[:, sl])
            h_ref[:, sl] = (o_ref[:, sl] * rinv * gs + shn_ref[:, sl]).astype(BF16)


def _outproj_kernel(m_ref, w_ref, x_ref, gate_ref, *rest, final):
    if final:
        unperm_ref, fg_ref, o_ref, m_scr = rest
        _residual_update(m_ref, w_ref, x_ref, gate_ref, o_ref, unperm_ref, m_scr, None)
        x = o_ref[...]
        ms = jnp.mean(x * x, axis=-1, keepdims=True)
        o_ref[...] = x * lax.rsqrt(ms + EPS) * fg_ref[...]
    else:
        gn_ref, shn_ref, scn_ref, o_ref, h_ref = rest
        _residual_update(m_ref, w_ref, x_ref, gate_ref, o_ref, None, None,
                         (gn_ref, shn_ref, scn_ref, h_ref))


def _next_norm_specs():
    return [pl.BlockSpec((1, D), lambda r: (0, 0)),
            pl.BlockSpec((None, 1, D), lambda r: (_mod_row(r, TM), 0, 0)),
            pl.BlockSpec((None, 1, D), lambda r: (_mod_row(r, TM), 0, 1))]


def _outproj(mbuf, w, xs, mods, n_rows, final=None, nxt=None):
    in_specs = [
        pl.BlockSpec((TM, D), lambda r: (r, 0)),
        _resident((D, D), lambda r: (0, 0)),
        pl.BlockSpec((TM, D), lambda r: (r, 0)),
        pl.BlockSpec((None, 1, D), lambda r: (_mod_row(r, TM), 0, 2)),
    ]
    args = [mbuf, w, xs, mods]
    tile = pl.BlockSpec((TM, D), lambda r: (r, 0))
    if final is not None:
        in_specs += [pl.BlockSpec((F_TC, F_TC), lambda r: (0, 0)), pl.BlockSpec((1, D), lambda r: (0, 0))]
        args += list(final)
        return pl.pallas_call(
            functools.partial(_outproj_kernel, final=True),
            grid=(n_rows // TM,),
            in_specs=in_specs,
            out_specs=tile,
            out_shape=jax.ShapeDtypeStruct((n_rows, D), F32),
            scratch_shapes=[pltpu.VMEM((TM, D), BF16)],
            compiler_params=_params(("parallel",), 48),
            name="outproj_final",
        )(*args)
    g_next, mods_next = nxt
    return pl.pallas_call(
        functools.partial(_outproj_kernel, final=False),
        grid=(n_rows // TM,),
        in_specs=in_specs + _next_norm_specs(),
        out_specs=[tile, tile],
        out_shape=[jax.ShapeDtypeStruct((N_ALL, D), F32), jax.ShapeDtypeStruct((N_ALL, D), BF16)],
        input_output_aliases={2: 0},
        compiler_params=_params(("parallel",), 48),
        name="outproj",
    )(*args, g_next, mods_next, mods_next)


def _outproj_join_kernel(ml_ref, mc_ref, w_ref, xl_ref, xc_ref, gate_ref, unperm_ref,
                         gn_ref, shn_ref, scn_ref, o_ref, h_ref, m_scr):
    is_lat = pl.program_id(0) < N_LAT // TM
    nxt = (gn_ref, shn_ref, scn_ref, h_ref)

    @pl.when(is_lat)
    def _():
        _residual_update(ml_ref, w_ref, xl_ref, gate_ref, o_ref, unperm_ref, m_scr, nxt)

    @pl.when(jnp.logical_not(is_lat))
    def _():
        _residual_update(mc_ref, w_ref, xc_ref, gate_ref, o_ref, None, None, nxt)


def _outproj_join(m_lat, m_ctx, w, x_lat, x_ctx, mods, unperm, nxt):
    nl = N_LAT // TM
    lat = pl.BlockSpec((TM, D), lambda r: (jnp.minimum(r, nl - 1), 0))
    cxt = pl.BlockSpec((TM, D), lambda r: (jnp.maximum(r - nl, 0), 0))
    tile = pl.BlockSpec((TM, D), lambda r: (r, 0))
    g_next, mods_next = nxt
    return pl.pallas_call(
        _outproj_join_kernel,
        grid=(N_ALL // TM,),
        in_specs=[lat, cxt, _resident((D, D), lambda r: (0, 0)), lat, cxt,
                  pl.BlockSpec((None, 1, D), lambda r: (_mod_row(r, TM), 0, 2)),
                  pl.BlockSpec((F_TC, F_TC), lambda r: (0, 0))] + _next_norm_specs(),
        out_specs=[tile, tile],
        out_shape=[jax.ShapeDtypeStruct((N_ALL, D), F32), jax.ShapeDtypeStruct((N_ALL, D), BF16)],
        scratch_shapes=[pltpu.VMEM((TM, D), BF16)],
        compiler_params=_params(("parallel",), 48),
        name="outproj_join",
    )(m_lat, m_ctx, w, x_lat, x_ctx, mods, unperm, g_next, mods_next, mods_next)


def _mlstm_proj_kernel(h_ref, w_ref, wg_ref, bg_ref, out_ref, gt_ref):
    half = pl.program_id(0)

    def chunks(post):
        for c in range(M_HALF // TN):
            sl = slice(c * TN, (c + 1) * TN)
            acc = jnp.dot(h_ref[...], w_ref[:, sl], preferred_element_type=F32)
            out_ref[:, sl] = post(c, acc).astype(BF16)

    @pl.when(half == 0)
    def _():
        gt_ref[...] = jnp.dot(h_ref[...], wg_ref[...], preferred_element_type=F32) + bg_ref[...]
        chunks(lambda c, acc: acc * (M_DQK ** -0.5) if c < M_QK // TN else acc)

    @pl.when(half == 1)
    def _():
        chunks(lambda c, acc: _sigmoid(acc) if c < M_V // TN else _silu(acc))


def _mlstm_proj(h, w, wg, bg):
    nr = N_ALL // TM
    return pl.pallas_call(
        _mlstm_proj_kernel,
        grid=(2, nr),
        in_specs=[
            pl.BlockSpec((TM, D), lambda hf, r: (r, 0)),
            _resident((D, M_HALF), lambda hf, r: (0, hf)),
            pl.BlockSpec((D, M_GATES), lambda hf, r: (0, 0)),
            pl.BlockSpec((1, M_GATES), lambda hf, r: (0, 0)),
        ],
        out_specs=[
            pl.BlockSpec((TM, M_HALF), lambda hf, r: (r, hf)),
            pl.BlockSpec((TM, M_GATES), lambda hf, r: (jnp.where(hf == 0, r, nr - 1), 0)),
        ],
        out_shape=[
            jax.ShapeDtypeStruct((N_ALL, 2 * M_HALF), BF16),
            jax.ShapeDtypeStruct((N_ALL, M_GATES), F32),
        ],
        compiler_params=_params(("arbitrary", "arbitrary"), 56),
        name="mlstm_proj",
    )(h, w, wg, bg)


def _split3(x):
    hi = x.astype(BF16)
    r1 = x - hi.astype(F32)
    mid = r1.astype(BF16)
    lo = (r1 - mid.astype(F32)).astype(BF16)
    return hi, mid, lo


def _cummax_rows(x, reverse):
    rows = x.shape[0]
    row = lax.broadcasted_iota(jnp.int32, x.shape, 0)
    sh = 1
    while sh < rows:
        if reverse:
            x = jnp.where(row < rows - sh, jnp.maximum(x, pltpu.roll(x, rows - sh, 0)), x)
        else:
            x = jnp.where(row >= sh, jnp.maximum(x, pltpu.roll(x, sh, 0)), x)
        sh *= 2
    return x


def _lanes(col, width):
    return jnp.broadcast_to(col, (col.shape[0], width))


def _mlstm_scan_kernel(q_ref, k_ref, v_ref, gt_ref, *rest, reverse):
    if reverse:
        o_ref, cn_scr, m_scr = rest
    else:
        hb_ref, so_ref, sz_ref, hn_ref, o_ref, cn_scr, m_scr = rest

    @pl.when(pl.program_id(1) == 0)
    def _():
        cn_scr[...] = jnp.zeros_like(cn_scr)
        m_scr[...] = jnp.zeros_like(m_scr)

    L = M_L
    lane0 = M_HEADS if reverse else 0
    ig = gt_ref[:, :128]
    ls = _log_sigmoid(gt_ref[:, 128:])
    row = lax.broadcasted_iota(jnp.int32, (L, L), 0)
    col = lax.broadcasted_iota(jnp.int32, (L, L), 1)
    order = (row <= col) if reverse else (row >= col)
    tri = jnp.where(order, 1.0, 0.0).astype(BF16)
    hi, mid, lo = _split3(ls)
    b = (jnp.dot(tri, hi, preferred_element_type=F32)
         + jnp.dot(tri, mid, preferred_element_type=F32)
         + jnp.dot(tri, lo, preferred_element_type=F32))
    end = 0 if reverse else L - 1

    m_prev = m_scr[...]
    r = ig - b
    inter = b + m_prev
    m_t = jnp.maximum(inter, b + _cummax_rows(r, reverse))
    a_all = jnp.exp(inter - m_t)
    u_all = b - m_t
    en_all = jnp.exp(-m_t)
    b_end = b[end:end + 1, :]
    gl = b_end - b + ig
    m_new = jnp.maximum(b_end + m_prev, jnp.max(gl, axis=0, keepdims=True))
    w_all = jnp.exp(gl - m_new)
    decay_all = jnp.exp(b_end + m_prev - m_new)
    r_t = r.T
    m_scr[...] = m_new

    ones_bf = jnp.ones((L, 128), BF16)
    mean_dv = jnp.full((M_DV, 128), 1.0 / M_DV, BF16)
    nt = (((1,), (1,)), ((), ()))
    tn = (((0,), (0,)), ((), ()))

    def stage_a(h):
        l = lane0 + h
        q = q_ref[:, h * M_DQK:(h + 1) * M_DQK]
        k = k_ref[:, h * M_DQK:(h + 1) * M_DQK]
        qk = lax.dot_general(q, k, nt, preferred_element_type=F32)
        qcn = jnp.dot(q, cn_scr[h].astype(BF16), preferred_element_type=F32)
        p = jnp.exp(jnp.where(order, u_all[:, l:l + 1] + r_t[l:l + 1, :], -jnp.inf))
        return qk, qcn, p

    def stage_b(h, qk, p):
        s = qk * p
        s_hi = s.astype(BF16)
        s_lo = (s - s_hi.astype(F32)).astype(BF16)
        vo = jnp.concatenate([v_ref[:, h * M_DV:(h + 1) * M_DV], ones_bf], axis=1)
        sv = jnp.dot(s_hi, vo, preferred_element_type=F32)
        return sv, jnp.dot(s_lo, ones_bf, preferred_element_type=F32)

    def stage_c(h, qcn, sv, rs_lo):
        l = lane0 + h
        a = _lanes(a_all[:, l:l + 1], 128)
        den = a * qcn[:, M_DV:] + (sv[:, M_DV:] + rs_lo)
        inv = 1.0 / jnp.maximum(jnp.abs(den), _lanes(en_all[:, l:l + 1], 128))
        cols = [slice(h * M_DV + c0, h * M_DV + c0 + 128) for c0 in range(0, M_DV, 128)]
        hid = [(a * qcn[:, c0:c0 + 128] + sv[:, c0:c0 + 128]) * inv for c0 in range(0, M_DV, 128)]
        if reverse:
            for sl, hv in zip(cols, hid):
                o_ref[:, sl] = hv
            return
        ys = [so_ref[:, sl].astype(F32) * (hv + hb_ref[:, sl]) for sl, hv in zip(cols, hid)]
        sq = jnp.concatenate([(y * y).astype(BF16) for y in ys], axis=1)
        ms = jnp.dot(sq, mean_dv, preferred_element_type=F32)
        scale = lax.rsqrt(ms + EPS)
        for sl, y in zip(cols, ys):
            o_ref[:, sl] = (y * scale * hn_ref[:, sl] * sz_ref[:, sl].astype(F32)).astype(BF16)

    def stage_d(h):
        l = lane0 + h
        k = k_ref[:, h * M_DQK:(h + 1) * M_DQK]
        kw = (k.astype(F32) * _lanes(w_all[:, l:l + 1], M_DQK)).astype(BF16)
        vo = jnp.concatenate([v_ref[:, h * M_DV:(h + 1) * M_DV], ones_bf], axis=1)
        upd = lax.dot_general(kw, vo, tn, preferred_element_type=F32)
        cn_scr[h] = decay_all[:, l:l + 1] * cn_scr[h] + upd

    sa = {}
    sb = {}
    for step in range(M_HEADS + 2):
        if step < M_HEADS:
            sa[step] = stage_a(step)
        if 1 <= step <= M_HEADS:
            h = step - 1
            sb[h] = stage_b(h, sa[h][0], sa[h][2])
        if step >= 2:
            h = step - 2
            stage_c(h, sa[h][1], *sb[h])
            stage_d(h)


def _mlstm_scan(proj, gates, hb=None, hn=None):
    reverse = hb is None
    nlc = SEQ // M_L
    ctx_blk0 = N_LAT // M_L

    def blk(b, i):
        lat = (nlc - i) if reverse else (i - 1)
        return jnp.where(i == 0, ctx_blk0 + b, b * nlc + lat)

    wide = lambda cb: pl.BlockSpec((M_L, M_V), lambda b, i: (blk(b, i), cb))
    in_specs = [
        pl.BlockSpec((M_L, M_QK), lambda b, i: (blk(b, i), 0)),
        pl.BlockSpec((M_L, M_QK), lambda b, i: (blk(b, i), 1)),
        wide(1),
        pl.BlockSpec((M_L, M_GATES), lambda b, i: (blk(b, i), 0)),
    ]
    args = [proj, proj, proj, gates]
    if not reverse:
        in_specs += [wide(0), wide(2), wide(3), pl.BlockSpec((1, M_V), lambda b, i: (0, 0))]
        args += [hb, proj, proj, hn]
    return pl.pallas_call(
        functools.partial(_mlstm_scan_kernel, reverse=reverse),
        grid=(BATCH, nlc + 1),
        in_specs=in_specs,
        out_specs=wide(0),
        out_shape=jax.ShapeDtypeStruct((N_ALL, M_V), F32 if reverse else BF16),
        scratch_shapes=[
            pltpu.VMEM((M_HEADS, M_DQK, M_DV + 128), F32),
            pltpu.VMEM((1, 128), F32),
        ],
        compiler_params=_params(("parallel", "arbitrary"), 48),
        name="mlstm_scan_bwd" if reverse else "mlstm_scan_fwd",
    )(*args)


A_QKW = A_Q + A_KVW


def _attn_qk_kernel(h_ref, w_ref, gain_ref, cos_ref, sin_ref, q_ref, k_ref):
    mean_mat = jnp.full((A_HD, A_HD), 1.0 / A_HD, BF16)
    cos = cos_ref[...]
    sin = sin_ref[...]
    for c in range(A_QKW // TN):
        acc = jnp.dot(h_ref[...], w_ref[:, c * TN:(c + 1) * TN], preferred_element_type=F32)
        for hh in range(TN // A_HD):
            lo = c * TN + hh * A_HD
            a = acc[:, hh * A_HD:(hh + 1) * A_HD]
            ms = jnp.dot((a * a).astype(BF16), mean_mat, preferred_element_type=F32)
            a = a * lax.rsqrt(ms + EPS) * gain_ref[:, lo:lo + A_HD]
            a = (a * cos + pltpu.roll(a, A_HD // 2, 1) * sin).astype(BF16)
            if lo < A_Q:
                q_ref[:, lo:lo + A_HD] = a
            else:
                k_ref[:, lo - A_Q:lo - A_Q + A_HD] = a


def _attn_qk(h, w, gain, cos, sin):
    lat_tiles = N_LAT // TM
    rope_blk = lambda r: (jnp.where(r < lat_tiles, r % (SEQ // TM), SEQ // TM), 0)
    return pl.pallas_call(
        _attn_qk_kernel,
        grid=(N_ALL // TM,),
        in_specs=[
            pl.BlockSpec((TM, D), lambda r: (r, 0)),
            _resident((D, A_QKW), lambda r: (0, 0)),
            pl.BlockSpec((1, A_QKW), lambda r: (0, 0)),
            pl.BlockSpec((TM, A_HD), rope_blk),
            pl.BlockSpec((TM, A_HD), rope_blk),
        ],
        out_specs=[
            pl.BlockSpec((TM, A_Q), lambda r: (r, 0)),
            pl.BlockSpec((TM, A_KVW), lambda r: (r, 0)),
        ],
        out_shape=[
            jax.ShapeDtypeStruct((N_ALL, A_Q), BF16),
            jax.ShapeDtypeStruct((N_ALL, A_KVW), BF16),
        ],
        compiler_params=_params(("parallel",), 48),
        name="attn_qk",
    )(h, w, gain, cos, sin)


def _attn_vz_kernel(h_ref, w_ref, v_ref, zg_ref):
    v_ref[...] = jnp.dot(h_ref[...], w_ref[:, :A_KVW], preferred_element_type=F32).astype(BF16)
    for c in range(A_Q // TN):
        lo = A_KVW + c * TN
        z = jnp.dot(h_ref[...], w_ref[:, lo:lo + TN], preferred_element_type=F32)
        zg_ref[:, c * TN:(c + 1) * TN] = _silu(z).astype(BF16)


def _attn_vz(h, w):
    return pl.pallas_call(
        _attn_vz_kernel,
        grid=(N_ALL // TM,),
        in_specs=[
            pl.BlockSpec((TM, D), lambda r: (r, 0)),
            _resident((D, A_KVW + A_Q), lambda r: (0, 0)),
        ],
        out_specs=[
            pl.BlockSpec((TM, A_KVW), lambda r: (r, 0)),
            pl.BlockSpec((TM, A_Q), lambda r: (r, 0)),
        ],
        out_shape=[
            jax.ShapeDtypeStruct((N_ALL, A_KVW), BF16),
            jax.ShapeDtypeStruct((N_ALL, A_Q), BF16),
        ],
        compiler_params=_params(("parallel",), 48),
        name="attn_vz",
    )(h, w)


A_LAT_TILES = SEQ // A_TQ


def _attn_kernel(q_ref, kl_ref, kc_ref, vl_ref, vc_ref, zg_ref, o_ref, vol_scr, voc_scr):
    t = pl.program_id(2)
    nt = (((1,), (1,)), ((), ()))
    heads = A_GW // A_HD

    @pl.when(t == 0)
    def _():
        vol_scr[:, :A_HD] = vl_ref[...]
        vol_scr[:, A_HD:] = jnp.ones((SEQ, A_HD), BF16)
        voc_scr[:, :A_HD] = vc_ref[...]
        voc_scr[:, A_HD:] = jnp.ones((CTX, A_HD), BF16)

    def run(with_latent_keys):
        def scores(hh):
            q = q_ref[:, hh * A_HD:(hh + 1) * A_HD]
            s_c = lax.dot_general(q, kc_ref[...], nt, preferred_element_type=F32)
            s_l = (lax.dot_general(q, kl_ref[...], nt, preferred_element_type=F32)
                   if with_latent_keys else None)
            return s_c, s_l

        def probs(s_c, s_l):
            m = jnp.max(s_c, axis=-1, keepdims=True)
            if with_latent_keys:
                m = jnp.maximum(m, jnp.max(s_l, axis=-1, keepdims=True))
            p_c = jnp.exp(s_c - m).astype(BF16)
            p_l = jnp.exp(s_l - m).astype(BF16) if with_latent_keys else None
            return p_c, p_l

        def output(hh, p_c, p_l):
            sl = slice(hh * A_HD, (hh + 1) * A_HD)
            acc = jnp.dot(p_c, voc_scr[...], preferred_element_type=F32)
            if with_latent_keys:
                acc = acc + jnp.dot(p_l, vol_scr[...], preferred_element_type=F32)
            o_ref[:, sl] = (acc[:, :A_HD] / acc[:, A_HD:] * zg_ref[:, sl].astype(F32)).astype(BF16)

        s, p = {}, {}
        for step in range(heads + 2):
            if step < heads:
                s[step] = scores(step)
            if 1 <= step <= heads:
                p[step - 1] = probs(*s[step - 1])
            if step >= 2:
                output(step - 2, *p[step - 2])

    @pl.when(t < A_LAT_TILES)
    def _():
        run(True)

    @pl.when(t >= A_LAT_TILES)
    def _():
        run(False)


def _attn(q, k, v, zg):
    ctx_blk0 = N_LAT // A_TQ
    qrow = lambda b, g, t: jnp.where(t < A_LAT_TILES, b * A_LAT_TILES + t, ctx_blk0 + b)
    return pl.pallas_call(
        _attn_kernel,
        grid=(BATCH, A_KV, A_LAT_TILES + 1),
        in_specs=[
            pl.BlockSpec((A_TQ, A_GW), lambda b, g, t: (qrow(b, g, t), g)),
            pl.BlockSpec((SEQ, A_HD), lambda b, g, t: (b, g)),
            pl.BlockSpec((CTX, A_HD), lambda b, g, t: (N_LAT // CTX + b, g)),
            pl.BlockSpec((SEQ, A_HD), lambda b, g, t: (b, g)),
            pl.BlockSpec((CTX, A_HD), lambda b, g, t: (N_LAT // CTX + b, g)),
            pl.BlockSpec((A_TQ, A_GW), lambda b, g, t: (qrow(b, g, t), g)),
        ],
        out_specs=pl.BlockSpec((A_TQ, A_GW), lambda b, g, t: (qrow(b, g, t), g)),
        out_shape=jax.ShapeDtypeStruct((N_ALL, A_Q), BF16),
        scratch_shapes=[pltpu.VMEM((SEQ, 2 * A_HD), BF16), pltpu.VMEM((CTX, 2 * A_HD), BF16)],
        compiler_params=_params(("parallel", "parallel", "arbitrary"), 48),
        name="attn",
    )(q, k, k, v, v, zg)


def _dft_tables(n, scale):
    idx = jnp.arange(n, dtype=jnp.int32)
    ang = ((idx[:, None] * idx[None, :]) % n).astype(F32) * (2.0 * math.pi / n)
    return (jnp.cos(ang) * scale).astype(BF16), (jnp.sin(ang) * scale).astype(BF16)


def _dif_tables():
    jt = jnp.arange(F_M, dtype=jnp.int32)
    r = jnp.arange(F_R, dtype=jnp.int32)
    k = ((F_R * jt[None, :, None] + r[:, None, None]) * jt[None, None, :]) % SEQ
    ang = k.astype(F32) * (2.0 * math.pi / SEQ)
    return (jnp.cos(ang) * SEQ ** -0.5).astype(BF16), (jnp.sin(ang) * SEQ ** -0.5).astype(BF16)


def _perm_tables():
    n = jnp.arange(F_TC)
    src = F_R * (n % F_PR) + n // F_PR
    perm = (src[:, None] == n[None, :]).astype(BF16)
    return perm, perm.T


def _rope_tables():
    t = jnp.arange(SEQ)
    freqs = ROPE_THETA ** (-jnp.arange(0, A_HD // 2, 2, dtype=F32) / (A_HD // 2))
    ang = jnp.concatenate([(t // GRID_W).astype(F32)[:, None] * freqs,
                           (t % GRID_W).astype(F32)[:, None] * freqs], axis=-1)
    cos = jnp.concatenate([jnp.cos(ang), jnp.cos(ang)], axis=-1)
    sin = jnp.concatenate([-jnp.sin(ang), jnp.sin(ang)], axis=-1)
    pad = lambda a, fill: jnp.concatenate([a, jnp.full((TM, A_HD), fill, F32)], axis=0)
    return pad(cos, 1.0), pad(sin, 0.0)


def _split_heads_even_odd(w, heads):
    perm = jnp.concatenate([jnp.arange(0, A_HD, 2), jnp.arange(1, A_HD, 2)])
    lead = w.shape[:-1]
    return w.reshape(lead + (heads, A_HD))[..., perm].reshape(lead + (heads * A_HD,))


def kernel(x, c, ctx, c_ctx, ada_w, ada_b, norm_g, fnet_w_gate, fnet_w_out, mlstm_w_in, mlstm_b_gate,
           mlstm_hn, mlstm_w_out, attn_w_in, attn_qn, attn_kn, attn_w_out, final_g):
    cc = jnp.concatenate([c, c_ctx[None, :], jnp.zeros((MOD_ROWS - BATCH - 1, D), F32)], axis=0)
    mods_all = _modvec(cc, ada_w, ada_b).reshape(DEPTH, MOD_ROWS, 1, 3 * D)

    cc_c, sc_c = _dft_tables(F_GW, F_GW ** -0.5)
    cs_chan = jnp.concatenate([cc_c, sc_c], axis=1)
    cr_lat, sr_lat = _dif_tables()
    ct_ctx, st_ctx = _dft_tables(CTX, CTX ** -0.5)
    perm, unperm = _perm_tables()

    xs = None
    h = None
    for i in range(DEPTH):
        kind, j = i % 3, i // 3
        last = i == DEPTH - 1
        mods = mods_all[i]
        g = norm_g[i].reshape(1, D)
        n_rows = N_LAT if last else N_ALL
        nxt = None if last else (norm_g[i + 1].reshape(1, D), mods_all[i + 1])
        if kind == 0:
            assert i == 0 or last
            wg = fnet_w_gate[j].astype(BF16)
            w_out = fnet_w_out[j].astype(BF16)
            x_lat = x.reshape(N_LAT, D) if i == 0 else xs
            p, q, hp = _fnet_chan_dif(*((x_lat, (g, mods)) if i == 0 else (h, None)), cs_chan, perm)
            m_lat = _fnet_mix_lat(hp, cr_lat, sr_lat, p, q, wg)
            if i == 0:
                x_ctx = ctx.reshape(N_CTX, D)
                a, b = _fnet_chan_ctx(x_ctx, 0, g, mods, cs_chan)
                m_ctx = _fnet_mix_ctx(x_ctx, 0, g, mods, ct_ctx, st_ctx, a, b, wg)
                xs, h = _outproj_join(m_lat, m_ctx, w_out, x_lat, x_ctx, mods, unperm, nxt)
            else:
                out = _outproj(m_lat, w_out, xs, mods, n_rows, final=(unperm, final_g.reshape(1, D)))
            continue
        if kind == 1:
            w_in = mlstm_w_in[j]
            n_main = 2 * M_QK + 2 * M_V
            w_main = jnp.concatenate([w_in[:, :n_main], w_in[:, n_main + 4 * M_HEADS:]],
                                     axis=1).astype(BF16)
            gate_tiles = lambda gcols: jnp.pad(
                jnp.concatenate([gcols[..., 0:8], gcols[..., 16:24]], axis=-1),
                [(0, 0)] * (gcols.ndim - 1) + [(0, 128 - 2 * M_HEADS)])
            gate_layout = lambda gcols: jnp.concatenate(
                [gate_tiles(gcols), gate_tiles(gcols[..., M_HEADS:])], axis=-1)
            w_gates = gate_layout(w_in[:, n_main:n_main + 4 * M_HEADS]).astype(BF16)
            b_gates = gate_layout(mlstm_b_gate[j][None, :])
            proj, gates = _mlstm_proj(h, w_main, w_gates, b_gates)
            hb = _mlstm_scan(proj, gates)
            mbuf = _mlstm_scan(proj, gates, hb, mlstm_hn[j].reshape(1, M_V))
            w_out = mlstm_w_out[j]
        else:
            w_in = attn_w_in[j]
            w_qk = jnp.concatenate([_split_heads_even_odd(w_in[:, :A_Q], A_HEADS),
                                    _split_heads_even_odd(w_in[:, A_Q:A_QKW], A_KV)], axis=1)
            qn = _split_heads_even_odd(attn_qn[j], 1) * (A_HD ** -0.5)
            kn = _split_heads_even_odd(attn_kn[j], 1)
            gain = jnp.concatenate([jnp.tile(qn, A_HEADS), jnp.tile(kn, A_KV)]).reshape(1, A_QKW)
            cos, sin = _rope_tables()
            q, k = _attn_qk(h, w_qk.astype(BF16), gain, cos, sin)
            v, zg = _attn_vz(h, w_in[:, A_QKW:].astype(BF16))
            mbuf = _attn(q, k, v, zg)
            w_out = attn_w_out[j]
        assert not last
        xs, h = _outproj(mbuf, w_out.astype(BF16), xs, mods, n_rows, nxt=nxt)

    return out.reshape(BATCH, SEQ, D)
```

```python
import functools
import math

import jax
import jax.numpy as jnp
from jax import lax
from jax.experimental import pallas as pl
from jax.experimental.pallas import tpu as pltpu

F32 = jnp.float32
BF16 = jnp.bfloat16

D = 2048
BATCH = 16
SEQ = 2048
CTX = 256
DEPTH = 4
EPS = 1e-6
N_LAT = BATCH * SEQ
N_CTX = BATCH * CTX
N_ALL = N_LAT + N_CTX
MOD_ROWS = 24
CTX_MOD_ROW = BATCH

F_GROUPS = 4
F_GW = D // F_GROUPS

M_HEADS = 8
M_DQK = 128
M_DV = 256
M_QK = M_HEADS * M_DQK
M_V = M_HEADS * M_DV
M_L = 256
M_GATES = 256
M_HALF = 2 * M_QK + M_V

A_HEADS = 16
A_KV = 4
A_HD = 128
A_Q = A_HEADS * A_HD
A_KVW = A_KV * A_HD
A_GW = A_Q // A_KV
A_TQ = 256
GRID_W = 64
ROPE_THETA = 10000.0

TM = 512
TN = 512
MIB = 1024 * 1024


def _params(sem, vmem_mib):
    return pltpu.CompilerParams(dimension_semantics=sem, vmem_limit_bytes=vmem_mib * MIB)


def _resident(shape, index_map):
    return pl.BlockSpec(shape, index_map, pipeline_mode=pl.Buffered(1))


def _sigmoid(x):
    return 1.0 / (1.0 + jnp.exp(-x))


def _silu(x):
    return x * _sigmoid(x)


def _log_sigmoid(x):
    return jnp.minimum(x, 0.0) - jnp.log1p(jnp.exp(-jnp.abs(x)))


def _modnorm(x, g, scale, shift):
    ms = jnp.mean(x * x, axis=-1, keepdims=True)
    y = x * lax.rsqrt(ms + EPS) * g
    return y * (1.0 + scale) + shift


def _store_modnorm(x_ref, g_ref, sc_ref, sh_ref, h_scr):
    rows = x_ref.shape[0]
    step = min(rows, 256)
    for r0 in range(0, rows, step):
        h_scr[r0:r0 + step, :] = _modnorm(x_ref[r0:r0 + step, :], g_ref[...], sc_ref[...],
                                          sh_ref[...]).astype(BF16)


def _mod_row(r, tm):
    return jnp.where(r < N_LAT // tm, r // (SEQ // tm), CTX_MOD_ROW)


def _modvec_kernel(c_ref, w_ref, b_ref, o_ref):
    s = _silu(c_ref[...])
    o_ref[...] = jnp.dot(s, w_ref[...], preferred_element_type=F32,
                         precision=lax.Precision.HIGHEST) + b_ref[...]


def _modvec(cc, ada_w, ada_b):
    tn = 1024
    return pl.pallas_call(
        _modvec_kernel,
        grid=(DEPTH, 3 * D // tn),
        in_specs=[
            pl.BlockSpec((MOD_ROWS, D), lambda i, j: (0, 0)),
            pl.BlockSpec((None, D, tn), lambda i, j: (i, 0, j)),
            pl.BlockSpec((None, 1, tn), lambda i, j: (i, 0, j)),
        ],
        out_specs=pl.BlockSpec((None, MOD_ROWS, tn), lambda i, j: (i, 0, j)),
        out_shape=jax.ShapeDtypeStruct((DEPTH, MOD_ROWS, 3 * D), F32),
        compiler_params=_params(("parallel", "parallel"), 40),
        name="modvec",
    )(cc, ada_w, ada_b.reshape(DEPTH, 1, 3 * D))


def _fnet_chan_kernel(x_ref, g_ref, sh_ref, sc_ref, cs_ref, a_ref, b_ref):
    h = _modnorm(x_ref[...], g_ref[...], sc_ref[...], sh_ref[...]).astype(BF16)
    for grp in range(F_GROUPS):
        sl = slice(grp * F_GW, (grp + 1) * F_GW)
        p = jnp.dot(h[:, sl], cs_ref[...], preferred_element_type=F32)
        a_ref[:, sl] = p[:, :F_GW].astype(BF16)
        b_ref[:, sl] = p[:, F_GW:].astype(BF16)


def _fnet_chan_ctx(xc, row_blk0, g, mods, cs_c):
    tm = 512
    out = jax.ShapeDtypeStruct((N_CTX, D), BF16)
    mod = lambda chunk: pl.BlockSpec((None, 1, D), lambda r: (CTX_MOD_ROW, 0, chunk))
    return pl.pallas_call(
        _fnet_chan_kernel,
        grid=(N_CTX // tm,),
        in_specs=[pl.BlockSpec((tm, D), lambda r: (row_blk0 + r, 0)),
                  pl.BlockSpec((1, D), lambda r: (0, 0)), mod(0), mod(1),
                  pl.BlockSpec((F_GW, 2 * F_GW), lambda r: (0, 0))],
        out_specs=[pl.BlockSpec((tm, D), lambda r: (r, 0))] * 2,
        out_shape=[out, out],
        compiler_params=_params(("parallel",), 48),
        name="fnet_chan_ctx",
    )(xc, g, mods, mods, cs_c)


F_R = 4
F_M = SEQ // F_R
F_TC = 256
F_PR = F_TC // F_R


def _fnet_chan_dif_kernel(x0_ref, x1_ref, x2_ref, x3_ref, *rest, prenormed):
    if prenormed:
        cs_ref, perm_ref, p_ref, q_ref, hp_ref = rest
        hs = [x[...] for x in (x0_ref, x1_ref, x2_ref, x3_ref)]
    else:
        g_ref, sh_ref, sc_ref, cs_ref, perm_ref, p_ref, q_ref, hp_ref = rest
        hs = [_modnorm(x[...], g_ref[...], sc_ref[...], sh_ref[...]).astype(BF16)
              for x in (x0_ref, x1_ref, x2_ref, x3_ref)]
    for qi, h in enumerate(hs):
        hp = jnp.dot(perm_ref[...], h, preferred_element_type=F32).astype(BF16)
        for r in range(F_R):
            hp_ref[qi, r] = hp[r * F_PR:(r + 1) * F_PR, :]
    for grp in range(F_GROUPS):
        sl = slice(grp * F_GW, (grp + 1) * F_GW)
        ab = [jnp.dot(h[:, sl], cs_ref[...], preferred_element_type=F32) for h in hs]
        a = [t[:, :F_GW] for t in ab]
        b = [t[:, F_GW:] for t in ab]
        sa02, da02, sa13, da13 = a[0] + a[2], a[0] - a[2], a[1] + a[3], a[1] - a[3]
        sb02, db02, sb13, db13 = b[0] + b[2], b[0] - b[2], b[1] + b[3], b[1] - b[3]
        re = (sa02 + sa13, da02 - db13, sa02 - sa13, da02 + db13)
        im = (sb02 + sb13, db02 + da13, sb02 - sb13, db02 - da13)
        for r in range(F_R):
            p_ref[r, :, sl] = re[r].astype(BF16)
            q_ref[r, :, sl] = im[r].astype(BF16)


def _fnet_chan_dif(xl, norm, cs_c, perm):
    nt = F_M // F_TC
    xspec = lambda q: pl.BlockSpec((F_TC, D), lambda b, i: (b * (SEQ // F_TC) + q * nt + i, 0))
    mod = lambda chunk: pl.BlockSpec((None, 1, D), lambda b, i: (b, 0, chunk))
    out = jax.ShapeDtypeStruct((BATCH, F_R, F_M, D), BF16)
    ospec = pl.BlockSpec((None, F_R, F_TC, D), lambda b, i: (b, 0, i, 0))
    norm_specs = [] if norm is None else [pl.BlockSpec((1, D), lambda b, i: (0, 0)), mod(0), mod(1)]
    norm_args = [] if norm is None else [norm[0], norm[1], norm[1]]
    p, q, hp = pl.pallas_call(
        functools.partial(_fnet_chan_dif_kernel, prenormed=norm is None),
        grid=(BATCH, nt),
        in_specs=[xspec(0), xspec(1), xspec(2), xspec(3)] + norm_specs + [
            pl.BlockSpec((F_GW, 2 * F_GW), lambda b, i: (0, 0)),
            pl.BlockSpec((F_TC, F_TC), lambda b, i: (0, 0))],
        out_specs=[ospec, ospec,
                   pl.BlockSpec((None, F_R, None, F_R, F_PR, D), lambda b, i: (b, 0, i, 0, 0, 0))],
        out_shape=[out, out, jax.ShapeDtypeStruct((BATCH, F_R, nt, F_R, F_PR, D), BF16)],
        compiler_params=_params(("parallel", "parallel"), 56),
        name="fnet_chan_dif",
    )(xl, xl, xl, xl, *norm_args, cs_c, perm)
    return p, q, hp.reshape(BATCH, SEQ // F_TC, F_R, F_PR, D)


def _fnet_mix_kernel(x_ref, g_ref, sh_ref, sc_ref, c_ref, s_ref, p_ref, q_ref, wg_ref, o_ref, h_scr):
    _store_modnorm(x_ref, g_ref, sc_ref, sh_ref, h_scr)
    for c in range(D // TN):
        sl = slice(c * TN, (c + 1) * TN)
        y = jnp.dot(c_ref[...], p_ref[:, sl], preferred_element_type=F32)
        y = y - jnp.dot(s_ref[...], q_ref[:, sl], preferred_element_type=F32)
        gate = jnp.dot(h_scr[...], wg_ref[:, sl], preferred_element_type=F32)
        o_ref[:, sl] = (y * _silu(gate)).astype(BF16)


def _fnet_mix_lat_kernel(h_ref, c_ref, s_ref, p_ref, q_ref, wg_ref, o_ref):
    nt = SEQ // F_TC
    h = h_ref[...].reshape(F_M, D)
    for c in range(D // TN):
        sl = slice(c * TN, (c + 1) * TN)
        y = jnp.dot(c_ref[...], p_ref[:, sl], preferred_element_type=F32)
        y = y - jnp.dot(s_ref[...], q_ref[:, sl], preferred_element_type=F32)
        gate = jnp.dot(h, wg_ref[:, sl], preferred_element_type=F32)
        val = (y * _silu(gate)).astype(BF16)
        for t in range(nt):
            o_ref[t, :, sl] = val[t * F_PR:(t + 1) * F_PR, :]


def _fnet_mix_lat(hp, cr, sr, p, q, wg):
    nt = SEQ // F_TC
    tab = pl.BlockSpec((None, F_M, F_M), lambda b, r: (r, 0, 0))
    pq = pl.BlockSpec((None, None, F_M, D), lambda b, r: (b, r, 0, 0))
    tiles = pl.BlockSpec((None, nt, None, F_PR, D), lambda b, r: (b, 0, r, 0, 0))
    out = pl.pallas_call(
        _fnet_mix_lat_kernel,
        grid=(BATCH, F_R),
        in_specs=[tiles, tab, tab, pq, pq, _resident((D, D), lambda b, r: (0, 0))],
        out_specs=tiles,
        out_shape=jax.ShapeDtypeStruct((BATCH, nt, F_R, F_PR, D), BF16),
        compiler_params=_params(("parallel", "parallel"), 48),
        name="fnet_mix_lat",
    )(hp, cr, sr, p, q, wg)
    return out.reshape(N_LAT, D)


def _fnet_mix_ctx(xc, row_blk0, g, mods, ct, st, a, b, wg):
    mod = lambda chunk: pl.BlockSpec((None, 1, D), lambda bi: (CTX_MOD_ROW, 0, chunk))
    tab = pl.BlockSpec((CTX, CTX), lambda bi: (0, 0))
    ab = pl.BlockSpec((CTX, D), lambda bi: (bi, 0))
    return pl.pallas_call(
        _fnet_mix_kernel,
        grid=(BATCH,),
        in_specs=[pl.BlockSpec((CTX, D), lambda bi: (row_blk0 + bi, 0)),
                  pl.BlockSpec((1, D), lambda bi: (0, 0)), mod(0), mod(1),
                  tab, tab, ab, ab, _resident((D, D), lambda bi: (0, 0))],
        out_specs=pl.BlockSpec((CTX, D), lambda bi: (bi, 0)),
        out_shape=jax.ShapeDtypeStruct((N_CTX, D), BF16),
        scratch_shapes=[pltpu.VMEM((CTX, D), BF16)],
        compiler_params=_params(("parallel",), 48),
        name="fnet_mix_ctx",
    )(xc, g, mods, mods, ct, st, a, b, wg)


def _residual_update(m_ref, w_ref, x_ref, gate_ref, o_ref, unperm_ref, m_scr, nxt):
    if unperm_ref is not None:
        for t in range(TM // F_TC):
            rows = slice(t * F_TC, (t + 1) * F_TC)
            m_scr[rows, :] = jnp.dot(unperm_ref[...], m_ref[rows, :],
                                     preferred_element_type=F32).astype(BF16)
        m_ref = m_scr
    ss = 0.0
    for c in range(D // TN):
        sl = slice(c * TN, (c + 1) * TN)
        y = jnp.dot(m_ref[...], w_ref[:, sl], preferred_element_type=F32)
        o = x_ref[:, sl] + gate_ref[:, sl] * y
        o_ref[:, sl] = o
        if nxt is not None:
            ss = ss + jnp.sum(o * o, axis=-1, keepdims=True)
    if nxt is not None:
        gn_ref, shn_ref, scn_ref, h_ref = nxt
        rinv = lax.rsqrt(ss * (1.0 / D) + EPS)
        for c in range(D // TN):
            sl = slice(c * TN, (c + 1) * TN)
            gs = gn_ref[:, sl] * (1.0 + scn_ref[:, sl])
            h_ref[:, sl] = (o_ref[:, sl] * rinv * gs + shn_ref[:, sl]).astype(BF16)


def _outproj_kernel(m_ref, w_ref, x_ref, gate_ref, *rest, final):
    if final:
        unperm_ref, fg_ref, o_ref, m_scr = rest
        _residual_update(m_ref, w_ref, x_ref, gate_ref, o_ref, unperm_ref, m_scr, None)
        x = o_ref[...]
        ms = jnp.mean(x * x, axis=-1, keepdims=True)
        o_ref[...] = x * lax.rsqrt(ms + EPS) * fg_ref[...]
    else:
        gn_ref, shn_ref, scn_ref, o_ref, h_ref = rest
        _residual_update(m_ref, w_ref, x_ref, gate_ref, o_ref, None, None,
                         (gn_ref, shn_ref, scn_ref, h_ref))


def _next_norm_specs():
    return [pl.BlockSpec((1, D), lambda r: (0, 0)),
            pl.BlockSpec((None, 1, D), lambda r: (_mod_row(r, TM), 0, 0)),
            pl.BlockSpec((None, 1, D), lambda r: (_mod_row(r, TM), 0, 1))]


def _outproj(mbuf, w, xs, mods, n_rows, final=None, nxt=None):
    in_specs = [
        pl.BlockSpec((TM, D), lambda r: (r, 0)),
        _resident((D, D), lambda r: (0, 0)),
        pl.BlockSpec((TM, D), lambda r: (r, 0)),
        pl.BlockSpec((None, 1, D), lambda r: (_mod_row(r, TM), 0, 2)),
    ]
    args = [mbuf, w, xs, mods]
    tile = pl.BlockSpec((TM, D), lambda r: (r, 0))
    if final is not None:
        in_specs += [pl.BlockSpec((F_TC, F_TC), lambda r: (0, 0)), pl.BlockSpec((1, D), lambda r: (0, 0))]
        args += list(final)
        return pl.pallas_call(
            functools.partial(_outproj_kernel, final=True),
            grid=(n_rows // TM,),
            in_specs=in_specs,
            out_specs=tile,
            out_shape=jax.ShapeDtypeStruct((n_rows, D), F32),
            scratch_shapes=[pltpu.VMEM((TM, D), BF16)],
            compiler_params=_params(("parallel",), 48),
            name="outproj_final",
        )(*args)
    g_next, mods_next = nxt
    return pl.pallas_call(
        functools.partial(_outproj_kernel, final=False),
        grid=(n_rows // TM,),
        in_specs=in_specs + _next_norm_specs(),
        out_specs=[tile, tile],
        out_shape=[jax.ShapeDtypeStruct((N_ALL, D), F32), jax.ShapeDtypeStruct((N_ALL, D), BF16)],
        input_output_aliases={2: 0},
        compiler_params=_params(("parallel",), 48),
        name="outproj",
    )(*args, g_next, mods_next, mods_next)


def _outproj_join_kernel(ml_ref, mc_ref, w_ref, xl_ref, xc_ref, gate_ref, unperm_ref,
                         gn_ref, shn_ref, scn_ref, o_ref, h_ref, m_scr):
    is_lat = pl.program_id(0) < N_LAT // TM
    nxt = (gn_ref, shn_ref, scn_ref, h_ref)

    @pl.when(is_lat)
    def _():
        _residual_update(ml_ref, w_ref, xl_ref, gate_ref, o_ref, unperm_ref, m_scr, nxt)

    @pl.when(jnp.logical_not(is_lat))
    def _():
        _residual_update(mc_ref, w_ref, xc_ref, gate_ref, o_ref, None, None, nxt)


def _outproj_join(m_lat, m_ctx, w, x_lat, x_ctx, mods, unperm, nxt):
    nl = N_LAT // TM
    lat = pl.BlockSpec((TM, D), lambda r: (jnp.minimum(r, nl - 1), 0))
    cxt = pl.BlockSpec((TM, D), lambda r: (jnp.maximum(r - nl, 0), 0))
    tile = pl.BlockSpec((TM, D), lambda r: (r, 0))
    g_next, mods_next = nxt
    return pl.pallas_call(
        _outproj_join_kernel,
        grid=(N_ALL // TM,),
        in_specs=[lat, cxt, _resident((D, D), lambda r: (0, 0)), lat, cxt,
                  pl.BlockSpec((None, 1, D), lambda r: (_mod_row(r, TM), 0, 2)),
                  pl.BlockSpec((F_TC, F_TC), lambda r: (0, 0))] + _next_norm_specs(),
        out_specs=[tile, tile],
        out_shape=[jax.ShapeDtypeStruct((N_ALL, D), F32), jax.ShapeDtypeStruct((N_ALL, D), BF16)],
        scratch_shapes=[pltpu.VMEM((TM, D), BF16)],
        compiler_params=_params(("parallel",), 48),
        name="outproj_join",
    )(m_lat, m_ctx, w, x_lat, x_ctx, mods, unperm, g_next, mods_next, mods_next)


def _mlstm_proj_kernel(h_ref, w_ref, wg_ref, bg_ref, out_ref, gt_ref):
    half = pl.program_id(0)

    def chunks(post):
        for c in range(M_HALF // TN):
            sl = slice(c * TN, (c + 1) * TN)
            acc = jnp.dot(h_ref[...], w_ref[:, sl], preferred_element_type=F32)
            out_ref[:, sl] = post(c, acc).astype(BF16)

    @pl.when(half == 0)
    def _():
        gt_ref[...] = jnp.dot(h_ref[...], wg_ref[...], preferred_element_type=F32) + bg_ref[...]
        chunks(lambda c, acc: acc * (M_DQK ** -0.5) if c < M_QK // TN else acc)

    @pl.when(half == 1)
    def _():
        chunks(lambda c, acc: _sigmoid(acc) if c < M_V // TN else _silu(acc))


def _mlstm_proj(h, w, wg, bg):
    nr = N_ALL // TM
    return pl.pallas_call(
        _mlstm_proj_kernel,
        grid=(2, nr),
        in_specs=[
            pl.BlockSpec((TM, D), lambda hf, r: (r, 0)),
            _resident((D, M_HALF), lambda hf, r: (0, hf)),
            pl.BlockSpec((D, M_GATES), lambda hf, r: (0, 0)),
            pl.BlockSpec((1, M_GATES), lambda hf, r: (0, 0)),
        ],
        out_specs=[
            pl.BlockSpec((TM, M_HALF), lambda hf, r: (r, hf)),
            pl.BlockSpec((TM, M_GATES), lambda hf, r: (jnp.where(hf == 0, r, nr - 1), 0)),
        ],
        out_shape=[
            jax.ShapeDtypeStruct((N_ALL, 2 * M_HALF), BF16),
            jax.ShapeDtypeStruct((N_ALL, M_GATES), F32),
        ],
        compiler_params=_params(("arbitrary", "arbitrary"), 56),
        name="mlstm_proj",
    )(h, w, wg, bg)


def _split3(x):
    hi = x.astype(BF16)
    r1 = x - hi.astype(F32)
    mid = r1.astype(BF16)
    lo = (r1 - mid.astype(F32)).astype(BF16)
    return hi, mid, lo


def _cummax_rows(x, reverse):
    rows = x.shape[0]
    row = lax.broadcasted_iota(jnp.int32, x.shape, 0)
    sh = 1
    while sh < rows:
        if reverse:
            x = jnp.where(row < rows - sh, jnp.maximum(x, pltpu.roll(x, rows - sh, 0)), x)
        else:
            x = jnp.where(row >= sh, jnp.maximum(x, pltpu.roll(x, sh, 0)), x)
        sh *= 2
    return x


def _lanes(col, width):
    return jnp.broadcast_to(col, (col.shape[0], width))


def _mlstm_scan_kernel(q_ref, k_ref, v_ref, gt_ref, *rest, reverse):
    if reverse:
        o_ref, cn_scr, m_scr = rest
    else:
        hb_ref, so_ref, sz_ref, hn_ref, o_ref, cn_scr, m_scr = rest

    @pl.when(pl.program_id(1) == 0)
    def _():
        cn_scr[...] = jnp.zeros_like(cn_scr)
        m_scr[...] = jnp.zeros_like(m_scr)

    L = M_L
    lane0 = M_HEADS if reverse else 0
    ig = gt_ref[:, :128]
    ls = _log_sigmoid(gt_ref[:, 128:])
    row = lax.broadcasted_iota(jnp.int32, (L, L), 0)
    col = lax.broadcasted_iota(jnp.int32, (L, L), 1)
    order = (row <= col) if reverse else (row >= col)
    tri = jnp.where(order, 1.0, 0.0).astype(BF16)
    hi, mid, lo = _split3(ls)
    b = (jnp.dot(tri, hi, preferred_element_type=F32)
         + jnp.dot(tri, mid, preferred_element_type=F32)
         + jnp.dot(tri, lo, preferred_element_type=F32))
    end = 0 if reverse else L - 1

    m_prev = m_scr[...]
    r = ig - b
    inter = b + m_prev
    m_t = jnp.maximum(inter, b + _cummax_rows(r, reverse))
    a_all = jnp.exp(inter - m_t)
    u_all = b - m_t
    en_all = jnp.exp(-m_t)
    b_end = b[end:end + 1, :]
    gl = b_end - b + ig
    m_new = jnp.maximum(b_end + m_prev, jnp.max(gl, axis=0, keepdims=True))
    w_all = jnp.exp(gl - m_new)
    decay_all = jnp.exp(b_end + m_prev - m_new)
    r_t = r.T
    m_scr[...] = m_new

    ones_bf = jnp.ones((L, 128), BF16)
    mean_dv = jnp.full((M_DV, 128), 1.0 / M_DV, BF16)
    nt = (((1,), (1,)), ((), ()))
    tn = (((0,), (0,)), ((), ()))

    def stage_a(h):
        l = lane0 + h
        q = q_ref[:, h * M_DQK:(h + 1) * M_DQK]
        k = k_ref[:, h * M_DQK:(h + 1) * M_DQK]
        qk = lax.dot_general(q, k, nt, preferred_element_type=F32)
        qcn = jnp.dot(q, cn_scr[h].astype(BF16), preferred_element_type=F32)
        p = jnp.exp(jnp.where(order, u_all[:, l:l + 1] + r_t[l:l + 1, :], -jnp.inf))
        return qk, qcn, p

    def stage_b(h, qk, p):
        s = qk * p
        s_hi = s.astype(BF16)
        s_lo = (s - s_hi.astype(F32)).astype(BF16)
        vo = jnp.concatenate([v_ref[:, h * M_DV:(h + 1) * M_DV], ones_bf], axis=1)
        sv = jnp.dot(s_hi, vo, preferred_element_type=F32)
        return sv, jnp.dot(s_lo, ones_bf, preferred_element_type=F32)

    def stage_c(h, qcn, sv, rs_lo):
        l = lane0 + h
        a = _lanes(a_all[:, l:l + 1], 128)
        den = a * qcn[:, M_DV:] + (sv[:, M_DV:] + rs_lo)
        inv = 1.0 / jnp.maximum(jnp.abs(den), _lanes(en_all[:, l:l + 1], 128))
        cols = [slice(h * M_DV + c0, h * M_DV + c0 + 128) for c0 in range(0, M_DV, 128)]
        hid = [(a * qcn[:, c0:c0 + 128] + sv[:, c0:c0 + 128]) * inv for c0 in range(0, M_DV, 128)]
        if reverse:
            for sl, hv in zip(cols, hid):
                o_ref[:, sl] = hv
            return
        ys = [so_ref[:, sl].astype(F32) * (hv + hb_ref[:, sl]) for sl, hv in zip(cols, hid)]
        sq = jnp.concatenate([(y * y).astype(BF16) for y in ys], axis=1)
        ms = jnp.dot(sq, mean_dv, preferred_element_type=F32)
        scale = lax.rsqrt(ms + EPS)
        for sl, y in zip(cols, ys):
            o_ref[:, sl] = (y * scale * hn_ref[:, sl] * sz_ref[:, sl].astype(F32)).astype(BF16)

    def stage_d(h):
        l = lane0 + h
        k = k_ref[:, h * M_DQK:(h + 1) * M_DQK]
        kw = (k.astype(F32) * _lanes(w_all[:, l:l + 1], M_DQK)).astype(BF16)
        vo = jnp.concatenate([v_ref[:, h * M_DV:(h + 1) * M_DV], ones_bf], axis=1)
        upd = lax.dot_general(kw, vo, tn, preferred_element_type=F32)
        cn_scr[h] = decay_all[:, l:l + 1] * cn_scr[h] + upd

    sa = {}
    sb = {}
    for step in range(M_HEADS + 2):
        if step < M_HEADS:
            sa[step] = stage_a(step)
        if 1 <= step <= M_HEADS:
            h = step - 1
            sb[h] = stage_b(h, sa[h][0], sa[h][2])
        if step >= 2:
            h = step - 2
            stage_c(h, sa[h][1], *sb[h])
            stage_d(h)


def _mlstm_scan(proj, gates, hb=None, hn=None):
    reverse = hb is None
    nlc = SEQ // M_L
    ctx_blk0 = N_LAT // M_L

    def blk(b, i):
        lat = (nlc - i) if reverse else (i - 1)
        return jnp.where(i == 0, ctx_blk0 + b, b * nlc + lat)

    wide = lambda cb: pl.BlockSpec((M_L, M_V), lambda b, i: (blk(b, i), cb))
    in_specs = [
        pl.BlockSpec((M_L, M_QK), lambda b, i: (blk(b, i), 0)),
        pl.BlockSpec((M_L, M_QK), lambda b, i: (blk(b, i), 1)),
        wide(1),
        pl.BlockSpec((M_L, M_GATES), lambda b, i: (blk(b, i), 0)),
    ]
    args = [proj, proj, proj, gates]
    if not reverse:
        in_specs += [wide(0), wide(2), wide(3), pl.BlockSpec((1, M_V), lambda b, i: (0, 0))]
        args += [hb, proj, proj, hn]
    return pl.pallas_call(
        functools.partial(_mlstm_scan_kernel, reverse=reverse),
        grid=(BATCH, nlc + 1),
        in_specs=in_specs,
        out_specs=wide(0),
        out_shape=jax.ShapeDtypeStruct((N_ALL, M_V), F32 if reverse else BF16),
        scratch_shapes=[
            pltpu.VMEM((M_HEADS, M_DQK, M_DV + 128), F32),
            pltpu.VMEM((1, 128), F32),
        ],
        compiler_params=_params(("parallel", "arbitrary"), 48),
        name="mlstm_scan_bwd" if reverse else "mlstm_scan_fwd",
    )(*args)


A_QKW = A_Q + A_KVW


def _attn_qk_kernel(h_ref, w_ref, gain_ref, cos_ref, sin_ref, q_ref, k_ref):
    mean_mat = jnp.full((A_HD, A_HD), 1.0 / A_HD, BF16)
    cos = cos_ref[...]
    sin = sin_ref[...]
    for c in range(A_QKW // TN):
        acc = jnp.dot(h_ref[...], w_ref[:, c * TN:(c + 1) * TN], preferred_element_type=F32)
        for hh in range(TN // A_HD):
            lo = c * TN + hh * A_HD
            a = acc[:, hh * A_HD:(hh + 1) * A_HD]
            ms = jnp.dot((a * a).astype(BF16), mean_mat, preferred_element_type=F32)
            a = a * lax.rsqrt(ms + EPS) * gain_ref[:, lo:lo + A_HD]
            a = (a * cos + pltpu.roll(a, A_HD // 2, 1) * sin).astype(BF16)
            if lo < A_Q:
                q_ref[:, lo:lo + A_HD] = a
            else:
                k_ref[:, lo - A_Q:lo - A_Q + A_HD] = a


def _attn_qk(h, w, gain, cos, sin):
    lat_tiles = N_LAT // TM
    rope_blk = lambda r: (jnp.where(r < lat_tiles, r % (SEQ // TM), SEQ // TM), 0)
    return pl.pallas_call(
        _attn_qk_kernel,
        grid=(N_ALL // TM,),
        in_specs=[
            pl.BlockSpec((TM, D), lambda r: (r, 0)),
            _resident((D, A_QKW), lambda r: (0, 0)),
            pl.BlockSpec((1, A_QKW), lambda r: (0, 0)),
            pl.BlockSpec((TM, A_HD), rope_blk),
            pl.BlockSpec((TM, A_HD), rope_blk),
        ],
        out_specs=[
            pl.BlockSpec((TM, A_Q), lambda r: (r, 0)),
            pl.BlockSpec((TM, A_KVW), lambda r: (r, 0)),
        ],
        out_shape=[
            jax.ShapeDtypeStruct((N_ALL, A_Q), BF16),
            jax.ShapeDtypeStruct((N_ALL, A_KVW), BF16),
        ],
        compiler_params=_params(("parallel",), 48),
        name="attn_qk",
    )(h, w, gain, cos, sin)


def _attn_vz_kernel(h_ref, w_ref, v_ref, zg_ref):
    v_ref[...] = jnp.dot(h_ref[...], w_ref[:, :A_KVW], preferred_element_type=F32).astype(BF16)
    for c in range(A_Q // TN):
        lo = A_KVW + c * TN
        z = jnp.dot(h_ref[...], w_ref[:, lo:lo + TN], preferred_element_type=F32)
        zg_ref[:, c * TN:(c + 1) * TN] = _silu(z).astype(BF16)


def _attn_vz(h, w):
    return pl.pallas_call(
        _attn_vz_kernel,
        grid=(N_ALL // TM,),
        in_specs=[
            pl.BlockSpec((TM, D), lambda r: (r, 0)),
            _resident((D, A_KVW + A_Q), lambda r: (0, 1)),
        ],
        out_specs=[
            pl.BlockSpec((TM, A_KVW), lambda r: (r, 0)),
            pl.BlockSpec((TM, A_Q), lambda r: (r, 0)),
        ],
        out_shape=[
            jax.ShapeDtypeStruct((N_ALL, A_KVW), BF16),
            jax.ShapeDtypeStruct((N_ALL, A_Q), BF16),
        ],
        compiler_params=_params(("parallel",), 48),
        name="attn_vz",
    )(h, w)


A_LAT_TILES = SEQ // A_TQ
A_GPS = 2
A_SW = A_GPS * A_GW
A_KW = A_GPS * A_HD


def _attn_kernel(q_ref, kl_ref, kc_ref, vl_ref, vc_ref, zg_ref, o_ref, vol_scr, voc_scr):
    t = pl.program_id(2)
    nt = (((1,), (1,)), ((), ()))
    heads = A_SW // A_HD
    hpg = A_GW // A_HD

    @pl.when(t == 0)
    def _():
        for g in range(A_GPS):
            vol_scr[g, :, :A_HD] = vl_ref[:, g * A_HD:(g + 1) * A_HD]
            vol_scr[g, :, A_HD:] = jnp.ones((SEQ, A_HD), BF16)
            voc_scr[g, :, :A_HD] = vc_ref[:, g * A_HD:(g + 1) * A_HD]
            voc_scr[g, :, A_HD:] = jnp.ones((CTX, A_HD), BF16)

    def run(with_latent_keys):
        def scores(hh):
            kcols = slice((hh // hpg) * A_HD, (hh // hpg + 1) * A_HD)
            q = q_ref[:, hh * A_HD:(hh + 1) * A_HD]
            s_c = lax.dot_general(q, kc_ref[:, kcols], nt, preferred_element_type=F32)
            s_l = (lax.dot_general(q, kl_ref[:, kcols], nt, preferred_element_type=F32)
                   if with_latent_keys else None)
            return s_c, s_l

        def probs(s_c, s_l):
            m = jnp.max(s_c, axis=-1, keepdims=True)
            if with_latent_keys:
                m = jnp.maximum(m, jnp.max(s_l, axis=-1, keepdims=True))
            p_c = jnp.exp(s_c - m).astype(BF16)
            p_l = jnp.exp(s_l - m).astype(BF16) if with_latent_keys else None
            return p_c, p_l

        def output(hh, p_c, p_l):
            sl = slice(hh * A_HD, (hh + 1) * A_HD)
            acc = jnp.dot(p_c, voc_scr[hh // hpg], preferred_element_type=F32)
            if with_latent_keys:
                acc = acc + jnp.dot(p_l, vol_scr[hh // hpg], preferred_element_type=F32)
            o_ref[:, sl] = (acc[:, :A_HD] / acc[:, A_HD:] * zg_ref[:, sl].astype(F32)).astype(BF16)

        s, p = {}, {}
        for step in range(heads + 2):
            if step < heads:
                s[step] = scores(step)
            if 1 <= step <= heads:
                p[step - 1] = probs(*s[step - 1])
            if step >= 2:
                output(step - 2, *p[step - 2])

    @pl.when(t < A_LAT_TILES)
    def _():
        run(True)

    @pl.when(t >= A_LAT_TILES)
    def _():
        run(False)


def _attn(q, k, v, zg):
    ctx_blk0 = N_LAT // A_TQ
    qrow = lambda b, g, t: jnp.where(t < A_LAT_TILES, b * A_LAT_TILES + t, ctx_blk0 + b)
    return pl.pallas_call(
        _attn_kernel,
        grid=(BATCH, A_KV // A_GPS, A_LAT_TILES + 1),
        in_specs=[
            pl.BlockSpec((A_TQ, A_SW), lambda b, g, t: (qrow(b, g, t), g)),
            pl.BlockSpec((SEQ, A_KW), lambda b, g, t: (b, g)),
            pl.BlockSpec((CTX, A_KW), lambda b, g, t: (N_LAT // CTX + b, g)),
            pl.BlockSpec((SEQ, A_KW), lambda b, g, t: (b, g)),
            pl.BlockSpec((CTX, A_KW), lambda b, g, t: (N_LAT // CTX + b, g)),
            pl.BlockSpec((A_TQ, A_SW), lambda b, g, t: (qrow(b, g, t), g)),
        ],
        out_specs=pl.BlockSpec((A_TQ, A_SW), lambda b, g, t: (qrow(b, g, t), g)),
        out_shape=jax.ShapeDtypeStruct((N_ALL, A_Q), BF16),
        scratch_shapes=[pltpu.VMEM((A_GPS, SEQ, 2 * A_HD), BF16), pltpu.VMEM((A_GPS, CTX, 2 * A_HD), BF16)],
        compiler_params=_params(("parallel", "parallel", "arbitrary"), 48),
        name="attn",
    )(q, k, k, v, v, zg)


def _dft_tables(n, scale):
    idx = jnp.arange(n, dtype=jnp.int32)
    ang = ((idx[:, None] * idx[None, :]) % n).astype(F32) * (2.0 * math.pi / n)
    return (jnp.cos(ang) * scale).astype(BF16), (jnp.sin(ang) * scale).astype(BF16)


def _dif_tables():
    jt = jnp.arange(F_M, dtype=jnp.int32)
    r = jnp.arange(F_R, dtype=jnp.int32)
    k = ((F_R * jt[None, :, None] + r[:, None, None]) * jt[None, None, :]) % SEQ
    ang = k.astype(F32) * (2.0 * math.pi / SEQ)
    return (jnp.cos(ang) * SEQ ** -0.5).astype(BF16), (jnp.sin(ang) * SEQ ** -0.5).astype(BF16)


def _perm_tables():
    n = jnp.arange(F_TC)
    src = F_R * (n % F_PR) + n // F_PR
    perm = (src[:, None] == n[None, :]).astype(BF16)
    return perm, perm.T


def _rope_tables():
    t = jnp.arange(SEQ)
    freqs = ROPE_THETA ** (-jnp.arange(0, A_HD // 2, 2, dtype=F32) / (A_HD // 2))
    ang = jnp.concatenate([(t // GRID_W).astype(F32)[:, None] * freqs,
                           (t % GRID_W).astype(F32)[:, None] * freqs], axis=-1)
    cos = jnp.concatenate([jnp.cos(ang), jnp.cos(ang)], axis=-1)
    sin = jnp.concatenate([-jnp.sin(ang), jnp.sin(ang)], axis=-1)
    pad = lambda a, fill: jnp.concatenate([a, jnp.full((TM, A_HD), fill, F32)], axis=0)
    return pad(cos, 1.0), pad(sin, 0.0)


def _split_heads_even_odd(w, heads):
    perm = jnp.concatenate([jnp.arange(0, A_HD, 2), jnp.arange(1, A_HD, 2)])
    lead = w.shape[:-1]
    return w.reshape(lead + (heads, A_HD))[..., perm].reshape(lead + (heads * A_HD,))


def kernel(x, c, ctx, c_ctx, ada_w, ada_b, norm_g, fnet_w_gate, fnet_w_out, mlstm_w_in, mlstm_b_gate,
           mlstm_hn, mlstm_w_out, attn_w_in, attn_qn, attn_kn, attn_w_out, final_g):
    cc = jnp.concatenate([c, c_ctx[None, :], jnp.zeros((MOD_ROWS - BATCH - 1, D), F32)], axis=0)
    mods_all = _modvec(cc, ada_w, ada_b).reshape(DEPTH, MOD_ROWS, 1, 3 * D)

    cc_c, sc_c = _dft_tables(F_GW, F_GW ** -0.5)
    cs_chan = jnp.concatenate([cc_c, sc_c], axis=1)
    cr_lat, sr_lat = _dif_tables()
    ct_ctx, st_ctx = _dft_tables(CTX, CTX ** -0.5)
    perm, unperm = _perm_tables()

    xs = None
    h = None
    for i in range(DEPTH):
        kind, j = i % 3, i // 3
        last = i == DEPTH - 1
        mods = mods_all[i]
        g = norm_g[i].reshape(1, D)
        n_rows = N_LAT if last else N_ALL
        nxt = None if last else (norm_g[i + 1].reshape(1, D), mods_all[i + 1])
        if kind == 0:
            assert i == 0 or last
            wg = fnet_w_gate[j].astype(BF16)
            w_out = fnet_w_out[j].astype(BF16)
            x_lat = x.reshape(N_LAT, D) if i == 0 else xs
            p, q, hp = _fnet_chan_dif(*((x_lat, (g, mods)) if i == 0 else (h, None)), cs_chan, perm)
            m_lat = _fnet_mix_lat(hp, cr_lat, sr_lat, p, q, wg)
            if i == 0:
                x_ctx = ctx.reshape(N_CTX, D)
                a, b = _fnet_chan_ctx(x_ctx, 0, g, mods, cs_chan)
                m_ctx = _fnet_mix_ctx(x_ctx, 0, g, mods, ct_ctx, st_ctx, a, b, wg)
                xs, h = _outproj_join(m_lat, m_ctx, w_out, x_lat, x_ctx, mods, unperm, nxt)
            else:
                out = _outproj(m_lat, w_out, xs, mods, n_rows, final=(unperm, final_g.reshape(1, D)))
            continue
        if kind == 1:
            w_in = mlstm_w_in[j].astype(BF16)
            n_main = 2 * M_QK + 2 * M_V
            w_main = jnp.concatenate([w_in[:, :n_main], w_in[:, n_main + 4 * M_HEADS:]],
                                     axis=1)
            gate_tiles = lambda gcols: jnp.pad(
                jnp.concatenate([gcols[..., 0:8], gcols[..., 16:24]], axis=-1),
                [(0, 0)] * (gcols.ndim - 1) + [(0, 128 - 2 * M_HEADS)])
            gate_layout = lambda gcols: jnp.concatenate(
                [gate_tiles(gcols), gate_tiles(gcols[..., M_HEADS:])], axis=-1)
            w_gates = gate_layout(w_in[:, n_main:n_main + 4 * M_HEADS])
            b_gates = gate_layout(mlstm_b_gate[j][None, :])
            proj, gates = _mlstm_proj(h, w_main, w_gates, b_gates)
            hb = _mlstm_scan(proj, gates)
            mbuf = _mlstm_scan(proj, gates, hb, mlstm_hn[j].reshape(1, M_V))
            w_out = mlstm_w_out[j]
        else:
            w_in = attn_w_in[j].astype(BF16)
            w_qk = _split_heads_even_odd(w_in[:, :A_QKW], A_HEADS + A_KV)
            qn = _split_heads_even_odd(attn_qn[j], 1) * (A_HD ** -0.5)
            kn = _split_heads_even_odd(attn_kn[j], 1)
            gain = jnp.concatenate([jnp.tile(qn, A_HEADS), jnp.tile(kn, A_KV)]).reshape(1, A_QKW)
            cos, sin = _rope_tables()
            q, k = _attn_qk(h, w_qk, gain, cos, sin)
            v, zg = _attn_vz(h, w_in)
            mbuf = _attn(q, k, v, zg)
            w_out = attn_w_out[j]
        assert not last
        xs, h = _outproj(mbuf, w_out.astype(BF16), xs, mods, n_rows, nxt=nxt)

    return out.reshape(BATCH, SEQ, D)
```

```python
import functools
import math

import jax
import jax.numpy as jnp
from jax import lax
from jax.experimental import pallas as pl
from jax.experimental.pallas import tpu as pltpu

F32 = jnp.float32
BF16 = jnp.bfloat16

D = 2048
BATCH = 16
SEQ = 2048
CTX = 256
DEPTH = 4
EPS = 1e-6
N_LAT = BATCH * SEQ
N_CTX = BATCH * CTX
N_ALL = N_LAT + N_CTX
MOD_ROWS = 24
CTX_MOD_ROW = BATCH

F_GROUPS = 4
F_GW = D // F_GROUPS

M_HEADS = 8
M_DQK = 128
M_DV = 256
M_QK = M_HEADS * M_DQK
M_V = M_HEADS * M_DV
M_L = 256
M_GATES = 256
M_HALF = 2 * M_QK + M_V

A_HEADS = 16
A_KV = 4
A_HD = 128
A_Q = A_HEADS * A_HD
A_KVW = A_KV * A_HD
A_GW = A_Q // A_KV
A_TQ = 256
GRID_W = 64
ROPE_THETA = 10000.0

TM = 512
TN = 512
MIB = 1024 * 1024


def _params(sem, vmem_mib):
    return pltpu.CompilerParams(dimension_semantics=sem, vmem_limit_bytes=vmem_mib * MIB)


def _resident(shape, index_map):
    return pl.BlockSpec(shape, index_map, pipeline_mode=pl.Buffered(1))


def _layer_weight(j):
    return _resident((None, D, D), lambda *ids: (j, 0, 0))


def _sigmoid(x):
    return 1.0 / (1.0 + jnp.exp(-x))


def _silu(x):
    return x * _sigmoid(x)


def _log_sigmoid(x):
    return jnp.minimum(x, 0.0) - jnp.log1p(jnp.exp(-jnp.abs(x)))


def _modnorm(x, g, scale, shift):
    ms = jnp.mean(x * x, axis=-1, keepdims=True)
    y = x * lax.rsqrt(ms + EPS) * g
    return y * (1.0 + scale) + shift


def _store_modnorm(x_ref, g_ref, sc_ref, sh_ref, h_scr):
    rows = x_ref.shape[0]
    step = min(rows, 256)
    for r0 in range(0, rows, step):
        h_scr[r0:r0 + step, :] = _modnorm(x_ref[r0:r0 + step, :], g_ref[...], sc_ref[...],
                                          sh_ref[...]).astype(BF16)


def _mod_row(r, tm):
    return jnp.where(r < N_LAT // tm, r // (SEQ // tm), CTX_MOD_ROW)


def _modvec_kernel(c_ref, w_ref, b_ref, o_ref):
    s = _silu(c_ref[...])
    o_ref[...] = jnp.dot(s, w_ref[...], preferred_element_type=F32,
                         precision=lax.Precision.HIGHEST) + b_ref[...]


def _modvec(cc, ada_w, ada_b):
    tn = 1024
    return pl.pallas_call(
        _modvec_kernel,
        grid=(DEPTH, 3 * D // tn),
        in_specs=[
            pl.BlockSpec((MOD_ROWS, D), lambda i, j: (0, 0)),
            pl.BlockSpec((None, D, tn), lambda i, j: (i, 0, j)),
            pl.BlockSpec((None, 1, tn), lambda i, j: (i, 0, j)),
        ],
        out_specs=pl.BlockSpec((None, MOD_ROWS, tn), lambda i, j: (i, 0, j)),
        out_shape=jax.ShapeDtypeStruct((DEPTH, MOD_ROWS, 3 * D), F32),
        compiler_params=_params(("parallel", "parallel"), 40),
        name="modvec",
    )(cc, ada_w, ada_b.reshape(DEPTH, 1, 3 * D))


def _fnet_chan_kernel(x_ref, g_ref, sh_ref, sc_ref, cs_ref, a_ref, b_ref):
    h = _modnorm(x_ref[...], g_ref[...], sc_ref[...], sh_ref[...]).astype(BF16)
    for grp in range(F_GROUPS):
        sl = slice(grp * F_GW, (grp + 1) * F_GW)
        p = jnp.dot(h[:, sl], cs_ref[...], preferred_element_type=F32)
        a_ref[:, sl] = p[:, :F_GW].astype(BF16)
        b_ref[:, sl] = p[:, F_GW:].astype(BF16)


def _fnet_chan_ctx(xc, row_blk0, g, mods, cs_c):
    tm = 512
    out = jax.ShapeDtypeStruct((N_CTX, D), BF16)
    mod = lambda chunk: pl.BlockSpec((None, 1, D), lambda r: (CTX_MOD_ROW, 0, chunk))
    return pl.pallas_call(
        _fnet_chan_kernel,
        grid=(N_CTX // tm,),
        in_specs=[pl.BlockSpec((tm, D), lambda r: (row_blk0 + r, 0)),
                  pl.BlockSpec((1, D), lambda r: (0, 0)), mod(0), mod(1),
                  pl.BlockSpec((F_GW, 2 * F_GW), lambda r: (0, 0))],
        out_specs=[pl.BlockSpec((tm, D), lambda r: (r, 0))] * 2,
        out_shape=[out, out],
        compiler_params=_params(("parallel",), 48),
        name="fnet_chan_ctx",
    )(xc, g, mods, mods, cs_c)


F_R = 4
F_M = SEQ // F_R
F_TC = 256
F_PR = F_TC // F_R


def _fnet_chan_dif_kernel(x0_ref, x1_ref, x2_ref, x3_ref, *rest, prenormed):
    if prenormed:
        cs_ref, perm_ref, p_ref, q_ref, hp_ref = rest
        hs = [x[...] for x in (x0_ref, x1_ref, x2_ref, x3_ref)]
    else:
        g_ref, sh_ref, sc_ref, cs_ref, perm_ref, p_ref, q_ref, hp_ref = rest
        hs = [_modnorm(x[...], g_ref[...], sc_ref[...], sh_ref[...]).astype(BF16)
              for x in (x0_ref, x1_ref, x2_ref, x3_ref)]
    for qi, h in enumerate(hs):
        hp = jnp.dot(perm_ref[...], h, preferred_element_type=F32).astype(BF16)
        for r in range(F_R):
            hp_ref[qi, r] = hp[r * F_PR:(r + 1) * F_PR, :]
    for grp in range(F_GROUPS):
        sl = slice(grp * F_GW, (grp + 1) * F_GW)
        ab = [jnp.dot(h[:, sl], cs_ref[...], preferred_element_type=F32) for h in hs]
        a = [t[:, :F_GW] for t in ab]
        b = [t[:, F_GW:] for t in ab]
        sa02, da02, sa13, da13 = a[0] + a[2], a[0] - a[2], a[1] + a[3], a[1] - a[3]
        sb02, db02, sb13, db13 = b[0] + b[2], b[0] - b[2], b[1] + b[3], b[1] - b[3]
        re = (sa02 + sa13, da02 - db13, sa02 - sa13, da02 + db13)
        im = (sb02 + sb13, db02 + da13, sb02 - sb13, db02 - da13)
        for r in range(F_R):
            p_ref[r, :, sl] = re[r].astype(BF16)
            q_ref[r, :, sl] = im[r].astype(BF16)


def _fnet_chan_dif(xl, norm, cs_c, perm):
    nt = F_M // F_TC
    xspec = lambda q: pl.BlockSpec((F_TC, D), lambda b, i: (b * (SEQ // F_TC) + q * nt + i, 0))
    mod = lambda chunk: pl.BlockSpec((None, 1, D), lambda b, i: (b, 0, chunk))
    out = jax.ShapeDtypeStruct((BATCH, F_R, F_M, D), BF16)
    ospec = pl.BlockSpec((None, F_R, F_TC, D), lambda b, i: (b, 0, i, 0))
    norm_specs = [] if norm is None else [pl.BlockSpec((1, D), lambda b, i: (0, 0)), mod(0), mod(1)]
    norm_args = [] if norm is None else [norm[0], norm[1], norm[1]]
    p, q, hp = pl.pallas_call(
        functools.partial(_fnet_chan_dif_kernel, prenormed=norm is None),
        grid=(BATCH, nt),
        in_specs=[xspec(0), xspec(1), xspec(2), xspec(3)] + norm_specs + [
            pl.BlockSpec((F_GW, 2 * F_GW), lambda b, i: (0, 0)),
            pl.BlockSpec((F_TC, F_TC), lambda b, i: (0, 0))],
        out_specs=[ospec, ospec,
                   pl.BlockSpec((None, F_R, None, F_R, F_PR, D), lambda b, i: (b, 0, i, 0, 0, 0))],
        out_shape=[out, out, jax.ShapeDtypeStruct((BATCH, F_R, nt, F_R, F_PR, D), BF16)],
        compiler_params=_params(("parallel", "parallel"), 56),
        name="fnet_chan_dif",
    )(xl, xl, xl, xl, *norm_args, cs_c, perm)
    return p, q, hp.reshape(BATCH, SEQ // F_TC, F_R, F_PR, D)


def _fnet_mix_kernel(x_ref, g_ref, sh_ref, sc_ref, c_ref, s_ref, p_ref, q_ref, wg_ref, o_ref, h_scr):
    _store_modnorm(x_ref, g_ref, sc_ref, sh_ref, h_scr)
    for c in range(D // TN):
        sl = slice(c * TN, (c + 1) * TN)
        y = jnp.dot(c_ref[...], p_ref[:, sl], preferred_element_type=F32)
        y = y - jnp.dot(s_ref[...], q_ref[:, sl], preferred_element_type=F32)
        gate = jnp.dot(h_scr[...], wg_ref[:, sl], preferred_element_type=F32)
        o_ref[:, sl] = (y * _silu(gate)).astype(BF16)


def _fnet_mix_lat_kernel(h_ref, c_ref, s_ref, p_ref, q_ref, wg_ref, o_ref):
    nt = SEQ // F_TC
    h = h_ref[...].reshape(F_M, D)
    for c in range(D // TN):
        sl = slice(c * TN, (c + 1) * TN)
        y = jnp.dot(c_ref[...], p_ref[:, sl], preferred_element_type=F32)
        y = y - jnp.dot(s_ref[...], q_ref[:, sl], preferred_element_type=F32)
        gate = jnp.dot(h, wg_ref[:, sl], preferred_element_type=F32)
        val = (y * _silu(gate)).astype(BF16)
        for t in range(nt):
            o_ref[t, :, sl] = val[t * F_PR:(t + 1) * F_PR, :]


def _fnet_mix_lat(hp, cr, sr, p, q, wg):
    nt = SEQ // F_TC
    tab = pl.BlockSpec((None, F_M, F_M), lambda b, r: (r, 0, 0))
    pq = pl.BlockSpec((None, None, F_M, D), lambda b, r: (b, r, 0, 0))
    tiles = pl.BlockSpec((None, nt, None, F_PR, D), lambda b, r: (b, 0, r, 0, 0))
    out = pl.pallas_call(
        _fnet_mix_lat_kernel,
        grid=(BATCH, F_R),
        in_specs=[tiles, tab, tab, pq, pq, _layer_weight(wg[1])],
        out_specs=tiles,
        out_shape=jax.ShapeDtypeStruct((BATCH, nt, F_R, F_PR, D), BF16),
        compiler_params=_params(("parallel", "parallel"), 48),
        name="fnet_mix_lat",
    )(hp, cr, sr, p, q, wg[0])
    return out.reshape(N_LAT, D)


def _fnet_mix_ctx(xc, row_blk0, g, mods, ct, st, a, b, wg):
    mod = lambda chunk: pl.BlockSpec((None, 1, D), lambda bi: (CTX_MOD_ROW, 0, chunk))
    tab = pl.BlockSpec((CTX, CTX), lambda bi: (0, 0))
    ab = pl.BlockSpec((CTX, D), lambda bi: (bi, 0))
    return pl.pallas_call(
        _fnet_mix_kernel,
        grid=(BATCH,),
        in_specs=[pl.BlockSpec((CTX, D), lambda bi: (row_blk0 + bi, 0)),
                  pl.BlockSpec((1, D), lambda bi: (0, 0)), mod(0), mod(1),
                  tab, tab, ab, ab, _layer_weight(wg[1])],
        out_specs=pl.BlockSpec((CTX, D), lambda bi: (bi, 0)),
        out_shape=jax.ShapeDtypeStruct((N_CTX, D), BF16),
        scratch_shapes=[pltpu.VMEM((CTX, D), BF16)],
        compiler_params=_params(("parallel",), 48),
        name="fnet_mix_ctx",
    )(xc, g, mods, mods, ct, st, a, b, wg[0])


def _residual_update(m_ref, w_ref, x_ref, gate_ref, o_ref, unperm_ref, m_scr, nxt):
    if unperm_ref is not None:
        for t in range(TM // F_TC):
            rows = slice(t * F_TC, (t + 1) * F_TC)
            m_scr[rows, :] = jnp.dot(unperm_ref[...], m_ref[rows, :],
                                     preferred_element_type=F32).astype(BF16)
        m_ref = m_scr
    ss = 0.0
    for c in range(D // TN):
        sl = slice(c * TN, (c + 1) * TN)
        y = jnp.dot(m_ref[...], w_ref[:, sl], preferred_element_type=F32)
        o = x_ref[:, sl] + gate_ref[:, sl] * y
        o_ref[:, sl] = o
        if nxt is not None:
            ss = ss + jnp.sum(o * o, axis=-1, keepdims=True)
    if nxt is not None:
        gn_ref, shn_ref, scn_ref, h_ref = nxt
        rinv = lax.rsqrt(ss * (1.0 / D) + EPS)
        for c in range(D // TN):
            sl = slice(c * TN, (c + 1) * TN)
            gs = gn_ref[:, sl] * (1.0 + scn_ref[:, sl])
            h_ref[:, sl] = (o_ref[:, sl] * rinv * gs + shn_ref[:, sl]).astype(BF16)


def _outproj_kernel(m_ref, w_ref, x_ref, gate_ref, *rest, final):
    if final:
        unperm_ref, fg_ref, o_ref, m_scr = rest
        _residual_update(m_ref, w_ref, x_ref, gate_ref, o_ref, unperm_ref, m_scr, None)
        x = o_ref[...]
        ms = jnp.mean(x * x, axis=-1, keepdims=True)
        o_ref[...] = x * lax.rsqrt(ms + EPS) * fg_ref[...]
    else:
        gn_ref, shn_ref, scn_ref, o_ref, h_ref = rest
        _residual_update(m_ref, w_ref, x_ref, gate_ref, o_ref, None, None,
                         (gn_ref, shn_ref, scn_ref, h_ref))


def _next_norm_specs():
    return [pl.BlockSpec((1, D), lambda r: (0, 0)),
            pl.BlockSpec((None, 1, D), lambda r: (_mod_row(r, TM), 0, 0)),
            pl.BlockSpec((None, 1, D), lambda r: (_mod_row(r, TM), 0, 1))]


def _outproj(mbuf, w, xs, mods, n_rows, final=None, nxt=None):
    in_specs = [
        pl.BlockSpec((TM, D), lambda r: (r, 0)),
        _layer_weight(w[1]),
        pl.BlockSpec((TM, D), lambda r: (r, 0)),
        pl.BlockSpec((None, 1, D), lambda r: (_mod_row(r, TM), 0, 2)),
    ]
    args = [mbuf, w[0], xs, mods]
    tile = pl.BlockSpec((TM, D), lambda r: (r, 0))
    if final is not None:
        in_specs += [pl.BlockSpec((F_TC, F_TC), lambda r: (0, 0)), pl.BlockSpec((1, D), lambda r: (0, 0))]
        args += list(final)
        return pl.pallas_call(
            functools.partial(_outproj_kernel, final=True),
            grid=(n_rows // TM,),
            in_specs=in_specs,
            out_specs=tile,
            out_shape=jax.ShapeDtypeStruct((n_rows, D), F32),
            scratch_shapes=[pltpu.VMEM((TM, D), BF16)],
            compiler_params=_params(("parallel",), 48),
            name="outproj_final",
        )(*args)
    g_next, mods_next = nxt
    return pl.pallas_call(
        functools.partial(_outproj_kernel, final=False),
        grid=(n_rows // TM,),
        in_specs=in_specs + _next_norm_specs(),
        out_specs=[tile, tile],
        out_shape=[jax.ShapeDtypeStruct((N_ALL, D), F32), jax.ShapeDtypeStruct((N_ALL, D), BF16)],
        input_output_aliases={2: 0},
        compiler_params=_params(("parallel",), 48),
        name="outproj",
    )(*args, g_next, mods_next, mods_next)


def _outproj_join_kernel(ml_ref, mc_ref, w_ref, xl_ref, xc_ref, gate_ref, unperm_ref,
                         gn_ref, shn_ref, scn_ref, o_ref, h_ref, m_scr):
    is_lat = pl.program_id(0) < N_LAT // TM
    nxt = (gn_ref, shn_ref, scn_ref, h_ref)

    @pl.when(is_lat)
    def _():
        _residual_update(ml_ref, w_ref, xl_ref, gate_ref, o_ref, unperm_ref, m_scr, nxt)

    @pl.when(jnp.logical_not(is_lat))
    def _():
        _residual_update(mc_ref, w_ref, xc_ref, gate_ref, o_ref, None, None, nxt)


def _outproj_join(m_lat, m_ctx, w, x_lat, x_ctx, mods, unperm, nxt):
    nl = N_LAT // TM
    lat = pl.BlockSpec((TM, D), lambda r: (jnp.minimum(r, nl - 1), 0))
    cxt = pl.BlockSpec((TM, D), lambda r: (jnp.maximum(r - nl, 0), 0))
    tile = pl.BlockSpec((TM, D), lambda r: (r, 0))
    g_next, mods_next = nxt
    return pl.pallas_call(
        _outproj_join_kernel,
        grid=(N_ALL // TM,),
        in_specs=[lat, cxt, _layer_weight(w[1]), lat, cxt,
                  pl.BlockSpec((None, 1, D), lambda r: (_mod_row(r, TM), 0, 2)),
                  pl.BlockSpec((F_TC, F_TC), lambda r: (0, 0))] + _next_norm_specs(),
        out_specs=[tile, tile],
        out_shape=[jax.ShapeDtypeStruct((N_ALL, D), F32), jax.ShapeDtypeStruct((N_ALL, D), BF16)],
        scratch_shapes=[pltpu.VMEM((TM, D), BF16)],
        compiler_params=_params(("parallel",), 48),
        name="outproj_join",
    )(m_lat, m_ctx, w[0], x_lat, x_ctx, mods, unperm, g_next, mods_next, mods_next)


def _mlstm_qkv_kernel(h_ref, w_ref, wg_ref, bg_ref, out_ref, gt_ref):
    gt_ref[...] = jnp.dot(h_ref[...], wg_ref[...], preferred_element_type=F32) + bg_ref[...]
    for c in range(M_HALF // TN):
        sl = slice(c * TN, (c + 1) * TN)
        acc = jnp.dot(h_ref[...], w_ref[:, sl], preferred_element_type=F32)
        if c < M_QK // TN:
            acc = acc * (M_DQK ** -0.5)
        out_ref[:, sl] = acc.astype(BF16)


def _mlstm_oz_kernel(h_ref, wo_ref, wz_ref, out_ref):
    for c in range(M_V // TN):
        sl = slice(c * TN, (c + 1) * TN)
        o = jnp.dot(h_ref[...], wo_ref[:, sl], preferred_element_type=F32)
        out_ref[:, sl] = _sigmoid(o).astype(BF16)
    for c in range(M_V // TN):
        sl = slice(c * TN, (c + 1) * TN)
        z = jnp.dot(h_ref[...], wz_ref[:, sl], preferred_element_type=F32)
        out_ref[:, M_V + c * TN:M_V + (c + 1) * TN] = _silu(z).astype(BF16)


def _mlstm_proj(h, w_in, w_z, wg, bg):
    tile = pl.BlockSpec((TM, D), lambda r: (r, 0))
    wide = pl.BlockSpec((TM, M_HALF), lambda r: (r, 0))
    qkv, gates = pl.pallas_call(
        _mlstm_qkv_kernel,
        grid=(N_ALL // TM,),
        in_specs=[tile, _resident((D, M_HALF), lambda r: (0, 0)),
                  pl.BlockSpec((D, M_GATES), lambda r: (0, 0)), pl.BlockSpec((1, M_GATES), lambda r: (0, 0))],
        out_specs=[wide, pl.BlockSpec((TM, M_GATES), lambda r: (r, 0))],
        out_shape=[jax.ShapeDtypeStruct((N_ALL, M_HALF), BF16), jax.ShapeDtypeStruct((N_ALL, M_GATES), F32)],
        compiler_params=_params(("parallel",), 48),
        name="mlstm_qkv",
    )(h, w_in, wg, bg)
    oz = pl.pallas_call(
        _mlstm_oz_kernel,
        grid=(N_ALL // TM,),
        in_specs=[tile, _resident((D, M_V), lambda r: (0, M_HALF // M_V)), _resident((D, M_V), lambda r: (0, 0))],
        out_specs=wide,
        out_shape=jax.ShapeDtypeStruct((N_ALL, M_HALF), BF16),
        compiler_params=_params(("parallel",), 48),
        name="mlstm_oz",
    )(h, w_in, w_z)
    return qkv, oz, gates


def _split3(x):
    hi = x.astype(BF16)
    r1 = x - hi.astype(F32)
    mid = r1.astype(BF16)
    lo = (r1 - mid.astype(F32)).astype(BF16)
    return hi, mid, lo


def _cummax_rows(x, reverse):
    rows = x.shape[0]
    row = lax.broadcasted_iota(jnp.int32, x.shape, 0)
    sh = 1
    while sh < rows:
        if reverse:
            x = jnp.where(row < rows - sh, jnp.maximum(x, pltpu.roll(x, rows - sh, 0)), x)
        else:
            x = jnp.where(row >= sh, jnp.maximum(x, pltpu.roll(x, sh, 0)), x)
        sh *= 2
    return x


def _lanes(col, width):
    return jnp.broadcast_to(col, (col.shape[0], width))


def _mlstm_scan_kernel(q_ref, k_ref, v_ref, gt_ref, *rest, reverse):
    if reverse:
        o_ref, cn_scr, m_scr = rest
    else:
        hb_ref, so_ref, sz_ref, hn_ref, o_ref, cn_scr, m_scr = rest

    @pl.when(pl.program_id(1) == 0)
    def _():
        cn_scr[...] = jnp.zeros_like(cn_scr)
        m_scr[...] = jnp.zeros_like(m_scr)

    L = M_L
    lane0 = M_HEADS if reverse else 0
    ig = gt_ref[:, :128]
    ls = _log_sigmoid(gt_ref[:, 128:])
    row = lax.broadcasted_iota(jnp.int32, (L, L), 0)
    col = lax.broadcasted_iota(jnp.int32, (L, L), 1)
    order = (row <= col) if reverse else (row >= col)
    tri = jnp.where(order, 1.0, 0.0).astype(BF16)
    hi, mid, lo = _split3(ls)
    b = (jnp.dot(tri, hi, preferred_element_type=F32)
         + jnp.dot(tri, mid, preferred_element_type=F32)
         + jnp.dot(tri, lo, preferred_element_type=F32))
    end = 0 if reverse else L - 1

    m_prev = m_scr[...]
    r = ig - b
    inter = b + m_prev
    m_t = jnp.maximum(inter, b + _cummax_rows(r, reverse))
    a_all = jnp.exp(inter - m_t)
    u_all = b - m_t
    en_all = jnp.exp(-m_t)
    b_end = b[end:end + 1, :]
    gl = b_end - b + ig
    m_new = jnp.maximum(b_end + m_prev, jnp.max(gl, axis=0, keepdims=True))
    w_all = jnp.exp(gl - m_new)
    decay_all = jnp.exp(b_end + m_prev - m_new)
    r_t = r.T
    m_scr[...] = m_new

    ones_bf = jnp.ones((L, 128), BF16)
    mean_dv = jnp.full((M_DV, 128), 1.0 / M_DV, BF16)
    nt = (((1,), (1,)), ((), ()))
    tn = (((0,), (0,)), ((), ()))

    def stage_a(h):
        l = lane0 + h
        q = q_ref[:, h * M_DQK:(h + 1) * M_DQK]
        k = k_ref[:, h * M_DQK:(h + 1) * M_DQK]
        qk = lax.dot_general(q, k, nt, preferred_element_type=F32)
        qcn = jnp.dot(q, cn_scr[h].astype(BF16), preferred_element_type=F32)
        p = jnp.exp(jnp.where(order, u_all[:, l:l + 1] + r_t[l:l + 1, :], -jnp.inf))
        return qk, qcn, p

    def stage_b(h, qk, p):
        s = qk * p
        s_hi = s.astype(BF16)
        s_lo = (s - s_hi.astype(F32)).astype(BF16)
        vo = jnp.concatenate([v_ref[:, h * M_DV:(h + 1) * M_DV], ones_bf], axis=1)
        sv = jnp.dot(s_hi, vo, preferred_element_type=F32)
        return sv, jnp.dot(s_lo, ones_bf, preferred_element_type=F32)

    def stage_c(h, qcn, sv, rs_lo):
        l = lane0 + h
        a = _lanes(a_all[:, l:l + 1], 128)
        den = a * qcn[:, M_DV:] + (sv[:, M_DV:] + rs_lo)
        inv = 1.0 / jnp.maximum(jnp.abs(den), _lanes(en_all[:, l:l + 1], 128))
        cols = [slice(h * M_DV + c0, h * M_DV + c0 + 128) for c0 in range(0, M_DV, 128)]
        hid = [(a * qcn[:, c0:c0 + 128] + sv[:, c0:c0 + 128]) * inv for c0 in range(0, M_DV, 128)]
        if reverse:
            for sl, hv in zip(cols, hid):
                o_ref[:, sl] = hv
            return
        ys = [so_ref[:, sl].astype(F32) * (hv + hb_ref[:, sl]) for sl, hv in zip(cols, hid)]
        sq = jnp.concatenate([(y * y).astype(BF16) for y in ys], axis=1)
        ms = jnp.dot(sq, mean_dv, preferred_element_type=F32)
        scale = lax.rsqrt(ms + EPS)
        for sl, y in zip(cols, ys):
            o_ref[:, sl] = (y * scale * hn_ref[:, sl] * sz_ref[:, sl].astype(F32)).astype(BF16)

    def stage_d(h):
        l = lane0 + h
        k = k_ref[:, h * M_DQK:(h + 1) * M_DQK]
        kw = (k.astype(F32) * _lanes(w_all[:, l:l + 1], M_DQK)).astype(BF16)
        vo = jnp.concatenate([v_ref[:, h * M_DV:(h + 1) * M_DV], ones_bf], axis=1)
        upd = lax.dot_general(kw, vo, tn, preferred_element_type=F32)
        cn_scr[h] = decay_all[:, l:l + 1] * cn_scr[h] + upd

    sa = {}
    sb = {}
    for step in range(M_HEADS + 2):
        if step < M_HEADS:
            sa[step] = stage_a(step)
        if 1 <= step <= M_HEADS:
            h = step - 1
            sb[h] = stage_b(h, sa[h][0], sa[h][2])
        if step >= 2:
            h = step - 2
            stage_c(h, sa[h][1], *sb[h])
            stage_d(h)


def _mlstm_scan(qkv, gates, oz=None, hb=None, hn=None):
    reverse = hb is None
    nlc = SEQ // M_L
    ctx_blk0 = N_LAT // M_L

    def blk(b, i):
        lat = (nlc - i) if reverse else (i - 1)
        return jnp.where(i == 0, ctx_blk0 + b, b * nlc + lat)

    wide = lambda cb: pl.BlockSpec((M_L, M_V), lambda b, i: (blk(b, i), cb))
    in_specs = [
        pl.BlockSpec((M_L, M_QK), lambda b, i: (blk(b, i), 0)),
        pl.BlockSpec((M_L, M_QK), lambda b, i: (blk(b, i), 1)),
        wide(1),
        pl.BlockSpec((M_L, M_GATES), lambda b, i: (blk(b, i), 0)),
    ]
    args = [qkv, qkv, qkv, gates]
    if not reverse:
        in_specs += [wide(0), wide(0), wide(1), pl.BlockSpec((1, M_V), lambda b, i: (0, 0))]
        args += [hb, oz, oz, hn]
    return pl.pallas_call(
        functools.partial(_mlstm_scan_kernel, reverse=reverse),
        grid=(BATCH, nlc + 1),
        in_specs=in_specs,
        out_specs=wide(0),
        out_shape=jax.ShapeDtypeStruct((N_ALL, M_V), F32 if reverse else BF16),
        scratch_shapes=[
            pltpu.VMEM((M_HEADS, M_DQK, M_DV + 128), F32),
            pltpu.VMEM((1, 128), F32),
        ],
        compiler_params=_params(("parallel", "arbitrary"), 48),
        name="mlstm_scan_bwd" if reverse else "mlstm_scan_fwd",
    )(*args)


A_QKW = A_Q + A_KVW


def _attn_qk_kernel(h_ref, w_ref, gain_ref, cos_ref, sin_ref, q_ref, k_ref):
    mean_mat = jnp.full((A_HD, A_HD), 1.0 / A_HD, BF16)
    cos = cos_ref[...]
    sin = sin_ref[...]
    for c in range(A_QKW // TN):
        acc = jnp.dot(h_ref[...], w_ref[:, c * TN:(c + 1) * TN], preferred_element_type=F32)
        for hh in range(TN // A_HD):
            lo = c * TN + hh * A_HD
            a = acc[:, hh * A_HD:(hh + 1) * A_HD]
            ms = jnp.dot((a * a).astype(BF16), mean_mat, preferred_element_type=F32)
            a = a * lax.rsqrt(ms + EPS) * gain_ref[:, lo:lo + A_HD]
            a = (a * cos + pltpu.roll(a, A_HD // 2, 1) * sin).astype(BF16)
            if lo < A_Q:
                q_ref[:, lo:lo + A_HD] = a
            else:
                k_ref[:, lo - A_Q:lo - A_Q + A_HD] = a


def _attn_qk(h, w, gain, cos, sin):
    lat_tiles = N_LAT // TM
    rope_blk = lambda r: (jnp.where(r < lat_tiles, r % (SEQ // TM), SEQ // TM), 0)
    return pl.pallas_call(
        _attn_qk_kernel,
        grid=(N_ALL // TM,),
        in_specs=[
            pl.BlockSpec((TM, D), lambda r: (r, 0)),
            _resident((D, A_QKW), lambda r: (0, 0)),
            pl.BlockSpec((1, A_QKW), lambda r: (0, 0)),
            pl.BlockSpec((TM, A_HD), rope_blk),
            pl.BlockSpec((TM, A_HD), rope_blk),
        ],
        out_specs=[
            pl.BlockSpec((TM, A_Q), lambda r: (r, 0)),
            pl.BlockSpec((TM, A_KVW), lambda r: (r, 0)),
        ],
        out_shape=[
            jax.ShapeDtypeStruct((N_ALL, A_Q), BF16),
            jax.ShapeDtypeStruct((N_ALL, A_KVW), BF16),
        ],
        compiler_params=_params(("parallel",), 48),
        name="attn_qk",
    )(h, w, gain, cos, sin)


def _attn_vz_kernel(h_ref, w_ref, v_ref, zg_ref):
    v_ref[...] = jnp.dot(h_ref[...], w_ref[:, :A_KVW], preferred_element_type=F32).astype(BF16)
    for c in range(A_Q // TN):
        lo = A_KVW + c * TN
        z = jnp.dot(h_ref[...], w_ref[:, lo:lo + TN], preferred_element_type=F32)
        zg_ref[:, c * TN:(c + 1) * TN] = _silu(z).astype(BF16)


def _attn_vz(h, w):
    return pl.pallas_call(
        _attn_vz_kernel,
        grid=(N_ALL // TM,),
        in_specs=[
            pl.BlockSpec((TM, D), lambda r: (r, 0)),
            _resident((D, A_KVW + A_Q), lambda r: (0, 1)),
        ],
        out_specs=[
            pl.BlockSpec((TM, A_KVW), lambda r: (r, 0)),
            pl.BlockSpec((TM, A_Q), lambda r: (r, 0)),
        ],
        out_shape=[
            jax.ShapeDtypeStruct((N_ALL, A_KVW), BF16),
            jax.ShapeDtypeStruct((N_ALL, A_Q), BF16),
        ],
        compiler_params=_params(("parallel",), 48),
        name="attn_vz",
    )(h, w)


A_LAT_TILES = SEQ // A_TQ
A_GPS = 2
A_SW = A_GPS * A_GW
A_KW = A_GPS * A_HD


def _attn_kernel(q_ref, kl_ref, kc_ref, vl_ref, vc_ref, zg_ref, o_ref, vol_scr, voc_scr):
    t = pl.program_id(2)
    nt = (((1,), (1,)), ((), ()))
    heads = A_SW // A_HD
    hpg = A_GW // A_HD

    @pl.when(t == 0)
    def _():
        for g in range(A_GPS):
            vol_scr[g, :, :A_HD] = vl_ref[:, g * A_HD:(g + 1) * A_HD]
            vol_scr[g, :, A_HD:] = jnp.ones((SEQ, A_HD), BF16)
            voc_scr[g, :, :A_HD] = vc_ref[:, g * A_HD:(g + 1) * A_HD]
            voc_scr[g, :, A_HD:] = jnp.ones((CTX, A_HD), BF16)

    def run(with_latent_keys):
        def scores(hh):
            kcols = slice((hh // hpg) * A_HD, (hh // hpg + 1) * A_HD)
            q = q_ref[:, hh * A_HD:(hh + 1) * A_HD]
            s_c = lax.dot_general(q, kc_ref[:, kcols], nt, preferred_element_type=F32)
            s_l = (lax.dot_general(q, kl_ref[:, kcols], nt, preferred_element_type=F32)
                   if with_latent_keys else None)
            return s_c, s_l

        def probs(s_c, s_l):
            m = jnp.max(s_c, axis=-1, keepdims=True)
            if with_latent_keys:
                m = jnp.maximum(m, jnp.max(s_l, axis=-1, keepdims=True))
            p_c = jnp.exp(s_c - m).astype(BF16)
            p_l = jnp.exp(s_l - m).astype(BF16) if with_latent_keys else None
            return p_c, p_l

        def output(hh, p_c, p_l):
            sl = slice(hh * A_HD, (hh + 1) * A_HD)
            acc = jnp.dot(p_c, voc_scr[hh // hpg], preferred_element_type=F32)
            if with_latent_keys:
                acc = acc + jnp.dot(p_l, vol_scr[hh // hpg], preferred_element_type=F32)
            o_ref[:, sl] = (acc[:, :A_HD] / acc[:, A_HD:] * zg_ref[:, sl].astype(F32)).astype(BF16)

        s, p = {}, {}
        for step in range(heads + 2):
            if step < heads:
                s[step] = scores(step)
            if 1 <= step <= heads:
                p[step - 1] = probs(*s[step - 1])
            if step >= 2:
                output(step - 2, *p[step - 2])

    @pl.when(t < A_LAT_TILES)
    def _():
        run(True)

    @pl.when(t >= A_LAT_TILES)
    def _():
        run(False)


def _attn(q, k, v, zg):
    ctx_blk0 = N_LAT // A_TQ
    qrow = lambda b, g, t: jnp.where(t < A_LAT_TILES, b * A_LAT_TILES + t, ctx_blk0 + b)
    return pl.pallas_call(
        _attn_kernel,
        grid=(BATCH, A_KV // A_GPS, A_LAT_TILES + 1),
        in_specs=[
            pl.BlockSpec((A_TQ, A_SW), lambda b, g, t: (qrow(b, g, t), g)),
            pl.BlockSpec((SEQ, A_KW), lambda b, g, t: (b, g)),
            pl.BlockSpec((CTX, A_KW), lambda b, g, t: (N_LAT // CTX + b, g)),
            pl.BlockSpec((SEQ, A_KW), lambda b, g, t: (b, g)),
            pl.BlockSpec((CTX, A_KW), lambda b, g, t: (N_LAT // CTX + b, g)),
            pl.BlockSpec((A_TQ, A_SW), lambda b, g, t: (qrow(b, g, t), g)),
        ],
        out_specs=pl.BlockSpec((A_TQ, A_SW), lambda b, g, t: (qrow(b, g, t), g)),
        out_shape=jax.ShapeDtypeStruct((N_ALL, A_Q), BF16),
        scratch_shapes=[pltpu.VMEM((A_GPS, SEQ, 2 * A_HD), BF16), pltpu.VMEM((A_GPS, CTX, 2 * A_HD), BF16)],
        compiler_params=_params(("parallel", "parallel", "arbitrary"), 48),
        name="attn",
    )(q, k, k, v, v, zg)


def _dft_tables(n, scale):
    idx = jnp.arange(n, dtype=jnp.int32)
    ang = ((idx[:, None] * idx[None, :]) % n).astype(F32) * (2.0 * math.pi / n)
    return (jnp.cos(ang) * scale).astype(BF16), (jnp.sin(ang) * scale).astype(BF16)


def _dif_tables():
    jt = jnp.arange(F_M, dtype=jnp.int32)
    r = jnp.arange(F_R, dtype=jnp.int32)
    k = ((F_R * jt[None, :, None] + r[:, None, None]) * jt[None, None, :]) % SEQ
    ang = k.astype(F32) * (2.0 * math.pi / SEQ)
    return (jnp.cos(ang) * SEQ ** -0.5).astype(BF16), (jnp.sin(ang) * SEQ ** -0.5).astype(BF16)


def _perm_tables():
    n = jnp.arange(F_TC)
    src = F_R * (n % F_PR) + n // F_PR
    perm = (src[:, None] == n[None, :]).astype(BF16)
    return perm, perm.T


def _rope_tables():
    t = jnp.arange(SEQ)
    freqs = ROPE_THETA ** (-jnp.arange(0, A_HD // 2, 2, dtype=F32) / (A_HD // 2))
    ang = jnp.concatenate([(t // GRID_W).astype(F32)[:, None] * freqs,
                           (t % GRID_W).astype(F32)[:, None] * freqs], axis=-1)
    cos = jnp.concatenate([jnp.cos(ang), jnp.cos(ang)], axis=-1)
    sin = jnp.concatenate([-jnp.sin(ang), jnp.sin(ang)], axis=-1)
    pad = lambda a, fill: jnp.concatenate([a, jnp.full((TM, A_HD), fill, F32)], axis=0)
    return pad(cos, 1.0), pad(sin, 0.0)


def _split_heads_even_odd(w, heads):
    lead = w.shape[:-1]
    pairs = w.reshape(lead + (heads, A_HD // 2, 2))
    return jnp.swapaxes(pairs, -1, -2).reshape(lead + (heads * A_HD,))


def kernel(x, c, ctx, c_ctx, ada_w, ada_b, norm_g, fnet_w_gate, fnet_w_out, mlstm_w_in, mlstm_b_gate,
           mlstm_hn, mlstm_w_out, attn_w_in, attn_qn, attn_kn, attn_w_out, final_g):
    cc = jnp.concatenate([c, c_ctx[None, :], jnp.zeros((MOD_ROWS - BATCH - 1, D), F32)], axis=0)
    mods_all = _modvec(cc, ada_w, ada_b).reshape(DEPTH, MOD_ROWS, 1, 3 * D)

    cc_c, sc_c = _dft_tables(F_GW, F_GW ** -0.5)
    cs_chan = jnp.concatenate([cc_c, sc_c], axis=1)
    cr_lat, sr_lat = _dif_tables()
    ct_ctx, st_ctx = _dft_tables(CTX, CTX ** -0.5)
    perm, unperm = _perm_tables()
    fnet_wg_bf = fnet_w_gate.astype(BF16)
    fnet_wo_bf = fnet_w_out.astype(BF16)

    xs = None
    h = None
    for i in range(DEPTH):
        kind, j = i % 3, i // 3
        last = i == DEPTH - 1
        mods = mods_all[i]
        g = norm_g[i].reshape(1, D)
        n_rows = N_LAT if last else N_ALL
        nxt = None if last else (norm_g[i + 1].reshape(1, D), mods_all[i + 1])
        if kind == 0:
            assert i == 0 or last
            wg = (fnet_wg_bf, j)
            w_out = (fnet_wo_bf, j)
            x_lat = x.reshape(N_LAT, D) if i == 0 else xs
            p, q, hp = _fnet_chan_dif(*((x_lat, (g, mods)) if i == 0 else (h, None)), cs_chan, perm)
            m_lat = _fnet_mix_lat(hp, cr_lat, sr_lat, p, q, wg)
            if i == 0:
                x_ctx = ctx.reshape(N_CTX, D)
                a, b = _fnet_chan_ctx(x_ctx, 0, g, mods, cs_chan)
                m_ctx = _fnet_mix_ctx(x_ctx, 0, g, mods, ct_ctx, st_ctx, a, b, wg)
                xs, h = _outproj_join(m_lat, m_ctx, w_out, x_lat, x_ctx, mods, unperm, nxt)
            else:
                out = _outproj(m_lat, w_out, xs, mods, n_rows, final=(unperm, final_g.reshape(1, D)))
            continue
        if kind == 1:
            w_in = mlstm_w_in[j].astype(BF16)
            n_main = 2 * M_QK + 2 * M_V
            gate_tiles = lambda gcols: jnp.pad(
                jnp.concatenate([gcols[..., 0:8], gcols[..., 16:24]], axis=-1),
                [(0, 0)] * (gcols.ndim - 1) + [(0, 128 - 2 * M_HEADS)])
            gate_layout = lambda gcols: jnp.concatenate(
                [gate_tiles(gcols), gate_tiles(gcols[..., M_HEADS:])], axis=-1)
            w_gates = gate_layout(w_in[:, n_main:n_main + 4 * M_HEADS])
            b_gates = gate_layout(mlstm_b_gate[j][None, :])
            qkv, oz, gates = _mlstm_proj(h, w_in, w_in[:, n_main + 4 * M_HEADS:], w_gates, b_gates)
            hb = _mlstm_scan(qkv, gates)
            mbuf = _mlstm_scan(qkv, gates, oz, hb, mlstm_hn[j].reshape(1, M_V))
            w_out = (mlstm_w_out.astype(BF16), j)
        else:
            w_in = attn_w_in[j].astype(BF16)
            w_qk = _split_heads_even_odd(w_in[:, :A_QKW], A_HEADS + A_KV)
            qn = _split_heads_even_odd(attn_qn[j], 1) * (A_HD ** -0.5)
            kn = _split_heads_even_odd(attn_kn[j], 1)
            gain = jnp.concatenate([jnp.tile(qn, A_HEADS), jnp.tile(kn, A_KV)]).reshape(1, A_QKW)
            cos, sin = _rope_tables()
            q, k = _attn_qk(h, w_qk, gain, cos, sin)
            v, zg = _attn_vz(h, w_in)
            mbuf = _attn(q, k, v, zg)
            w_out = (attn_w_out.astype(BF16), j)
        assert not last
        xs, h = _outproj(mbuf, w_out, xs, mods, n_rows, nxt=nxt)

    return out.reshape(BATCH, SEQ, D)
```

```python
import functools
import math

import jax
import jax.numpy as jnp
from jax import lax
from jax.experimental import pallas as pl
from jax.experimental.pallas import tpu as pltpu

F32 = jnp.float32
BF16 = jnp.bfloat16

D = 2048
BATCH = 16
SEQ = 2048
CTX = 256
DEPTH = 4
EPS = 1e-6
N_LAT = BATCH * SEQ
N_CTX = BATCH * CTX
N_ALL = N_LAT + N_CTX
MOD_ROWS = 24
CTX_MOD_ROW = BATCH

F_GROUPS = 4
F_GW = D // F_GROUPS

M_HEADS = 8
M_DQK = 128
M_DV = 256
M_QK = M_HEADS * M_DQK
M_V = M_HEADS * M_DV
M_L = 256
M_GATES = 256
M_HALF = 2 * M_QK + M_V

A_HEADS = 16
A_KV = 4
A_HD = 128
A_Q = A_HEADS * A_HD
A_KVW = A_KV * A_HD
A_GW = A_Q // A_KV
A_TQ = 256
GRID_W = 64
ROPE_THETA = 10000.0

TM = 512
TP = 1024
TN = 512
MIB = 1024 * 1024


def _params(sem, vmem_mib):
    return pltpu.CompilerParams(dimension_semantics=sem, vmem_limit_bytes=vmem_mib * MIB)


def _resident(shape, index_map):
    return pl.BlockSpec(shape, index_map, pipeline_mode=pl.Buffered(1))


def _layer_weight(j):
    return _resident((None, D, D), lambda *ids: (j, 0, 0))


def _sigmoid(x):
    return 1.0 / (1.0 + jnp.exp(-x))


def _silu(x):
    return x * _sigmoid(x)


def _log_sigmoid(x):
    return jnp.minimum(x, 0.0) - jnp.log1p(jnp.exp(-jnp.abs(x)))


def _modnorm(x, g, scale, shift):
    ms = jnp.mean(x * x, axis=-1, keepdims=True)
    y = x * lax.rsqrt(ms + EPS) * g
    return y * (1.0 + scale) + shift


def _store_modnorm(x_ref, g_ref, sc_ref, sh_ref, h_scr):
    rows = x_ref.shape[0]
    step = min(rows, 256)
    for r0 in range(0, rows, step):
        h_scr[r0:r0 + step, :] = _modnorm(x_ref[r0:r0 + step, :], g_ref[...], sc_ref[...],
                                          sh_ref[...]).astype(BF16)


def _mod_row(r, tm):
    return jnp.where(r < N_LAT // tm, r // (SEQ // tm), CTX_MOD_ROW)


def _modvec_kernel(c_ref, w_ref, b_ref, o_ref):
    s = _silu(c_ref[...])
    o_ref[...] = jnp.dot(s, w_ref[...], preferred_element_type=F32,
                         precision=lax.Precision.HIGHEST) + b_ref[...]


def _modvec(cc, ada_w, ada_b):
    tn = 2048
    return pl.pallas_call(
        _modvec_kernel,
        grid=(DEPTH, 3 * D // tn),
        in_specs=[
            pl.BlockSpec((MOD_ROWS, D), lambda i, j: (0, 0)),
            pl.BlockSpec((None, D, tn), lambda i, j: (i, 0, j)),
            pl.BlockSpec((None, 1, tn), lambda i, j: (i, 0, j)),
        ],
        out_specs=pl.BlockSpec((None, MOD_ROWS, tn), lambda i, j: (i, 0, j)),
        out_shape=jax.ShapeDtypeStruct((DEPTH, MOD_ROWS, 3 * D), F32),
        compiler_params=_params(("parallel", "parallel"), 48),
        name="modvec",
    )(cc, ada_w, ada_b.reshape(DEPTH, 1, 3 * D))


def _fnet_chan_kernel(x_ref, g_ref, sh_ref, sc_ref, cs_ref, a_ref, b_ref):
    h = _modnorm(x_ref[...], g_ref[...], sc_ref[...], sh_ref[...]).astype(BF16)
    for grp in range(F_GROUPS):
        sl = slice(grp * F_GW, (grp + 1) * F_GW)
        p = jnp.dot(h[:, sl], cs_ref[...], preferred_element_type=F32)
        a_ref[:, sl] = p[:, :F_GW].astype(BF16)
        b_ref[:, sl] = p[:, F_GW:].astype(BF16)


def _fnet_chan_ctx(xc, row_blk0, g, mods, cs_c):
    tm = 512
    out = jax.ShapeDtypeStruct((N_CTX, D), BF16)
    mod = lambda chunk: pl.BlockSpec((None, 1, D), lambda r: (CTX_MOD_ROW, 0, chunk))
    return pl.pallas_call(
        _fnet_chan_kernel,
        grid=(N_CTX // tm,),
        in_specs=[pl.BlockSpec((tm, D), lambda r: (row_blk0 + r, 0)),
                  pl.BlockSpec((1, D), lambda r: (0, 0)), mod(0), mod(1),
                  pl.BlockSpec((F_GW, 2 * F_GW), lambda r: (0, 0))],
        out_specs=[pl.BlockSpec((tm, D), lambda r: (r, 0))] * 2,
        out_shape=[out, out],
        compiler_params=_params(("parallel",), 48),
        name="fnet_chan_ctx",
    )(xc, g, mods, mods, cs_c)


F_R = 4
F_M = SEQ // F_R
F_TC = 256
F_PR = F_TC // F_R


def _fnet_chan_dif_kernel(x0_ref, x1_ref, x2_ref, x3_ref, *rest, prenormed):
    if prenormed:
        cs_ref, perm_ref, p_ref, q_ref, hp_ref = rest
        hs = [x[...] for x in (x0_ref, x1_ref, x2_ref, x3_ref)]
    else:
        g_ref, sh_ref, sc_ref, cs_ref, perm_ref, p_ref, q_ref, hp_ref = rest
        hs = [_modnorm(x[...], g_ref[...], sc_ref[...], sh_ref[...]).astype(BF16)
              for x in (x0_ref, x1_ref, x2_ref, x3_ref)]
    for qi, h in enumerate(hs):
        hp = jnp.dot(perm_ref[...], h, preferred_element_type=F32).astype(BF16)
        for r in range(F_R):
            hp_ref[qi, r] = hp[r * F_PR:(r + 1) * F_PR, :]
    for grp in range(F_GROUPS):
        sl = slice(grp * F_GW, (grp + 1) * F_GW)
        ab = [jnp.dot(h[:, sl], cs_ref[...], preferred_element_type=F32) for h in hs]
        a = [t[:, :F_GW] for t in ab]
        b = [t[:, F_GW:] for t in ab]
        sa02, da02, sa13, da13 = a[0] + a[2], a[0] - a[2], a[1] + a[3], a[1] - a[3]
        sb02, db02, sb13, db13 = b[0] + b[2], b[0] - b[2], b[1] + b[3], b[1] - b[3]
        re = (sa02 + sa13, da02 - db13, sa02 - sa13, da02 + db13)
        im = (sb02 + sb13, db02 + da13, sb02 - sb13, db02 - da13)
        for r in range(F_R):
            p_ref[r, :, sl] = re[r].astype(BF16)
            q_ref[r, :, sl] = im[r].astype(BF16)


def _fnet_chan_dif(xl, norm, cs_c, perm):
    nt = F_M // F_TC
    xspec = lambda q: pl.BlockSpec((F_TC, D), lambda b, i: (b * (SEQ // F_TC) + q * nt + i, 0))
    mod = lambda chunk: pl.BlockSpec((None, 1, D), lambda b, i: (b, 0, chunk))
    out = jax.ShapeDtypeStruct((BATCH, F_R, F_M, D), BF16)
    ospec = pl.BlockSpec((None, F_R, F_TC, D), lambda b, i: (b, 0, i, 0))
    norm_specs = [] if norm is None else [pl.BlockSpec((1, D), lambda b, i: (0, 0)), mod(0), mod(1)]
    norm_args = [] if norm is None else [norm[0], norm[1], norm[1]]
    p, q, hp = pl.pallas_call(
        functools.partial(_fnet_chan_dif_kernel, prenormed=norm is None),
        grid=(BATCH, nt),
        in_specs=[xspec(0), xspec(1), xspec(2), xspec(3)] + norm_specs + [
            pl.BlockSpec((F_GW, 2 * F_GW), lambda b, i: (0, 0)),
            pl.BlockSpec((F_TC, F_TC), lambda b, i: (0, 0))],
        out_specs=[ospec, ospec,
                   pl.BlockSpec((None, F_R, None, F_R, F_PR, D), lambda b, i: (b, 0, i, 0, 0, 0))],
        out_shape=[out, out, jax.ShapeDtypeStruct((BATCH, F_R, nt, F_R, F_PR, D), BF16)],
        compiler_params=_params(("parallel", "parallel"), 56),
        name="fnet_chan_dif",
    )(xl, xl, xl, xl, *norm_args, cs_c, perm)
    return p, q, hp.reshape(BATCH, SEQ // F_TC, F_R, F_PR, D)


def _fnet_mix_kernel(x_ref, g_ref, sh_ref, sc_ref, c_ref, s_ref, p_ref, q_ref, wg_ref, o_ref, h_scr):
    _store_modnorm(x_ref, g_ref, sc_ref, sh_ref, h_scr)
    for c in range(D // TN):
        sl = slice(c * TN, (c + 1) * TN)
        y = jnp.dot(c_ref[...], p_ref[:, sl], preferred_element_type=F32)
        y = y - jnp.dot(s_ref[...], q_ref[:, sl], preferred_element_type=F32)
        gate = jnp.dot(h_scr[...], wg_ref[:, sl], preferred_element_type=F32)
        o_ref[:, sl] = (y * _silu(gate)).astype(BF16)


def _fnet_mix_lat_kernel(h_ref, c_ref, s_ref, p_ref, q_ref, wg_ref, o_ref):
    nt = SEQ // F_TC
    h = h_ref[...].reshape(F_M, D)
    for c in range(D // TN):
        sl = slice(c * TN, (c + 1) * TN)
        y = jnp.dot(c_ref[...], p_ref[:, sl], preferred_element_type=F32)
        y = y - jnp.dot(s_ref[...], q_ref[:, sl], preferred_element_type=F32)
        gate = jnp.dot(h, wg_ref[:, sl], preferred_element_type=F32)
        val = (y * _silu(gate)).astype(BF16)
        for t in range(nt):
            o_ref[t, :, sl] = val[t * F_PR:(t + 1) * F_PR, :]


def _fnet_mix_lat(hp, cr, sr, p, q, wg):
    nt = SEQ // F_TC
    tab = pl.BlockSpec((None, F_M, F_M), lambda b, r: (r, 0, 0))
    pq = pl.BlockSpec((None, None, F_M, D), lambda b, r: (b, r, 0, 0))
    tiles = pl.BlockSpec((None, nt, None, F_PR, D), lambda b, r: (b, 0, r, 0, 0))
    out = pl.pallas_call(
        _fnet_mix_lat_kernel,
        grid=(BATCH, F_R),
        in_specs=[tiles, tab, tab, pq, pq, _layer_weight(wg[1])],
        out_specs=tiles,
        out_shape=jax.ShapeDtypeStruct((BATCH, nt, F_R, F_PR, D), BF16),
        compiler_params=_params(("parallel", "parallel"), 48),
        name="fnet_mix_lat",
    )(hp, cr, sr, p, q, wg[0])
    return out.reshape(N_LAT, D)


def _fnet_mix_ctx(xc, row_blk0, g, mods, ct, st, a, b, wg):
    mod = lambda chunk: pl.BlockSpec((None, 1, D), lambda bi: (CTX_MOD_ROW, 0, chunk))
    tab = pl.BlockSpec((CTX, CTX), lambda bi: (0, 0))
    ab = pl.BlockSpec((CTX, D), lambda bi: (bi, 0))
    return pl.pallas_call(
        _fnet_mix_kernel,
        grid=(BATCH,),
        in_specs=[pl.BlockSpec((CTX, D), lambda bi: (row_blk0 + bi, 0)),
                  pl.BlockSpec((1, D), lambda bi: (0, 0)), mod(0), mod(1),
                  tab, tab, ab, ab, _layer_weight(wg[1])],
        out_specs=pl.BlockSpec((CTX, D), lambda bi: (bi, 0)),
        out_shape=jax.ShapeDtypeStruct((N_CTX, D), BF16),
        scratch_shapes=[pltpu.VMEM((CTX, D), BF16)],
        compiler_params=_params(("parallel",), 48),
        name="fnet_mix_ctx",
    )(xc, g, mods, mods, ct, st, a, b, wg[0])


def _residual_update(m_ref, w_ref, x_ref, gate_ref, o_ref, unperm_ref, m_scr, nxt):
    if unperm_ref is not None:
        for t in range(TM // F_TC):
            rows = slice(t * F_TC, (t + 1) * F_TC)
            m_scr[rows, :] = jnp.dot(unperm_ref[...], m_ref[rows, :],
                                     preferred_element_type=F32).astype(BF16)
        m_ref = m_scr
    ss = 0.0
    for c in range(D // TN):
        sl = slice(c * TN, (c + 1) * TN)
        y = jnp.dot(m_ref[...], w_ref[:, sl], preferred_element_type=F32)
        o = x_ref[:, sl] + gate_ref[:, sl] * y
        o_ref[:, sl] = o
        if nxt is not None:
            ss = ss + jnp.sum(o * o, axis=-1, keepdims=True)
    if nxt is not None:
        gn_ref, shn_ref, scn_ref, h_ref = nxt
        rinv = lax.rsqrt(ss * (1.0 / D) + EPS)
        for c in range(D // TN):
            sl = slice(c * TN, (c + 1) * TN)
            gs = gn_ref[:, sl] * (1.0 + scn_ref[:, sl])
            h_ref[:, sl] = (o_ref[:, sl] * rinv * gs + shn_ref[:, sl]).astype(BF16)


def _outproj_kernel(m_ref, w_ref, x_ref, gate_ref, *rest, final):
    if final:
        unperm_ref, fg_ref, o_ref, m_scr = rest
        _residual_update(m_ref, w_ref, x_ref, gate_ref, o_ref, unperm_ref, m_scr, None)
        x = o_ref[...]
        ms = jnp.mean(x * x, axis=-1, keepdims=True)
        o_ref[...] = x * lax.rsqrt(ms + EPS) * fg_ref[...]
    else:
        gn_ref, shn_ref, scn_ref, o_ref, h_ref = rest
        _residual_update(m_ref, w_ref, x_ref, gate_ref, o_ref, None, None,
                         (gn_ref, shn_ref, scn_ref, h_ref))


def _next_norm_specs():
    return [pl.BlockSpec((1, D), lambda r: (0, 0)),
            pl.BlockSpec((None, 1, D), lambda r: (_mod_row(r, TM), 0, 0)),
            pl.BlockSpec((None, 1, D), lambda r: (_mod_row(r, TM), 0, 1))]


def _outproj(mbuf, w, xs, mods, n_rows, final=None, nxt=None):
    nr = n_rows // TM
    tile = pl.BlockSpec((TM, D), lambda r: (r, 0))
    in_specs = [
        tile,
        _layer_weight(w[1]),
        tile,
        pl.BlockSpec((None, 1, D), lambda r: (_mod_row(r, TM), 0, 2)),
    ]
    args = [mbuf, w[0], xs, mods]
    if final is not None:
        in_specs += [pl.BlockSpec((F_TC, F_TC), lambda r: (0, 0)), pl.BlockSpec((1, D), lambda r: (0, 0))]
        args += list(final)
        return pl.pallas_call(
            functools.partial(_outproj_kernel, final=True),
            grid=(n_rows // TM,),
            in_specs=in_specs,
            out_specs=tile,
            out_shape=jax.ShapeDtypeStruct((n_rows, D), F32),
            scratch_shapes=[pltpu.VMEM((TM, D), BF16)],
            compiler_params=_params(("parallel",), 48),
            name="outproj_final",
        )(*args)
    g_next, mods_next = nxt
    return pl.pallas_call(
        functools.partial(_outproj_kernel, final=False),
        grid=(nr,),
        in_specs=in_specs + _next_norm_specs(),
        out_specs=[tile, tile],
        out_shape=[jax.ShapeDtypeStruct((N_ALL, D), F32), jax.ShapeDtypeStruct((N_ALL, D), BF16)],
        input_output_aliases={2: 0},
        compiler_params=_params(("parallel",), 48),
        name="outproj",
    )(*args, g_next, mods_next, mods_next)


def _outproj_join_kernel(ml_ref, mc_ref, w_ref, xl_ref, xc_ref, gate_ref, unperm_ref,
                         gn_ref, shn_ref, scn_ref, o_ref, h_ref, m_scr):
    is_lat = pl.program_id(0) < N_LAT // TM
    nxt = (gn_ref, shn_ref, scn_ref, h_ref)

    @pl.when(is_lat)
    def _():
        _residual_update(ml_ref, w_ref, xl_ref, gate_ref, o_ref, unperm_ref, m_scr, nxt)

    @pl.when(jnp.logical_not(is_lat))
    def _():
        _residual_update(mc_ref, w_ref, xc_ref, gate_ref, o_ref, None, None, nxt)


def _outproj_join(m_lat, m_ctx, w, x_lat, x_ctx, mods, unperm, nxt):
    nl = N_LAT // TM
    lat = pl.BlockSpec((TM, D), lambda r: (jnp.minimum(r, nl - 1), 0))
    cxt = pl.BlockSpec((TM, D), lambda r: (jnp.maximum(r - nl, 0), 0))
    tile = pl.BlockSpec((TM, D), lambda r: (r, 0))
    g_next, mods_next = nxt
    return pl.pallas_call(
        _outproj_join_kernel,
        grid=(N_ALL // TM,),
        in_specs=[lat, cxt, _layer_weight(w[1]), lat, cxt,
                  pl.BlockSpec((None, 1, D), lambda r: (_mod_row(r, TM), 0, 2)),
                  pl.BlockSpec((F_TC, F_TC), lambda r: (0, 0))] + _next_norm_specs(),
        out_specs=[tile, tile],
        out_shape=[jax.ShapeDtypeStruct((N_ALL, D), F32), jax.ShapeDtypeStruct((N_ALL, D), BF16)],
        scratch_shapes=[pltpu.VMEM((TM, D), BF16)],
        compiler_params=_params(("parallel",), 48),
        name="outproj_join",
    )(m_lat, m_ctx, w[0], x_lat, x_ctx, mods, unperm, g_next, mods_next, mods_next)


def _mlstm_qkv_kernel(h_ref, w_ref, wg_ref, bg_ref, out_ref, gt_ref):
    gt_ref[...] = jnp.dot(h_ref[...], wg_ref[...], preferred_element_type=F32) + bg_ref[...]
    for c in range(M_HALF // TN):
        sl = slice(c * TN, (c + 1) * TN)
        acc = jnp.dot(h_ref[...], w_ref[:, sl], preferred_element_type=F32)
        if c < M_QK // TN:
            acc = acc * (M_DQK ** -0.5)
        out_ref[:, sl] = acc.astype(BF16)


def _mlstm_oz_kernel(h_ref, wo_ref, wz_ref, out_ref):
    for c in range(M_V // TN):
        sl = slice(c * TN, (c + 1) * TN)
        o = jnp.dot(h_ref[...], wo_ref[:, sl], preferred_element_type=F32)
        out_ref[:, sl] = _sigmoid(o).astype(BF16)
    for c in range(M_V // TN):
        sl = slice(c * TN, (c + 1) * TN)
        z = jnp.dot(h_ref[...], wz_ref[:, sl], preferred_element_type=F32)
        out_ref[:, M_V + c * TN:M_V + (c + 1) * TN] = _silu(z).astype(BF16)


def _mlstm_proj(h, w_in, w_z, wg, bg):
    tile = pl.BlockSpec((TP, D), lambda r: (r, 0))
    wide = pl.BlockSpec((TP, M_HALF), lambda r: (r, 0))
    qkv, gates = pl.pallas_call(
        _mlstm_qkv_kernel,
        grid=(N_ALL // TP,),
        in_specs=[tile, _resident((D, M_HALF), lambda r: (0, 0)),
                  pl.BlockSpec((D, M_GATES), lambda r: (0, 0)), pl.BlockSpec((1, M_GATES), lambda r: (0, 0))],
        out_specs=[wide, pl.BlockSpec((TP, M_GATES), lambda r: (r, 0))],
        out_shape=[jax.ShapeDtypeStruct((N_ALL, M_HALF), BF16), jax.ShapeDtypeStruct((N_ALL, M_GATES), F32)],
        compiler_params=_params(("parallel",), 48),
        name="mlstm_qkv",
    )(h, w_in, wg, bg)
    oz = pl.pallas_call(
        _mlstm_oz_kernel,
        grid=(N_ALL // TP,),
        in_specs=[tile, _resident((D, M_V), lambda r: (0, M_HALF // M_V)), _resident((D, M_V), lambda r: (0, 0))],
        out_specs=wide,
        out_shape=jax.ShapeDtypeStruct((N_ALL, M_HALF), BF16),
        compiler_params=_params(("parallel",), 48),
        name="mlstm_oz",
    )(h, w_in, w_z)
    return qkv, oz, gates


def _split3(x):
    hi = x.astype(BF16)
    r1 = x - hi.astype(F32)
    mid = r1.astype(BF16)
    lo = (r1 - mid.astype(F32)).astype(BF16)
    return hi, mid, lo


def _cummax_rows(x, reverse):
    rows = x.shape[0]
    row = lax.broadcasted_iota(jnp.int32, x.shape, 0)
    sh = 1
    while sh < rows:
        if reverse:
            x = jnp.where(row < rows - sh, jnp.maximum(x, pltpu.roll(x, rows - sh, 0)), x)
        else:
            x = jnp.where(row >= sh, jnp.maximum(x, pltpu.roll(x, sh, 0)), x)
        sh *= 2
    return x


def _lanes(col, width):
    return jnp.broadcast_to(col, (col.shape[0], width))


def _mlstm_scan_kernel(q_ref, k_ref, v_ref, gt_ref, *rest, reverse):
    if reverse:
        o_ref, cn_scr, m_scr = rest
    else:
        hb_ref, so_ref, sz_ref, hn_ref, o_ref, cn_scr, m_scr = rest

    @pl.when(pl.program_id(1) == 0)
    def _():
        cn_scr[...] = jnp.zeros_like(cn_scr)
        m_scr[...] = jnp.zeros_like(m_scr)

    L = M_L
    lane0 = M_HEADS if reverse else 0
    ig = gt_ref[:, :128]
    ls = _log_sigmoid(gt_ref[:, 128:])
    row = lax.broadcasted_iota(jnp.int32, (L, L), 0)
    col = lax.broadcasted_iota(jnp.int32, (L, L), 1)
    order = (row <= col) if reverse else (row >= col)
    tri = jnp.where(order, 1.0, 0.0).astype(BF16)
    hi, mid, lo = _split3(ls)
    b = (jnp.dot(tri, hi, preferred_element_type=F32)
         + jnp.dot(tri, mid, preferred_element_type=F32)
         + jnp.dot(tri, lo, preferred_element_type=F32))
    end = 0 if reverse else L - 1

    m_prev = m_scr[...]
    r = ig - b
    inter = b + m_prev
    m_t = jnp.maximum(inter, b + _cummax_rows(r, reverse))
    a_all = jnp.exp(inter - m_t)
    u_all = b - m_t
    en_all = jnp.exp(-m_t)
    b_end = b[end:end + 1, :]
    gl = b_end - b + ig
    m_new = jnp.maximum(b_end + m_prev, jnp.max(gl, axis=0, keepdims=True))
    w_all = jnp.exp(gl - m_new)
    decay_all = jnp.exp(b_end + m_prev - m_new)
    r_t = r.T
    m_scr[...] = m_new

    ones_bf = jnp.ones((L, 128), BF16)
    mean_dv = jnp.full((M_DV, 128), 1.0 / M_DV, BF16)
    nt = (((1,), (1,)), ((), ()))
    tn = (((0,), (0,)), ((), ()))

    def stage_a(h):
        l = lane0 + h
        q = q_ref[:, h * M_DQK:(h + 1) * M_DQK]
        k = k_ref[:, h * M_DQK:(h + 1) * M_DQK]
        qk = lax.dot_general(q, k, nt, preferred_element_type=F32)
        qcn = jnp.dot(q, cn_scr[h].astype(BF16), preferred_element_type=F32)
        p = jnp.exp(jnp.where(order, u_all[:, l:l + 1] + r_t[l:l + 1, :], -jnp.inf))
        return qk, qcn, p

    def stage_b(h, qk, p):
        s = qk * p
        s_hi = s.astype(BF16)
        s_lo = (s - s_hi.astype(F32)).astype(BF16)
        vo = jnp.concatenate([v_ref[:, h * M_DV:(h + 1) * M_DV], ones_bf], axis=1)
        sv = jnp.dot(s_hi, vo, preferred_element_type=F32)
        return sv, jnp.dot(s_lo, ones_bf, preferred_element_type=F32)

    def stage_c(h, qcn, sv, rs_lo):
        l = lane0 + h
        a = _lanes(a_all[:, l:l + 1], 128)
        den = a * qcn[:, M_DV:] + (sv[:, M_DV:] + rs_lo)
        inv = 1.0 / jnp.maximum(jnp.abs(den), _lanes(en_all[:, l:l + 1], 128))
        cols = [slice(h * M_DV + c0, h * M_DV + c0 + 128) for c0 in range(0, M_DV, 128)]
        hid = [(a * qcn[:, c0:c0 + 128] + sv[:, c0:c0 + 128]) * inv for c0 in range(0, M_DV, 128)]
        if reverse:
            for sl, hv in zip(cols, hid):
                o_ref[:, sl] = hv
            return
        ys = [so_ref[:, sl].astype(F32) * (hv + hb_ref[:, sl]) for sl, hv in zip(cols, hid)]
        sq = jnp.concatenate([(y * y).astype(BF16) for y in ys], axis=1)
        ms = jnp.dot(sq, mean_dv, preferred_element_type=F32)
        scale = lax.rsqrt(ms + EPS)
        for sl, y in zip(cols, ys):
            o_ref[:, sl] = (y * scale * hn_ref[:, sl] * sz_ref[:, sl].astype(F32)).astype(BF16)

    def stage_d(h):
        l = lane0 + h
        k = k_ref[:, h * M_DQK:(h + 1) * M_DQK]
        kw = (k.astype(F32) * _lanes(w_all[:, l:l + 1], M_DQK)).astype(BF16)
        vo = jnp.concatenate([v_ref[:, h * M_DV:(h + 1) * M_DV], ones_bf], axis=1)
        upd = lax.dot_general(kw, vo, tn, preferred_element_type=F32)
        cn_scr[h] = decay_all[:, l:l + 1] * cn_scr[h] + upd

    sa = {}
    sb = {}
    for step in range(M_HEADS + 2):
        if step < M_HEADS:
            sa[step] = stage_a(step)
        if 1 <= step <= M_HEADS:
            h = step - 1
            sb[h] = stage_b(h, sa[h][0], sa[h][2])
        if step >= 2:
            h = step - 2
            stage_c(h, sa[h][1], *sb[h])
            stage_d(h)


def _mlstm_scan(qkv, gates, oz=None, hb=None, hn=None):
    reverse = hb is None
    nlc = SEQ // M_L
    ctx_blk0 = N_LAT // M_L

    def blk(b, i):
        lat = (nlc - i) if reverse else (i - 1)
        return jnp.where(i == 0, ctx_blk0 + b, b * nlc + lat)

    wide = lambda cb: pl.BlockSpec((M_L, M_V), lambda b, i: (blk(b, i), cb))
    in_specs = [
        pl.BlockSpec((M_L, M_QK), lambda b, i: (blk(b, i), 0)),
        pl.BlockSpec((M_L, M_QK), lambda b, i: (blk(b, i), 1)),
        wide(1),
        pl.BlockSpec((M_L, M_GATES), lambda b, i: (blk(b, i), 0)),
    ]
    args = [qkv, qkv, qkv, gates]
    if not reverse:
        in_specs += [wide(0), wide(0), wide(1), pl.BlockSpec((1, M_V), lambda b, i: (0, 0))]
        args += [hb, oz, oz, hn]
    return pl.pallas_call(
        functools.partial(_mlstm_scan_kernel, reverse=reverse),
        grid=(BATCH, nlc + 1),
        in_specs=in_specs,
        out_specs=wide(0),
        out_shape=jax.ShapeDtypeStruct((N_ALL, M_V), F32 if reverse else BF16),
        scratch_shapes=[
            pltpu.VMEM((M_HEADS, M_DQK, M_DV + 128), F32),
            pltpu.VMEM((1, 128), F32),
        ],
        compiler_params=_params(("parallel", "arbitrary"), 48),
        name="mlstm_scan_bwd" if reverse else "mlstm_scan_fwd",
    )(*args)


A_QKW = A_Q + A_KVW


def _attn_qk_kernel(h_ref, w_ref, gain_ref, cos_ref, sin_ref, q_ref, k_ref):
    mean_mat = jnp.full((A_HD, A_HD), 1.0 / A_HD, BF16)
    cos = cos_ref[...]
    sin = sin_ref[...]
    for c in range(A_QKW // TN):
        acc = jnp.dot(h_ref[...], w_ref[:, c * TN:(c + 1) * TN], preferred_element_type=F32)
        for hh in range(TN // A_HD):
            lo = c * TN + hh * A_HD
            a = acc[:, hh * A_HD:(hh + 1) * A_HD]
            ms = jnp.dot((a * a).astype(BF16), mean_mat, preferred_element_type=F32)
            a = a * lax.rsqrt(ms + EPS) * gain_ref[:, lo:lo + A_HD]
            a = (a * cos + pltpu.roll(a, A_HD // 2, 1) * sin).astype(BF16)
            if lo < A_Q:
                q_ref[:, lo:lo + A_HD] = a
            else:
                k_ref[:, lo - A_Q:lo - A_Q + A_HD] = a


def _attn_qk(h, w, gain, cos, sin):
    lat_tiles = N_LAT // TP
    rope_blk = lambda r: (jnp.where(r < lat_tiles, r % (SEQ // TP), SEQ // TP), 0)
    return pl.pallas_call(
        _attn_qk_kernel,
        grid=(N_ALL // TP,),
        in_specs=[
            pl.BlockSpec((TP, D), lambda r: (r, 0)),
            _resident((D, A_QKW), lambda r: (0, 0)),
            pl.BlockSpec((1, A_QKW), lambda r: (0, 0)),
            pl.BlockSpec((TP, A_HD), rope_blk),
            pl.BlockSpec((TP, A_HD), rope_blk),
        ],
        out_specs=[
            pl.BlockSpec((TP, A_Q), lambda r: (r, 0)),
            pl.BlockSpec((TP, A_KVW), lambda r: (r, 0)),
        ],
        out_shape=[
            jax.ShapeDtypeStruct((N_ALL, A_Q), BF16),
            jax.ShapeDtypeStruct((N_ALL, A_KVW), BF16),
        ],
        compiler_params=_params(("parallel",), 48),
        name="attn_qk",
    )(h, w, gain, cos, sin)


def _attn_vz_kernel(h_ref, w_ref, v_ref, zg_ref):
    v_ref[...] = jnp.dot(h_ref[...], w_ref[:, :A_KVW], preferred_element_type=F32).astype(BF16)
    for c in range(A_Q // TN):
        lo = A_KVW + c * TN
        z = jnp.dot(h_ref[...], w_ref[:, lo:lo + TN], preferred_element_type=F32)
        zg_ref[:, c * TN:(c + 1) * TN] = _silu(z).astype(BF16)


def _attn_vz(h, w):
    return pl.pallas_call(
        _attn_vz_kernel,
        grid=(N_ALL // TP,),
        in_specs=[
            pl.BlockSpec((TP, D), lambda r: (r, 0)),
            _resident((D, A_KVW + A_Q), lambda r: (0, 1)),
        ],
        out_specs=[
            pl.BlockSpec((TP, A_KVW), lambda r: (r, 0)),
            pl.BlockSpec((TP, A_Q), lambda r: (r, 0)),
        ],
        out_shape=[
            jax.ShapeDtypeStruct((N_ALL, A_KVW), BF16),
            jax.ShapeDtypeStruct((N_ALL, A_Q), BF16),
        ],
        compiler_params=_params(("parallel",), 48),
        name="attn_vz",
    )(h, w)


A_LAT_TILES = SEQ // A_TQ
A_GPS = A_KV
A_SW = A_GPS * A_GW
A_KW = A_GPS * A_HD


def _attn_kernel(q_ref, kl_ref, kc_ref, vl_ref, vc_ref, zg_ref, o_ref, vol_scr, voc_scr):
    t = pl.program_id(2)
    nt = (((1,), (1,)), ((), ()))
    heads = A_SW // A_HD
    hpg = A_GW // A_HD

    @pl.when(t == 0)
    def _():
        for g in range(A_GPS):
            vol_scr[g, :, :A_HD] = vl_ref[:, g * A_HD:(g + 1) * A_HD]
            vol_scr[g, :, A_HD:] = jnp.ones((SEQ, A_HD), BF16)
            voc_scr[g, :, :A_HD] = vc_ref[:, g * A_HD:(g + 1) * A_HD]
            voc_scr[g, :, A_HD:] = jnp.ones((CTX, A_HD), BF16)

    def run(with_latent_keys):
        def scores(hh):
            kcols = slice((hh // hpg) * A_HD, (hh // hpg + 1) * A_HD)
            q = q_ref[:, hh * A_HD:(hh + 1) * A_HD]
            s_c = lax.dot_general(q, kc_ref[:, kcols], nt, preferred_element_type=F32)
            s_l = (lax.dot_general(q, kl_ref[:, kcols], nt, preferred_element_type=F32)
                   if with_latent_keys else None)
            return s_c, s_l

        def probs(s_c, s_l):
            m = jnp.max(s_c, axis=-1, keepdims=True)
            if with_latent_keys:
                m = jnp.maximum(m, jnp.max(s_l, axis=-1, keepdims=True))
            p_c = jnp.exp(s_c - m).astype(BF16)
            p_l = jnp.exp(s_l - m).astype(BF16) if with_latent_keys else None
            return p_c, p_l

        def output(hh, p_c, p_l):
            sl = slice(hh * A_HD, (hh + 1) * A_HD)
            acc = jnp.dot(p_c, voc_scr[hh // hpg], preferred_element_type=F32)
            if with_latent_keys:
                acc = acc + jnp.dot(p_l, vol_scr[hh // hpg], preferred_element_type=F32)
            o_ref[:, sl] = (acc[:, :A_HD] / acc[:, A_HD:] * zg_ref[:, sl].astype(F32)).astype(BF16)

        s, p = {}, {}
        for step in range(heads + 2):
            if step < heads:
                s[step] = scores(step)
            if 1 <= step <= heads:
                p[step - 1] = probs(*s[step - 1])
            if step >= 2:
                output(step - 2, *p[step - 2])

    @pl.when(t < A_LAT_TILES)
    def _():
        run(True)

    @pl.when(t >= A_LAT_TILES)
    def _():
        run(False)


def _attn(q, k, v, zg):
    ctx_blk0 = N_LAT // A_TQ
    qrow = lambda b, g, t: jnp.where(t < A_LAT_TILES, b * A_LAT_TILES + t, ctx_blk0 + b)
    return pl.pallas_call(
        _attn_kernel,
        grid=(BATCH, A_KV // A_GPS, A_LAT_TILES + 1),
        in_specs=[
            pl.BlockSpec((A_TQ, A_SW), lambda b, g, t: (qrow(b, g, t), g)),
            pl.BlockSpec((SEQ, A_KW), lambda b, g, t: (b, g)),
            pl.BlockSpec((CTX, A_KW), lambda b, g, t: (N_LAT // CTX + b, g)),
            pl.BlockSpec((SEQ, A_KW), lambda b, g, t: (b, g)),
            pl.BlockSpec((CTX, A_KW), lambda b, g, t: (N_LAT // CTX + b, g)),
            pl.BlockSpec((A_TQ, A_SW), lambda b, g, t: (qrow(b, g, t), g)),
        ],
        out_specs=pl.BlockSpec((A_TQ, A_SW), lambda b, g, t: (qrow(b, g, t), g)),
        out_shape=jax.ShapeDtypeStruct((N_ALL, A_Q), BF16),
        scratch_shapes=[pltpu.VMEM((A_GPS, SEQ, 2 * A_HD), BF16), pltpu.VMEM((A_GPS, CTX, 2 * A_HD), BF16)],
        compiler_params=_params(("parallel", "parallel", "arbitrary"), 48),
        name="attn",
    )(q, k, k, v, v, zg)


def _dft_tables(n, scale):
    idx = jnp.arange(n, dtype=jnp.int32)
    ang = ((idx[:, None] * idx[None, :]) % n).astype(F32) * (2.0 * math.pi / n)
    return (jnp.cos(ang) * scale).astype(BF16), (jnp.sin(ang) * scale).astype(BF16)


def _dif_tables():
    jt = jnp.arange(F_M, dtype=jnp.int32)
    r = jnp.arange(F_R, dtype=jnp.int32)
    k = ((F_R * jt[None, :, None] + r[:, None, None]) * jt[None, None, :]) % SEQ
    ang = k.astype(F32) * (2.0 * math.pi / SEQ)
    return (jnp.cos(ang) * SEQ ** -0.5).astype(BF16), (jnp.sin(ang) * SEQ ** -0.5).astype(BF16)


def _perm_tables():
    n = jnp.arange(F_TC)
    src = F_R * (n % F_PR) + n // F_PR
    perm = (src[:, None] == n[None, :]).astype(BF16)
    return perm, perm.T


def _rope_tables():
    t = jnp.arange(SEQ)
    freqs = ROPE_THETA ** (-jnp.arange(0, A_HD // 2, 2, dtype=F32) / (A_HD // 2))
    ang = jnp.concatenate([(t // GRID_W).astype(F32)[:, None] * freqs,
                           (t % GRID_W).astype(F32)[:, None] * freqs], axis=-1)
    cos = jnp.concatenate([jnp.cos(ang), jnp.cos(ang)], axis=-1)
    sin = jnp.concatenate([-jnp.sin(ang), jnp.sin(ang)], axis=-1)
    pad = lambda a, fill: jnp.concatenate([a, jnp.full((TP, A_HD), fill, F32)], axis=0)
    return pad(cos, 1.0), pad(sin, 0.0)


def _split_heads_even_odd(w, heads):
    lead = w.shape[:-1]
    pairs = w.reshape(lead + (heads, A_HD // 2, 2))
    return jnp.swapaxes(pairs, -1, -2).reshape(lead + (heads * A_HD,))


def kernel(x, c, ctx, c_ctx, ada_w, ada_b, norm_g, fnet_w_gate, fnet_w_out, mlstm_w_in, mlstm_b_gate,
           mlstm_hn, mlstm_w_out, attn_w_in, attn_qn, attn_kn, attn_w_out, final_g):
    cc = jnp.concatenate([c, c_ctx[None, :], jnp.zeros((MOD_ROWS - BATCH - 1, D), F32)], axis=0)
    mods_all = _modvec(cc, ada_w, ada_b).reshape(DEPTH, MOD_ROWS, 1, 3 * D)

    cc_c, sc_c = _dft_tables(F_GW, F_GW ** -0.5)
    cs_chan = jnp.concatenate([cc_c, sc_c], axis=1)
    cr_lat, sr_lat = _dif_tables()
    ct_ctx, st_ctx = _dft_tables(CTX, CTX ** -0.5)
    perm, unperm = _perm_tables()
    fnet_wg_bf = fnet_w_gate.astype(BF16)
    fnet_wo_bf = fnet_w_out.astype(BF16)

    xs = None
    h = None
    for i in range(DEPTH):
        kind, j = i % 3, i // 3
        last = i == DEPTH - 1
        mods = mods_all[i]
        g = norm_g[i].reshape(1, D)
        n_rows = N_LAT if last else N_ALL
        nxt = None if last else (norm_g[i + 1].reshape(1, D), mods_all[i + 1])
        if kind == 0:
            assert i == 0 or last
            wg = (fnet_wg_bf, j)
            w_out = (fnet_wo_bf, j)
            x_lat = x.reshape(N_LAT, D) if i == 0 else xs
            p, q, hp = _fnet_chan_dif(*((x_lat, (g, mods)) if i == 0 else (h, None)), cs_chan, perm)
            m_lat = _fnet_mix_lat(hp, cr_lat, sr_lat, p, q, wg)
            if i == 0:
                x_ctx = ctx.reshape(N_CTX, D)
                a, b = _fnet_chan_ctx(x_ctx, 0, g, mods, cs_chan)
                m_ctx = _fnet_mix_ctx(x_ctx, 0, g, mods, ct_ctx, st_ctx, a, b, wg)
                xs, h = _outproj_join(m_lat, m_ctx, w_out, x_lat, x_ctx, mods, unperm, nxt)
            else:
                out = _outproj(m_lat, w_out, xs, mods, n_rows, final=(unperm, final_g.reshape(1, D)))
            continue
        if kind == 1:
            w_in = mlstm_w_in[j].astype(BF16)
            n_main = 2 * M_QK + 2 * M_V
            gate_tiles = lambda gcols: jnp.pad(
                jnp.concatenate([gcols[..., 0:8], gcols[..., 16:24]], axis=-1),
                [(0, 0)] * (gcols.ndim - 1) + [(0, 128 - 2 * M_HEADS)])
            gate_layout = lambda gcols: jnp.concatenate(
                [gate_tiles(gcols), gate_tiles(gcols[..., M_HEADS:])], axis=-1)
            w_gates = gate_layout(w_in[:, n_main:n_main + 4 * M_HEADS])
            b_gates = gate_layout(mlstm_b_gate[j][None, :])
            qkv, oz, gates = _mlstm_proj(h, w_in, w_in[:, n_main + 4 * M_HEADS:], w_gates, b_gates)
            hb = _mlstm_scan(qkv, gates)
            mbuf = _mlstm_scan(qkv, gates, oz, hb, mlstm_hn[j].reshape(1, M_V))
            w_out = (mlstm_w_out.astype(BF16), j)
        else:
            w_in = attn_w_in[j].astype(BF16)
            w_qk = _split_heads_even_odd(w_in[:, :A_QKW], A_HEADS + A_KV)
            qn = _split_heads_even_odd(attn_qn[j], 1) * (A_HD ** -0.5)
            kn = _split_heads_even_odd(attn_kn[j], 1)
            gain = jnp.concatenate([jnp.tile(qn, A_HEADS), jnp.tile(kn, A_KV)]).reshape(1, A_QKW)
            cos, sin = _rope_tables()
            q, k = _attn_qk(h, w_qk, gain, cos, sin)
            v, zg = _attn_vz(h, w_in)
            mbuf = _attn(q, k, v, zg)
            w_out = (attn_w_out.astype(BF16), j)
        assert not last
        xs, h = _outproj(mbuf, w_out, xs, mods, n_rows, nxt=nxt)

    return out.reshape(BATCH, SEQ, D)
```

```python
import functools
import math

import jax
import jax.numpy as jnp
from jax import lax
from jax.experimental import pallas as pl
from jax.experimental.pallas import tpu as pltpu

F32 = jnp.float32
BF16 = jnp.bfloat16

D = 2048
BATCH = 16
SEQ = 2048
CTX = 256
DEPTH = 4
EPS = 1e-6
N_LAT = BATCH * SEQ
N_CTX = BATCH * CTX
N_ALL = N_LAT + N_CTX
MOD_ROWS = 24
CTX_MOD_ROW = BATCH

F_GROUPS = 4
F_GW = D // F_GROUPS

M_HEADS = 8
M_DQK = 128
M_DV = 256
M_QK = M_HEADS * M_DQK
M_V = M_HEADS * M_DV
M_L = 256
M_GATES = 256
M_HALF = 2 * M_QK + M_V

A_HEADS = 16
A_KV = 4
A_HD = 128
A_Q = A_HEADS * A_HD
A_KVW = A_KV * A_HD
A_GW = A_Q // A_KV
A_TQ = 256
GRID_W = 64
ROPE_THETA = 10000.0

TM = 512
TP = 1024
TN = 512
MIB = 1024 * 1024


def _params(sem, vmem_mib):
    return pltpu.CompilerParams(dimension_semantics=sem, vmem_limit_bytes=vmem_mib * MIB)


def _resident(shape, index_map):
    return pl.BlockSpec(shape, index_map, pipeline_mode=pl.Buffered(1))


def _layer_weight(j):
    return _resident((None, D, D), lambda *ids: (j, 0, 0))


def _sigmoid(x):
    return 1.0 / (1.0 + jnp.exp(-x))


def _silu(x):
    return x * _sigmoid(x)


def _log_sigmoid(x):
    return jnp.minimum(x, 0.0) - jnp.log1p(jnp.exp(-jnp.abs(x)))


def _modnorm(x, g, scale, shift):
    ms = jnp.mean(x * x, axis=-1, keepdims=True)
    y = x * lax.rsqrt(ms + EPS) * g
    return y * (1.0 + scale) + shift


def _store_modnorm(x_ref, g_ref, sc_ref, sh_ref, h_scr):
    rows = x_ref.shape[0]
    step = min(rows, 256)
    for r0 in range(0, rows, step):
        h_scr[r0:r0 + step, :] = _modnorm(x_ref[r0:r0 + step, :], g_ref[...], sc_ref[...],
                                          sh_ref[...]).astype(BF16)


def _mod_row(r, tm):
    return jnp.where(r < N_LAT // tm, r // (SEQ // tm), CTX_MOD_ROW)


def _modvec_kernel(c_ref, w_ref, b_ref, o_ref):
    s = _silu(c_ref[...])
    o_ref[...] = jnp.dot(s, w_ref[...], preferred_element_type=F32,
                         precision=lax.Precision.HIGHEST) + b_ref[...]


def _modvec(cc, ada_w, ada_b):
    tn = 2048
    return pl.pallas_call(
        _modvec_kernel,
        grid=(DEPTH, 3 * D // tn),
        in_specs=[
            pl.BlockSpec((MOD_ROWS, D), lambda i, j: (0, 0)),
            pl.BlockSpec((None, D, tn), lambda i, j: (i, 0, j)),
            pl.BlockSpec((None, 1, tn), lambda i, j: (i, 0, j)),
        ],
        out_specs=pl.BlockSpec((None, MOD_ROWS, tn), lambda i, j: (i, 0, j)),
        out_shape=jax.ShapeDtypeStruct((DEPTH, MOD_ROWS, 3 * D), F32),
        compiler_params=_params(("parallel", "parallel"), 48),
        name="modvec",
    )(cc, ada_w, ada_b.reshape(DEPTH, 1, 3 * D))


def _fnet_chan_kernel(x_ref, g_ref, sh_ref, sc_ref, cs_ref, a_ref, b_ref):
    h = _modnorm(x_ref[...], g_ref[...], sc_ref[...], sh_ref[...]).astype(BF16)
    for grp in range(F_GROUPS):
        sl = slice(grp * F_GW, (grp + 1) * F_GW)
        p = jnp.dot(h[:, sl], cs_ref[...], preferred_element_type=F32)
        a_ref[:, sl] = p[:, :F_GW].astype(BF16)
        b_ref[:, sl] = p[:, F_GW:].astype(BF16)


def _fnet_chan_ctx(xc, row_blk0, g, mods, cs_c):
    tm = 512
    out = jax.ShapeDtypeStruct((N_CTX, D), BF16)
    mod = lambda chunk: pl.BlockSpec((None, 1, D), lambda r: (CTX_MOD_ROW, 0, chunk))
    return pl.pallas_call(
        _fnet_chan_kernel,
        grid=(N_CTX // tm,),
        in_specs=[pl.BlockSpec((tm, D), lambda r: (row_blk0 + r, 0)),
                  pl.BlockSpec((1, D), lambda r: (0, 0)), mod(0), mod(1),
                  pl.BlockSpec((F_GW, 2 * F_GW), lambda r: (0, 0))],
        out_specs=[pl.BlockSpec((tm, D), lambda r: (r, 0))] * 2,
        out_shape=[out, out],
        compiler_params=_params(("parallel",), 48),
        name="fnet_chan_ctx",
    )(xc, g, mods, mods, cs_c)


F_R = 4
F_M = SEQ // F_R
F_TC = 256
F_SL = TN // 128


def _fnet_chan_dif_kernel(x0_ref, x1_ref, x2_ref, x3_ref, *rest, prenormed):
    if prenormed:
        cs_ref, p_ref, q_ref = rest
        hs = [x[...] for x in (x0_ref, x1_ref, x2_ref, x3_ref)]
    else:
        g_ref, sh_ref, sc_ref, cs_ref, p_ref, q_ref, h_ref = rest
        hs = [_modnorm(x[...], g_ref[...], sc_ref[...], sh_ref[...]).astype(BF16)
              for x in (x0_ref, x1_ref, x2_ref, x3_ref)]
        for qi, h in enumerate(hs):
            h_ref[qi] = h
    for grp in range(F_GROUPS):
        sl = slice(grp * F_GW, (grp + 1) * F_GW)
        ab = [jnp.dot(h[:, sl], cs_ref[...], preferred_element_type=F32) for h in hs]
        a = [t[:, :F_GW] for t in ab]
        b = [t[:, F_GW:] for t in ab]
        sa02, da02, sa13, da13 = a[0] + a[2], a[0] - a[2], a[1] + a[3], a[1] - a[3]
        sb02, db02, sb13, db13 = b[0] + b[2], b[0] - b[2], b[1] + b[3], b[1] - b[3]
        re = (sa02 + sa13, da02 - db13, sa02 - sa13, da02 + db13)
        im = (sb02 + sb13, db02 + da13, sb02 - sb13, db02 - da13)
        for r in range(F_R):
            p_ref[r, :, sl] = re[r].astype(BF16)
            q_ref[r, :, sl] = im[r].astype(BF16)


def _fnet_chan_dif(xl, norm, cs_c):
    nt = F_M // F_TC
    xspec = lambda q: pl.BlockSpec((F_TC, D), lambda b, i: (b * (SEQ // F_TC) + q * nt + i, 0))
    mod = lambda chunk: pl.BlockSpec((None, 1, D), lambda b, i: (b, 0, chunk))
    out = jax.ShapeDtypeStruct((BATCH, F_R, F_M, D), BF16)
    ospec = pl.BlockSpec((None, F_R, F_TC, D), lambda b, i: (b, 0, i, 0))
    out_specs, out_shape = [ospec, ospec], [out, out]
    norm_specs, norm_args = [], []
    if norm is not None:
        norm_specs = [pl.BlockSpec((1, D), lambda b, i: (0, 0)), mod(0), mod(1)]
        norm_args = [norm[0], norm[1], norm[1]]
        out_specs.append(pl.BlockSpec((None, F_R, None, F_TC, D), lambda b, i: (b, 0, i, 0, 0)))
        out_shape.append(jax.ShapeDtypeStruct((BATCH, F_R, nt, F_TC, D), BF16))
    res = pl.pallas_call(
        functools.partial(_fnet_chan_dif_kernel, prenormed=norm is None),
        grid=(BATCH, nt),
        in_specs=[xspec(0), xspec(1), xspec(2), xspec(3)] + norm_specs + [
            pl.BlockSpec((F_GW, 2 * F_GW), lambda b, i: (0, 0))],
        out_specs=out_specs,
        out_shape=out_shape,
        compiler_params=_params(("parallel", "parallel"), 56),
        name="fnet_chan_dif",
    )(xl, xl, xl, xl, *norm_args, cs_c)
    return (res[0], res[1], xl) if norm is None else (res[0], res[1], res[2].reshape(N_LAT, D))


def _fnet_mix_kernel(x_ref, g_ref, sh_ref, sc_ref, c_ref, s_ref, p_ref, q_ref, wg_ref, o_ref, h_scr):
    _store_modnorm(x_ref, g_ref, sc_ref, sh_ref, h_scr)
    for c in range(D // TN):
        sl = slice(c * TN, (c + 1) * TN)
        y = jnp.dot(c_ref[...], p_ref[:, sl], preferred_element_type=F32)
        y = y - jnp.dot(s_ref[...], q_ref[:, sl], preferred_element_type=F32)
        gate = jnp.dot(h_scr[...], wg_ref[:, sl], preferred_element_type=F32)
        o_ref[:, sl] = (y * _silu(gate)).astype(BF16)


def _fnet_mix_lat_kernel(h_ref, c_ref, s_ref, p_ref, q_ref, wg_ref, o_ref, y_scr):
    for r in range(F_R):
        y = jnp.dot(c_ref[r], p_ref[r], preferred_element_type=F32)
        y = y - jnp.dot(s_ref[r], q_ref[r], preferred_element_type=F32)
        for c in range(F_SL):
            y_scr[c, pl.ds(r, F_M, stride=F_R), :] = y[:, c * 128:(c + 1) * 128]
    for t in range(SEQ // F_M):
        rows = slice(t * F_M, (t + 1) * F_M)
        gate = jnp.dot(h_ref[rows, :], wg_ref[...], preferred_element_type=F32)
        for c in range(F_SL):
            cols = slice(c * 128, (c + 1) * 128)
            o_ref[rows, cols] = (y_scr[c, rows, :] * _silu(gate[:, cols])).astype(BF16)


def _fnet_mix_lat(h, cr, sr, p, q, wg):
    tab = _resident((F_R, F_M, F_M), lambda b, n: (0, 0, 0))
    pq = pl.BlockSpec((None, F_R, F_M, TN), lambda b, n: (b, 0, 0, n))
    return pl.pallas_call(
        _fnet_mix_lat_kernel,
        grid=(BATCH, D // TN),
        in_specs=[pl.BlockSpec((SEQ, D), lambda b, n: (b, 0)), tab, tab, pq, pq,
                  pl.BlockSpec((None, D, TN), lambda b, n: (wg[1], 0, n))],
        out_specs=pl.BlockSpec((SEQ, TN), lambda b, n: (b, n)),
        out_shape=jax.ShapeDtypeStruct((N_LAT, D), BF16),
        scratch_shapes=[pltpu.VMEM((F_SL, SEQ, 128), F32)],
        compiler_params=_params(("parallel", "arbitrary"), 56),
        name="fnet_mix_lat",
    )(h, cr, sr, p, q, wg[0])


def _fnet_mix_ctx(xc, row_blk0, g, mods, ct, st, a, b, wg):
    mod = lambda chunk: pl.BlockSpec((None, 1, D), lambda bi: (CTX_MOD_ROW, 0, chunk))
    tab = pl.BlockSpec((CTX, CTX), lambda bi: (0, 0))
    ab = pl.BlockSpec((CTX, D), lambda bi: (bi, 0))
    return pl.pallas_call(
        _fnet_mix_kernel,
        grid=(BATCH,),
        in_specs=[pl.BlockSpec((CTX, D), lambda bi: (row_blk0 + bi, 0)),
                  pl.BlockSpec((1, D), lambda bi: (0, 0)), mod(0), mod(1),
                  tab, tab, ab, ab, _layer_weight(wg[1])],
        out_specs=pl.BlockSpec((CTX, D), lambda bi: (bi, 0)),
        out_shape=jax.ShapeDtypeStruct((N_CTX, D), BF16),
        scratch_shapes=[pltpu.VMEM((CTX, D), BF16)],
        compiler_params=_params(("parallel",), 48),
        name="fnet_mix_ctx",
    )(xc, g, mods, mods, ct, st, a, b, wg[0])


def _residual_update(m_ref, w_ref, x_ref, gate_ref, o_ref, nxt):
    ss = 0.0
    for c in range(D // TN):
        sl = slice(c * TN, (c + 1) * TN)
        y = jnp.dot(m_ref[...], w_ref[:, sl], preferred_element_type=F32)
        o = x_ref[:, sl] + gate_ref[:, sl] * y
        o_ref[:, sl] = o
        if nxt is not None:
            ss = ss + jnp.sum(o * o, axis=-1, keepdims=True)
    if nxt is not None:
        gn_ref, shn_ref, scn_ref, h_ref = nxt
        rinv = lax.rsqrt(ss * (1.0 / D) + EPS)
        for c in range(D // TN):
            sl = slice(c * TN, (c + 1) * TN)
            gs = gn_ref[:, sl] * (1.0 + scn_ref[:, sl])
            h_ref[:, sl] = (o_ref[:, sl] * rinv * gs + shn_ref[:, sl]).astype(BF16)


def _outproj_kernel(m_ref, w_ref, x_ref, gate_ref, *rest, final):
    if final:
        fg_ref, o_ref = rest
        _residual_update(m_ref, w_ref, x_ref, gate_ref, o_ref, None)
        x = o_ref[...]
        ms = jnp.mean(x * x, axis=-1, keepdims=True)
        o_ref[...] = x * lax.rsqrt(ms + EPS) * fg_ref[...]
    else:
        gn_ref, shn_ref, scn_ref, o_ref, h_ref = rest
        _residual_update(m_ref, w_ref, x_ref, gate_ref, o_ref, (gn_ref, shn_ref, scn_ref, h_ref))


def _next_norm_specs():
    return [pl.BlockSpec((1, D), lambda r: (0, 0)),
            pl.BlockSpec((None, 1, D), lambda r: (_mod_row(r, TM), 0, 0)),
            pl.BlockSpec((None, 1, D), lambda r: (_mod_row(r, TM), 0, 1))]


def _outproj(mbuf, w, xs, mods, n_rows, final=None, nxt=None):
    nr = n_rows // TM
    tile = pl.BlockSpec((TM, D), lambda r: (r, 0))
    in_specs = [
        tile,
        _layer_weight(w[1]),
        tile,
        pl.BlockSpec((None, 1, D), lambda r: (_mod_row(r, TM), 0, 2)),
    ]
    args = [mbuf, w[0], xs, mods]
    if final is not None:
        in_specs.append(pl.BlockSpec((1, D), lambda r: (0, 0)))
        args.append(final)
        return pl.pallas_call(
            functools.partial(_outproj_kernel, final=True),
            grid=(n_rows // TM,),
            in_specs=in_specs,
            out_specs=tile,
            out_shape=jax.ShapeDtypeStruct((n_rows, D), F32),
            compiler_params=_params(("parallel",), 48),
            name="outproj_final",
        )(*args)
    g_next, mods_next = nxt
    return pl.pallas_call(
        functools.partial(_outproj_kernel, final=False),
        grid=(nr,),
        in_specs=in_specs + _next_norm_specs(),
        out_specs=[tile, tile],
        out_shape=[jax.ShapeDtypeStruct((N_ALL, D), F32), jax.ShapeDtypeStruct((N_ALL, D), BF16)],
        input_output_aliases={2: 0},
        compiler_params=_params(("parallel",), 48),
        name="outproj",
    )(*args, g_next, mods_next, mods_next)


def _outproj_join_kernel(ml_ref, mc_ref, w_ref, xl_ref, xc_ref, gate_ref,
                         gn_ref, shn_ref, scn_ref, o_ref, h_ref):
    is_lat = pl.program_id(0) < N_LAT // TM
    nxt = (gn_ref, shn_ref, scn_ref, h_ref)

    @pl.when(is_lat)
    def _():
        _residual_update(ml_ref, w_ref, xl_ref, gate_ref, o_ref, nxt)

    @pl.when(jnp.logical_not(is_lat))
    def _():
        _residual_update(mc_ref, w_ref, xc_ref, gate_ref, o_ref, nxt)


def _outproj_join(m_lat, m_ctx, w, x_lat, x_ctx, mods, nxt):
    nl = N_LAT // TM
    lat = pl.BlockSpec((TM, D), lambda r: (jnp.minimum(r, nl - 1), 0))
    cxt = pl.BlockSpec((TM, D), lambda r: (jnp.maximum(r - nl, 0), 0))
    tile = pl.BlockSpec((TM, D), lambda r: (r, 0))
    g_next, mods_next = nxt
    return pl.pallas_call(
        _outproj_join_kernel,
        grid=(N_ALL // TM,),
        in_specs=[lat, cxt, _layer_weight(w[1]), lat, cxt,
                  pl.BlockSpec((None, 1, D), lambda r: (_mod_row(r, TM), 0, 2))] + _next_norm_specs(),
        out_specs=[tile, tile],
        out_shape=[jax.ShapeDtypeStruct((N_ALL, D), F32), jax.ShapeDtypeStruct((N_ALL, D), BF16)],
        compiler_params=_params(("parallel",), 48),
        name="outproj_join",
    )(m_lat, m_ctx, w[0], x_lat, x_ctx, mods, g_next, mods_next, mods_next)


def _mlstm_qkv_kernel(h_ref, w_ref, wg_ref, bg_ref, out_ref, gt_ref):
    gt_ref[...] = jnp.dot(h_ref[...], wg_ref[...], preferred_element_type=F32) + bg_ref[...]
    for c in range(M_HALF // TN):
        sl = slice(c * TN, (c + 1) * TN)
        acc = jnp.dot(h_ref[...], w_ref[:, sl], preferred_element_type=F32)
        if c < M_QK // TN:
            acc = acc * (M_DQK ** -0.5)
        out_ref[:, sl] = acc.astype(BF16)


def _mlstm_oz_kernel(h_ref, wo_ref, wz_ref, out_ref):
    for c in range(M_V // TN):
        sl = slice(c * TN, (c + 1) * TN)
        o = jnp.dot(h_ref[...], wo_ref[:, sl], preferred_element_type=F32)
        out_ref[:, sl] = _sigmoid(o).astype(BF16)
    for c in range(M_V // TN):
        sl = slice(c * TN, (c + 1) * TN)
        z = jnp.dot(h_ref[...], wz_ref[:, sl], preferred_element_type=F32)
        out_ref[:, M_V + c * TN:M_V + (c + 1) * TN] = _silu(z).astype(BF16)


def _mlstm_proj(h, w_in, w_z, wg, bg):
    tile = pl.BlockSpec((TP, D), lambda r: (r, 0))
    wide = pl.BlockSpec((TP, M_HALF), lambda r: (r, 0))
    qkv, gates = pl.pallas_call(
        _mlstm_qkv_kernel,
        grid=(N_ALL // TP,),
        in_specs=[tile, _resident((D, M_HALF), lambda r: (0, 0)),
                  pl.BlockSpec((D, M_GATES), lambda r: (0, 0)), pl.BlockSpec((1, M_GATES), lambda r: (0, 0))],
        out_specs=[wide, pl.BlockSpec((TP, M_GATES), lambda r: (r, 0))],
        out_shape=[jax.ShapeDtypeStruct((N_ALL, M_HALF), BF16), jax.ShapeDtypeStruct((N_ALL, M_GATES), F32)],
        compiler_params=_params(("parallel",), 48),
        name="mlstm_qkv",
    )(h, w_in, wg, bg)
    oz = pl.pallas_call(
        _mlstm_oz_kernel,
        grid=(N_ALL // TP,),
        in_specs=[tile, _resident((D, M_V), lambda r: (0, M_HALF // M_V)), _resident((D, M_V), lambda r: (0, 0))],
        out_specs=wide,
        out_shape=jax.ShapeDtypeStruct((N_ALL, M_HALF), BF16),
        compiler_params=_params(("parallel",), 48),
        name="mlstm_oz",
    )(h, w_in, w_z)
    return qkv, oz, gates


def _split3(x):
    hi = x.astype(BF16)
    r1 = x - hi.astype(F32)
    mid = r1.astype(BF16)
    lo = (r1 - mid.astype(F32)).astype(BF16)
    return hi, mid, lo


def _cummax_rows(x, reverse):
    rows = x.shape[0]
    row = lax.broadcasted_iota(jnp.int32, x.shape, 0)
    sh = 1
    while sh < rows:
        if reverse:
            x = jnp.where(row < rows - sh, jnp.maximum(x, pltpu.roll(x, rows - sh, 0)), x)
        else:
            x = jnp.where(row >= sh, jnp.maximum(x, pltpu.roll(x, sh, 0)), x)
        sh *= 2
    return x


def _lanes(col, width):
    return jnp.broadcast_to(col, (col.shape[0], width))


def _mlstm_scan_kernel(q_ref, k_ref, v_ref, gt_ref, *rest, reverse):
    if reverse:
        o_ref, cn_scr, m_scr = rest
    else:
        hb_ref, so_ref, sz_ref, hn_ref, o_ref, cn_scr, m_scr = rest

    @pl.when(pl.program_id(1) == 0)
    def _():
        cn_scr[...] = jnp.zeros_like(cn_scr)
        m_scr[...] = jnp.zeros_like(m_scr)

    L = M_L
    lane0 = M_HEADS if reverse else 0
    ig = gt_ref[:, :128]
    ls = _log_sigmoid(gt_ref[:, 128:])
    row = lax.broadcasted_iota(jnp.int32, (L, L), 0)
    col = lax.broadcasted_iota(jnp.int32, (L, L), 1)
    order = (row <= col) if reverse else (row >= col)
    tri = jnp.where(order, 1.0, 0.0).astype(BF16)
    hi, mid, lo = _split3(ls)
    b = (jnp.dot(tri, hi, preferred_element_type=F32)
         + jnp.dot(tri, mid, preferred_element_type=F32)
         + jnp.dot(tri, lo, preferred_element_type=F32))
    end = 0 if reverse else L - 1

    m_prev = m_scr[...]
    r = ig - b
    inter = b + m_prev
    m_t = jnp.maximum(inter, b + _cummax_rows(r, reverse))
    a_all = jnp.exp(inter - m_t)
    u_all = b - m_t
    en_all = jnp.exp(-m_t)
    b_end = b[end:end + 1, :]
    gl = b_end - b + ig
    m_new = jnp.maximum(b_end + m_prev, jnp.max(gl, axis=0, keepdims=True))
    w_all = jnp.exp(gl - m_new)
    decay_all = jnp.exp(b_end + m_prev - m_new)
    r_t = r.T
    m_scr[...] = m_new

    ones_bf = jnp.ones((L, 128), BF16)
    mean_dv = jnp.full((M_DV, 128), 1.0 / M_DV, BF16)
    nt = (((1,), (1,)), ((), ()))
    tn = (((0,), (0,)), ((), ()))

    def stage_a(h):
        l = lane0 + h
        q = q_ref[:, h * M_DQK:(h + 1) * M_DQK]
        k = k_ref[:, h * M_DQK:(h + 1) * M_DQK]
        qk = lax.dot_general(q, k, nt, preferred_element_type=F32)
        qcn = jnp.dot(q, cn_scr[h].astype(BF16), preferred_element_type=F32)
        p = jnp.exp(jnp.where(order, u_all[:, l:l + 1] + r_t[l:l + 1, :], -jnp.inf))
        return qk, qcn, p

    def stage_b(h, qk, p):
        s = qk * p
        s_hi = s.astype(BF16)
        s_lo = (s - s_hi.astype(F32)).astype(BF16)
        vo = jnp.concatenate([v_ref[:, h * M_DV:(h + 1) * M_DV], ones_bf], axis=1)
        sv = jnp.dot(s_hi, vo, preferred_element_type=F32)
        return sv, jnp.dot(s_lo, ones_bf, preferred_element_type=F32)

    def stage_c(h, qcn, sv, rs_lo):
        l = lane0 + h
        a = _lanes(a_all[:, l:l + 1], 128)
        den = a * qcn[:, M_DV:] + (sv[:, M_DV:] + rs_lo)
        inv = 1.0 / jnp.maximum(jnp.abs(den), _lanes(en_all[:, l:l + 1], 128))
        cols = [slice(h * M_DV + c0, h * M_DV + c0 + 128) for c0 in range(0, M_DV, 128)]
        hid = [(a * qcn[:, c0:c0 + 128] + sv[:, c0:c0 + 128]) * inv for c0 in range(0, M_DV, 128)]
        if reverse:
            for sl, hv in zip(cols, hid):
                o_ref[:, sl] = hv
            return
        ys = [so_ref[:, sl].astype(F32) * (hv + hb_ref[:, sl]) for sl, hv in zip(cols, hid)]
        sq = jnp.concatenate([(y * y).astype(BF16) for y in ys], axis=1)
        ms = jnp.dot(sq, mean_dv, preferred_element_type=F32)
        scale = lax.rsqrt(ms + EPS)
        for sl, y in zip(cols, ys):
            o_ref[:, sl] = (y * scale * hn_ref[:, sl] * sz_ref[:, sl].astype(F32)).astype(BF16)

    def stage_d(h):
        l = lane0 + h
        k = k_ref[:, h * M_DQK:(h + 1) * M_DQK]
        kw = (k.astype(F32) * _lanes(w_all[:, l:l + 1], M_DQK)).astype(BF16)
        vo = jnp.concatenate([v_ref[:, h * M_DV:(h + 1) * M_DV], ones_bf], axis=1)
        upd = lax.dot_general(kw, vo, tn, preferred_element_type=F32)
        cn_scr[h] = decay_all[:, l:l + 1] * cn_scr[h] + upd

    sa = {}
    sb = {}
    for step in range(M_HEADS + 2):
        if step < M_HEADS:
            sa[step] = stage_a(step)
        if 1 <= step <= M_HEADS:
            h = step - 1
            sb[h] = stage_b(h, sa[h][0], sa[h][2])
        if step >= 2:
            h = step - 2
            stage_c(h, sa[h][1], *sb[h])
            stage_d(h)


def _mlstm_scan(qkv, gates, oz=None, hb=None, hn=None):
    reverse = hb is None
    nlc = SEQ // M_L
    ctx_blk0 = N_LAT // M_L

    def blk(b, i):
        lat = (nlc - i) if reverse else (i - 1)
        return jnp.where(i == 0, ctx_blk0 + b, b * nlc + lat)

    wide = lambda cb: pl.BlockSpec((M_L, M_V), lambda b, i: (blk(b, i), cb))
    in_specs = [
        pl.BlockSpec((M_L, M_QK), lambda b, i: (blk(b, i), 0)),
        pl.BlockSpec((M_L, M_QK), lambda b, i: (blk(b, i), 1)),
        wide(1),
        pl.BlockSpec((M_L, M_GATES), lambda b, i: (blk(b, i), 0)),
    ]
    args = [qkv, qkv, qkv, gates]
    if not reverse:
        in_specs += [wide(0), wide(0), wide(1), pl.BlockSpec((1, M_V), lambda b, i: (0, 0))]
        args += [hb, oz, oz, hn]
    return pl.pallas_call(
        functools.partial(_mlstm_scan_kernel, reverse=reverse),
        grid=(BATCH, nlc + 1),
        in_specs=in_specs,
        out_specs=wide(0),
        out_shape=jax.ShapeDtypeStruct((N_ALL, M_V), F32 if reverse else BF16),
        scratch_shapes=[
            pltpu.VMEM((M_HEADS, M_DQK, M_DV + 128), F32),
            pltpu.VMEM((1, 128), F32),
        ],
        compiler_params=_params(("parallel", "arbitrary"), 48),
        name="mlstm_scan_bwd" if reverse else "mlstm_scan_fwd",
    )(*args)


A_QKW = A_Q + A_KVW


def _attn_qk_kernel(h_ref, w_ref, gain_ref, cos_ref, sin_ref, q_ref, k_ref):
    mean_mat = jnp.full((A_HD, A_HD), 1.0 / A_HD, BF16)
    cos = cos_ref[...]
    sin = sin_ref[...]
    for c in range(A_QKW // TN):
        acc = jnp.dot(h_ref[...], w_ref[:, c * TN:(c + 1) * TN], preferred_element_type=F32)
        for hh in range(TN // A_HD):
            lo = c * TN + hh * A_HD
            a = acc[:, hh * A_HD:(hh + 1) * A_HD]
            ms = jnp.dot((a * a).astype(BF16), mean_mat, preferred_element_type=F32)
            a = a * lax.rsqrt(ms + EPS) * gain_ref[:, lo:lo + A_HD]
            a = (a * cos + pltpu.roll(a, A_HD // 2, 1) * sin).astype(BF16)
            if lo < A_Q:
                q_ref[:, lo:lo + A_HD] = a
            else:
                k_ref[:, lo - A_Q:lo - A_Q + A_HD] = a


def _attn_qk(h, w, gain, cos, sin):
    lat_tiles = N_LAT // TP
    rope_blk = lambda r: (jnp.where(r < lat_tiles, r % (SEQ // TP), SEQ // TP), 0)
    return pl.pallas_call(
        _attn_qk_kernel,
        grid=(N_ALL // TP,),
        in_specs=[
            pl.BlockSpec((TP, D), lambda r: (r, 0)),
            _resident((D, A_QKW), lambda r: (0, 0)),
            pl.BlockSpec((1, A_QKW), lambda r: (0, 0)),
            pl.BlockSpec((TP, A_HD), rope_blk),
            pl.BlockSpec((TP, A_HD), rope_blk),
        ],
        out_specs=[
            pl.BlockSpec((TP, A_Q), lambda r: (r, 0)),
            pl.BlockSpec((TP, A_KVW), lambda r: (r, 0)),
        ],
        out_shape=[
            jax.ShapeDtypeStruct((N_ALL, A_Q), BF16),
            jax.ShapeDtypeStruct((N_ALL, A_KVW), BF16),
        ],
        compiler_params=_params(("parallel",), 48),
        name="attn_qk",
    )(h, w, gain, cos, sin)


def _attn_vz_kernel(h_ref, w_ref, v_ref, zg_ref):
    v_ref[...] = jnp.dot(h_ref[...], w_ref[:, :A_KVW], preferred_element_type=F32).astype(BF16)
    for c in range(A_Q // TN):
        lo = A_KVW + c * TN
        z = jnp.dot(h_ref[...], w_ref[:, lo:lo + TN], preferred_element_type=F32)
        zg_ref[:, c * TN:(c + 1) * TN] = _silu(z).astype(BF16)


def _attn_vz(h, w):
    return pl.pallas_call(
        _attn_vz_kernel,
        grid=(N_ALL // TP,),
        in_specs=[
            pl.BlockSpec((TP, D), lambda r: (r, 0)),
            _resident((D, A_KVW + A_Q), lambda r: (0, 1)),
        ],
        out_specs=[
            pl.BlockSpec((TP, A_KVW), lambda r: (r, 0)),
            pl.BlockSpec((TP, A_Q), lambda r: (r, 0)),
        ],
        out_shape=[
            jax.ShapeDtypeStruct((N_ALL, A_KVW), BF16),
            jax.ShapeDtypeStruct((N_ALL, A_Q), BF16),
        ],
        compiler_params=_params(("parallel",), 48),
        name="attn_vz",
    )(h, w)


A_LAT_TILES = SEQ // A_TQ
A_GPS = A_KV
A_SW = A_GPS * A_GW
A_KW = A_GPS * A_HD


def _attn_kernel(q_ref, kl_ref, kc_ref, vl_ref, vc_ref, zg_ref, o_ref, vol_scr, voc_scr):
    t = pl.program_id(2)
    nt = (((1,), (1,)), ((), ()))
    heads = A_SW // A_HD
    hpg = A_GW // A_HD

    @pl.when(t == 0)
    def _():
        for g in range(A_GPS):
            vol_scr[g, :, :A_HD] = vl_ref[:, g * A_HD:(g + 1) * A_HD]
            vol_scr[g, :, A_HD:] = jnp.ones((SEQ, A_HD), BF16)
            voc_scr[g, :, :A_HD] = vc_ref[:, g * A_HD:(g + 1) * A_HD]
            voc_scr[g, :, A_HD:] = jnp.ones((CTX, A_HD), BF16)

    def run(with_latent_keys):
        def scores(hh):
            kcols = slice((hh // hpg) * A_HD, (hh // hpg + 1) * A_HD)
            q = q_ref[:, hh * A_HD:(hh + 1) * A_HD]
            s_c = lax.dot_general(q, kc_ref[:, kcols], nt, preferred_element_type=F32)
            s_l = (lax.dot_general(q, kl_ref[:, kcols], nt, preferred_element_type=F32)
                   if with_latent_keys else None)
            return s_c, s_l

        def probs(s_c, s_l):
            m = jnp.max(s_c, axis=-1, keepdims=True)
            if with_latent_keys:
                m = jnp.maximum(m, jnp.max(s_l, axis=-1, keepdims=True))
            p_c = jnp.exp(s_c - m).astype(BF16)
            p_l = jnp.exp(s_l - m).astype(BF16) if with_latent_keys else None
            return p_c, p_l

        def output(hh, p_c, p_l):
            sl = slice(hh * A_HD, (hh + 1) * A_HD)
            acc = jnp.dot(p_c, voc_scr[hh // hpg], preferred_element_type=F32)
            if with_latent_keys:
                acc = acc + jnp.dot(p_l, vol_scr[hh // hpg], preferred_element_type=F32)
            o_ref[:, sl] = (acc[:, :A_HD] / acc[:, A_HD:] * zg_ref[:, sl].astype(F32)).astype(BF16)

        s, p = {}, {}
        for step in range(heads + 2):
            if step < heads:
                s[step] = scores(step)
            if 1 <= step <= heads:
                p[step - 1] = probs(*s[step - 1])
            if step >= 2:
                output(step - 2, *p[step - 2])

    @pl.when(t < A_LAT_TILES)
    def _():
        run(True)

    @pl.when(t >= A_LAT_TILES)
    def _():
        run(False)


def _attn(q, k, v, zg):
    ctx_blk0 = N_LAT // A_TQ
    qrow = lambda b, g, t: jnp.where(t < A_LAT_TILES, b * A_LAT_TILES + t, ctx_blk0 + b)
    return pl.pallas_call(
        _attn_kernel,
        grid=(BATCH, A_KV // A_GPS, A_LAT_TILES + 1),
        in_specs=[
            pl.BlockSpec((A_TQ, A_SW), lambda b, g, t: (qrow(b, g, t), g)),
            pl.BlockSpec((SEQ, A_KW), lambda b, g, t: (b, g)),
            pl.BlockSpec((CTX, A_KW), lambda b, g, t: (N_LAT // CTX + b, g)),
            pl.BlockSpec((SEQ, A_KW), lambda b, g, t: (b, g)),
            pl.BlockSpec((CTX, A_KW), lambda b, g, t: (N_LAT // CTX + b, g)),
            pl.BlockSpec((A_TQ, A_SW), lambda b, g, t: (qrow(b, g, t), g)),
        ],
        out_specs=pl.BlockSpec((A_TQ, A_SW), lambda b, g, t: (qrow(b, g, t), g)),
        out_shape=jax.ShapeDtypeStruct((N_ALL, A_Q), BF16),
        scratch_shapes=[pltpu.VMEM((A_GPS, SEQ, 2 * A_HD), BF16), pltpu.VMEM((A_GPS, CTX, 2 * A_HD), BF16)],
        compiler_params=_params(("parallel", "parallel", "arbitrary"), 48),
        name="attn",
    )(q, k, k, v, v, zg)


def _dft_tables(n, scale):
    idx = jnp.arange(n, dtype=jnp.int32)
    ang = ((idx[:, None] * idx[None, :]) % n).astype(F32) * (2.0 * math.pi / n)
    return (jnp.cos(ang) * scale).astype(BF16), (jnp.sin(ang) * scale).astype(BF16)


def _dif_tables():
    jt = jnp.arange(F_M, dtype=jnp.int32)
    r = jnp.arange(F_R, dtype=jnp.int32)
    k = ((F_R * jt[None, :, None] + r[:, None, None]) * jt[None, None, :]) % SEQ
    ang = k.astype(F32) * (2.0 * math.pi / SEQ)
    return (jnp.cos(ang) * SEQ ** -0.5).astype(BF16), (jnp.sin(ang) * SEQ ** -0.5).astype(BF16)


def _rope_tables():
    t = jnp.arange(SEQ)
    freqs = ROPE_THETA ** (-jnp.arange(0, A_HD // 2, 2, dtype=F32) / (A_HD // 2))
    ang = jnp.concatenate([(t // GRID_W).astype(F32)[:, None] * freqs,
                           (t % GRID_W).astype(F32)[:, None] * freqs], axis=-1)
    cos = jnp.concatenate([jnp.cos(ang), jnp.cos(ang)], axis=-1)
    sin = jnp.concatenate([-jnp.sin(ang), jnp.sin(ang)], axis=-1)
    pad = lambda a, fill: jnp.concatenate([a, jnp.full((TP, A_HD), fill, F32)], axis=0)
    return pad(cos, 1.0), pad(sin, 0.0)


def _split_heads_even_odd(w, heads):
    lead = w.shape[:-1]
    pairs = w.reshape(lead + (heads, A_HD // 2, 2))
    return jnp.swapaxes(pairs, -1, -2).reshape(lead + (heads * A_HD,))


def kernel(x, c, ctx, c_ctx, ada_w, ada_b, norm_g, fnet_w_gate, fnet_w_out, mlstm_w_in, mlstm_b_gate,
           mlstm_hn, mlstm_w_out, attn_w_in, attn_qn, attn_kn, attn_w_out, final_g):
    cc = jnp.concatenate([c, c_ctx[None, :], jnp.zeros((MOD_ROWS - BATCH - 1, D), F32)], axis=0)
    mods_all = _modvec(cc, ada_w, ada_b).reshape(DEPTH, MOD_ROWS, 1, 3 * D)

    cc_c, sc_c = _dft_tables(F_GW, F_GW ** -0.5)
    cs_chan = jnp.concatenate([cc_c, sc_c], axis=1)
    cr_lat, sr_lat = _dif_tables()
    ct_ctx, st_ctx = _dft_tables(CTX, CTX ** -0.5)
    fnet_wg_bf = fnet_w_gate.astype(BF16)
    fnet_wo_bf = fnet_w_out.astype(BF16)

    xs = None
    h = None
    for i in range(DEPTH):
        kind, j = i % 3, i // 3
        last = i == DEPTH - 1
        mods = mods_all[i]
        g = norm_g[i].reshape(1, D)
        n_rows = N_LAT if last else N_ALL
        nxt = None if last else (norm_g[i + 1].reshape(1, D), mods_all[i + 1])
        if kind == 0:
            assert i == 0 or last
            wg = (fnet_wg_bf, j)
            w_out = (fnet_wo_bf, j)
            x_lat = x.reshape(N_LAT, D) if i == 0 else xs
            p, q, h_lat = _fnet_chan_dif(*((x_lat, (g, mods)) if i == 0 else (h, None)), cs_chan)
            m_lat = _fnet_mix_lat(h_lat, cr_lat, sr_lat, p, q, wg)
            if i == 0:
                x_ctx = ctx.reshape(N_CTX, D)
                a, b = _fnet_chan_ctx(x_ctx, 0, g, mods, cs_chan)
                m_ctx = _fnet_mix_ctx(x_ctx, 0, g, mods, ct_ctx, st_ctx, a, b, wg)
                xs, h = _outproj_join(m_lat, m_ctx, w_out, x_lat, x_ctx, mods, nxt)
            else:
                out = _outproj(m_lat, w_out, xs, mods, n_rows, final=final_g.reshape(1, D))
            continue
        if kind == 1:
            w_in = mlstm_w_in[j].astype(BF16)
            n_main = 2 * M_QK + 2 * M_V
            gate_tiles = lambda gcols: jnp.pad(
                jnp.concatenate([gcols[..., 0:8], gcols[..., 16:24]], axis=-1),
                [(0, 0)] * (gcols.ndim - 1) + [(0, 128 - 2 * M_HEADS)])
            gate_layout = lambda gcols: jnp.concatenate(
                [gate_tiles(gcols), gate_tiles(gcols[..., M_HEADS:])], axis=-1)
            w_gates = gate_layout(w_in[:, n_main:n_main + 4 * M_HEADS])
            b_gates = gate_layout(mlstm_b_gate[j][None, :])
            qkv, oz, gates = _mlstm_proj(h, w_in, w_in[:, n_main + 4 * M_HEADS:], w_gates, b_gates)
            hb = _mlstm_scan(qkv, gates)
            mbuf = _mlstm_scan(qkv, gates, oz, hb, mlstm_hn[j].reshape(1, M_V))
            w_out = (mlstm_w_out.astype(BF16), j)
        else:
            w_in = attn_w_in[j].astype(BF16)
            w_qk = _split_heads_even_odd(w_in[:, :A_QKW], A_HEADS + A_KV)
            qn = _split_heads_even_odd(attn_qn[j], 1) * (A_HD ** -0.5)
            kn = _split_heads_even_odd(attn_kn[j], 1)
            gain = jnp.concatenate([jnp.tile(qn, A_HEADS), jnp.tile(kn, A_KV)]).reshape(1, A_QKW)
            cos, sin = _rope_tables()
            q, k = _attn_qk(h, w_qk, gain, cos, sin)
            v, zg = _attn_vz(h, w_in)
            mbuf = _attn(q, k, v, zg)
            w_out = (attn_w_out.astype(BF16), j)
        assert not last
        xs, h = _outproj(mbuf, w_out, xs, mods, n_rows, nxt=nxt)

    return out.reshape(BATCH, SEQ, D)
```

```python
import functools
import math

import jax
import jax.numpy as jnp
from jax import lax
from jax.experimental import pallas as pl
from jax.experimental.pallas import tpu as pltpu

F32 = jnp.float32
BF16 = jnp.bfloat16

D = 2048
BATCH = 16
SEQ = 2048
CTX = 256
DEPTH = 4
EPS = 1e-6
N_LAT = BATCH * SEQ
N_CTX = BATCH * CTX
N_ALL = N_LAT + N_CTX
MOD_ROWS = 24
CTX_MOD_ROW = BATCH

F_GROUPS = 4
F_GW = D // F_GROUPS

M_HEADS = 8
M_DQK = 128
M_DV = 256
M_QK = M_HEADS * M_DQK
M_V = M_HEADS * M_DV
M_L = 256
M_GATES = 256
M_HALF = 2 * M_QK + M_V

A_HEADS = 16
A_KV = 4
A_HD = 128
A_Q = A_HEADS * A_HD
A_KVW = A_KV * A_HD
A_GW = A_Q // A_KV
A_TQ = 256
GRID_W = 64
ROPE_THETA = 10000.0

TM = 512
TP = 1024
TN = 512
MIB = 1024 * 1024


def _params(sem, vmem_mib):
    return pltpu.CompilerParams(dimension_semantics=sem, vmem_limit_bytes=vmem_mib * MIB)


def _resident(shape, index_map):
    return pl.BlockSpec(shape, index_map, pipeline_mode=pl.Buffered(1))


def _layer_weight(j):
    return _resident((None, D, D), lambda *ids: (j, 0, 0))


def _sigmoid(x):
    return 1.0 / (1.0 + jnp.exp(-x))


def _silu(x):
    return x * _sigmoid(x)


def _log_sigmoid(x):
    return jnp.minimum(x, 0.0) - jnp.log1p(jnp.exp(-jnp.abs(x)))


def _modnorm(x, g, scale, shift):
    ms = jnp.mean(x * x, axis=-1, keepdims=True)
    y = x * lax.rsqrt(ms + EPS) * g
    return y * (1.0 + scale) + shift


def _store_modnorm(x_ref, g_ref, sc_ref, sh_ref, h_scr):
    rows = x_ref.shape[0]
    step = min(rows, 256)
    for r0 in range(0, rows, step):
        h_scr[r0:r0 + step, :] = _modnorm(x_ref[r0:r0 + step, :], g_ref[...], sc_ref[...],
                                          sh_ref[...]).astype(BF16)


def _mod_row(r, tm):
    return jnp.where(r < N_LAT // tm, r // (SEQ // tm), CTX_MOD_ROW)


def _modvec_kernel(c_ref, w_ref, b_ref, o_ref):
    s = _silu(c_ref[...])
    w = w_ref[...]
    s_hi = s.astype(BF16)
    s_lo = (s - s_hi.astype(F32)).astype(BF16)
    w_hi = w.astype(BF16)
    w_lo = (w - w_hi.astype(F32)).astype(BF16)
    acc = jnp.dot(s_hi, w_hi, preferred_element_type=F32)
    acc = acc + jnp.dot(s_lo, w_hi, preferred_element_type=F32)
    acc = acc + jnp.dot(s_hi, w_lo, preferred_element_type=F32)
    o_ref[...] = acc + b_ref[...]


def _modvec(cc, ada_w, ada_b):
    tn = 2048
    return pl.pallas_call(
        _modvec_kernel,
        grid=(DEPTH, 3 * D // tn),
        in_specs=[
            pl.BlockSpec((MOD_ROWS, D), lambda i, j: (0, 0)),
            pl.BlockSpec((None, D, tn), lambda i, j: (i, 0, j)),
            pl.BlockSpec((None, 1, tn), lambda i, j: (i, 0, j)),
        ],
        out_specs=pl.BlockSpec((None, MOD_ROWS, tn), lambda i, j: (i, 0, j)),
        out_shape=jax.ShapeDtypeStruct((DEPTH, MOD_ROWS, 3 * D), F32),
        compiler_params=_params(("parallel", "parallel"), 48),
        name="modvec",
    )(cc, ada_w, ada_b.reshape(DEPTH, 1, 3 * D))


def _fnet_chan_kernel(x_ref, g_ref, sh_ref, sc_ref, cs_ref, a_ref, b_ref):
    h = _modnorm(x_ref[...], g_ref[...], sc_ref[...], sh_ref[...]).astype(BF16)
    for grp in range(F_GROUPS):
        sl = slice(grp * F_GW, (grp + 1) * F_GW)
        p = jnp.dot(h[:, sl], cs_ref[...], preferred_element_type=F32)
        a_ref[:, sl] = p[:, :F_GW].astype(BF16)
        b_ref[:, sl] = p[:, F_GW:].astype(BF16)


def _fnet_chan_ctx(xc, row_blk0, g, mods, cs_c):
    tm = 512
    out = jax.ShapeDtypeStruct((N_CTX, D), BF16)
    mod = lambda chunk: pl.BlockSpec((None, 1, D), lambda r: (CTX_MOD_ROW, 0, chunk))
    return pl.pallas_call(
        _fnet_chan_kernel,
        grid=(N_CTX // tm,),
        in_specs=[pl.BlockSpec((tm, D), lambda r: (row_blk0 + r, 0)),
                  pl.BlockSpec((1, D), lambda r: (0, 0)), mod(0), mod(1),
                  pl.BlockSpec((F_GW, 2 * F_GW), lambda r: (0, 0))],
        out_specs=[pl.BlockSpec((tm, D), lambda r: (r, 0))] * 2,
        out_shape=[out, out],
        compiler_params=_params(("parallel",), 48),
        name="fnet_chan_ctx",
    )(xc, g, mods, mods, cs_c)


F_R = 4
F_M = SEQ // F_R
F_TC = 256
F_SL = TN // 128


def _fnet_chan_dif_kernel(x0_ref, x1_ref, x2_ref, x3_ref, *rest, prenormed):
    if prenormed:
        cs_ref, p_ref, q_ref = rest
        hs = [x[...] for x in (x0_ref, x1_ref, x2_ref, x3_ref)]
    else:
        g_ref, sh_ref, sc_ref, cs_ref, p_ref, q_ref, h_ref = rest
        hs = [_modnorm(x[...], g_ref[...], sc_ref[...], sh_ref[...]).astype(BF16)
              for x in (x0_ref, x1_ref, x2_ref, x3_ref)]
        for qi, h in enumerate(hs):
            h_ref[qi] = h
    for grp in range(F_GROUPS):
        sl = slice(grp * F_GW, (grp + 1) * F_GW)
        ab = [jnp.dot(h[:, sl], cs_ref[...], preferred_element_type=F32) for h in hs]
        a = [t[:, :F_GW] for t in ab]
        b = [t[:, F_GW:] for t in ab]
        sa02, da02, sa13, da13 = a[0] + a[2], a[0] - a[2], a[1] + a[3], a[1] - a[3]
        sb02, db02, sb13, db13 = b[0] + b[2], b[0] - b[2], b[1] + b[3], b[1] - b[3]
        re = (sa02 + sa13, da02 - db13, sa02 - sa13, da02 + db13)
        im = (sb02 + sb13, db02 + da13, sb02 - sb13, db02 - da13)
        for r in range(F_R):
            p_ref[r, :, sl] = re[r].astype(BF16)
            q_ref[r, :, sl] = im[r].astype(BF16)


def _fnet_chan_dif(xl, norm, cs_c):
    nt = F_M // F_TC
    xspec = lambda q: pl.BlockSpec((F_TC, D), lambda b, i: (b * (SEQ // F_TC) + q * nt + i, 0))
    mod = lambda chunk: pl.BlockSpec((None, 1, D), lambda b, i: (b, 0, chunk))
    out = jax.ShapeDtypeStruct((BATCH, F_R, F_M, D), BF16)
    ospec = pl.BlockSpec((None, F_R, F_TC, D), lambda b, i: (b, 0, i, 0))
    out_specs, out_shape = [ospec, ospec], [out, out]
    norm_specs, norm_args = [], []
    if norm is not None:
        norm_specs = [pl.BlockSpec((1, D), lambda b, i: (0, 0)), mod(0), mod(1)]
        norm_args = [norm[0], norm[1], norm[1]]
        out_specs.append(pl.BlockSpec((None, F_R, None, F_TC, D), lambda b, i: (b, 0, i, 0, 0)))
        out_shape.append(jax.ShapeDtypeStruct((BATCH, F_R, nt, F_TC, D), BF16))
    res = pl.pallas_call(
        functools.partial(_fnet_chan_dif_kernel, prenormed=norm is None),
        grid=(BATCH, nt),
        in_specs=[xspec(0), xspec(1), xspec(2), xspec(3)] + norm_specs + [
            pl.BlockSpec((F_GW, 2 * F_GW), lambda b, i: (0, 0))],
        out_specs=out_specs,
        out_shape=out_shape,
        compiler_params=_params(("parallel", "parallel"), 56),
        name="fnet_chan_dif",
    )(xl, xl, xl, xl, *norm_args, cs_c)
    return (res[0], res[1], xl) if norm is None else (res[0], res[1], res[2].reshape(N_LAT, D))


def _fnet_mix_kernel(x_ref, g_ref, sh_ref, sc_ref, c_ref, s_ref, p_ref, q_ref, wg_ref, o_ref, h_scr):
    _store_modnorm(x_ref, g_ref, sc_ref, sh_ref, h_scr)
    for c in range(D // TN):
        sl = slice(c * TN, (c + 1) * TN)
        y = jnp.dot(c_ref[...], p_ref[:, sl], preferred_element_type=F32)
        y = y - jnp.dot(s_ref[...], q_ref[:, sl], preferred_element_type=F32)
        gate = jnp.dot(h_scr[...], wg_ref[:, sl], preferred_element_type=F32)
        o_ref[:, sl] = (y * _silu(gate)).astype(BF16)


def _fnet_mix_lat_kernel(h_ref, c_ref, s_ref, p_ref, q_ref, wg_ref, o_ref, y_scr):
    for r in range(F_R):
        y = jnp.dot(c_ref[r], p_ref[r], preferred_element_type=F32)
        y = y - jnp.dot(s_ref[r], q_ref[r], preferred_element_type=F32)
        for c in range(F_SL):
            y_scr[c, pl.ds(r, F_M, stride=F_R), :] = y[:, c * 128:(c + 1) * 128]
    for t in range(SEQ // F_M):
        rows = slice(t * F_M, (t + 1) * F_M)
        gate = jnp.dot(h_ref[rows, :], wg_ref[...], preferred_element_type=F32)
        for c in range(F_SL):
            cols = slice(c * 128, (c + 1) * 128)
            o_ref[rows, cols] = (y_scr[c, rows, :] * _silu(gate[:, cols])).astype(BF16)


def _fnet_mix_lat(h, cr, sr, p, q, wg):
    tab = _resident((F_R, F_M, F_M), lambda b, n: (0, 0, 0))
    pq = pl.BlockSpec((None, F_R, F_M, TN), lambda b, n: (b, 0, 0, n))
    return pl.pallas_call(
        _fnet_mix_lat_kernel,
        grid=(BATCH, D // TN),
        in_specs=[pl.BlockSpec((SEQ, D), lambda b, n: (b, 0)), tab, tab, pq, pq,
                  pl.BlockSpec((None, D, TN), lambda b, n: (wg[1], 0, n))],
        out_specs=pl.BlockSpec((SEQ, TN), lambda b, n: (b, n)),
        out_shape=jax.ShapeDtypeStruct((N_LAT, D), BF16),
        scratch_shapes=[pltpu.VMEM((F_SL, SEQ, 128), F32)],
        compiler_params=_params(("parallel", "arbitrary"), 56),
        name="fnet_mix_lat",
    )(h, cr, sr, p, q, wg[0])


def _fnet_mix_ctx(xc, row_blk0, g, mods, ct, st, a, b, wg):
    mod = lambda chunk: pl.BlockSpec((None, 1, D), lambda bi: (CTX_MOD_ROW, 0, chunk))
    tab = pl.BlockSpec((CTX, CTX), lambda bi: (0, 0))
    ab = pl.BlockSpec((CTX, D), lambda bi: (bi, 0))
    return pl.pallas_call(
        _fnet_mix_kernel,
        grid=(BATCH,),
        in_specs=[pl.BlockSpec((CTX, D), lambda bi: (row_blk0 + bi, 0)),
                  pl.BlockSpec((1, D), lambda bi: (0, 0)), mod(0), mod(1),
                  tab, tab, ab, ab, _layer_weight(wg[1])],
        out_specs=pl.BlockSpec((CTX, D), lambda bi: (bi, 0)),
        out_shape=jax.ShapeDtypeStruct((N_CTX, D), BF16),
        scratch_shapes=[pltpu.VMEM((CTX, D), BF16)],
        compiler_params=_params(("parallel",), 48),
        name="fnet_mix_ctx",
    )(xc, g, mods, mods, ct, st, a, b, wg[0])


def _residual_update(m_ref, w_ref, x_ref, gate_ref, o_ref, nxt):
    ss = 0.0
    for c in range(D // TN):
        sl = slice(c * TN, (c + 1) * TN)
        y = jnp.dot(m_ref[...], w_ref[:, sl], preferred_element_type=F32)
        o = x_ref[:, sl] + gate_ref[:, sl] * y
        o_ref[:, sl] = o
        if nxt is not None:
            ss = ss + jnp.sum(o * o, axis=-1, keepdims=True)
    if nxt is not None:
        gn_ref, shn_ref, scn_ref, h_ref = nxt
        rinv = lax.rsqrt(ss * (1.0 / D) + EPS)
        for c in range(D // TN):
            sl = slice(c * TN, (c + 1) * TN)
            gs = gn_ref[:, sl] * (1.0 + scn_ref[:, sl])
            h_ref[:, sl] = (o_ref[:, sl] * rinv * gs + shn_ref[:, sl]).astype(BF16)


def _outproj_kernel(m_ref, w_ref, x_ref, gate_ref, *rest, final):
    if final:
        fg_ref, o_ref = rest
        _residual_update(m_ref, w_ref, x_ref, gate_ref, o_ref, None)
        x = o_ref[...]
        ms = jnp.mean(x * x, axis=-1, keepdims=True)
        o_ref[...] = x * lax.rsqrt(ms + EPS) * fg_ref[...]
    else:
        gn_ref, shn_ref, scn_ref, o_ref, h_ref = rest
        _residual_update(m_ref, w_ref, x_ref, gate_ref, o_ref, (gn_ref, shn_ref, scn_ref, h_ref))


def _next_norm_specs():
    return [pl.BlockSpec((1, D), lambda r: (0, 0)),
            pl.BlockSpec((None, 1, D), lambda r: (_mod_row(r, TM), 0, 0)),
            pl.BlockSpec((None, 1, D), lambda r: (_mod_row(r, TM), 0, 1))]


def _outproj(mbuf, w, xs, mods, n_rows, final=None, nxt=None):
    nr = n_rows // TM
    tile = pl.BlockSpec((TM, D), lambda r: (r, 0))
    in_specs = [
        tile,
        _layer_weight(w[1]),
        tile,
        pl.BlockSpec((None, 1, D), lambda r: (_mod_row(r, TM), 0, 2)),
    ]
    args = [mbuf, w[0], xs, mods]
    if final is not None:
        in_specs.append(pl.BlockSpec((1, D), lambda r: (0, 0)))
        args.append(final)
        return pl.pallas_call(
            functools.partial(_outproj_kernel, final=True),
            grid=(n_rows // TM,),
            in_specs=in_specs,
            out_specs=tile,
            out_shape=jax.ShapeDtypeStruct((n_rows, D), F32),
            compiler_params=_params(("parallel",), 48),
            name="outproj_final",
        )(*args)
    g_next, mods_next = nxt
    return pl.pallas_call(
        functools.partial(_outproj_kernel, final=False),
        grid=(nr,),
        in_specs=in_specs + _next_norm_specs(),
        out_specs=[tile, tile],
        out_shape=[jax.ShapeDtypeStruct((N_ALL, D), F32), jax.ShapeDtypeStruct((N_ALL, D), BF16)],
        input_output_aliases={2: 0},
        compiler_params=_params(("parallel",), 48),
        name="outproj",
    )(*args, g_next, mods_next, mods_next)


def _outproj_join_kernel(ml_ref, mc_ref, w_ref, xl_ref, xc_ref, gate_ref,
                         gn_ref, shn_ref, scn_ref, o_ref, h_ref):
    is_lat = pl.program_id(0) < N_LAT // TM
    nxt = (gn_ref, shn_ref, scn_ref, h_ref)

    @pl.when(is_lat)
    def _():
        _residual_update(ml_ref, w_ref, xl_ref, gate_ref, o_ref, nxt)

    @pl.when(jnp.logical_not(is_lat))
    def _():
        _residual_update(mc_ref, w_ref, xc_ref, gate_ref, o_ref, nxt)


def _outproj_join(m_lat, m_ctx, w, x_lat, x_ctx, mods, nxt):
    nl = N_LAT // TM
    lat = pl.BlockSpec((TM, D), lambda r: (jnp.minimum(r, nl - 1), 0))
    cxt = pl.BlockSpec((TM, D), lambda r: (jnp.maximum(r - nl, 0), 0))
    tile = pl.BlockSpec((TM, D), lambda r: (r, 0))
    g_next, mods_next = nxt
    return pl.pallas_call(
        _outproj_join_kernel,
        grid=(N_ALL // TM,),
        in_specs=[lat, cxt, _layer_weight(w[1]), lat, cxt,
                  pl.BlockSpec((None, 1, D), lambda r: (_mod_row(r, TM), 0, 2))] + _next_norm_specs(),
        out_specs=[tile, tile],
        out_shape=[jax.ShapeDtypeStruct((N_ALL, D), F32), jax.ShapeDtypeStruct((N_ALL, D), BF16)],
        compiler_params=_params(("parallel",), 48),
        name="outproj_join",
    )(m_lat, m_ctx, w[0], x_lat, x_ctx, mods, g_next, mods_next, mods_next)


def _mlstm_qkv_kernel(h_ref, w_ref, wg_ref, bg_ref, out_ref, gt_ref):
    gt_ref[...] = jnp.dot(h_ref[...], wg_ref[...], preferred_element_type=F32) + bg_ref[...]
    for c in range(M_HALF // TN):
        sl = slice(c * TN, (c + 1) * TN)
        acc = jnp.dot(h_ref[...], w_ref[:, sl], preferred_element_type=F32)
        if c < M_QK // TN:
            acc = acc * (M_DQK ** -0.5)
        out_ref[:, sl] = acc.astype(BF16)


def _mlstm_oz_kernel(h_ref, wo_ref, wz_ref, out_ref):
    for c in range(M_V // TN):
        sl = slice(c * TN, (c + 1) * TN)
        o = jnp.dot(h_ref[...], wo_ref[:, sl], preferred_element_type=F32)
        out_ref[:, sl] = _sigmoid(o).astype(BF16)
    for c in range(M_V // TN):
        sl = slice(c * TN, (c + 1) * TN)
        z = jnp.dot(h_ref[...], wz_ref[:, sl], preferred_element_type=F32)
        out_ref[:, M_V + c * TN:M_V + (c + 1) * TN] = _silu(z).astype(BF16)


def _mlstm_proj(h, w_in, w_z, wg, bg):
    tile = pl.BlockSpec((TP, D), lambda r: (r, 0))
    wide = pl.BlockSpec((TP, M_HALF), lambda r: (r, 0))
    qkv, gates = pl.pallas_call(
        _mlstm_qkv_kernel,
        grid=(N_ALL // TP,),
        in_specs=[tile, _resident((D, M_HALF), lambda r: (0, 0)),
                  pl.BlockSpec((D, M_GATES), lambda r: (0, 0)), pl.BlockSpec((1, M_GATES), lambda r: (0, 0))],
        out_specs=[wide, pl.BlockSpec((TP, M_GATES), lambda r: (r, 0))],
        out_shape=[jax.ShapeDtypeStruct((N_ALL, M_HALF), BF16), jax.ShapeDtypeStruct((N_ALL, M_GATES), F32)],
        compiler_params=_params(("parallel",), 48),
        name="mlstm_qkv",
    )(h, w_in, wg, bg)
    oz = pl.pallas_call(
        _mlstm_oz_kernel,
        grid=(N_ALL // TP,),
        in_specs=[tile, _resident((D, M_V), lambda r: (0, M_HALF // M_V)), _resident((D, M_V), lambda r: (0, 0))],
        out_specs=wide,
        out_shape=jax.ShapeDtypeStruct((N_ALL, M_HALF), BF16),
        compiler_params=_params(("parallel",), 48),
        name="mlstm_oz",
    )(h, w_in, w_z)
    return qkv, oz, gates


def _split3(x):
    hi = x.astype(BF16)
    r1 = x - hi.astype(F32)
    mid = r1.astype(BF16)
    lo = (r1 - mid.astype(F32)).astype(BF16)
    return hi, mid, lo


def _cummax_rows(x, reverse):
    rows = x.shape[0]
    row = lax.broadcasted_iota(jnp.int32, x.shape, 0)
    sh = 1
    while sh < rows:
        if reverse:
            x = jnp.where(row < rows - sh, jnp.maximum(x, pltpu.roll(x, rows - sh, 0)), x)
        else:
            x = jnp.where(row >= sh, jnp.maximum(x, pltpu.roll(x, sh, 0)), x)
        sh *= 2
    return x


def _lanes(col, width):
    return jnp.broadcast_to(col, (col.shape[0], width))


def _mlstm_scan_kernel(q_ref, k_ref, v_ref, gt_ref, *rest, reverse):
    if reverse:
        o_ref, cn_scr, m_scr = rest
    else:
        hb_ref, so_ref, sz_ref, hn_ref, o_ref, cn_scr, m_scr = rest

    @pl.when(pl.program_id(1) == 0)
    def _():
        cn_scr[...] = jnp.zeros_like(cn_scr)
        m_scr[...] = jnp.zeros_like(m_scr)

    L = M_L
    lane0 = M_HEADS if reverse else 0
    ig = gt_ref[:, :128]
    ls = _log_sigmoid(gt_ref[:, 128:])
    row = lax.broadcasted_iota(jnp.int32, (L, L), 0)
    col = lax.broadcasted_iota(jnp.int32, (L, L), 1)
    order = (row <= col) if reverse else (row >= col)
    tri = jnp.where(order, 1.0, 0.0).astype(BF16)
    hi, mid, lo = _split3(ls)
    b = (jnp.dot(tri, hi, preferred_element_type=F32)
         + jnp.dot(tri, mid, preferred_element_type=F32)
         + jnp.dot(tri, lo, preferred_element_type=F32))
    end = 0 if reverse else L - 1

    m_prev = m_scr[...]
    r = ig - b
    inter = b + m_prev
    m_t = jnp.maximum(inter, b + _cummax_rows(r, reverse))
    a_all = jnp.exp(inter - m_t)
    u_all = b - m_t
    en_all = jnp.exp(-m_t)
    b_end = b[end:end + 1, :]
    gl = b_end - b + ig
    m_new = jnp.maximum(b_end + m_prev, jnp.max(gl, axis=0, keepdims=True))
    w_all = jnp.exp(gl - m_new)
    decay_all = jnp.exp(b_end + m_prev - m_new)
    r_t = r.T
    m_scr[...] = m_new

    ones_bf = jnp.ones((L, 128), BF16)
    mean_dv = jnp.full((M_DV, 128), 1.0 / M_DV, BF16)
    nt = (((1,), (1,)), ((), ()))
    tn = (((0,), (0,)), ((), ()))

    def stage_a(h):
        l = lane0 + h
        q = q_ref[:, h * M_DQK:(h + 1) * M_DQK]
        k = k_ref[:, h * M_DQK:(h + 1) * M_DQK]
        qk = lax.dot_general(q, k, nt, preferred_element_type=F32)
        qcn = jnp.dot(q, cn_scr[h].astype(BF16), preferred_element_type=F32)
        p = jnp.exp(jnp.where(order, u_all[:, l:l + 1] + r_t[l:l + 1, :], -jnp.inf))
        return qk, qcn, p

    def stage_b(h, qk, p):
        s = qk * p
        s_hi = s.astype(BF16)
        s_lo = (s - s_hi.astype(F32)).astype(BF16)
        vo = jnp.concatenate([v_ref[:, h * M_DV:(h + 1) * M_DV], ones_bf], axis=1)
        sv = jnp.dot(s_hi, vo, preferred_element_type=F32)
        return sv, jnp.dot(s_lo, ones_bf, preferred_element_type=F32)

    def stage_c(h, qcn, sv, rs_lo):
        l = lane0 + h
        a = _lanes(a_all[:, l:l + 1], 128)
        den = a * qcn[:, M_DV:] + (sv[:, M_DV:] + rs_lo)
        inv = 1.0 / jnp.maximum(jnp.abs(den), _lanes(en_all[:, l:l + 1], 128))
        cols = [slice(h * M_DV + c0, h * M_DV + c0 + 128) for c0 in range(0, M_DV, 128)]
        hid = [(a * qcn[:, c0:c0 + 128] + sv[:, c0:c0 + 128]) * inv for c0 in range(0, M_DV, 128)]
        if reverse:
            for sl, hv in zip(cols, hid):
                o_ref[:, sl] = hv
            return
        ys = [so_ref[:, sl].astype(F32) * (hv + hb_ref[:, sl]) for sl, hv in zip(cols, hid)]
        sq = jnp.concatenate([(y * y).astype(BF16) for y in ys], axis=1)
        ms = jnp.dot(sq, mean_dv, preferred_element_type=F32)
        scale = lax.rsqrt(ms + EPS)
        for sl, y in zip(cols, ys):
            o_ref[:, sl] = (y * scale * hn_ref[:, sl] * sz_ref[:, sl].astype(F32)).astype(BF16)

    def stage_d(h):
        l = lane0 + h
        k = k_ref[:, h * M_DQK:(h + 1) * M_DQK]
        kw = (k.astype(F32) * _lanes(w_all[:, l:l + 1], M_DQK)).astype(BF16)
        vo = jnp.concatenate([v_ref[:, h * M_DV:(h + 1) * M_DV], ones_bf], axis=1)
        upd = lax.dot_general(kw, vo, tn, preferred_element_type=F32)
        cn_scr[h] = decay_all[:, l:l + 1] * cn_scr[h] + upd

    sa = {}
    sb = {}
    for step in range(M_HEADS + 2):
        if step < M_HEADS:
            sa[step] = stage_a(step)
        if 1 <= step <= M_HEADS:
            h = step - 1
            sb[h] = stage_b(h, sa[h][0], sa[h][2])
        if step >= 2:
            h = step - 2
            stage_c(h, sa[h][1], *sb[h])
            stage_d(h)


def _mlstm_scan(qkv, gates, oz=None, hb=None, hn=None):
    reverse = hb is None
    nlc = SEQ // M_L
    ctx_blk0 = N_LAT // M_L

    def blk(b, i):
        lat = (nlc - i) if reverse else (i - 1)
        return jnp.where(i == 0, ctx_blk0 + b, b * nlc + lat)

    wide = lambda cb: pl.BlockSpec((M_L, M_V), lambda b, i: (blk(b, i), cb))
    in_specs = [
        pl.BlockSpec((M_L, M_QK), lambda b, i: (blk(b, i), 0)),
        pl.BlockSpec((M_L, M_QK), lambda b, i: (blk(b, i), 1)),
        wide(1),
        pl.BlockSpec((M_L, M_GATES), lambda b, i: (blk(b, i), 0)),
    ]
    args = [qkv, qkv, qkv, gates]
    if not reverse:
        in_specs += [wide(0), wide(0), wide(1), pl.BlockSpec((1, M_V), lambda b, i: (0, 0))]
        args += [hb, oz, oz, hn]
    return pl.pallas_call(
        functools.partial(_mlstm_scan_kernel, reverse=reverse),
        grid=(BATCH, nlc + 1),
        in_specs=in_specs,
        out_specs=wide(0),
        out_shape=jax.ShapeDtypeStruct((N_ALL, M_V), F32 if reverse else BF16),
        scratch_shapes=[
            pltpu.VMEM((M_HEADS, M_DQK, M_DV + 128), F32),
            pltpu.VMEM((1, 128), F32),
        ],
        compiler_params=_params(("parallel", "arbitrary"), 48),
        name="mlstm_scan_bwd" if reverse else "mlstm_scan_fwd",
    )(*args)


A_QKW = A_Q + A_KVW


def _attn_qk_kernel(h_ref, w_ref, gain_ref, cos_ref, sin_ref, q_ref, k_ref):
    mean_mat = jnp.full((A_HD, A_HD), 1.0 / A_HD, BF16)
    cos = cos_ref[...]
    sin = sin_ref[...]
    for c in range(A_QKW // TN):
        acc = jnp.dot(h_ref[...], w_ref[:, c * TN:(c + 1) * TN], preferred_element_type=F32)
        for hh in range(TN // A_HD):
            lo = c * TN + hh * A_HD
            a = acc[:, hh * A_HD:(hh + 1) * A_HD]
            ms = jnp.dot((a * a).astype(BF16), mean_mat, preferred_element_type=F32)
            a = a * lax.rsqrt(ms + EPS) * gain_ref[:, lo:lo + A_HD]
            a = (a * cos + pltpu.roll(a, A_HD // 2, 1) * sin).astype(BF16)
            if lo < A_Q:
                q_ref[:, lo:lo + A_HD] = a
            else:
                k_ref[:, lo - A_Q:lo - A_Q + A_HD] = a


def _attn_qk(h, w, gain, cos, sin):
    lat_tiles = N_LAT // TP
    rope_blk = lambda r: (jnp.where(r < lat_tiles, r % (SEQ // TP), SEQ // TP), 0)
    return pl.pallas_call(
        _attn_qk_kernel,
        grid=(N_ALL // TP,),
        in_specs=[
            pl.BlockSpec((TP, D), lambda r: (r, 0)),
            _resident((D, A_QKW), lambda r: (0, 0)),
            pl.BlockSpec((1, A_QKW), lambda r: (0, 0)),
            pl.BlockSpec((TP, A_HD), rope_blk),
            pl.BlockSpec((TP, A_HD), rope_blk),
        ],
        out_specs=[
            pl.BlockSpec((TP, A_Q), lambda r: (r, 0)),
            pl.BlockSpec((TP, A_KVW), lambda r: (r, 0)),
        ],
        out_shape=[
            jax.ShapeDtypeStruct((N_ALL, A_Q), BF16),
            jax.ShapeDtypeStruct((N_ALL, A_KVW), BF16),
        ],
        compiler_params=_params(("parallel",), 48),
        name="attn_qk",
    )(h, w, gain, cos, sin)


def _attn_vz_kernel(h_ref, w_ref, v_ref, zg_ref):
    v_ref[...] = jnp.dot(h_ref[...], w_ref[:, :A_KVW], preferred_element_type=F32).astype(BF16)
    for c in range(A_Q // TN):
        lo = A_KVW + c * TN
        z = jnp.dot(h_ref[...], w_ref[:, lo:lo + TN], preferred_element_type=F32)
        zg_ref[:, c * TN:(c + 1) * TN] = _silu(z).astype(BF16)


def _attn_vz(h, w):
    return pl.pallas_call(
        _attn_vz_kernel,
        grid=(N_ALL // TP,),
        in_specs=[
            pl.BlockSpec((TP, D), lambda r: (r, 0)),
            _resident((D, A_KVW + A_Q), lambda r: (0, 1)),
        ],
        out_specs=[
            pl.BlockSpec((TP, A_KVW), lambda r: (r, 0)),
            pl.BlockSpec((TP, A_Q), lambda r: (r, 0)),
        ],
        out_shape=[
            jax.ShapeDtypeStruct((N_ALL, A_KVW), BF16),
            jax.ShapeDtypeStruct((N_ALL, A_Q), BF16),
        ],
        compiler_params=_params(("parallel",), 48),
        name="attn_vz",
    )(h, w)


A_LAT_TILES = SEQ // A_TQ
A_GPS = A_KV
A_SW = A_GPS * A_GW
A_KW = A_GPS * A_HD


def _attn_kernel(q_ref, kl_ref, kc_ref, vl_ref, vc_ref, zg_ref, o_ref, vol_scr, voc_scr):
    t = pl.program_id(2)
    nt = (((1,), (1,)), ((), ()))
    heads = A_SW // A_HD
    hpg = A_GW // A_HD

    @pl.when(t == 0)
    def _():
        for g in range(A_GPS):
            vol_scr[g, :, :A_HD] = vl_ref[:, g * A_HD:(g + 1) * A_HD]
            vol_scr[g, :, A_HD:] = jnp.ones((SEQ, A_HD), BF16)
            voc_scr[g, :, :A_HD] = vc_ref[:, g * A_HD:(g + 1) * A_HD]
            voc_scr[g, :, A_HD:] = jnp.ones((CTX, A_HD), BF16)

    def run(with_latent_keys):
        def scores(hh):
            kcols = slice((hh // hpg) * A_HD, (hh // hpg + 1) * A_HD)
            q = q_ref[:, hh * A_HD:(hh + 1) * A_HD]
            s_c = lax.dot_general(q, kc_ref[:, kcols], nt, preferred_element_type=F32)
            s_l = (lax.dot_general(q, kl_ref[:, kcols], nt, preferred_element_type=F32)
                   if with_latent_keys else None)
            return s_c, s_l

        def probs(s_c, s_l):
            m = jnp.max(s_c, axis=-1, keepdims=True)
            if with_latent_keys:
                m = jnp.maximum(m, jnp.max(s_l, axis=-1, keepdims=True))
            p_c = jnp.exp(s_c - m).astype(BF16)
            p_l = jnp.exp(s_l - m).astype(BF16) if with_latent_keys else None
            return p_c, p_l

        def output(hh, p_c, p_l):
            sl = slice(hh * A_HD, (hh + 1) * A_HD)
            acc = jnp.dot(p_c, voc_scr[hh // hpg], preferred_element_type=F32)
            if with_latent_keys:
                acc = acc + jnp.dot(p_l, vol_scr[hh // hpg], preferred_element_type=F32)
            o_ref[:, sl] = (acc[:, :A_HD] / acc[:, A_HD:] * zg_ref[:, sl].astype(F32)).astype(BF16)

        s, p = {}, {}
        for step in range(heads + 2):
            if step < heads:
                s[step] = scores(step)
            if 1 <= step <= heads:
                p[step - 1] = probs(*s[step - 1])
            if step >= 2:
                output(step - 2, *p[step - 2])

    @pl.when(t < A_LAT_TILES)
    def _():
        run(True)

    @pl.when(t >= A_LAT_TILES)
    def _():
        run(False)


def _attn(q, k, v, zg):
    ctx_blk0 = N_LAT // A_TQ
    qrow = lambda b, g, t: jnp.where(t < A_LAT_TILES, b * A_LAT_TILES + t, ctx_blk0 + b)
    return pl.pallas_call(
        _attn_kernel,
        grid=(BATCH, A_KV // A_GPS, A_LAT_TILES + 1),
        in_specs=[
            pl.BlockSpec((A_TQ, A_SW), lambda b, g, t: (qrow(b, g, t), g)),
            pl.BlockSpec((SEQ, A_KW), lambda b, g, t: (b, g)),
            pl.BlockSpec((CTX, A_KW), lambda b, g, t: (N_LAT // CTX + b, g)),
            pl.BlockSpec((SEQ, A_KW), lambda b, g, t: (b, g)),
            pl.BlockSpec((CTX, A_KW), lambda b, g, t: (N_LAT // CTX + b, g)),
            pl.BlockSpec((A_TQ, A_SW), lambda b, g, t: (qrow(b, g, t), g)),
        ],
        out_specs=pl.BlockSpec((A_TQ, A_SW), lambda b, g, t: (qrow(b, g, t), g)),
        out_shape=jax.ShapeDtypeStruct((N_ALL, A_Q), BF16),
        scratch_shapes=[pltpu.VMEM((A_GPS, SEQ, 2 * A_HD), BF16), pltpu.VMEM((A_GPS, CTX, 2 * A_HD), BF16)],
        compiler_params=_params(("parallel", "parallel", "arbitrary"), 48),
        name="attn",
    )(q, k, k, v, v, zg)


def _dft_tables(n, scale):
    idx = jnp.arange(n, dtype=jnp.int32)
    ang = ((idx[:, None] * idx[None, :]) % n).astype(F32) * (2.0 * math.pi / n)
    return (jnp.cos(ang) * scale).astype(BF16), (jnp.sin(ang) * scale).astype(BF16)


def _dif_tables():
    jt = jnp.arange(F_M, dtype=jnp.int32)
    r = jnp.arange(F_R, dtype=jnp.int32)
    k = ((F_R * jt[None, :, None] + r[:, None, None]) * jt[None, None, :]) % SEQ
    ang = k.astype(F32) * (2.0 * math.pi / SEQ)
    return (jnp.cos(ang) * SEQ ** -0.5).astype(BF16), (jnp.sin(ang) * SEQ ** -0.5).astype(BF16)


def _rope_tables():
    t = jnp.arange(SEQ)
    freqs = ROPE_THETA ** (-jnp.arange(0, A_HD // 2, 2, dtype=F32) / (A_HD // 2))
    ang = jnp.concatenate([(t // GRID_W).astype(F32)[:, None] * freqs,
                           (t % GRID_W).astype(F32)[:, None] * freqs], axis=-1)
    cos = jnp.concatenate([jnp.cos(ang), jnp.cos(ang)], axis=-1)
    sin = jnp.concatenate([-jnp.sin(ang), jnp.sin(ang)], axis=-1)
    pad = lambda a, fill: jnp.concatenate([a, jnp.full((TP, A_HD), fill, F32)], axis=0)
    return pad(cos, 1.0), pad(sin, 0.0)


def _split_heads_even_odd(w, heads):
    lead = w.shape[:-1]
    pairs = w.reshape(lead + (heads, A_HD // 2, 2))
    return jnp.swapaxes(pairs, -1, -2).reshape(lead + (heads * A_HD,))


def kernel(x, c, ctx, c_ctx, ada_w, ada_b, norm_g, fnet_w_gate, fnet_w_out, mlstm_w_in, mlstm_b_gate,
           mlstm_hn, mlstm_w_out, attn_w_in, attn_qn, attn_kn, attn_w_out, final_g):
    cc = jnp.concatenate([c, c_ctx[None, :], jnp.zeros((MOD_ROWS - BATCH - 1, D), F32)], axis=0)
    mods_all = _modvec(cc, ada_w, ada_b).reshape(DEPTH, MOD_ROWS, 1, 3 * D)

    cc_c, sc_c = _dft_tables(F_GW, F_GW ** -0.5)
    cs_chan = jnp.concatenate([cc_c, sc_c], axis=1)
    cr_lat, sr_lat = _dif_tables()
    ct_ctx, st_ctx = _dft_tables(CTX, CTX ** -0.5)
    fnet_wg_bf = fnet_w_gate.astype(BF16)
    fnet_wo_bf = fnet_w_out.astype(BF16)

    xs = None
    h = None
    for i in range(DEPTH):
        kind, j = i % 3, i // 3
        last = i == DEPTH - 1
        mods = mods_all[i]
        g = norm_g[i].reshape(1, D)
        n_rows = N_LAT if last else N_ALL
        nxt = None if last else (norm_g[i + 1].reshape(1, D), mods_all[i + 1])
        if kind == 0:
            assert i == 0 or last
            wg = (fnet_wg_bf, j)
            w_out = (fnet_wo_bf, j)
            x_lat = x.reshape(N_LAT, D) if i == 0 else xs
            p, q, h_lat = _fnet_chan_dif(*((x_lat, (g, mods)) if i == 0 else (h, None)), cs_chan)
            m_lat = _fnet_mix_lat(h_lat, cr_lat, sr_lat, p, q, wg)
            if i == 0:
                x_ctx = ctx.reshape(N_CTX, D)
                a, b = _fnet_chan_ctx(x_ctx, 0, g, mods, cs_chan)
                m_ctx = _fnet_mix_ctx(x_ctx, 0, g, mods, ct_ctx, st_ctx, a, b, wg)
                xs, h = _outproj_join(m_lat, m_ctx, w_out, x_lat, x_ctx, mods, nxt)
            else:
                out = _outproj(m_lat, w_out, xs, mods, n_rows, final=final_g.reshape(1, D))
            continue
        if kind == 1:
            w_in = mlstm_w_in[j]
            n_main = 2 * M_QK + 2 * M_V
            gate_tiles = lambda gcols: jnp.pad(
                jnp.concatenate([gcols[..., 0:8], gcols[..., 16:24]], axis=-1),
                [(0, 0)] * (gcols.ndim - 1) + [(0, 128 - 2 * M_HEADS)])
            gate_layout = lambda gcols: jnp.concatenate(
                [gate_tiles(gcols), gate_tiles(gcols[..., M_HEADS:])], axis=-1)
            w_gates = gate_layout(w_in[:, n_main:n_main + 4 * M_HEADS]).astype(BF16)
            b_gates = gate_layout(mlstm_b_gate[j][None, :])
            qkv, oz, gates = _mlstm_proj(h, w_in[:, :n_main].astype(BF16),
                                         w_in[:, n_main + 4 * M_HEADS:].astype(BF16), w_gates, b_gates)
            hb = _mlstm_scan(qkv, gates)
            mbuf = _mlstm_scan(qkv, gates, oz, hb, mlstm_hn[j].reshape(1, M_V))
            w_out = (mlstm_w_out.astype(BF16), j)
        else:
            w_in = attn_w_in[j].astype(BF16)
            w_qk = _split_heads_even_odd(w_in[:, :A_QKW], A_HEADS + A_KV)
            qn = _split_heads_even_odd(attn_qn[j], 1) * (A_HD ** -0.5)
            kn = _split_heads_even_odd(attn_kn[j], 1)
            gain = jnp.concatenate([jnp.tile(qn, A_HEADS), jnp.tile(kn, A_KV)]).reshape(1, A_QKW)
            cos, sin = _rope_tables()
            q, k = _attn_qk(h, w_qk, gain, cos, sin)
            v, zg = _attn_vz(h, w_in)
            mbuf = _attn(q, k, v, zg)
            w_out = (attn_w_out.astype(BF16), j)
        assert not last
        xs, h = _outproj(mbuf, w_out, xs, mods, n_rows, nxt=nxt)

    return out.reshape(BATCH, SEQ, D)
```

```python
import functools
import math

import jax
import jax.numpy as jnp
from jax import lax
from jax.experimental import pallas as pl
from jax.experimental.pallas import tpu as pltpu

F32 = jnp.float32
BF16 = jnp.bfloat16

D = 2048
BATCH = 16
SEQ = 2048
CTX = 256
DEPTH = 4
EPS = 1e-6
N_LAT = BATCH * SEQ
N_CTX = BATCH * CTX
N_ALL = N_LAT + N_CTX
MOD_ROWS = 24
CTX_MOD_ROW = BATCH

F_GROUPS = 4
F_GW = D // F_GROUPS

M_HEADS = 8
M_DQK = 128
M_DV = 256
M_QK = M_HEADS * M_DQK
M_V = M_HEADS * M_DV
M_L = 256
M_GATES = 256
M_HALF = 2 * M_QK + M_V

A_HEADS = 16
A_KV = 4
A_HD = 128
A_Q = A_HEADS * A_HD
A_KVW = A_KV * A_HD
A_GW = A_Q // A_KV
A_TQ = 256
GRID_W = 64
ROPE_THETA = 10000.0

TM = 512
TP = 1024
TN = 512
MIB = 1024 * 1024


def _params(sem, vmem_mib):
    return pltpu.CompilerParams(dimension_semantics=sem, vmem_limit_bytes=vmem_mib * MIB)


def _resident(shape, index_map):
    return pl.BlockSpec(shape, index_map, pipeline_mode=pl.Buffered(1))


def _layer_weight(j):
    return _resident((None, D, D), lambda *ids: (j, 0, 0))


def _sigmoid(x):
    return 1.0 / (1.0 + jnp.exp(-x))


def _silu(x):
    return x * _sigmoid(x)


def _log_sigmoid(x):
    return jnp.minimum(x, 0.0) - jnp.log1p(jnp.exp(-jnp.abs(x)))


def _modnorm(x, g, scale, shift):
    ms = jnp.mean(x * x, axis=-1, keepdims=True)
    y = x * lax.rsqrt(ms + EPS) * g
    return y * (1.0 + scale) + shift


def _store_modnorm(x_ref, g_ref, sc_ref, sh_ref, h_scr):
    rows = x_ref.shape[0]
    step = min(rows, 256)
    for r0 in range(0, rows, step):
        h_scr[r0:r0 + step, :] = _modnorm(x_ref[r0:r0 + step, :], g_ref[...], sc_ref[...],
                                          sh_ref[...]).astype(BF16)


def _mod_row(r, tm):
    return jnp.where(r < N_LAT // tm, r // (SEQ // tm), CTX_MOD_ROW)


def _modvec_kernel(c_ref, w_ref, b_ref, o_ref):
    s = _silu(c_ref[...])
    w = w_ref[...]
    s_hi = s.astype(BF16)
    s_lo = (s - s_hi.astype(F32)).astype(BF16)
    w_hi = w.astype(BF16)
    w_lo = (w - w_hi.astype(F32)).astype(BF16)
    acc = jnp.dot(s_hi, w_hi, preferred_element_type=F32)
    acc = acc + jnp.dot(s_lo, w_hi, preferred_element_type=F32)
    acc = acc + jnp.dot(s_hi, w_lo, preferred_element_type=F32)
    o_ref[...] = acc + b_ref[...]


def _modvec(cc, ada_w, ada_b):
    tn = 2048
    return pl.pallas_call(
        _modvec_kernel,
        grid=(DEPTH, 3 * D // tn),
        in_specs=[
            pl.BlockSpec((MOD_ROWS, D), lambda i, j: (0, 0)),
            pl.BlockSpec((None, D, tn), lambda i, j: (i, 0, j)),
            pl.BlockSpec((None, 1, tn), lambda i, j: (i, 0, j)),
        ],
        out_specs=pl.BlockSpec((None, MOD_ROWS, tn), lambda i, j: (i, 0, j)),
        out_shape=jax.ShapeDtypeStruct((DEPTH, MOD_ROWS, 3 * D), F32),
        compiler_params=_params(("parallel", "parallel"), 48),
        name="modvec",
    )(cc, ada_w, ada_b.reshape(DEPTH, 1, 3 * D))


def _fnet_chan_kernel(x_ref, g_ref, sh_ref, sc_ref, cs_ref, a_ref, b_ref):
    h = _modnorm(x_ref[...], g_ref[...], sc_ref[...], sh_ref[...]).astype(BF16)
    for grp in range(F_GROUPS):
        sl = slice(grp * F_GW, (grp + 1) * F_GW)
        p = jnp.dot(h[:, sl], cs_ref[...], preferred_element_type=F32)
        a_ref[:, sl] = p[:, :F_GW].astype(BF16)
        b_ref[:, sl] = p[:, F_GW:].astype(BF16)


def _fnet_chan_ctx(xc, row_blk0, g, mods, cs_c):
    tm = 512
    out = jax.ShapeDtypeStruct((N_CTX, D), BF16)
    mod = lambda chunk: pl.BlockSpec((None, 1, D), lambda r: (CTX_MOD_ROW, 0, chunk))
    return pl.pallas_call(
        _fnet_chan_kernel,
        grid=(N_CTX // tm,),
        in_specs=[pl.BlockSpec((tm, D), lambda r: (row_blk0 + r, 0)),
                  pl.BlockSpec((1, D), lambda r: (0, 0)), mod(0), mod(1),
                  pl.BlockSpec((F_GW, 2 * F_GW), lambda r: (0, 0))],
        out_specs=[pl.BlockSpec((tm, D), lambda r: (r, 0))] * 2,
        out_shape=[out, out],
        compiler_params=_params(("parallel",), 48),
        name="fnet_chan_ctx",
    )(xc, g, mods, mods, cs_c)


F_R = 4
F_M = SEQ // F_R
F_TC = 256
F_SL = TN // 128


def _fnet_chan_dif_kernel(x0_ref, x1_ref, x2_ref, x3_ref, *rest, prenormed):
    if prenormed:
        cs_ref, p_ref, q_ref = rest
        hs = [x[...] for x in (x0_ref, x1_ref, x2_ref, x3_ref)]
    else:
        g_ref, sh_ref, sc_ref, cs_ref, p_ref, q_ref, h_ref = rest
        hs = [_modnorm(x[...], g_ref[...], sc_ref[...], sh_ref[...]).astype(BF16)
              for x in (x0_ref, x1_ref, x2_ref, x3_ref)]
        for qi, h in enumerate(hs):
            h_ref[qi] = h
    for grp in range(F_GROUPS):
        sl = slice(grp * F_GW, (grp + 1) * F_GW)
        ab = [jnp.dot(h[:, sl], cs_ref[...], preferred_element_type=F32) for h in hs]
        a = [t[:, :F_GW] for t in ab]
        b = [t[:, F_GW:] for t in ab]
        sa02, da02, sa13, da13 = a[0] + a[2], a[0] - a[2], a[1] + a[3], a[1] - a[3]
        sb02, db02, sb13, db13 = b[0] + b[2], b[0] - b[2], b[1] + b[3], b[1] - b[3]
        re = (sa02 + sa13, da02 - db13, sa02 - sa13, da02 + db13)
        im = (sb02 + sb13, db02 + da13, sb02 - sb13, db02 - da13)
        for r in range(F_R):
            p_ref[r, :, sl] = re[r].astype(BF16)
            q_ref[r, :, sl] = im[r].astype(BF16)


def _fnet_chan_dif(xl, norm, cs_c):
    nt = F_M // F_TC
    xspec = lambda q: pl.BlockSpec((F_TC, D), lambda b, i: (b * (SEQ // F_TC) + q * nt + i, 0))
    mod = lambda chunk: pl.BlockSpec((None, 1, D), lambda b, i: (b, 0, chunk))
    out = jax.ShapeDtypeStruct((BATCH, F_R, F_M, D), BF16)
    ospec = pl.BlockSpec((None, F_R, F_TC, D), lambda b, i: (b, 0, i, 0))
    out_specs, out_shape = [ospec, ospec], [out, out]
    norm_specs, norm_args = [], []
    if norm is not None:
        norm_specs = [pl.BlockSpec((1, D), lambda b, i: (0, 0)), mod(0), mod(1)]
        norm_args = [norm[0], norm[1], norm[1]]
        out_specs.append(pl.BlockSpec((None, F_R, None, F_TC, D), lambda b, i: (b, 0, i, 0, 0)))
        out_shape.append(jax.ShapeDtypeStruct((BATCH, F_R, nt, F_TC, D), BF16))
    res = pl.pallas_call(
        functools.partial(_fnet_chan_dif_kernel, prenormed=norm is None),
        grid=(BATCH, nt),
        in_specs=[xspec(0), xspec(1), xspec(2), xspec(3)] + norm_specs + [
            pl.BlockSpec((F_GW, 2 * F_GW), lambda b, i: (0, 0))],
        out_specs=out_specs,
        out_shape=out_shape,
        compiler_params=_params(("parallel", "parallel"), 56),
        name="fnet_chan_dif",
    )(xl, xl, xl, xl, *norm_args, cs_c)
    return (res[0], res[1], xl) if norm is None else (res[0], res[1], res[2].reshape(N_LAT, D))


def _fnet_mix_kernel(x_ref, g_ref, sh_ref, sc_ref, c_ref, s_ref, p_ref, q_ref, wg_ref, o_ref, h_scr):
    _store_modnorm(x_ref, g_ref, sc_ref, sh_ref, h_scr)
    for c in range(D // TN):
        sl = slice(c * TN, (c + 1) * TN)
        y = jnp.dot(c_ref[...], p_ref[:, sl], preferred_element_type=F32)
        y = y - jnp.dot(s_ref[...], q_ref[:, sl], preferred_element_type=F32)
        gate = jnp.dot(h_scr[...], wg_ref[:, sl], preferred_element_type=F32)
        o_ref[:, sl] = (y * _silu(gate)).astype(BF16)


def _fnet_mix_lat_kernel(h_ref, c_ref, s_ref, p_ref, q_ref, wg_ref, o_ref, y_scr):
    for r in range(F_R):
        y = jnp.dot(c_ref[r], p_ref[r], preferred_element_type=F32)
        y = y - jnp.dot(s_ref[r], q_ref[r], preferred_element_type=F32)
        for c in range(F_SL):
            y_scr[c, pl.ds(r, F_M, stride=F_R), :] = y[:, c * 128:(c + 1) * 128]
    for t in range(SEQ // F_M):
        rows = slice(t * F_M, (t + 1) * F_M)
        gate = jnp.dot(h_ref[rows, :], wg_ref[...], preferred_element_type=F32)
        for c in range(F_SL):
            cols = slice(c * 128, (c + 1) * 128)
            o_ref[rows, cols] = (y_scr[c, rows, :] * _silu(gate[:, cols])).astype(BF16)


def _fnet_mix_lat(h, cr, sr, p, q, wg):
    tab = _resident((F_R, F_M, F_M), lambda b, n: (0, 0, 0))
    pq = pl.BlockSpec((None, F_R, F_M, TN), lambda b, n: (b, 0, 0, n))
    return pl.pallas_call(
        _fnet_mix_lat_kernel,
        grid=(BATCH, D // TN),
        in_specs=[pl.BlockSpec((SEQ, D), lambda b, n: (b, 0)), tab, tab, pq, pq,
                  pl.BlockSpec((None, D, TN), lambda b, n: (wg[1], 0, n))],
        out_specs=pl.BlockSpec((SEQ, TN), lambda b, n: (b, n)),
        out_shape=jax.ShapeDtypeStruct((N_LAT, D), BF16),
        scratch_shapes=[pltpu.VMEM((F_SL, SEQ, 128), F32)],
        compiler_params=_params(("parallel", "arbitrary"), 56),
        name="fnet_mix_lat",
    )(h, cr, sr, p, q, wg[0])


def _fnet_mix_ctx(xc, row_blk0, g, mods, ct, st, a, b, wg):
    mod = lambda chunk: pl.BlockSpec((None, 1, D), lambda bi: (CTX_MOD_ROW, 0, chunk))
    tab = pl.BlockSpec((CTX, CTX), lambda bi: (0, 0))
    ab = pl.BlockSpec((CTX, D), lambda bi: (bi, 0))
    return pl.pallas_call(
        _fnet_mix_kernel,
        grid=(BATCH,),
        in_specs=[pl.BlockSpec((CTX, D), lambda bi: (row_blk0 + bi, 0)),
                  pl.BlockSpec((1, D), lambda bi: (0, 0)), mod(0), mod(1),
                  tab, tab, ab, ab, _layer_weight(wg[1])],
        out_specs=pl.BlockSpec((CTX, D), lambda bi: (bi, 0)),
        out_shape=jax.ShapeDtypeStruct((N_CTX, D), BF16),
        scratch_shapes=[pltpu.VMEM((CTX, D), BF16)],
        compiler_params=_params(("parallel",), 48),
        name="fnet_mix_ctx",
    )(xc, g, mods, mods, ct, st, a, b, wg[0])


def _residual_update(m_ref, w_ref, x_ref, gate_ref, o_ref, nxt):
    ss = 0.0
    for c in range(D // TN):
        sl = slice(c * TN, (c + 1) * TN)
        y = jnp.dot(m_ref[...], w_ref[:, sl], preferred_element_type=F32)
        o = x_ref[:, sl] + gate_ref[:, sl] * y
        o_ref[:, sl] = o
        if nxt is not None:
            ss = ss + jnp.sum(o * o, axis=-1, keepdims=True)
    if nxt is not None:
        gn_ref, shn_ref, scn_ref, h_ref = nxt
        rinv = lax.rsqrt(ss * (1.0 / D) + EPS)
        for c in range(D // TN):
            sl = slice(c * TN, (c + 1) * TN)
            gs = gn_ref[:, sl] * (1.0 + scn_ref[:, sl])
            h_ref[:, sl] = (o_ref[:, sl] * rinv * gs + shn_ref[:, sl]).astype(BF16)


def _outproj_kernel(m_ref, w_ref, x_ref, gate_ref, *rest, final):
    if final:
        fg_ref, o_ref = rest
        _residual_update(m_ref, w_ref, x_ref, gate_ref, o_ref, None)
        x = o_ref[...]
        ms = jnp.mean(x * x, axis=-1, keepdims=True)
        o_ref[...] = x * lax.rsqrt(ms + EPS) * fg_ref[...]
    else:
        gn_ref, shn_ref, scn_ref, o_ref, h_ref = rest
        _residual_update(m_ref, w_ref, x_ref, gate_ref, o_ref, (gn_ref, shn_ref, scn_ref, h_ref))


def _next_norm_specs():
    return [pl.BlockSpec((1, D), lambda r: (0, 0)),
            pl.BlockSpec((None, 1, D), lambda r: (_mod_row(r, TM), 0, 0)),
            pl.BlockSpec((None, 1, D), lambda r: (_mod_row(r, TM), 0, 1))]


def _outproj(mbuf, w, xs, mods, n_rows, final=None, nxt=None):
    nr = n_rows // TM
    tile = pl.BlockSpec((TM, D), lambda r: (r, 0))
    in_specs = [
        tile,
        _layer_weight(w[1]),
        tile,
        pl.BlockSpec((None, 1, D), lambda r: (_mod_row(r, TM), 0, 2)),
    ]
    args = [mbuf, w[0], xs, mods]
    if final is not None:
        in_specs.append(pl.BlockSpec((1, D), lambda r: (0, 0)))
        args.append(final)
        return pl.pallas_call(
            functools.partial(_outproj_kernel, final=True),
            grid=(n_rows // TM,),
            in_specs=in_specs,
            out_specs=tile,
            out_shape=jax.ShapeDtypeStruct((n_rows, D), F32),
            compiler_params=_params(("parallel",), 48),
            name="outproj_final",
        )(*args)
    g_next, mods_next = nxt
    return pl.pallas_call(
        functools.partial(_outproj_kernel, final=False),
        grid=(nr,),
        in_specs=in_specs + _next_norm_specs(),
        out_specs=[tile, tile],
        out_shape=[jax.ShapeDtypeStruct((N_ALL, D), F32), jax.ShapeDtypeStruct((N_ALL, D), BF16)],
        input_output_aliases={2: 0},
        compiler_params=_params(("parallel",), 48),
        name="outproj",
    )(*args, g_next, mods_next, mods_next)


def _outproj_join_kernel(ml_ref, mc_ref, w_ref, xl_ref, xc_ref, gate_ref,
                         gn_ref, shn_ref, scn_ref, o_ref, h_ref):
    is_lat = pl.program_id(0) < N_LAT // TM
    nxt = (gn_ref, shn_ref, scn_ref, h_ref)

    @pl.when(is_lat)
    def _():
        _residual_update(ml_ref, w_ref, xl_ref, gate_ref, o_ref, nxt)

    @pl.when(jnp.logical_not(is_lat))
    def _():
        _residual_update(mc_ref, w_ref, xc_ref, gate_ref, o_ref, nxt)


def _outproj_join(m_lat, m_ctx, w, x_lat, x_ctx, mods, nxt):
    nl = N_LAT // TM
    lat = pl.BlockSpec((TM, D), lambda r: (jnp.minimum(r, nl - 1), 0))
    cxt = pl.BlockSpec((TM, D), lambda r: (jnp.maximum(r - nl, 0), 0))
    tile = pl.BlockSpec((TM, D), lambda r: (r, 0))
    g_next, mods_next = nxt
    return pl.pallas_call(
        _outproj_join_kernel,
        grid=(N_ALL // TM,),
        in_specs=[lat, cxt, _layer_weight(w[1]), lat, cxt,
                  pl.BlockSpec((None, 1, D), lambda r: (_mod_row(r, TM), 0, 2))] + _next_norm_specs(),
        out_specs=[tile, tile],
        out_shape=[jax.ShapeDtypeStruct((N_ALL, D), F32), jax.ShapeDtypeStruct((N_ALL, D), BF16)],
        compiler_params=_params(("parallel",), 48),
        name="outproj_join",
    )(m_lat, m_ctx, w[0], x_lat, x_ctx, mods, g_next, mods_next, mods_next)


def _mlstm_qkv_kernel(h_ref, w_ref, wg_ref, bg_ref, out_ref, gt_ref):
    gt_ref[...] = jnp.dot(h_ref[...], wg_ref[...], preferred_element_type=F32) + bg_ref[...]
    for c in range(M_HALF // TN):
        sl = slice(c * TN, (c + 1) * TN)
        acc = jnp.dot(h_ref[...], w_ref[:, sl], preferred_element_type=F32)
        if c < M_QK // TN:
            acc = acc * (M_DQK ** -0.5)
        out_ref[:, sl] = acc.astype(BF16)


def _mlstm_oz_kernel(h_ref, wo_ref, wz_ref, out_ref):
    for c in range(M_V // TN):
        sl = slice(c * TN, (c + 1) * TN)
        o = jnp.dot(h_ref[...], wo_ref[:, sl], preferred_element_type=F32)
        out_ref[:, sl] = _sigmoid(o).astype(BF16)
    for c in range(M_V // TN):
        sl = slice(c * TN, (c + 1) * TN)
        z = jnp.dot(h_ref[...], wz_ref[:, sl], preferred_element_type=F32)
        out_ref[:, M_V + c * TN:M_V + (c + 1) * TN] = _silu(z).astype(BF16)


def _mlstm_proj(h, w_in, w_z, wg, bg):
    tile = pl.BlockSpec((TP, D), lambda r: (r, 0))
    wide = pl.BlockSpec((TP, M_HALF), lambda r: (r, 0))
    qkv, gates = pl.pallas_call(
        _mlstm_qkv_kernel,
        grid=(N_ALL // TP,),
        in_specs=[tile, _resident((D, M_HALF), lambda r: (0, 0)),
                  pl.BlockSpec((D, M_GATES), lambda r: (0, 0)), pl.BlockSpec((1, M_GATES), lambda r: (0, 0))],
        out_specs=[wide, pl.BlockSpec((TP, M_GATES), lambda r: (r, 0))],
        out_shape=[jax.ShapeDtypeStruct((N_ALL, M_HALF), BF16), jax.ShapeDtypeStruct((N_ALL, M_GATES), F32)],
        compiler_params=_params(("parallel",), 48),
        name="mlstm_qkv",
    )(h, w_in, wg, bg)
    oz = pl.pallas_call(
        _mlstm_oz_kernel,
        grid=(N_ALL // TP,),
        in_specs=[tile, _resident((D, M_V), lambda r: (0, M_HALF // M_V)), _resident((D, M_V), lambda r: (0, 0))],
        out_specs=wide,
        out_shape=jax.ShapeDtypeStruct((N_ALL, M_HALF), BF16),
        compiler_params=_params(("parallel",), 48),
        name="mlstm_oz",
    )(h, w_in, w_z)
    return qkv, oz, gates


def _split3(x):
    hi = x.astype(BF16)
    r1 = x - hi.astype(F32)
    mid = r1.astype(BF16)
    lo = (r1 - mid.astype(F32)).astype(BF16)
    return hi, mid, lo


def _cummax_rows(x, reverse):
    rows = x.shape[0]
    row = lax.broadcasted_iota(jnp.int32, x.shape, 0)
    sh = 1
    while sh < rows:
        if reverse:
            x = jnp.where(row < rows - sh, jnp.maximum(x, pltpu.roll(x, rows - sh, 0)), x)
        else:
            x = jnp.where(row >= sh, jnp.maximum(x, pltpu.roll(x, sh, 0)), x)
        sh *= 2
    return x


def _lanes(col, width):
    return jnp.broadcast_to(col, (col.shape[0], width))


def _mlstm_scan_kernel(q_ref, k_ref, v_ref, gt_ref, *rest, reverse):
    if reverse:
        o_ref, cn_scr, m_scr = rest
    else:
        hb_ref, so_ref, sz_ref, hn_ref, o_ref, cn_scr, m_scr = rest

    @pl.when(pl.program_id(1) == 0)
    def _():
        cn_scr[...] = jnp.zeros_like(cn_scr)
        m_scr[...] = jnp.zeros_like(m_scr)

    L = M_L
    lane0 = M_HEADS if reverse else 0
    ig = gt_ref[:, :128]
    ls = _log_sigmoid(gt_ref[:, 128:])
    row = lax.broadcasted_iota(jnp.int32, (L, L), 0)
    col = lax.broadcasted_iota(jnp.int32, (L, L), 1)
    order = (row <= col) if reverse else (row >= col)
    tri = jnp.where(order, 1.0, 0.0).astype(BF16)
    hi, mid, lo = _split3(ls)
    b = (jnp.dot(tri, hi, preferred_element_type=F32)
         + jnp.dot(tri, mid, preferred_element_type=F32)
         + jnp.dot(tri, lo, preferred_element_type=F32))
    end = 0 if reverse else L - 1

    m_prev = m_scr[...]
    r = ig - b
    inter = b + m_prev
    m_t = jnp.maximum(inter, b + _cummax_rows(r, reverse))
    a_all = jnp.exp(inter - m_t)
    u_all = b - m_t
    en_all = jnp.exp(-m_t)
    b_end = b[end:end + 1, :]
    gl = b_end - b + ig
    m_new = jnp.maximum(b_end + m_prev, jnp.max(gl, axis=0, keepdims=True))
    w_all = jnp.exp(gl - m_new)
    decay_all = jnp.exp(b_end + m_prev - m_new)
    r_t = r.T
    m_scr[...] = m_new

    ones_bf = jnp.ones((L, 128), BF16)
    mean_dv = jnp.full((M_DV, 128), 1.0 / M_DV, BF16)
    nt = (((1,), (1,)), ((), ()))
    tn = (((0,), (0,)), ((), ()))

    def stage_a(h):
        l = lane0 + h
        q = q_ref[:, h * M_DQK:(h + 1) * M_DQK]
        k = k_ref[:, h * M_DQK:(h + 1) * M_DQK]
        qk = lax.dot_general(q, k, nt, preferred_element_type=F32)
        qcn = jnp.dot(q, cn_scr[h].astype(BF16), preferred_element_type=F32)
        p = jnp.exp(jnp.where(order, u_all[:, l:l + 1] + r_t[l:l + 1, :], -jnp.inf))
        return qk, qcn, p

    def stage_b(h, qk, p):
        s = qk * p
        s_hi = s.astype(BF16)
        s_lo = (s - s_hi.astype(F32)).astype(BF16)
        vo = jnp.concatenate([v_ref[:, h * M_DV:(h + 1) * M_DV], ones_bf], axis=1)
        sv = jnp.dot(s_hi, vo, preferred_element_type=F32)
        return sv, jnp.dot(s_lo, ones_bf, preferred_element_type=F32)

    def stage_c(h, qcn, sv, rs_lo):
        l = lane0 + h
        a = _lanes(a_all[:, l:l + 1], 128)
        den = a * qcn[:, M_DV:] + (sv[:, M_DV:] + rs_lo)
        inv = 1.0 / jnp.maximum(jnp.abs(den), _lanes(en_all[:, l:l + 1], 128))
        cols = [slice(h * M_DV + c0, h * M_DV + c0 + 128) for c0 in range(0, M_DV, 128)]
        hid = [(a * qcn[:, c0:c0 + 128] + sv[:, c0:c0 + 128]) * inv for c0 in range(0, M_DV, 128)]
        if reverse:
            for sl, hv in zip(cols, hid):
                o_ref[:, sl] = hv
            return
        ys = [so_ref[:, sl].astype(F32) * (hv + hb_ref[:, sl]) for sl, hv in zip(cols, hid)]
        sq = jnp.concatenate([(y * y).astype(BF16) for y in ys], axis=1)
        ms = jnp.dot(sq, mean_dv, preferred_element_type=F32)
        scale = lax.rsqrt(ms + EPS)
        for sl, y in zip(cols, ys):
            o_ref[:, sl] = (y * scale * hn_ref[:, sl] * sz_ref[:, sl].astype(F32)).astype(BF16)

    def stage_d(h):
        l = lane0 + h
        k = k_ref[:, h * M_DQK:(h + 1) * M_DQK]
        kw = (k.astype(F32) * _lanes(w_all[:, l:l + 1], M_DQK)).astype(BF16)
        vo = jnp.concatenate([v_ref[:, h * M_DV:(h + 1) * M_DV], ones_bf], axis=1)
        upd = lax.dot_general(kw, vo, tn, preferred_element_type=F32)
        cn_scr[h] = decay_all[:, l:l + 1] * cn_scr[h] + upd

    sa = {}
    sb = {}
    for step in range(M_HEADS + 2):
        if step < M_HEADS:
            sa[step] = stage_a(step)
        if 1 <= step <= M_HEADS:
            h = step - 1
            sb[h] = stage_b(h, sa[h][0], sa[h][2])
        if step >= 2:
            h = step - 2
            stage_c(h, sa[h][1], *sb[h])
            stage_d(h)


def _mlstm_scan(qkv, gates, oz=None, hb=None, hn=None):
    reverse = hb is None
    nlc = SEQ // M_L
    ctx_blk0 = N_LAT // M_L

    def blk(b, i):
        lat = (nlc - i) if reverse else (i - 1)
        return jnp.where(i == 0, ctx_blk0 + b, b * nlc + lat)

    wide = lambda cb: pl.BlockSpec((M_L, M_V), lambda b, i: (blk(b, i), cb))
    in_specs = [
        pl.BlockSpec((M_L, M_QK), lambda b, i: (blk(b, i), 0)),
        pl.BlockSpec((M_L, M_QK), lambda b, i: (blk(b, i), 1)),
        wide(1),
        pl.BlockSpec((M_L, M_GATES), lambda b, i: (blk(b, i), 0)),
    ]
    args = [qkv, qkv, qkv, gates]
    if not reverse:
        in_specs += [wide(0), wide(0), wide(1), pl.BlockSpec((1, M_V), lambda b, i: (0, 0))]
        args += [hb, oz, oz, hn]
    return pl.pallas_call(
        functools.partial(_mlstm_scan_kernel, reverse=reverse),
        grid=(BATCH, nlc + 1),
        in_specs=in_specs,
        out_specs=wide(0),
        out_shape=jax.ShapeDtypeStruct((N_ALL, M_V), F32 if reverse else BF16),
        scratch_shapes=[
            pltpu.VMEM((M_HEADS, M_DQK, M_DV + 128), F32),
            pltpu.VMEM((1, 128), F32),
        ],
        compiler_params=_params(("parallel", "arbitrary"), 48),
        name="mlstm_scan_bwd" if reverse else "mlstm_scan_fwd",
    )(*args)


A_QKW = A_Q + A_KVW


def _attn_qk_kernel(h_ref, w_ref, gain_ref, cos_ref, sin_next_ref, sin_prev_ref, q_ref, k_ref):
    mean_mat = jnp.full((A_HD, A_HD), 1.0 / A_HD, BF16)
    cos = cos_ref[...]
    sin_next = sin_next_ref[...]
    sin_prev = sin_prev_ref[...]
    for c in range(A_QKW // TN):
        acc = jnp.dot(h_ref[...], w_ref[:, c * TN:(c + 1) * TN], preferred_element_type=F32)
        for hh in range(TN // A_HD):
            lo = c * TN + hh * A_HD
            a = acc[:, hh * A_HD:(hh + 1) * A_HD]
            ms = jnp.dot((a * a).astype(BF16), mean_mat, preferred_element_type=F32)
            a = a * lax.rsqrt(ms + EPS) * gain_ref[:, lo:lo + A_HD]
            a = (a * cos + pltpu.roll(a, A_HD - 1, 1) * sin_next + pltpu.roll(a, 1, 1) * sin_prev).astype(BF16)
            if lo < A_Q:
                q_ref[:, lo:lo + A_HD] = a
            else:
                k_ref[:, lo - A_Q:lo - A_Q + A_HD] = a


def _attn_qk(h, w, gain, rope):
    lat_tiles = N_LAT // TP
    rope_blk = lambda r: (jnp.where(r < lat_tiles, r % (SEQ // TP), SEQ // TP), 0)
    return pl.pallas_call(
        _attn_qk_kernel,
        grid=(N_ALL // TP,),
        in_specs=[
            pl.BlockSpec((TP, D), lambda r: (r, 0)),
            _resident((D, A_QKW), lambda r: (0, 0)),
            pl.BlockSpec((1, A_QKW), lambda r: (0, 0)),
            pl.BlockSpec((TP, A_HD), rope_blk),
            pl.BlockSpec((TP, A_HD), rope_blk),
            pl.BlockSpec((TP, A_HD), rope_blk),
        ],
        out_specs=[
            pl.BlockSpec((TP, A_Q), lambda r: (r, 0)),
            pl.BlockSpec((TP, A_KVW), lambda r: (r, 0)),
        ],
        out_shape=[
            jax.ShapeDtypeStruct((N_ALL, A_Q), BF16),
            jax.ShapeDtypeStruct((N_ALL, A_KVW), BF16),
        ],
        compiler_params=_params(("parallel",), 48),
        name="attn_qk",
    )(h, w, gain, *rope)


def _attn_vz_kernel(h_ref, w_ref, v_ref, zg_ref):
    v_ref[...] = jnp.dot(h_ref[...], w_ref[:, :A_KVW], preferred_element_type=F32).astype(BF16)
    for c in range(A_Q // TN):
        lo = A_KVW + c * TN
        z = jnp.dot(h_ref[...], w_ref[:, lo:lo + TN], preferred_element_type=F32)
        zg_ref[:, c * TN:(c + 1) * TN] = _silu(z).astype(BF16)


def _attn_vz(h, w):
    return pl.pallas_call(
        _attn_vz_kernel,
        grid=(N_ALL // TP,),
        in_specs=[
            pl.BlockSpec((TP, D), lambda r: (r, 0)),
            _resident((D, A_KVW + A_Q), lambda r: (0, 1)),
        ],
        out_specs=[
            pl.BlockSpec((TP, A_KVW), lambda r: (r, 0)),
            pl.BlockSpec((TP, A_Q), lambda r: (r, 0)),
        ],
        out_shape=[
            jax.ShapeDtypeStruct((N_ALL, A_KVW), BF16),
            jax.ShapeDtypeStruct((N_ALL, A_Q), BF16),
        ],
        compiler_params=_params(("parallel",), 48),
        name="attn_vz",
    )(h, w)


A_LAT_TILES = SEQ // A_TQ
A_GPS = A_KV
A_SW = A_GPS * A_GW
A_KW = A_GPS * A_HD


def _attn_kernel(q_ref, kl_ref, kc_ref, vl_ref, vc_ref, zg_ref, o_ref, vol_scr, voc_scr):
    t = pl.program_id(2)
    nt = (((1,), (1,)), ((), ()))
    heads = A_SW // A_HD
    hpg = A_GW // A_HD

    @pl.when(t == 0)
    def _():
        for g in range(A_GPS):
            vol_scr[g, :, :A_HD] = vl_ref[:, g * A_HD:(g + 1) * A_HD]
            vol_scr[g, :, A_HD:] = jnp.ones((SEQ, A_HD), BF16)
            voc_scr[g, :, :A_HD] = vc_ref[:, g * A_HD:(g + 1) * A_HD]
            voc_scr[g, :, A_HD:] = jnp.ones((CTX, A_HD), BF16)

    def run(with_latent_keys):
        def scores(hh):
            kcols = slice((hh // hpg) * A_HD, (hh // hpg + 1) * A_HD)
            q = q_ref[:, hh * A_HD:(hh + 1) * A_HD]
            s_c = lax.dot_general(q, kc_ref[:, kcols], nt, preferred_element_type=F32)
            s_l = (lax.dot_general(q, kl_ref[:, kcols], nt, preferred_element_type=F32)
                   if with_latent_keys else None)
            return s_c, s_l

        def probs(s_c, s_l):
            m = jnp.max(s_c, axis=-1, keepdims=True)
            if with_latent_keys:
                m = jnp.maximum(m, jnp.max(s_l, axis=-1, keepdims=True))
            p_c = jnp.exp(s_c - m).astype(BF16)
            p_l = jnp.exp(s_l - m).astype(BF16) if with_latent_keys else None
            return p_c, p_l

        def output(hh, p_c, p_l):
            sl = slice(hh * A_HD, (hh + 1) * A_HD)
            acc = jnp.dot(p_c, voc_scr[hh // hpg], preferred_element_type=F32)
            if with_latent_keys:
                acc = acc + jnp.dot(p_l, vol_scr[hh // hpg], preferred_element_type=F32)
            o_ref[:, sl] = (acc[:, :A_HD] / acc[:, A_HD:] * zg_ref[:, sl].astype(F32)).astype(BF16)

        s, p = {}, {}
        for step in range(heads + 2):
            if step < heads:
                s[step] = scores(step)
            if 1 <= step <= heads:
                p[step - 1] = probs(*s[step - 1])
            if step >= 2:
                output(step - 2, *p[step - 2])

    @pl.when(t < A_LAT_TILES)
    def _():
        run(True)

    @pl.when(t >= A_LAT_TILES)
    def _():
        run(False)


def _attn(q, k, v, zg):
    ctx_blk0 = N_LAT // A_TQ
    qrow = lambda b, g, t: jnp.where(t < A_LAT_TILES, b * A_LAT_TILES + t, ctx_blk0 + b)
    return pl.pallas_call(
        _attn_kernel,
        grid=(BATCH, A_KV // A_GPS, A_LAT_TILES + 1),
        in_specs=[
            pl.BlockSpec((A_TQ, A_SW), lambda b, g, t: (qrow(b, g, t), g)),
            pl.BlockSpec((SEQ, A_KW), lambda b, g, t: (b, g)),
            pl.BlockSpec((CTX, A_KW), lambda b, g, t: (N_LAT // CTX + b, g)),
            pl.BlockSpec((SEQ, A_KW), lambda b, g, t: (b, g)),
            pl.BlockSpec((CTX, A_KW), lambda b, g, t: (N_LAT // CTX + b, g)),
            pl.BlockSpec((A_TQ, A_SW), lambda b, g, t: (qrow(b, g, t), g)),
        ],
        out_specs=pl.BlockSpec((A_TQ, A_SW), lambda b, g, t: (qrow(b, g, t), g)),
        out_shape=jax.ShapeDtypeStruct((N_ALL, A_Q), BF16),
        scratch_shapes=[pltpu.VMEM((A_GPS, SEQ, 2 * A_HD), BF16), pltpu.VMEM((A_GPS, CTX, 2 * A_HD), BF16)],
        compiler_params=_params(("parallel", "parallel", "arbitrary"), 48),
        name="attn",
    )(q, k, k, v, v, zg)


def _dft_tables(n, scale):
    idx = jnp.arange(n, dtype=jnp.int32)
    ang = ((idx[:, None] * idx[None, :]) % n).astype(F32) * (2.0 * math.pi / n)
    return (jnp.cos(ang) * scale).astype(BF16), (jnp.sin(ang) * scale).astype(BF16)


def _dif_tables():
    jt = jnp.arange(F_M, dtype=jnp.int32)
    r = jnp.arange(F_R, dtype=jnp.int32)
    k = ((F_R * jt[None, :, None] + r[:, None, None]) * jt[None, None, :]) % SEQ
    ang = k.astype(F32) * (2.0 * math.pi / SEQ)
    return (jnp.cos(ang) * SEQ ** -0.5).astype(BF16), (jnp.sin(ang) * SEQ ** -0.5).astype(BF16)


def _rope_tables():
    t = jnp.arange(SEQ)
    freqs = ROPE_THETA ** (-jnp.arange(0, A_HD // 2, 2, dtype=F32) / (A_HD // 2))
    ang = jnp.concatenate([(t // GRID_W).astype(F32)[:, None] * freqs,
                           (t % GRID_W).astype(F32)[:, None] * freqs], axis=-1)
    cos = jnp.repeat(jnp.cos(ang), 2, axis=-1)
    sin = jnp.repeat(jnp.sin(ang), 2, axis=-1)
    even = (jnp.arange(A_HD) % 2 == 0)[None, :]
    pad = lambda a, fill: jnp.concatenate([a, jnp.full((TP, A_HD), fill, F32)], axis=0)
    return pad(cos, 1.0), pad(jnp.where(even, -sin, 0.0), 0.0), pad(jnp.where(even, 0.0, sin), 0.0)


def kernel(x, c, ctx, c_ctx, ada_w, ada_b, norm_g, fnet_w_gate, fnet_w_out, mlstm_w_in, mlstm_b_gate,
           mlstm_hn, mlstm_w_out, attn_w_in, attn_qn, attn_kn, attn_w_out, final_g):
    cc = jnp.concatenate([c, c_ctx[None, :], jnp.zeros((MOD_ROWS - BATCH - 1, D), F32)], axis=0)
    mods_all = _modvec(cc, ada_w, ada_b).reshape(DEPTH, MOD_ROWS, 1, 3 * D)

    cc_c, sc_c = _dft_tables(F_GW, F_GW ** -0.5)
    cs_chan = jnp.concatenate([cc_c, sc_c], axis=1)
    cr_lat, sr_lat = _dif_tables()
    ct_ctx, st_ctx = _dft_tables(CTX, CTX ** -0.5)
    fnet_wg_bf = fnet_w_gate.astype(BF16)
    fnet_wo_bf = fnet_w_out.astype(BF16)

    xs = None
    h = None
    for i in range(DEPTH):
        kind, j = i % 3, i // 3
        last = i == DEPTH - 1
        mods = mods_all[i]
        g = norm_g[i].reshape(1, D)
        n_rows = N_LAT if last else N_ALL
        nxt = None if last else (norm_g[i + 1].reshape(1, D), mods_all[i + 1])
        if kind == 0:
            assert i == 0 or last
            wg = (fnet_wg_bf, j)
            w_out = (fnet_wo_bf, j)
            x_lat = x.reshape(N_LAT, D) if i == 0 else xs
            p, q, h_lat = _fnet_chan_dif(*((x_lat, (g, mods)) if i == 0 else (h, None)), cs_chan)
            m_lat = _fnet_mix_lat(h_lat, cr_lat, sr_lat, p, q, wg)
            if i == 0:
                x_ctx = ctx.reshape(N_CTX, D)
                a, b = _fnet_chan_ctx(x_ctx, 0, g, mods, cs_chan)
                m_ctx = _fnet_mix_ctx(x_ctx, 0, g, mods, ct_ctx, st_ctx, a, b, wg)
                xs, h = _outproj_join(m_lat, m_ctx, w_out, x_lat, x_ctx, mods, nxt)
            else:
                out = _outproj(m_lat, w_out, xs, mods, n_rows, final=final_g.reshape(1, D))
            continue
        if kind == 1:
            w_in = mlstm_w_in[j]
            n_main = 2 * M_QK + 2 * M_V
            gate_tiles = lambda gcols: jnp.pad(
                jnp.concatenate([gcols[..., 0:8], gcols[..., 16:24]], axis=-1),
                [(0, 0)] * (gcols.ndim - 1) + [(0, 128 - 2 * M_HEADS)])
            gate_layout = lambda gcols: jnp.concatenate(
                [gate_tiles(gcols), gate_tiles(gcols[..., M_HEADS:])], axis=-1)
            w_gates = gate_layout(w_in[:, n_main:n_main + 4 * M_HEADS]).astype(BF16)
            b_gates = gate_layout(mlstm_b_gate[j][None, :])
            qkv, oz, gates = _mlstm_proj(h, w_in[:, :n_main].astype(BF16),
                                         w_in[:, n_main + 4 * M_HEADS:].astype(BF16), w_gates, b_gates)
            hb = _mlstm_scan(qkv, gates)
            mbuf = _mlstm_scan(qkv, gates, oz, hb, mlstm_hn[j].reshape(1, M_V))
            w_out = (mlstm_w_out.astype(BF16), j)
        else:
            w_in = attn_w_in[j].astype(BF16)
            qn = attn_qn[j] * (A_HD ** -0.5)
            gain = jnp.concatenate([jnp.tile(qn, A_HEADS), jnp.tile(attn_kn[j], A_KV)]).reshape(1, A_QKW)
            q, k = _attn_qk(h, w_in, gain, _rope_tables())
            v, zg = _attn_vz(h, w_in)
            mbuf = _attn(q, k, v, zg)
            w_out = (attn_w_out.astype(BF16), j)
        assert not last
        xs, h = _outproj(mbuf, w_out, xs, mods, n_rows, nxt=nxt)

    return out.reshape(BATCH, SEQ, D)
```

```python
import functools
import math

import jax
import jax.numpy as jnp
from jax import lax
from jax.experimental import pallas as pl
from jax.experimental.pallas import tpu as pltpu

F32 = jnp.float32
BF16 = jnp.bfloat16

D = 2048
BATCH = 16
SEQ = 2048
CTX = 256
DEPTH = 4
EPS = 1e-6
N_LAT = BATCH * SEQ
N_CTX = BATCH * CTX
N_ALL = N_LAT + N_CTX
MOD_ROWS = 24
CTX_MOD_ROW = BATCH

F_GROUPS = 4
F_GW = D // F_GROUPS

M_HEADS = 8
M_DQK = 128
M_DV = 256
M_QK = M_HEADS * M_DQK
M_V = M_HEADS * M_DV
M_L = 256
M_GATES = 256
M_HALF = 2 * M_QK + M_V

A_HEADS = 16
A_KV = 4
A_HD = 128
A_Q = A_HEADS * A_HD
A_KVW = A_KV * A_HD
A_GW = A_Q // A_KV
A_TQ = 256
GRID_W = 64
ROPE_THETA = 10000.0

TM = 512
TP = 1024
TN = 512
MIB = 1024 * 1024


def _params(sem, vmem_mib):
    return pltpu.CompilerParams(dimension_semantics=sem, vmem_limit_bytes=vmem_mib * MIB)


def _resident(shape, index_map):
    return pl.BlockSpec(shape, index_map, pipeline_mode=pl.Buffered(1))


def _layer_weight(j):
    return _resident((None, D, D), lambda *ids: (j, 0, 0))


def _sigmoid(x):
    return 1.0 / (1.0 + jnp.exp(-x))


def _silu(x):
    return x * _sigmoid(x)


def _log_sigmoid(x):
    return jnp.minimum(x, 0.0) - jnp.log1p(jnp.exp(-jnp.abs(x)))


def _modnorm(x, g, scale, shift):
    ms = jnp.mean(x * x, axis=-1, keepdims=True)
    y = x * lax.rsqrt(ms + EPS) * g
    return y * (1.0 + scale) + shift


def _store_modnorm(x_ref, g_ref, sc_ref, sh_ref, h_scr):
    rows = x_ref.shape[0]
    step = min(rows, 256)
    for r0 in range(0, rows, step):
        h_scr[r0:r0 + step, :] = _modnorm(x_ref[r0:r0 + step, :], g_ref[...], sc_ref[...],
                                          sh_ref[...]).astype(BF16)


def _mod_row(r, tm):
    return jnp.where(r < N_LAT // tm, r // (SEQ // tm), CTX_MOD_ROW)


def _modvec_kernel(c_ref, w_ref, b_ref, o_ref):
    s = _silu(c_ref[...])
    w = w_ref[...]
    s_hi = s.astype(BF16)
    s_lo = (s - s_hi.astype(F32)).astype(BF16)
    w_hi = w.astype(BF16)
    w_lo = (w - w_hi.astype(F32)).astype(BF16)
    acc = jnp.dot(s_hi, w_hi, preferred_element_type=F32)
    acc = acc + jnp.dot(s_lo, w_hi, preferred_element_type=F32)
    acc = acc + jnp.dot(s_hi, w_lo, preferred_element_type=F32)
    o_ref[...] = acc + b_ref[...]


def _modvec(cc, ada_w, ada_b):
    tn = 2048
    return pl.pallas_call(
        _modvec_kernel,
        grid=(DEPTH, 3 * D // tn),
        in_specs=[
            pl.BlockSpec((MOD_ROWS, D), lambda i, j: (0, 0)),
            pl.BlockSpec((None, D, tn), lambda i, j: (i, 0, j)),
            pl.BlockSpec((None, 1, tn), lambda i, j: (i, 0, j)),
        ],
        out_specs=pl.BlockSpec((None, MOD_ROWS, tn), lambda i, j: (i, 0, j)),
        out_shape=jax.ShapeDtypeStruct((DEPTH, MOD_ROWS, 3 * D), F32),
        compiler_params=_params(("parallel", "parallel"), 48),
        name="modvec",
    )(cc, ada_w, ada_b.reshape(DEPTH, 1, 3 * D))


def _fnet_chan_kernel(x_ref, g_ref, sh_ref, sc_ref, cs_ref, a_ref, b_ref):
    h = _modnorm(x_ref[...], g_ref[...], sc_ref[...], sh_ref[...]).astype(BF16)
    for grp in range(F_GROUPS):
        sl = slice(grp * F_GW, (grp + 1) * F_GW)
        p = jnp.dot(h[:, sl], cs_ref[...], preferred_element_type=F32)
        a_ref[:, sl] = p[:, :F_GW].astype(BF16)
        b_ref[:, sl] = p[:, F_GW:].astype(BF16)


def _fnet_chan_ctx(xc, row_blk0, g, mods, cs_c):
    tm = 512
    out = jax.ShapeDtypeStruct((N_CTX, D), BF16)
    mod = lambda chunk: pl.BlockSpec((None, 1, D), lambda r: (CTX_MOD_ROW, 0, chunk))
    return pl.pallas_call(
        _fnet_chan_kernel,
        grid=(N_CTX // tm,),
        in_specs=[pl.BlockSpec((tm, D), lambda r: (row_blk0 + r, 0)),
                  pl.BlockSpec((1, D), lambda r: (0, 0)), mod(0), mod(1),
                  pl.BlockSpec((F_GW, 2 * F_GW), lambda r: (0, 0))],
        out_specs=[pl.BlockSpec((tm, D), lambda r: (r, 0))] * 2,
        out_shape=[out, out],
        compiler_params=_params(("parallel",), 48),
        name="fnet_chan_ctx",
    )(xc, g, mods, mods, cs_c)


F_R = 4
F_M = SEQ // F_R
F_TC = 256
F_SL = TN // 128


def _fnet_chan_dif_kernel(x0_ref, x1_ref, x2_ref, x3_ref, *rest, prenormed):
    if prenormed:
        cs_ref, p_ref, q_ref = rest
        hs = [x[...] for x in (x0_ref, x1_ref, x2_ref, x3_ref)]
    else:
        g_ref, sh_ref, sc_ref, cs_ref, p_ref, q_ref, h_ref = rest
        hs = [_modnorm(x[...], g_ref[...], sc_ref[...], sh_ref[...]).astype(BF16)
              for x in (x0_ref, x1_ref, x2_ref, x3_ref)]
        for qi, h in enumerate(hs):
            h_ref[qi] = h
    for grp in range(F_GROUPS):
        sl = slice(grp * F_GW, (grp + 1) * F_GW)
        ab = [jnp.dot(h[:, sl], cs_ref[...], preferred_element_type=F32) for h in hs]
        a = [t[:, :F_GW] for t in ab]
        b = [t[:, F_GW:] for t in ab]
        sa02, da02, sa13, da13 = a[0] + a[2], a[0] - a[2], a[1] + a[3], a[1] - a[3]
        sb02, db02, sb13, db13 = b[0] + b[2], b[0] - b[2], b[1] + b[3], b[1] - b[3]
        re = (sa02 + sa13, da02 - db13, sa02 - sa13, da02 + db13)
        im = (sb02 + sb13, db02 + da13, sb02 - sb13, db02 - da13)
        for r in range(F_R):
            p_ref[r, :, sl] = re[r].astype(BF16)
            q_ref[r, :, sl] = im[r].astype(BF16)


def _fnet_chan_dif(xl, norm, cs_c):
    nt = F_M // F_TC
    xspec = lambda q: pl.BlockSpec((F_TC, D), lambda b, i: (b * (SEQ // F_TC) + q * nt + i, 0))
    mod = lambda chunk: pl.BlockSpec((None, 1, D), lambda b, i: (b, 0, chunk))
    out = jax.ShapeDtypeStruct((BATCH, F_R, F_M, D), BF16)
    ospec = pl.BlockSpec((None, F_R, F_TC, D), lambda b, i: (b, 0, i, 0))
    out_specs, out_shape = [ospec, ospec], [out, out]
    norm_specs, norm_args = [], []
    if norm is not None:
        norm_specs = [pl.BlockSpec((1, D), lambda b, i: (0, 0)), mod(0), mod(1)]
        norm_args = [norm[0], norm[1], norm[1]]
        out_specs.append(pl.BlockSpec((None, F_R, None, F_TC, D), lambda b, i: (b, 0, i, 0, 0)))
        out_shape.append(jax.ShapeDtypeStruct((BATCH, F_R, nt, F_TC, D), BF16))
    res = pl.pallas_call(
        functools.partial(_fnet_chan_dif_kernel, prenormed=norm is None),
        grid=(BATCH, nt),
        in_specs=[xspec(0), xspec(1), xspec(2), xspec(3)] + norm_specs + [
            pl.BlockSpec((F_GW, 2 * F_GW), lambda b, i: (0, 0))],
        out_specs=out_specs,
        out_shape=out_shape,
        compiler_params=_params(("parallel", "parallel"), 56),
        name="fnet_chan_dif",
    )(xl, xl, xl, xl, *norm_args, cs_c)
    return (res[0], res[1], xl) if norm is None else (res[0], res[1], res[2].reshape(N_LAT, D))


def _fnet_mix_kernel(x_ref, g_ref, sh_ref, sc_ref, c_ref, s_ref, p_ref, q_ref, wg_ref, o_ref, h_scr):
    _store_modnorm(x_ref, g_ref, sc_ref, sh_ref, h_scr)
    for c in range(D // TN):
        sl = slice(c * TN, (c + 1) * TN)
        y = jnp.dot(c_ref[...], p_ref[:, sl], preferred_element_type=F32)
        y = y - jnp.dot(s_ref[...], q_ref[:, sl], preferred_element_type=F32)
        gate = jnp.dot(h_scr[...], wg_ref[:, sl], preferred_element_type=F32)
        o_ref[:, sl] = (y * _silu(gate)).astype(BF16)


def _fnet_mix_lat_kernel(h_ref, c_ref, s_ref, p_ref, q_ref, wg_ref, o_ref, y_scr):
    for r in range(F_R):
        y = jnp.dot(c_ref[r], p_ref[r], preferred_element_type=F32)
        y = y - jnp.dot(s_ref[r], q_ref[r], preferred_element_type=F32)
        for c in range(F_SL):
            y_scr[c, pl.ds(r, F_M, stride=F_R), :] = y[:, c * 128:(c + 1) * 128]
    for t in range(SEQ // F_M):
        rows = slice(t * F_M, (t + 1) * F_M)
        gate = jnp.dot(h_ref[rows, :], wg_ref[...], preferred_element_type=F32)
        for c in range(F_SL):
            cols = slice(c * 128, (c + 1) * 128)
            o_ref[rows, cols] = (y_scr[c, rows, :] * _silu(gate[:, cols])).astype(BF16)


def _fnet_mix_lat(h, cr, sr, p, q, wg):
    tab = _resident((F_R, F_M, F_M), lambda b, n: (0, 0, 0))
    pq = pl.BlockSpec((None, F_R, F_M, TN), lambda b, n: (b, 0, 0, n))
    return pl.pallas_call(
        _fnet_mix_lat_kernel,
        grid=(BATCH, D // TN),
        in_specs=[pl.BlockSpec((SEQ, D), lambda b, n: (b, 0)), tab, tab, pq, pq,
                  pl.BlockSpec((None, D, TN), lambda b, n: (wg[1], 0, n))],
        out_specs=pl.BlockSpec((SEQ, TN), lambda b, n: (b, n)),
        out_shape=jax.ShapeDtypeStruct((N_LAT, D), BF16),
        scratch_shapes=[pltpu.VMEM((F_SL, SEQ, 128), F32)],
        compiler_params=_params(("parallel", "arbitrary"), 56),
        name="fnet_mix_lat",
    )(h, cr, sr, p, q, wg[0])


def _fnet_mix_ctx(xc, row_blk0, g, mods, ct, st, a, b, wg):
    mod = lambda chunk: pl.BlockSpec((None, 1, D), lambda bi: (CTX_MOD_ROW, 0, chunk))
    tab = pl.BlockSpec((CTX, CTX), lambda bi: (0, 0))
    ab = pl.BlockSpec((CTX, D), lambda bi: (bi, 0))
    return pl.pallas_call(
        _fnet_mix_kernel,
        grid=(BATCH,),
        in_specs=[pl.BlockSpec((CTX, D), lambda bi: (row_blk0 + bi, 0)),
                  pl.BlockSpec((1, D), lambda bi: (0, 0)), mod(0), mod(1),
                  tab, tab, ab, ab, _layer_weight(wg[1])],
        out_specs=pl.BlockSpec((CTX, D), lambda bi: (bi, 0)),
        out_shape=jax.ShapeDtypeStruct((N_CTX, D), BF16),
        scratch_shapes=[pltpu.VMEM((CTX, D), BF16)],
        compiler_params=_params(("parallel",), 48),
        name="fnet_mix_ctx",
    )(xc, g, mods, mods, ct, st, a, b, wg[0])


def _residual_update(m_ref, w_ref, x_ref, gate_ref, o_ref, nxt):
    ss = 0.0
    for c in range(D // TN):
        sl = slice(c * TN, (c + 1) * TN)
        y = jnp.dot(m_ref[...], w_ref[:, sl], preferred_element_type=F32)
        o = x_ref[:, sl] + gate_ref[:, sl] * y
        o_ref[:, sl] = o
        if nxt is not None:
            ss = ss + jnp.sum(o * o, axis=-1, keepdims=True)
    if nxt is not None:
        gn_ref, shn_ref, scn_ref, h_ref = nxt
        rinv = lax.rsqrt(ss * (1.0 / D) + EPS)
        for c in range(D // TN):
            sl = slice(c * TN, (c + 1) * TN)
            gs = gn_ref[:, sl] * (1.0 + scn_ref[:, sl])
            h_ref[:, sl] = (o_ref[:, sl] * rinv * gs + shn_ref[:, sl]).astype(BF16)


def _outproj_kernel(m_ref, w_ref, x_ref, gate_ref, *rest, final):
    if final:
        fg_ref, o_ref = rest
        _residual_update(m_ref, w_ref, x_ref, gate_ref, o_ref, None)
        x = o_ref[...]
        ms = jnp.mean(x * x, axis=-1, keepdims=True)
        o_ref[...] = x * lax.rsqrt(ms + EPS) * fg_ref[...]
    else:
        gn_ref, shn_ref, scn_ref, o_ref, h_ref = rest
        _residual_update(m_ref, w_ref, x_ref, gate_ref, o_ref, (gn_ref, shn_ref, scn_ref, h_ref))


def _next_norm_specs():
    return [pl.BlockSpec((1, D), lambda r: (0, 0)),
            pl.BlockSpec((None, 1, D), lambda r: (_mod_row(r, TM), 0, 0)),
            pl.BlockSpec((None, 1, D), lambda r: (_mod_row(r, TM), 0, 1))]


def _outproj(mbuf, w, xs, mods, n_rows, final=None, nxt=None):
    nr = n_rows // TM
    tile = pl.BlockSpec((TM, D), lambda r: (r, 0))
    in_specs = [
        tile,
        _layer_weight(w[1]),
        tile,
        pl.BlockSpec((None, 1, D), lambda r: (_mod_row(r, TM), 0, 2)),
    ]
    args = [mbuf, w[0], xs, mods]
    if final is not None:
        in_specs.append(pl.BlockSpec((1, D), lambda r: (0, 0)))
        args.append(final)
        return pl.pallas_call(
            functools.partial(_outproj_kernel, final=True),
            grid=(n_rows // TM,),
            in_specs=in_specs,
            out_specs=tile,
            out_shape=jax.ShapeDtypeStruct((n_rows, D), F32),
            compiler_params=_params(("parallel",), 48),
            name="outproj_final",
        )(*args)
    g_next, mods_next = nxt
    return pl.pallas_call(
        functools.partial(_outproj_kernel, final=False),
        grid=(nr,),
        in_specs=in_specs + _next_norm_specs(),
        out_specs=[tile, tile],
        out_shape=[jax.ShapeDtypeStruct((N_ALL, D), F32), jax.ShapeDtypeStruct((N_ALL, D), BF16)],
        input_output_aliases={2: 0},
        compiler_params=_params(("parallel",), 48),
        name="outproj",
    )(*args, g_next, mods_next, mods_next)


def _outproj_join_kernel(ml_ref, mc_ref, w_ref, xl_ref, xc_ref, gate_ref,
                         gn_ref, shn_ref, scn_ref, o_ref, h_ref):
    is_lat = pl.program_id(0) < N_LAT // TM
    nxt = (gn_ref, shn_ref, scn_ref, h_ref)

    @pl.when(is_lat)
    def _():
        _residual_update(ml_ref, w_ref, xl_ref, gate_ref, o_ref, nxt)

    @pl.when(jnp.logical_not(is_lat))
    def _():
        _residual_update(mc_ref, w_ref, xc_ref, gate_ref, o_ref, nxt)


def _outproj_join(m_lat, m_ctx, w, x_lat, x_ctx, mods, nxt):
    nl = N_LAT // TM
    lat = pl.BlockSpec((TM, D), lambda r: (jnp.minimum(r, nl - 1), 0))
    cxt = pl.BlockSpec((TM, D), lambda r: (jnp.maximum(r - nl, 0), 0))
    tile = pl.BlockSpec((TM, D), lambda r: (r, 0))
    g_next, mods_next = nxt
    return pl.pallas_call(
        _outproj_join_kernel,
        grid=(N_ALL // TM,),
        in_specs=[lat, cxt, _layer_weight(w[1]), lat, cxt,
                  pl.BlockSpec((None, 1, D), lambda r: (_mod_row(r, TM), 0, 2))] + _next_norm_specs(),
        out_specs=[tile, tile],
        out_shape=[jax.ShapeDtypeStruct((N_ALL, D), F32), jax.ShapeDtypeStruct((N_ALL, D), BF16)],
        compiler_params=_params(("parallel",), 48),
        name="outproj_join",
    )(m_lat, m_ctx, w[0], x_lat, x_ctx, mods, g_next, mods_next, mods_next)


def _mlstm_qkv_kernel(h_ref, w_ref, wg_ref, bg_ref, out_ref, gt_ref):
    gt_ref[...] = jnp.dot(h_ref[...], wg_ref[...], preferred_element_type=F32) + bg_ref[...]
    for c in range(M_HALF // TN):
        sl = slice(c * TN, (c + 1) * TN)
        acc = jnp.dot(h_ref[...], w_ref[:, sl], preferred_element_type=F32)
        if c < M_QK // TN:
            acc = acc * (M_DQK ** -0.5)
        out_ref[:, sl] = acc.astype(BF16)


def _mlstm_oz_kernel(h_ref, wo_ref, wz_ref, out_ref):
    for c in range(M_V // TN):
        sl = slice(c * TN, (c + 1) * TN)
        o = jnp.dot(h_ref[...], wo_ref[:, sl], preferred_element_type=F32)
        out_ref[:, sl] = _sigmoid(o).astype(BF16)
    for c in range(M_V // TN):
        sl = slice(c * TN, (c + 1) * TN)
        z = jnp.dot(h_ref[...], wz_ref[:, sl], preferred_element_type=F32)
        out_ref[:, M_V + c * TN:M_V + (c + 1) * TN] = _silu(z).astype(BF16)


def _mlstm_proj(h, w_in, w_z, wg, bg):
    tile = pl.BlockSpec((TP, D), lambda r: (r, 0))
    wide = pl.BlockSpec((TP, M_HALF), lambda r: (r, 0))
    qkv, gates = pl.pallas_call(
        _mlstm_qkv_kernel,
        grid=(N_ALL // TP,),
        in_specs=[tile, _resident((D, M_HALF), lambda r: (0, 0)),
                  pl.BlockSpec((D, M_GATES), lambda r: (0, 0)), pl.BlockSpec((1, M_GATES), lambda r: (0, 0))],
        out_specs=[wide, pl.BlockSpec((TP, M_GATES), lambda r: (r, 0))],
        out_shape=[jax.ShapeDtypeStruct((N_ALL, M_HALF), BF16), jax.ShapeDtypeStruct((N_ALL, M_GATES), F32)],
        compiler_params=_params(("parallel",), 48),
        name="mlstm_qkv",
    )(h, w_in, wg, bg)
    oz = pl.pallas_call(
        _mlstm_oz_kernel,
        grid=(N_ALL // TP,),
        in_specs=[tile, _resident((D, M_V), lambda r: (0, M_HALF // M_V)), _resident((D, M_V), lambda r: (0, 0))],
        out_specs=wide,
        out_shape=jax.ShapeDtypeStruct((N_ALL, M_HALF), BF16),
        compiler_params=_params(("parallel",), 48),
        name="mlstm_oz",
    )(h, w_in, w_z)
    return qkv, oz, gates


def _split3(x):
    hi = x.astype(BF16)
    r1 = x - hi.astype(F32)
    mid = r1.astype(BF16)
    lo = (r1 - mid.astype(F32)).astype(BF16)
    return hi, mid, lo


def _cummax_rows(x, reverse):
    rows = x.shape[0]
    row = lax.broadcasted_iota(jnp.int32, x.shape, 0)
    sh = 1
    while sh < rows:
        if reverse:
            x = jnp.where(row < rows - sh, jnp.maximum(x, pltpu.roll(x, rows - sh, 0)), x)
        else:
            x = jnp.where(row >= sh, jnp.maximum(x, pltpu.roll(x, sh, 0)), x)
        sh *= 2
    return x


def _lanes(col, width):
    return jnp.broadcast_to(col, (col.shape[0], width))


def _mlstm_scan_kernel(q_ref, k_ref, v_ref, gt_ref, *rest, reverse):
    if reverse:
        o_ref, cn_scr, m_scr = rest
    else:
        hb_ref, so_ref, sz_ref, hn_ref, o_ref, cn_scr, m_scr = rest

    @pl.when(pl.program_id(1) == 0)
    def _():
        cn_scr[...] = jnp.zeros_like(cn_scr)
        m_scr[...] = jnp.zeros_like(m_scr)

    L = M_L
    lane0 = M_HEADS if reverse else 0
    ig = gt_ref[:, :128]
    ls = _log_sigmoid(gt_ref[:, 128:])
    row = lax.broadcasted_iota(jnp.int32, (L, L), 0)
    col = lax.broadcasted_iota(jnp.int32, (L, L), 1)
    order = (row <= col) if reverse else (row >= col)
    tri = jnp.where(order, 1.0, 0.0).astype(BF16)
    hi, mid, lo = _split3(ls)
    b = (jnp.dot(tri, hi, preferred_element_type=F32)
         + jnp.dot(tri, mid, preferred_element_type=F32)
         + jnp.dot(tri, lo, preferred_element_type=F32))
    end = 0 if reverse else L - 1

    m_prev = m_scr[...]
    r = ig - b
    inter = b + m_prev
    m_t = jnp.maximum(inter, b + _cummax_rows(r, reverse))
    a_all = jnp.exp(inter - m_t)
    u_all = b - m_t
    en_all = jnp.exp(-m_t)
    b_end = b[end:end + 1, :]
    gl = b_end - b + ig
    m_new = jnp.maximum(b_end + m_prev, jnp.max(gl, axis=0, keepdims=True))
    w_all = jnp.exp(gl - m_new)
    decay_all = jnp.exp(b_end + m_prev - m_new)
    r_t = r.T
    m_scr[...] = m_new

    ones_bf = jnp.ones((L, 128), BF16)
    mean_dv = jnp.full((M_DV, 128), 1.0 / M_DV, BF16)
    nt = (((1,), (1,)), ((), ()))
    tn = (((0,), (0,)), ((), ()))

    def stage_a(h):
        l = lane0 + h
        q = q_ref[:, h * M_DQK:(h + 1) * M_DQK]
        k = k_ref[:, h * M_DQK:(h + 1) * M_DQK]
        qk = lax.dot_general(q, k, nt, preferred_element_type=F32)
        qcn = jnp.dot(q, cn_scr[h].astype(BF16), preferred_element_type=F32)
        p = jnp.exp(jnp.where(order, u_all[:, l:l + 1] + r_t[l:l + 1, :], -jnp.inf))
        return qk, qcn, p

    def stage_b(h, qk, p):
        s = qk * p
        s_hi = s.astype(BF16)
        s_lo = (s - s_hi.astype(F32)).astype(BF16)
        vo = jnp.concatenate([v_ref[:, h * M_DV:(h + 1) * M_DV], ones_bf], axis=1)
        sv = jnp.dot(s_hi, vo, preferred_element_type=F32)
        return sv, jnp.dot(s_lo, ones_bf, preferred_element_type=F32)

    def stage_c(h, qcn, sv, rs_lo):
        l = lane0 + h
        a = _lanes(a_all[:, l:l + 1], 128)
        den = a * qcn[:, M_DV:] + (sv[:, M_DV:] + rs_lo)
        inv = 1.0 / jnp.maximum(jnp.abs(den), _lanes(en_all[:, l:l + 1], 128))
        cols = [slice(h * M_DV + c0, h * M_DV + c0 + 128) for c0 in range(0, M_DV, 128)]
        hid = [(a * qcn[:, c0:c0 + 128] + sv[:, c0:c0 + 128]) * inv for c0 in range(0, M_DV, 128)]
        if reverse:
            for sl, hv in zip(cols, hid):
                o_ref[:, sl] = hv
            return
        ys = [so_ref[:, sl].astype(F32) * (hv + hb_ref[:, sl]) for sl, hv in zip(cols, hid)]
        sq = jnp.concatenate([(y * y).astype(BF16) for y in ys], axis=1)
        ms = jnp.dot(sq, mean_dv, preferred_element_type=F32)
        scale = lax.rsqrt(ms + EPS)
        for sl, y in zip(cols, ys):
            o_ref[:, sl] = (y * scale * hn_ref[:, sl] * sz_ref[:, sl].astype(F32)).astype(BF16)

    def stage_d(h):
        l = lane0 + h
        k = k_ref[:, h * M_DQK:(h + 1) * M_DQK]
        kw = (k.astype(F32) * _lanes(w_all[:, l:l + 1], M_DQK)).astype(BF16)
        vo = jnp.concatenate([v_ref[:, h * M_DV:(h + 1) * M_DV], ones_bf], axis=1)
        upd = lax.dot_general(kw, vo, tn, preferred_element_type=F32)
        cn_scr[h] = decay_all[:, l:l + 1] * cn_scr[h] + upd

    sa = {}
    sb = {}
    for step in range(M_HEADS + 2):
        if step < M_HEADS:
            sa[step] = stage_a(step)
        if 1 <= step <= M_HEADS:
            h = step - 1
            sb[h] = stage_b(h, sa[h][0], sa[h][2])
        if step >= 2:
            h = step - 2
            stage_c(h, sa[h][1], *sb[h])
            stage_d(h)


def _mlstm_scan(qkv, gates, oz=None, hb=None, hn=None):
    reverse = hb is None
    nlc = SEQ // M_L
    ctx_blk0 = N_LAT // M_L

    def blk(b, i):
        lat = (nlc - i) if reverse else (i - 1)
        return jnp.where(i == 0, ctx_blk0 + b, b * nlc + lat)

    wide = lambda cb: pl.BlockSpec((M_L, M_V), lambda b, i: (blk(b, i), cb))
    in_specs = [
        pl.BlockSpec((M_L, M_QK), lambda b, i: (blk(b, i), 0)),
        pl.BlockSpec((M_L, M_QK), lambda b, i: (blk(b, i), 1)),
        wide(1),
        pl.BlockSpec((M_L, M_GATES), lambda b, i: (blk(b, i), 0)),
    ]
    args = [qkv, qkv, qkv, gates]
    if not reverse:
        in_specs += [wide(0), wide(0), wide(1), pl.BlockSpec((1, M_V), lambda b, i: (0, 0))]
        args += [hb, oz, oz, hn]
    return pl.pallas_call(
        functools.partial(_mlstm_scan_kernel, reverse=reverse),
        grid=(BATCH, nlc + 1),
        in_specs=in_specs,
        out_specs=wide(0),
        out_shape=jax.ShapeDtypeStruct((N_ALL, M_V), F32 if reverse else BF16),
        scratch_shapes=[
            pltpu.VMEM((M_HEADS, M_DQK, M_DV + 128), F32),
            pltpu.VMEM((1, 128), F32),
        ],
        compiler_params=_params(("parallel", "arbitrary"), 48),
        name="mlstm_scan_bwd" if reverse else "mlstm_scan_fwd",
    )(*args)


A_QKW = A_Q + A_KVW


def _attn_qk_kernel(h_ref, w_ref, gain_ref, cos_ref, sin_next_ref, sin_prev_ref, q_ref, k_ref):
    mean_mat = jnp.full((A_HD, A_HD), 1.0 / A_HD, BF16)
    cos = cos_ref[...]
    sin_next = sin_next_ref[...]
    sin_prev = sin_prev_ref[...]
    for c in range(A_QKW // TN):
        acc = jnp.dot(h_ref[...], w_ref[:, c * TN:(c + 1) * TN], preferred_element_type=F32)
        for hh in range(TN // A_HD):
            lo = c * TN + hh * A_HD
            a = acc[:, hh * A_HD:(hh + 1) * A_HD]
            ms = jnp.dot((a * a).astype(BF16), mean_mat, preferred_element_type=F32)
            a = a * lax.rsqrt(ms + EPS) * gain_ref[:, lo:lo + A_HD]
            a = (a * cos + pltpu.roll(a, A_HD - 1, 1) * sin_next + pltpu.roll(a, 1, 1) * sin_prev).astype(BF16)
            if lo < A_Q:
                q_ref[:, lo:lo + A_HD] = a
            else:
                k_ref[:, lo - A_Q:lo - A_Q + A_HD] = a


def _attn_qk(h, w, gain, rope):
    lat_tiles = N_LAT // TP
    rope_blk = lambda r: (jnp.where(r < lat_tiles, r % (SEQ // TP), SEQ // TP), 0)
    return pl.pallas_call(
        _attn_qk_kernel,
        grid=(N_ALL // TP,),
        in_specs=[
            pl.BlockSpec((TP, D), lambda r: (r, 0)),
            _resident((D, A_QKW), lambda r: (0, 0)),
            pl.BlockSpec((1, A_QKW), lambda r: (0, 0)),
            pl.BlockSpec((TP, A_HD), rope_blk),
            pl.BlockSpec((TP, A_HD), rope_blk),
            pl.BlockSpec((TP, A_HD), rope_blk),
        ],
        out_specs=[
            pl.BlockSpec((TP, A_Q), lambda r: (r, 0)),
            pl.BlockSpec((TP, A_KVW), lambda r: (r, 0)),
        ],
        out_shape=[
            jax.ShapeDtypeStruct((N_ALL, A_Q), BF16),
            jax.ShapeDtypeStruct((N_ALL, A_KVW), BF16),
        ],
        compiler_params=_params(("parallel",), 48),
        name="attn_qk",
    )(h, w, gain, *rope)


def _attn_vz_kernel(h_ref, w_ref, v_ref, zg_ref):
    v_ref[...] = jnp.dot(h_ref[...], w_ref[:, :A_KVW], preferred_element_type=F32).astype(BF16)
    for c in range(A_Q // TN):
        lo = A_KVW + c * TN
        z = jnp.dot(h_ref[...], w_ref[:, lo:lo + TN], preferred_element_type=F32)
        zg_ref[:, c * TN:(c + 1) * TN] = _silu(z).astype(BF16)


def _attn_vz(h, w):
    return pl.pallas_call(
        _attn_vz_kernel,
        grid=(N_ALL // TP,),
        in_specs=[
            pl.BlockSpec((TP, D), lambda r: (r, 0)),
            _resident((D, A_KVW + A_Q), lambda r: (0, 1)),
        ],
        out_specs=[
            pl.BlockSpec((TP, A_KVW), lambda r: (r, 0)),
            pl.BlockSpec((TP, A_Q), lambda r: (r, 0)),
        ],
        out_shape=[
            jax.ShapeDtypeStruct((N_ALL, A_KVW), BF16),
            jax.ShapeDtypeStruct((N_ALL, A_Q), BF16),
        ],
        compiler_params=_params(("parallel",), 48),
        name="attn_vz",
    )(h, w)


A_LAT_TILES = SEQ // A_TQ
A_GPS = A_KV
A_SW = A_GPS * A_GW
A_KW = A_GPS * A_HD
A_KEYS = CTX + SEQ


def _attn_kernel(q_ref, kl_ref, kc_ref, vl_ref, vc_ref, zg_ref, o_ref, k_scr, vo_scr):
    t = pl.program_id(2)
    nt = (((1,), (1,)), ((), ()))
    heads = A_SW // A_HD
    hpg = A_GW // A_HD

    @pl.when(t == 0)
    def _():
        for g in range(A_GPS):
            cols = slice(g * A_HD, (g + 1) * A_HD)
            k_scr[g, :CTX, :] = kc_ref[:, cols]
            k_scr[g, CTX:, :] = kl_ref[:, cols]
            vo_scr[g, :CTX, :A_HD] = vc_ref[:, cols]
            vo_scr[g, CTX:, :A_HD] = vl_ref[:, cols]
            vo_scr[g, :, A_HD:] = jnp.ones((A_KEYS, A_HD), BF16)

    def run(nkeys):
        def scores(hh):
            q = q_ref[:, hh * A_HD:(hh + 1) * A_HD]
            return lax.dot_general(q, k_scr[hh // hpg, :nkeys, :], nt, preferred_element_type=F32)

        def probs(s):
            return jnp.exp(s - jnp.max(s, axis=-1, keepdims=True)).astype(BF16)

        def output(hh, p):
            sl = slice(hh * A_HD, (hh + 1) * A_HD)
            acc = jnp.dot(p, vo_scr[hh // hpg, :nkeys, :], preferred_element_type=F32)
            o_ref[:, sl] = (acc[:, :A_HD] / acc[:, A_HD:] * zg_ref[:, sl].astype(F32)).astype(BF16)

        s, p = {}, {}
        for step in range(heads + 2):
            if step < heads:
                s[step] = scores(step)
            if 1 <= step <= heads:
                p[step - 1] = probs(s[step - 1])
            if step >= 2:
                output(step - 2, p[step - 2])

    @pl.when(t < A_LAT_TILES)
    def _():
        run(A_KEYS)

    @pl.when(t >= A_LAT_TILES)
    def _():
        run(CTX)


def _attn(q, k, v, zg):
    ctx_blk0 = N_LAT // A_TQ
    qrow = lambda b, g, t: jnp.where(t < A_LAT_TILES, b * A_LAT_TILES + t, ctx_blk0 + b)
    return pl.pallas_call(
        _attn_kernel,
        grid=(BATCH, A_KV // A_GPS, A_LAT_TILES + 1),
        in_specs=[
            pl.BlockSpec((A_TQ, A_SW), lambda b, g, t: (qrow(b, g, t), g)),
            pl.BlockSpec((SEQ, A_KW), lambda b, g, t: (b, g)),
            pl.BlockSpec((CTX, A_KW), lambda b, g, t: (N_LAT // CTX + b, g)),
            pl.BlockSpec((SEQ, A_KW), lambda b, g, t: (b, g)),
            pl.BlockSpec((CTX, A_KW), lambda b, g, t: (N_LAT // CTX + b, g)),
            pl.BlockSpec((A_TQ, A_SW), lambda b, g, t: (qrow(b, g, t), g)),
        ],
        out_specs=pl.BlockSpec((A_TQ, A_SW), lambda b, g, t: (qrow(b, g, t), g)),
        out_shape=jax.ShapeDtypeStruct((N_ALL, A_Q), BF16),
        scratch_shapes=[pltpu.VMEM((A_GPS, A_KEYS, A_HD), BF16), pltpu.VMEM((A_GPS, A_KEYS, 2 * A_HD), BF16)],
        compiler_params=_params(("parallel", "parallel", "arbitrary"), 48),
        name="attn",
    )(q, k, k, v, v, zg)


def _dft_tables(n, scale):
    idx = jnp.arange(n, dtype=jnp.int32)
    ang = ((idx[:, None] * idx[None, :]) % n).astype(F32) * (2.0 * math.pi / n)
    return (jnp.cos(ang) * scale).astype(BF16), (jnp.sin(ang) * scale).astype(BF16)


def _dif_tables():
    jt = jnp.arange(F_M, dtype=jnp.int32)
    r = jnp.arange(F_R, dtype=jnp.int32)
    k = ((F_R * jt[None, :, None] + r[:, None, None]) * jt[None, None, :]) % SEQ
    ang = k.astype(F32) * (2.0 * math.pi / SEQ)
    return (jnp.cos(ang) * SEQ ** -0.5).astype(BF16), (jnp.sin(ang) * SEQ ** -0.5).astype(BF16)


def _rope_tables():
    t = jnp.arange(SEQ)
    freqs = ROPE_THETA ** (-jnp.arange(0, A_HD // 2, 2, dtype=F32) / (A_HD // 2))
    ang = jnp.concatenate([(t // GRID_W).astype(F32)[:, None] * freqs,
                           (t % GRID_W).astype(F32)[:, None] * freqs], axis=-1)
    cos = jnp.repeat(jnp.cos(ang), 2, axis=-1)
    sin = jnp.repeat(jnp.sin(ang), 2, axis=-1)
    even = (jnp.arange(A_HD) % 2 == 0)[None, :]
    pad = lambda a, fill: jnp.concatenate([a, jnp.full((TP, A_HD), fill, F32)], axis=0)
    return pad(cos, 1.0), pad(jnp.where(even, -sin, 0.0), 0.0), pad(jnp.where(even, 0.0, sin), 0.0)


def kernel(x, c, ctx, c_ctx, ada_w, ada_b, norm_g, fnet_w_gate, fnet_w_out, mlstm_w_in, mlstm_b_gate,
           mlstm_hn, mlstm_w_out, attn_w_in, attn_qn, attn_kn, attn_w_out, final_g):
    cc = jnp.concatenate([c, c_ctx[None, :], jnp.zeros((MOD_ROWS - BATCH - 1, D), F32)], axis=0)
    mods_all = _modvec(cc, ada_w, ada_b).reshape(DEPTH, MOD_ROWS, 1, 3 * D)

    cc_c, sc_c = _dft_tables(F_GW, F_GW ** -0.5)
    cs_chan = jnp.concatenate([cc_c, sc_c], axis=1)
    cr_lat, sr_lat = _dif_tables()
    ct_ctx, st_ctx = _dft_tables(CTX, CTX ** -0.5)
    fnet_wg_bf = fnet_w_gate.astype(BF16)
    fnet_wo_bf = fnet_w_out.astype(BF16)

    xs = None
    h = None
    for i in range(DEPTH):
        kind, j = i % 3, i // 3
        last = i == DEPTH - 1
        mods = mods_all[i]
        g = norm_g[i].reshape(1, D)
        n_rows = N_LAT if last else N_ALL
        nxt = None if last else (norm_g[i + 1].reshape(1, D), mods_all[i + 1])
        if kind == 0:
            assert i == 0 or last
            wg = (fnet_wg_bf, j)
            w_out = (fnet_wo_bf, j)
            x_lat = x.reshape(N_LAT, D) if i == 0 else xs
            p, q, h_lat = _fnet_chan_dif(*((x_lat, (g, mods)) if i == 0 else (h, None)), cs_chan)
            m_lat = _fnet_mix_lat(h_lat, cr_lat, sr_lat, p, q, wg)
            if i == 0:
                x_ctx = ctx.reshape(N_CTX, D)
                a, b = _fnet_chan_ctx(x_ctx, 0, g, mods, cs_chan)
                m_ctx = _fnet_mix_ctx(x_ctx, 0, g, mods, ct_ctx, st_ctx, a, b, wg)
                xs, h = _outproj_join(m_lat, m_ctx, w_out, x_lat, x_ctx, mods, nxt)
            else:
                out = _outproj(m_lat, w_out, xs, mods, n_rows, final=final_g.reshape(1, D))
            continue
        if kind == 1:
            w_in = mlstm_w_in[j].astype(BF16)
            n_main = 2 * M_QK + 2 * M_V
            gate_tiles = lambda gcols: jnp.pad(
                jnp.concatenate([gcols[..., 0:8], gcols[..., 16:24]], axis=-1),
                [(0, 0)] * (gcols.ndim - 1) + [(0, 128 - 2 * M_HEADS)])
            gate_layout = lambda gcols: jnp.concatenate(
                [gate_tiles(gcols), gate_tiles(gcols[..., M_HEADS:])], axis=-1)
            w_gates = gate_layout(w_in[:, n_main:n_main + 4 * M_HEADS])
            b_gates = gate_layout(mlstm_b_gate[j][None, :])
            qkv, oz, gates = _mlstm_proj(h, w_in, w_in[:, n_main + 4 * M_HEADS:], w_gates, b_gates)
            hb = _mlstm_scan(qkv, gates)
            mbuf = _mlstm_scan(qkv, gates, oz, hb, mlstm_hn[j].reshape(1, M_V))
            w_out = (mlstm_w_out.astype(BF16), j)
        else:
            w_in = attn_w_in[j].astype(BF16)
            qn = attn_qn[j] * (A_HD ** -0.5)
            gain = jnp.concatenate([jnp.tile(qn, A_HEADS), jnp.tile(attn_kn[j], A_KV)]).reshape(1, A_QKW)
            q, k = _attn_qk(h, w_in, gain, _rope_tables())
            v, zg = _attn_vz(h, w_in)
            mbuf = _attn(q, k, v, zg)
            w_out = (attn_w_out.astype(BF16), j)
        assert not last
        xs, h = _outproj(mbuf, w_out, xs, mods, n_rows, nxt=nxt)

    return out.reshape(BATCH, SEQ, D)
```

```python
import functools
import math

import jax
import jax.numpy as jnp
from jax import lax
from jax.experimental import pallas as pl
from jax.experimental.pallas import tpu as pltpu

F32 = jnp.float32
BF16 = jnp.bfloat16

D = 2048
BATCH = 16
SEQ = 2048
CTX = 256
DEPTH = 4
EPS = 1e-6
N_LAT = BATCH * SEQ
N_CTX = BATCH * CTX
N_ALL = N_LAT + N_CTX
MOD_ROWS = 24
CTX_MOD_ROW = BATCH

F_GROUPS = 4
F_GW = D // F_GROUPS

M_HEADS = 8
M_DQK = 128
M_DV = 256
M_QK = M_HEADS * M_DQK
M_V = M_HEADS * M_DV
M_L = 256
M_GATES = 256
M_HALF = 2 * M_QK + M_V

A_HEADS = 16
A_KV = 4
A_HD = 128
A_Q = A_HEADS * A_HD
A_KVW = A_KV * A_HD
A_GW = A_Q // A_KV
A_TQ = 256
GRID_W = 64
ROPE_THETA = 10000.0

TM = 512
TP = 1024
TN = 512
MIB = 1024 * 1024
LANES = 128


def _params(sem, vmem_mib):
    return pltpu.CompilerParams(dimension_semantics=sem, vmem_limit_bytes=vmem_mib * MIB)


def _resident(shape, index_map):
    return pl.BlockSpec(shape, index_map, pipeline_mode=pl.Buffered(1))


def _layer_weight(j):
    return _resident((None, D, D), lambda *ids: (j, 0, 0))


def _sigmoid(x):
    return 1.0 / (1.0 + jnp.exp(-x))


def _silu(x):
    return x * _sigmoid(x)


def _log_sigmoid(x):
    return jnp.minimum(x, 0.0) - jnp.log1p(jnp.exp(-jnp.abs(x)))


def _modnorm(x, g, scale, shift):
    ms = jnp.mean(x * x, axis=-1, keepdims=True)
    y = x * lax.rsqrt(ms + EPS) * g
    return y * (1.0 + scale) + shift


def _store_modnorm(x_ref, g_ref, sc_ref, sh_ref, h_scr):
    rows = x_ref.shape[0]
    step = min(rows, 256)
    for r0 in range(0, rows, step):
        h_scr[r0:r0 + step, :] = _modnorm(x_ref[r0:r0 + step, :], g_ref[...], sc_ref[...],
                                          sh_ref[...]).astype(BF16)


def _mod_row(r, tm):
    return jnp.where(r < N_LAT // tm, r // (SEQ // tm), CTX_MOD_ROW)


def _modvec_kernel(c_ref, w_ref, b_ref, o_ref):
    s = _silu(c_ref[...])
    w = w_ref[...]
    s_hi = s.astype(BF16)
    s_lo = (s - s_hi.astype(F32)).astype(BF16)
    w_hi = w.astype(BF16)
    w_lo = (w - w_hi.astype(F32)).astype(BF16)
    acc = jnp.dot(s_hi, w_hi, preferred_element_type=F32)
    acc = acc + jnp.dot(s_lo, w_hi, preferred_element_type=F32)
    acc = acc + jnp.dot(s_hi, w_lo, preferred_element_type=F32)
    o_ref[...] = acc + b_ref[...]


def _modvec(cc, ada_w, ada_b):
    tn = 2048
    return pl.pallas_call(
        _modvec_kernel,
        grid=(DEPTH, 3 * D // tn),
        in_specs=[
            pl.BlockSpec((MOD_ROWS, D), lambda i, j: (0, 0)),
            pl.BlockSpec((None, D, tn), lambda i, j: (i, 0, j)),
            pl.BlockSpec((None, 1, tn), lambda i, j: (i, 0, j)),
        ],
        out_specs=pl.BlockSpec((None, MOD_ROWS, tn), lambda i, j: (i, 0, j)),
        out_shape=jax.ShapeDtypeStruct((DEPTH, MOD_ROWS, 3 * D), F32),
        compiler_params=_params(("parallel", "parallel"), 48),
        name="modvec",
    )(cc, ada_w, ada_b.reshape(DEPTH, 1, 3 * D))


def _fnet_chan_kernel(x_ref, g_ref, sh_ref, sc_ref, cs_ref, a_ref, b_ref):
    h = _modnorm(x_ref[...], g_ref[...], sc_ref[...], sh_ref[...]).astype(BF16)
    for grp in range(F_GROUPS):
        sl = slice(grp * F_GW, (grp + 1) * F_GW)
        p = jnp.dot(h[:, sl], cs_ref[...], preferred_element_type=F32)
        a_ref[:, sl] = p[:, :F_GW].astype(BF16)
        b_ref[:, sl] = p[:, F_GW:].astype(BF16)


def _fnet_chan_ctx(xc, row_blk0, g, mods, cs_c):
    tm = 512
    out = jax.ShapeDtypeStruct((N_CTX, D), BF16)
    mod = lambda chunk: pl.BlockSpec((None, 1, D), lambda r: (CTX_MOD_ROW, 0, chunk))
    return pl.pallas_call(
        _fnet_chan_kernel,
        grid=(N_CTX // tm,),
        in_specs=[pl.BlockSpec((tm, D), lambda r: (row_blk0 + r, 0)),
                  pl.BlockSpec((1, D), lambda r: (0, 0)), mod(0), mod(1),
                  pl.BlockSpec((F_GW, 2 * F_GW), lambda r: (0, 0))],
        out_specs=[pl.BlockSpec((tm, D), lambda r: (r, 0))] * 2,
        out_shape=[out, out],
        compiler_params=_params(("parallel",), 48),
        name="fnet_chan_ctx",
    )(xc, g, mods, mods, cs_c)


F_R = 4
F_M = SEQ // F_R
F_TC = 256
F_SL = TN // LANES


def _fnet_chan_dif_kernel(x0_ref, x1_ref, x2_ref, x3_ref, *rest, prenormed):
    if prenormed:
        cs_ref, p_ref, q_ref = rest
        hs = [x[...] for x in (x0_ref, x1_ref, x2_ref, x3_ref)]
    else:
        g_ref, sh_ref, sc_ref, cs_ref, p_ref, q_ref, h_ref = rest
        hs = [_modnorm(x[...], g_ref[...], sc_ref[...], sh_ref[...]).astype(BF16)
              for x in (x0_ref, x1_ref, x2_ref, x3_ref)]
        for qi, h in enumerate(hs):
            h_ref[qi] = h
    for grp in range(F_GROUPS):
        sl = slice(grp * F_GW, (grp + 1) * F_GW)
        ab = [jnp.dot(h[:, sl], cs_ref[...], preferred_element_type=F32) for h in hs]
        a = [t[:, :F_GW] for t in ab]
        b = [t[:, F_GW:] for t in ab]
        sa02, da02, sa13, da13 = a[0] + a[2], a[0] - a[2], a[1] + a[3], a[1] - a[3]
        sb02, db02, sb13, db13 = b[0] + b[2], b[0] - b[2], b[1] + b[3], b[1] - b[3]
        re = (sa02 + sa13, da02 - db13, sa02 - sa13, da02 + db13)
        im = (sb02 + sb13, db02 + da13, sb02 - sb13, db02 - da13)
        for r in range(F_R):
            p_ref[r, :, sl] = re[r].astype(BF16)
            q_ref[r, :, sl] = im[r].astype(BF16)


def _fnet_chan_dif(xl, norm, cs_c):
    nt = F_M // F_TC
    xspec = lambda q: pl.BlockSpec((F_TC, D), lambda b, i: (b * (SEQ // F_TC) + q * nt + i, 0))
    mod = lambda chunk: pl.BlockSpec((None, 1, D), lambda b, i: (b, 0, chunk))
    out = jax.ShapeDtypeStruct((BATCH, F_R, F_M, D), BF16)
    ospec = pl.BlockSpec((None, F_R, F_TC, D), lambda b, i: (b, 0, i, 0))
    out_specs, out_shape = [ospec, ospec], [out, out]
    norm_specs, norm_args = [], []
    if norm is not None:
        norm_specs = [pl.BlockSpec((1, D), lambda b, i: (0, 0)), mod(0), mod(1)]
        norm_args = [norm[0], norm[1], norm[1]]
        out_specs.append(pl.BlockSpec((None, F_R, None, F_TC, D), lambda b, i: (b, 0, i, 0, 0)))
        out_shape.append(jax.ShapeDtypeStruct((BATCH, F_R, nt, F_TC, D), BF16))
    res = pl.pallas_call(
        functools.partial(_fnet_chan_dif_kernel, prenormed=norm is None),
        grid=(BATCH, nt),
        in_specs=[xspec(0), xspec(1), xspec(2), xspec(3)] + norm_specs + [
            pl.BlockSpec((F_GW, 2 * F_GW), lambda b, i: (0, 0))],
        out_specs=out_specs,
        out_shape=out_shape,
        compiler_params=_params(("parallel", "parallel"), 56),
        name="fnet_chan_dif",
    )(xl, xl, xl, xl, *norm_args, cs_c)
    return (res[0], res[1], xl) if norm is None else (res[0], res[1], res[2].reshape(N_LAT, D))


def _fnet_mix_kernel(x_ref, g_ref, sh_ref, sc_ref, c_ref, s_ref, p_ref, q_ref, wg_ref, o_ref, h_scr):
    _store_modnorm(x_ref, g_ref, sc_ref, sh_ref, h_scr)
    for c in range(D // TN):
        sl = slice(c * TN, (c + 1) * TN)
        y = jnp.dot(c_ref[...], p_ref[:, sl], preferred_element_type=F32)
        y = y - jnp.dot(s_ref[...], q_ref[:, sl], preferred_element_type=F32)
        gate = jnp.dot(h_scr[...], wg_ref[:, sl], preferred_element_type=F32)
        o_ref[:, sl] = (y * _silu(gate)).astype(BF16)


def _fnet_mix_lat_kernel(h_ref, c_ref, s_ref, p_ref, q_ref, wg_ref, o_ref, y_scr):
    for r in range(F_R):
        y = jnp.dot(c_ref[r], p_ref[r], preferred_element_type=F32)
        y = y - jnp.dot(s_ref[r], q_ref[r], preferred_element_type=F32)
        for c in range(F_SL):
            y_scr[c, pl.ds(r, F_M, stride=F_R), :] = y[:, c * LANES:(c + 1) * LANES]
    for t in range(SEQ // F_M):
        rows = slice(t * F_M, (t + 1) * F_M)
        gate = jnp.dot(h_ref[rows, :], wg_ref[...], preferred_element_type=F32)
        for c in range(F_SL):
            cols = slice(c * LANES, (c + 1) * LANES)
            o_ref[rows, cols] = (y_scr[c, rows, :] * _silu(gate[:, cols])).astype(BF16)


def _fnet_mix_lat(h, cr, sr, p, q, wg):
    tab = _resident((F_R, F_M, F_M), lambda b, n: (0, 0, 0))
    pq = pl.BlockSpec((None, F_R, F_M, TN), lambda b, n: (b, 0, 0, n))
    return pl.pallas_call(
        _fnet_mix_lat_kernel,
        grid=(BATCH, D // TN),
        in_specs=[pl.BlockSpec((SEQ, D), lambda b, n: (b, 0)), tab, tab, pq, pq,
                  pl.BlockSpec((None, D, TN), lambda b, n: (wg[1], 0, n))],
        out_specs=pl.BlockSpec((SEQ, TN), lambda b, n: (b, n)),
        out_shape=jax.ShapeDtypeStruct((N_LAT, D), BF16),
        scratch_shapes=[pltpu.VMEM((F_SL, SEQ, LANES), F32)],
        compiler_params=_params(("parallel", "arbitrary"), 56),
        name="fnet_mix_lat",
    )(h, cr, sr, p, q, wg[0])


def _fnet_mix_ctx(xc, row_blk0, g, mods, ct, st, a, b, wg):
    mod = lambda chunk: pl.BlockSpec((None, 1, D), lambda bi: (CTX_MOD_ROW, 0, chunk))
    tab = pl.BlockSpec((CTX, CTX), lambda bi: (0, 0))
    ab = pl.BlockSpec((CTX, D), lambda bi: (bi, 0))
    return pl.pallas_call(
        _fnet_mix_kernel,
        grid=(BATCH,),
        in_specs=[pl.BlockSpec((CTX, D), lambda bi: (row_blk0 + bi, 0)),
                  pl.BlockSpec((1, D), lambda bi: (0, 0)), mod(0), mod(1),
                  tab, tab, ab, ab, _layer_weight(wg[1])],
        out_specs=pl.BlockSpec((CTX, D), lambda bi: (bi, 0)),
        out_shape=jax.ShapeDtypeStruct((N_CTX, D), BF16),
        scratch_shapes=[pltpu.VMEM((CTX, D), BF16)],
        compiler_params=_params(("parallel",), 48),
        name="fnet_mix_ctx",
    )(xc, g, mods, mods, ct, st, a, b, wg[0])


def _residual_update(m_ref, w_ref, x_ref, gate_ref, o_ref, nxt):
    ss = 0.0
    for c in range(D // TN):
        sl = slice(c * TN, (c + 1) * TN)
        y = jnp.dot(m_ref[...], w_ref[:, sl], preferred_element_type=F32)
        o = x_ref[:, sl] + gate_ref[:, sl] * y
        o_ref[:, sl] = o
        if nxt is not None:
            ss = ss + jnp.sum(o * o, axis=-1, keepdims=True)
    if nxt is not None:
        gn_ref, shn_ref, scn_ref, h_ref = nxt
        rinv = lax.rsqrt(ss * (1.0 / D) + EPS)
        for c in range(D // TN):
            sl = slice(c * TN, (c + 1) * TN)
            gs = gn_ref[:, sl] * (1.0 + scn_ref[:, sl])
            h_ref[:, sl] = (o_ref[:, sl] * rinv * gs + shn_ref[:, sl]).astype(BF16)


def _outproj_kernel(m_ref, w_ref, x_ref, gate_ref, *rest, final):
    if final:
        fg_ref, o_ref = rest
        _residual_update(m_ref, w_ref, x_ref, gate_ref, o_ref, None)
        x = o_ref[...]
        ms = jnp.mean(x * x, axis=-1, keepdims=True)
        o_ref[...] = x * lax.rsqrt(ms + EPS) * fg_ref[...]
    else:
        gn_ref, shn_ref, scn_ref, o_ref, h_ref = rest
        _residual_update(m_ref, w_ref, x_ref, gate_ref, o_ref, (gn_ref, shn_ref, scn_ref, h_ref))


def _next_norm_specs():
    return [pl.BlockSpec((1, D), lambda r: (0, 0)),
            pl.BlockSpec((None, 1, D), lambda r: (_mod_row(r, TM), 0, 0)),
            pl.BlockSpec((None, 1, D), lambda r: (_mod_row(r, TM), 0, 1))]


def _outproj(mbuf, w, xs, mods, n_rows, final=None, nxt=None):
    nr = n_rows // TM
    tile = pl.BlockSpec((TM, D), lambda r: (r, 0))
    in_specs = [
        tile,
        _layer_weight(w[1]),
        tile,
        pl.BlockSpec((None, 1, D), lambda r: (_mod_row(r, TM), 0, 2)),
    ]
    args = [mbuf, w[0], xs, mods]
    if final is not None:
        in_specs.append(pl.BlockSpec((1, D), lambda r: (0, 0)))
        args.append(final)
        return pl.pallas_call(
            functools.partial(_outproj_kernel, final=True),
            grid=(n_rows // TM,),
            in_specs=in_specs,
            out_specs=tile,
            out_shape=jax.ShapeDtypeStruct((n_rows, D), F32),
            compiler_params=_params(("parallel",), 48),
            name="outproj_final",
        )(*args)
    g_next, mods_next = nxt
    return pl.pallas_call(
        functools.partial(_outproj_kernel, final=False),
        grid=(nr,),
        in_specs=in_specs + _next_norm_specs(),
        out_specs=[tile, tile],
        out_shape=[jax.ShapeDtypeStruct((N_ALL, D), F32), jax.ShapeDtypeStruct((N_ALL, D), BF16)],
        input_output_aliases={2: 0},
        compiler_params=_params(("parallel",), 48),
        name="outproj",
    )(*args, g_next, mods_next, mods_next)


def _outproj_join_kernel(ml_ref, mc_ref, w_ref, xl_ref, xc_ref, gate_ref,
                         gn_ref, shn_ref, scn_ref, o_ref, h_ref):
    is_lat = pl.program_id(0) < N_LAT // TM
    nxt = (gn_ref, shn_ref, scn_ref, h_ref)

    @pl.when(is_lat)
    def _():
        _residual_update(ml_ref, w_ref, xl_ref, gate_ref, o_ref, nxt)

    @pl.when(jnp.logical_not(is_lat))
    def _():
        _residual_update(mc_ref, w_ref, xc_ref, gate_ref, o_ref, nxt)


def _outproj_join(m_lat, m_ctx, w, x_lat, x_ctx, mods, nxt):
    nl = N_LAT // TM
    lat = pl.BlockSpec((TM, D), lambda r: (jnp.minimum(r, nl - 1), 0))
    cxt = pl.BlockSpec((TM, D), lambda r: (jnp.maximum(r - nl, 0), 0))
    tile = pl.BlockSpec((TM, D), lambda r: (r, 0))
    g_next, mods_next = nxt
    return pl.pallas_call(
        _outproj_join_kernel,
        grid=(N_ALL // TM,),
        in_specs=[lat, cxt, _layer_weight(w[1]), lat, cxt,
                  pl.BlockSpec((None, 1, D), lambda r: (_mod_row(r, TM), 0, 2))] + _next_norm_specs(),
        out_specs=[tile, tile],
        out_shape=[jax.ShapeDtypeStruct((N_ALL, D), F32), jax.ShapeDtypeStruct((N_ALL, D), BF16)],
        compiler_params=_params(("parallel",), 48),
        name="outproj_join",
    )(m_lat, m_ctx, w[0], x_lat, x_ctx, mods, g_next, mods_next, mods_next)


def _mlstm_qkv_kernel(h_ref, w_ref, wg_ref, bg_ref, out_ref, gt_ref):
    gt_ref[...] = jnp.dot(h_ref[...], wg_ref[...], preferred_element_type=F32) + bg_ref[...]
    for c in range(M_HALF // TN):
        sl = slice(c * TN, (c + 1) * TN)
        acc = jnp.dot(h_ref[...], w_ref[:, sl], preferred_element_type=F32)
        if c < M_QK // TN:
            acc = acc * (M_DQK ** -0.5)
        out_ref[:, sl] = acc.astype(BF16)


def _mlstm_oz_kernel(h_ref, wo_ref, wz_ref, out_ref):
    for c in range(M_V // TN):
        sl = slice(c * TN, (c + 1) * TN)
        o = jnp.dot(h_ref[...], wo_ref[:, sl], preferred_element_type=F32)
        out_ref[:, sl] = _sigmoid(o).astype(BF16)
    for c in range(M_V // TN):
        sl = slice(c * TN, (c + 1) * TN)
        z = jnp.dot(h_ref[...], wz_ref[:, sl], preferred_element_type=F32)
        out_ref[:, M_V + c * TN:M_V + (c + 1) * TN] = _silu(z).astype(BF16)


def _mlstm_proj(h, w_in, w_z, wg, bg):
    tile = pl.BlockSpec((TP, D), lambda r: (r, 0))
    wide = pl.BlockSpec((TP, M_HALF), lambda r: (r, 0))
    qkv, gates = pl.pallas_call(
        _mlstm_qkv_kernel,
        grid=(N_ALL // TP,),
        in_specs=[tile, _resident((D, M_HALF), lambda r: (0, 0)),
                  pl.BlockSpec((D, M_GATES), lambda r: (0, 0)), pl.BlockSpec((1, M_GATES), lambda r: (0, 0))],
        out_specs=[wide, pl.BlockSpec((TP, M_GATES), lambda r: (r, 0))],
        out_shape=[jax.ShapeDtypeStruct((N_ALL, M_HALF), BF16), jax.ShapeDtypeStruct((N_ALL, M_GATES), F32)],
        compiler_params=_params(("parallel",), 48),
        name="mlstm_qkv",
    )(h, w_in, wg, bg)
    oz = pl.pallas_call(
        _mlstm_oz_kernel,
        grid=(N_ALL // TP,),
        in_specs=[tile, _resident((D, M_V), lambda r: (0, M_HALF // M_V)), _resident((D, M_V), lambda r: (0, 0))],
        out_specs=wide,
        out_shape=jax.ShapeDtypeStruct((N_ALL, M_HALF), BF16),
        compiler_params=_params(("parallel",), 48),
        name="mlstm_oz",
    )(h, w_in, w_z)
    return qkv, oz, gates


def _split3(x):
    hi = x.astype(BF16)
    r1 = x - hi.astype(F32)
    mid = r1.astype(BF16)
    lo = (r1 - mid.astype(F32)).astype(BF16)
    return hi, mid, lo


def _cummax_rows(x, reverse):
    rows = x.shape[0]
    row = lax.broadcasted_iota(jnp.int32, x.shape, 0)
    sh = 1
    while sh < rows:
        if reverse:
            x = jnp.where(row < rows - sh, jnp.maximum(x, pltpu.roll(x, rows - sh, 0)), x)
        else:
            x = jnp.where(row >= sh, jnp.maximum(x, pltpu.roll(x, sh, 0)), x)
        sh *= 2
    return x


def _lanes(col, width):
    return jnp.broadcast_to(col, (col.shape[0], width))


def _mlstm_scan_kernel(q_ref, k_ref, v_ref, gt_ref, *rest, reverse):
    if reverse:
        o_ref, cn_scr, m_scr = rest
    else:
        hb_ref, so_ref, sz_ref, hn_ref, o_ref, cn_scr, m_scr = rest

    @pl.when(pl.program_id(1) == 0)
    def _():
        cn_scr[...] = jnp.zeros_like(cn_scr)
        m_scr[...] = jnp.zeros_like(m_scr)

    L = M_L
    lane0 = M_HEADS if reverse else 0
    ig = gt_ref[:, :LANES]
    ls = _log_sigmoid(gt_ref[:, LANES:])
    row = lax.broadcasted_iota(jnp.int32, (L, L), 0)
    col = lax.broadcasted_iota(jnp.int32, (L, L), 1)
    order = (row <= col) if reverse else (row >= col)
    tri = jnp.where(order, 1.0, 0.0).astype(BF16)
    hi, mid, lo = _split3(ls)
    b = (jnp.dot(tri, hi, preferred_element_type=F32)
         + jnp.dot(tri, mid, preferred_element_type=F32)
         + jnp.dot(tri, lo, preferred_element_type=F32))
    end = 0 if reverse else L - 1

    m_prev = m_scr[...]
    r = ig - b
    inter = b + m_prev
    m_t = jnp.maximum(inter, b + _cummax_rows(r, reverse))
    a_all = jnp.exp(inter - m_t)
    u_all = b - m_t
    en_all = jnp.exp(-m_t)
    b_end = b[end:end + 1, :]
    gl = b_end - b + ig
    m_new = jnp.maximum(b_end + m_prev, jnp.max(gl, axis=0, keepdims=True))
    w_all = jnp.exp(gl - m_new)
    decay_all = jnp.exp(b_end + m_prev - m_new)
    r_t = r.T
    m_scr[...] = m_new

    ones_bf = jnp.ones((L, LANES), BF16)
    mean_dv = jnp.full((M_DV, LANES), 1.0 / M_DV, BF16)
    nt = (((1,), (1,)), ((), ()))
    tn = (((0,), (0,)), ((), ()))

    def stage_a(h):
        l = lane0 + h
        q = q_ref[:, h * M_DQK:(h + 1) * M_DQK]
        k = k_ref[:, h * M_DQK:(h + 1) * M_DQK]
        qk = lax.dot_general(q, k, nt, preferred_element_type=F32)
        qcn = jnp.dot(q, cn_scr[h].astype(BF16), preferred_element_type=F32)
        p = jnp.exp(jnp.where(order, u_all[:, l:l + 1] + r_t[l:l + 1, :], -jnp.inf))
        return qk, qcn, p

    def stage_b(h, qk, p):
        s = qk * p
        s_hi = s.astype(BF16)
        s_lo = (s - s_hi.astype(F32)).astype(BF16)
        vo = jnp.concatenate([v_ref[:, h * M_DV:(h + 1) * M_DV], ones_bf], axis=1)
        sv = jnp.dot(s_hi, vo, preferred_element_type=F32)
        return sv, jnp.dot(s_lo, ones_bf, preferred_element_type=F32)

    def stage_c(h, qcn, sv, rs_lo):
        l = lane0 + h
        a = _lanes(a_all[:, l:l + 1], LANES)
        den = a * qcn[:, M_DV:] + (sv[:, M_DV:] + rs_lo)
        inv = 1.0 / jnp.maximum(jnp.abs(den), _lanes(en_all[:, l:l + 1], LANES))
        cols = [slice(h * M_DV + c0, h * M_DV + c0 + LANES) for c0 in range(0, M_DV, LANES)]
        hid = [(a * qcn[:, c0:c0 + LANES] + sv[:, c0:c0 + LANES]) * inv for c0 in range(0, M_DV, LANES)]
        if reverse:
            for sl, hv in zip(cols, hid):
                o_ref[:, sl] = hv
            return
        ys = [so_ref[:, sl].astype(F32) * (hv + hb_ref[:, sl]) for sl, hv in zip(cols, hid)]
        sq = jnp.concatenate([(y * y).astype(BF16) for y in ys], axis=1)
        ms = jnp.dot(sq, mean_dv, preferred_element_type=F32)
        scale = lax.rsqrt(ms + EPS)
        for sl, y in zip(cols, ys):
            o_ref[:, sl] = (y * scale * hn_ref[:, sl] * sz_ref[:, sl].astype(F32)).astype(BF16)

    def stage_d(h):
        l = lane0 + h
        k = k_ref[:, h * M_DQK:(h + 1) * M_DQK]
        kw = (k.astype(F32) * _lanes(w_all[:, l:l + 1], M_DQK)).astype(BF16)
        vo = jnp.concatenate([v_ref[:, h * M_DV:(h + 1) * M_DV], ones_bf], axis=1)
        upd = lax.dot_general(kw, vo, tn, preferred_element_type=F32)
        cn_scr[h] = decay_all[:, l:l + 1] * cn_scr[h] + upd

    sa = {}
    sb = {}
    for step in range(M_HEADS + 2):
        if step < M_HEADS:
            sa[step] = stage_a(step)
        if 1 <= step <= M_HEADS:
            h = step - 1
            sb[h] = stage_b(h, sa[h][0], sa[h][2])
        if step >= 2:
            h = step - 2
            stage_c(h, sa[h][1], *sb[h])
            stage_d(h)


def _mlstm_scan(qkv, gates, oz=None, hb=None, hn=None):
    reverse = hb is None
    nlc = SEQ // M_L
    ncc = CTX // M_L
    ctx_blk0 = N_LAT // M_L

    def blk(b, i):
        lat = (nlc + ncc - 1 - i) if reverse else (i - ncc)
        cxt = (ncc - 1 - i) if reverse else i
        return jnp.where(i < ncc, ctx_blk0 + b * ncc + cxt, b * nlc + lat)

    wide = lambda cb: pl.BlockSpec((M_L, M_V), lambda b, i: (blk(b, i), cb))
    in_specs = [
        pl.BlockSpec((M_L, M_QK), lambda b, i: (blk(b, i), 0)),
        pl.BlockSpec((M_L, M_QK), lambda b, i: (blk(b, i), 1)),
        wide(1),
        pl.BlockSpec((M_L, M_GATES), lambda b, i: (blk(b, i), 0)),
    ]
    args = [qkv, qkv, qkv, gates]
    if not reverse:
        in_specs += [wide(0), wide(0), wide(1), pl.BlockSpec((1, M_V), lambda b, i: (0, 0))]
        args += [hb, oz, oz, hn]
    return pl.pallas_call(
        functools.partial(_mlstm_scan_kernel, reverse=reverse),
        grid=(BATCH, nlc + ncc),
        in_specs=in_specs,
        out_specs=wide(0),
        out_shape=jax.ShapeDtypeStruct((N_ALL, M_V), F32 if reverse else BF16),
        scratch_shapes=[
            pltpu.VMEM((M_HEADS, M_DQK, M_DV + LANES), F32),
            pltpu.VMEM((1, LANES), F32),
        ],
        compiler_params=_params(("parallel", "arbitrary"), 48),
        name="mlstm_scan_bwd" if reverse else "mlstm_scan_fwd",
    )(*args)


A_QKW = A_Q + A_KVW


def _attn_qk_kernel(h_ref, w_ref, gain_ref, cos_ref, sin_next_ref, sin_prev_ref, q_ref, k_ref):
    mean_mat = jnp.full((A_HD, A_HD), 1.0 / A_HD, BF16)
    cos = cos_ref[...]
    sin_next = sin_next_ref[...]
    sin_prev = sin_prev_ref[...]
    for c in range(A_QKW // TN):
        acc = jnp.dot(h_ref[...], w_ref[:, c * TN:(c + 1) * TN], preferred_element_type=F32)
        for hh in range(TN // A_HD):
            lo = c * TN + hh * A_HD
            a = acc[:, hh * A_HD:(hh + 1) * A_HD]
            ms = jnp.dot((a * a).astype(BF16), mean_mat, preferred_element_type=F32)
            a = a * lax.rsqrt(ms + EPS) * gain_ref[:, lo:lo + A_HD]
            a = (a * cos + pltpu.roll(a, A_HD - 1, 1) * sin_next + pltpu.roll(a, 1, 1) * sin_prev).astype(BF16)
            if lo < A_Q:
                q_ref[:, lo:lo + A_HD] = a
            else:
                k_ref[:, lo - A_Q:lo - A_Q + A_HD] = a


def _attn_qk(h, w, gain, rope):
    lat_tiles = N_LAT // TP
    rope_blk = lambda r: (jnp.where(r < lat_tiles, r % (SEQ // TP), SEQ // TP), 0)
    return pl.pallas_call(
        _attn_qk_kernel,
        grid=(N_ALL // TP,),
        in_specs=[
            pl.BlockSpec((TP, D), lambda r: (r, 0)),
            _resident((D, A_QKW), lambda r: (0, 0)),
            pl.BlockSpec((1, A_QKW), lambda r: (0, 0)),
            pl.BlockSpec((TP, A_HD), rope_blk),
            pl.BlockSpec((TP, A_HD), rope_blk),
            pl.BlockSpec((TP, A_HD), rope_blk),
        ],
        out_specs=[
            pl.BlockSpec((TP, A_Q), lambda r: (r, 0)),
            pl.BlockSpec((TP, A_KVW), lambda r: (r, 0)),
        ],
        out_shape=[
            jax.ShapeDtypeStruct((N_ALL, A_Q), BF16),
            jax.ShapeDtypeStruct((N_ALL, A_KVW), BF16),
        ],
        compiler_params=_params(("parallel",), 48),
        name="attn_qk",
    )(h, w, gain, *rope)


def _attn_vz_kernel(h_ref, w_ref, v_ref, zg_ref):
    v_ref[...] = jnp.dot(h_ref[...], w_ref[:, :A_KVW], preferred_element_type=F32).astype(BF16)
    for c in range(A_Q // TN):
        lo = A_KVW + c * TN
        z = jnp.dot(h_ref[...], w_ref[:, lo:lo + TN], preferred_element_type=F32)
        zg_ref[:, c * TN:(c + 1) * TN] = _silu(z).astype(BF16)


def _attn_vz(h, w):
    return pl.pallas_call(
        _attn_vz_kernel,
        grid=(N_ALL // TP,),
        in_specs=[
            pl.BlockSpec((TP, D), lambda r: (r, 0)),
            _resident((D, A_KVW + A_Q), lambda r: (0, 1)),
        ],
        out_specs=[
            pl.BlockSpec((TP, A_KVW), lambda r: (r, 0)),
            pl.BlockSpec((TP, A_Q), lambda r: (r, 0)),
        ],
        out_shape=[
            jax.ShapeDtypeStruct((N_ALL, A_KVW), BF16),
            jax.ShapeDtypeStruct((N_ALL, A_Q), BF16),
        ],
        compiler_params=_params(("parallel",), 48),
        name="attn_vz",
    )(h, w)


A_LAT_TILES = SEQ // A_TQ
A_GPS = A_KV
A_SW = A_GPS * A_GW
A_KW = A_GPS * A_HD
A_KEYS = CTX + SEQ


def _attn_kernel(q_ref, kl_ref, kc_ref, vl_ref, vc_ref, zg_ref, o_ref, k_scr, vo_scr):
    t = pl.program_id(2)
    nt = (((1,), (1,)), ((), ()))
    heads = A_SW // A_HD
    hpg = A_GW // A_HD

    @pl.when(t == 0)
    def _():
        for g in range(A_GPS):
            cols = slice(g * A_HD, (g + 1) * A_HD)
            k_scr[g, :CTX, :] = kc_ref[:, cols]
            k_scr[g, CTX:, :] = kl_ref[:, cols]
            vo_scr[g, :CTX, :A_HD] = vc_ref[:, cols]
            vo_scr[g, CTX:, :A_HD] = vl_ref[:, cols]
            vo_scr[g, :, A_HD:] = jnp.ones((A_KEYS, A_HD), BF16)

    def run(nkeys):
        def scores(hh):
            q = q_ref[:, hh * A_HD:(hh + 1) * A_HD]
            return lax.dot_general(q, k_scr[hh // hpg, :nkeys, :], nt, preferred_element_type=F32)

        def probs(s):
            return jnp.exp(s - jnp.max(s, axis=-1, keepdims=True)).astype(BF16)

        def output(hh, p):
            sl = slice(hh * A_HD, (hh + 1) * A_HD)
            acc = jnp.dot(p, vo_scr[hh // hpg, :nkeys, :], preferred_element_type=F32)
            o_ref[:, sl] = (acc[:, :A_HD] / acc[:, A_HD:] * zg_ref[:, sl].astype(F32)).astype(BF16)

        s, p = {}, {}
        for step in range(heads + 2):
            if step < heads:
                s[step] = scores(step)
            if 1 <= step <= heads:
                p[step - 1] = probs(s[step - 1])
            if step >= 2:
                output(step - 2, p[step - 2])

    @pl.when(t < A_LAT_TILES)
    def _():
        run(A_KEYS)

    @pl.when(t >= A_LAT_TILES)
    def _():
        run(CTX)


def _attn(q, k, v, zg):
    ctx_blk0 = N_LAT // A_TQ
    qrow = lambda b, g, t: jnp.where(t < A_LAT_TILES, b * A_LAT_TILES + t, ctx_blk0 + b)
    return pl.pallas_call(
        _attn_kernel,
        grid=(BATCH, A_KV // A_GPS, A_LAT_TILES + 1),
        in_specs=[
            pl.BlockSpec((A_TQ, A_SW), lambda b, g, t: (qrow(b, g, t), g)),
            pl.BlockSpec((SEQ, A_KW), lambda b, g, t: (b, g)),
            pl.BlockSpec((CTX, A_KW), lambda b, g, t: (N_LAT // CTX + b, g)),
            pl.BlockSpec((SEQ, A_KW), lambda b, g, t: (b, g)),
            pl.BlockSpec((CTX, A_KW), lambda b, g, t: (N_LAT // CTX + b, g)),
            pl.BlockSpec((A_TQ, A_SW), lambda b, g, t: (qrow(b, g, t), g)),
        ],
        out_specs=pl.BlockSpec((A_TQ, A_SW), lambda b, g, t: (qrow(b, g, t), g)),
        out_shape=jax.ShapeDtypeStruct((N_ALL, A_Q), BF16),
        scratch_shapes=[pltpu.VMEM((A_GPS, A_KEYS, A_HD), BF16), pltpu.VMEM((A_GPS, A_KEYS, 2 * A_HD), BF16)],
        compiler_params=_params(("parallel", "parallel", "arbitrary"), 48),
        name="attn",
    )(q, k, k, v, v, zg)


def _dft_tables(n, scale):
    idx = jnp.arange(n, dtype=jnp.int32)
    ang = ((idx[:, None] * idx[None, :]) % n).astype(F32) * (2.0 * math.pi / n)
    return (jnp.cos(ang) * scale).astype(BF16), (jnp.sin(ang) * scale).astype(BF16)


def _dif_tables():
    jt = jnp.arange(F_M, dtype=jnp.int32)
    r = jnp.arange(F_R, dtype=jnp.int32)
    k = ((F_R * jt[None, :, None] + r[:, None, None]) * jt[None, None, :]) % SEQ
    ang = k.astype(F32) * (2.0 * math.pi / SEQ)
    return (jnp.cos(ang) * SEQ ** -0.5).astype(BF16), (jnp.sin(ang) * SEQ ** -0.5).astype(BF16)


def _rope_tables():
    t = jnp.arange(SEQ)
    freqs = ROPE_THETA ** (-jnp.arange(0, A_HD // 2, 2, dtype=F32) / (A_HD // 2))
    ang = jnp.concatenate([(t // GRID_W).astype(F32)[:, None] * freqs,
                           (t % GRID_W).astype(F32)[:, None] * freqs], axis=-1)
    cos = jnp.repeat(jnp.cos(ang), 2, axis=-1)
    sin = jnp.repeat(jnp.sin(ang), 2, axis=-1)
    even = (jnp.arange(A_HD) % 2 == 0)[None, :]
    pad = lambda a, fill: jnp.concatenate([a, jnp.full((TP, A_HD), fill, F32)], axis=0)
    return pad(cos, 1.0), pad(jnp.where(even, -sin, 0.0), 0.0), pad(jnp.where(even, 0.0, sin), 0.0)


def kernel(x, c, ctx, c_ctx, ada_w, ada_b, norm_g, fnet_w_gate, fnet_w_out, mlstm_w_in, mlstm_b_gate,
           mlstm_hn, mlstm_w_out, attn_w_in, attn_qn, attn_kn, attn_w_out, final_g):
    cc = jnp.concatenate([c, c_ctx[None, :], jnp.zeros((MOD_ROWS - BATCH - 1, D), F32)], axis=0)
    mods_all = _modvec(cc, ada_w, ada_b).reshape(DEPTH, MOD_ROWS, 1, 3 * D)

    cc_c, sc_c = _dft_tables(F_GW, F_GW ** -0.5)
    cs_chan = jnp.concatenate([cc_c, sc_c], axis=1)
    cr_lat, sr_lat = _dif_tables()
    ct_ctx, st_ctx = _dft_tables(CTX, CTX ** -0.5)
    fnet_wg_bf = fnet_w_gate.astype(BF16)
    fnet_wo_bf = fnet_w_out.astype(BF16)

    xs = None
    h = None
    for i in range(DEPTH):
        kind, j = i % 3, i // 3
        last = i == DEPTH - 1
        mods = mods_all[i]
        g = norm_g[i].reshape(1, D)
        n_rows = N_LAT if last else N_ALL
        nxt = None if last else (norm_g[i + 1].reshape(1, D), mods_all[i + 1])
        if kind == 0:
            assert i == 0 or last
            wg = (fnet_wg_bf, j)
            w_out = (fnet_wo_bf, j)
            x_lat = x.reshape(N_LAT, D) if i == 0 else xs
            p, q, h_lat = _fnet_chan_dif(*((x_lat, (g, mods)) if i == 0 else (h, None)), cs_chan)
            m_lat = _fnet_mix_lat(h_lat, cr_lat, sr_lat, p, q, wg)
            if i == 0:
                x_ctx = ctx.reshape(N_CTX, D)
                a, b = _fnet_chan_ctx(x_ctx, 0, g, mods, cs_chan)
                m_ctx = _fnet_mix_ctx(x_ctx, 0, g, mods, ct_ctx, st_ctx, a, b, wg)
                xs, h = _outproj_join(m_lat, m_ctx, w_out, x_lat, x_ctx, mods, nxt)
            else:
                out = _outproj(m_lat, w_out, xs, mods, n_rows, final=final_g.reshape(1, D))
            continue
        if kind == 1:
            w_in = mlstm_w_in[j].astype(BF16)
            n_main = 2 * M_QK + 2 * M_V
            gate_tiles = lambda gcols: jnp.pad(
                jnp.concatenate([gcols[..., 0:8], gcols[..., 16:24]], axis=-1),
                [(0, 0)] * (gcols.ndim - 1) + [(0, LANES - 2 * M_HEADS)])
            gate_layout = lambda gcols: jnp.concatenate(
                [gate_tiles(gcols), gate_tiles(gcols[..., M_HEADS:])], axis=-1)
            w_gates = gate_layout(w_in[:, n_main:n_main + 4 * M_HEADS])
            b_gates = gate_layout(mlstm_b_gate[j][None, :])
            qkv, oz, gates = _mlstm_proj(h, w_in, w_in[:, n_main + 4 * M_HEADS:], w_gates, b_gates)
            hb = _mlstm_scan(qkv, gates)
            mbuf = _mlstm_scan(qkv, gates, oz, hb, mlstm_hn[j].reshape(1, M_V))
            w_out = (mlstm_w_out.astype(BF16), j)
        else:
            w_in = attn_w_in[j].astype(BF16)
            qn = attn_qn[j] * (A_HD ** -0.5)
            gain = jnp.concatenate([jnp.tile(qn, A_HEADS), jnp.tile(attn_kn[j], A_KV)]).reshape(1, A_QKW)
            q, k = _attn_qk(h, w_in, gain, _rope_tables())
            v, zg = _attn_vz(h, w_in)
            mbuf = _attn(q, k, v, zg)
            w_out = (attn_w_out.astype(BF16), j)
        assert not last
        xs, h = _outproj(mbuf, w_out, xs, mods, n_rows, nxt=nxt)

    return out.reshape(BATCH, SEQ, D)
```

```python
import functools
import math

import jax
import jax.numpy as jnp
from jax import lax
from jax.experimental import pallas as pl
from jax.experimental.pallas import tpu as pltpu

F32 = jnp.float32
BF16 = jnp.bfloat16

D = 2048
BATCH = 16
SEQ = 2048
CTX = 256
DEPTH = 4
EPS = 1e-6
N_LAT = BATCH * SEQ
N_CTX = BATCH * CTX
N_ALL = N_LAT + N_CTX
MOD_ROWS = 24
CTX_MOD_ROW = BATCH

F_GROUPS = 4
F_GW = D // F_GROUPS

M_HEADS = 8
M_DQK = 128
M_DV = 256
M_QK = M_HEADS * M_DQK
M_V = M_HEADS * M_DV
M_L = 256
M_GATES = 256
M_HALF = 2 * M_QK + M_V

A_HEADS = 16
A_KV = 4
A_HD = 128
A_Q = A_HEADS * A_HD
A_KVW = A_KV * A_HD
A_GW = A_Q // A_KV
A_TQ = 256
GRID_W = 64
ROPE_THETA = 10000.0

TM = 512
TP = 1024
TN = 512
TW = 1024
MIB = 1024 * 1024
LANES = 128


def _params(sem, vmem_mib):
    return pltpu.CompilerParams(dimension_semantics=sem, vmem_limit_bytes=vmem_mib * MIB)


def _resident(shape, index_map):
    return pl.BlockSpec(shape, index_map, pipeline_mode=pl.Buffered(1))


def _layer_weight(j):
    return _resident((None, D, D), lambda *ids: (j, 0, 0))


def _sigmoid(x):
    return 1.0 / (1.0 + jnp.exp(-x))


def _silu(x):
    return x * _sigmoid(x)


def _log_sigmoid(x):
    return jnp.minimum(x, 0.0) - jnp.log1p(jnp.exp(-jnp.abs(x)))


def _modnorm(x, g, scale, shift):
    ms = jnp.mean(x * x, axis=-1, keepdims=True)
    y = x * lax.rsqrt(ms + EPS) * g
    return y * (1.0 + scale) + shift


def _store_modnorm(x_ref, g_ref, sc_ref, sh_ref, h_scr):
    rows = x_ref.shape[0]
    step = min(rows, 256)
    for r0 in range(0, rows, step):
        h_scr[r0:r0 + step, :] = _modnorm(x_ref[r0:r0 + step, :], g_ref[...], sc_ref[...],
                                          sh_ref[...]).astype(BF16)


def _mod_row(r, tm):
    return jnp.where(r < N_LAT // tm, r // (SEQ // tm), CTX_MOD_ROW)


def _modvec_kernel(c_ref, w_ref, b_ref, o_ref):
    s = _silu(c_ref[...])
    w = w_ref[...]
    s_hi = s.astype(BF16)
    s_lo = (s - s_hi.astype(F32)).astype(BF16)
    w_hi = w.astype(BF16)
    w_lo = (w - w_hi.astype(F32)).astype(BF16)
    acc = jnp.dot(s_hi, w_hi, preferred_element_type=F32)
    acc = acc + jnp.dot(s_lo, w_hi, preferred_element_type=F32)
    acc = acc + jnp.dot(s_hi, w_lo, preferred_element_type=F32)
    o_ref[...] = acc + b_ref[...]


def _modvec(cc, ada_w, ada_b):
    tn = 2048
    return pl.pallas_call(
        _modvec_kernel,
        grid=(DEPTH, 3 * D // tn),
        in_specs=[
            pl.BlockSpec((MOD_ROWS, D), lambda i, j: (0, 0)),
            pl.BlockSpec((None, D, tn), lambda i, j: (i, 0, j)),
            pl.BlockSpec((None, 1, tn), lambda i, j: (i, 0, j)),
        ],
        out_specs=pl.BlockSpec((None, MOD_ROWS, tn), lambda i, j: (i, 0, j)),
        out_shape=jax.ShapeDtypeStruct((DEPTH, MOD_ROWS, 3 * D), F32),
        compiler_params=_params(("parallel", "parallel"), 48),
        name="modvec",
    )(cc, ada_w, ada_b.reshape(DEPTH, 1, 3 * D))


def _fnet_chan_kernel(x_ref, g_ref, sh_ref, sc_ref, cs_ref, a_ref, b_ref):
    h = _modnorm(x_ref[...], g_ref[...], sc_ref[...], sh_ref[...]).astype(BF16)
    for grp in range(F_GROUPS):
        sl = slice(grp * F_GW, (grp + 1) * F_GW)
        p = jnp.dot(h[:, sl], cs_ref[...], preferred_element_type=F32)
        a_ref[:, sl] = p[:, :F_GW].astype(BF16)
        b_ref[:, sl] = p[:, F_GW:].astype(BF16)


def _fnet_chan_ctx(xc, row_blk0, g, mods, cs_c):
    tm = 512
    out = jax.ShapeDtypeStruct((N_CTX, D), BF16)
    mod = lambda chunk: pl.BlockSpec((None, 1, D), lambda r: (CTX_MOD_ROW, 0, chunk))
    return pl.pallas_call(
        _fnet_chan_kernel,
        grid=(N_CTX // tm,),
        in_specs=[pl.BlockSpec((tm, D), lambda r: (row_blk0 + r, 0)),
                  pl.BlockSpec((1, D), lambda r: (0, 0)), mod(0), mod(1),
                  pl.BlockSpec((F_GW, 2 * F_GW), lambda r: (0, 0))],
        out_specs=[pl.BlockSpec((tm, D), lambda r: (r, 0))] * 2,
        out_shape=[out, out],
        compiler_params=_params(("parallel",), 48),
        name="fnet_chan_ctx",
    )(xc, g, mods, mods, cs_c)


F_R = 4
F_M = SEQ // F_R
F_TC = 256
F_SL = TN // LANES


def _fnet_chan_dif_kernel(x0_ref, x1_ref, x2_ref, x3_ref, *rest, prenormed):
    if prenormed:
        cs_ref, p_ref, q_ref = rest
        hs = [x[...] for x in (x0_ref, x1_ref, x2_ref, x3_ref)]
    else:
        g_ref, sh_ref, sc_ref, cs_ref, p_ref, q_ref, h_ref = rest
        hs = [_modnorm(x[...], g_ref[...], sc_ref[...], sh_ref[...]).astype(BF16)
              for x in (x0_ref, x1_ref, x2_ref, x3_ref)]
        for qi, h in enumerate(hs):
            h_ref[qi] = h
    for grp in range(F_GROUPS):
        sl = slice(grp * F_GW, (grp + 1) * F_GW)
        ab = [jnp.dot(h[:, sl], cs_ref[...], preferred_element_type=F32) for h in hs]
        a = [t[:, :F_GW] for t in ab]
        b = [t[:, F_GW:] for t in ab]
        sa02, da02, sa13, da13 = a[0] + a[2], a[0] - a[2], a[1] + a[3], a[1] - a[3]
        sb02, db02, sb13, db13 = b[0] + b[2], b[0] - b[2], b[1] + b[3], b[1] - b[3]
        re = (sa02 + sa13, da02 - db13, sa02 - sa13, da02 + db13)
        im = (sb02 + sb13, db02 + da13, sb02 - sb13, db02 - da13)
        for r in range(F_R):
            p_ref[r, :, sl] = re[r].astype(BF16)
            q_ref[r, :, sl] = im[r].astype(BF16)


def _fnet_chan_dif(xl, norm, cs_c):
    nt = F_M // F_TC
    xspec = lambda q: pl.BlockSpec((F_TC, D), lambda b, i: (b * (SEQ // F_TC) + q * nt + i, 0))
    mod = lambda chunk: pl.BlockSpec((None, 1, D), lambda b, i: (b, 0, chunk))
    out = jax.ShapeDtypeStruct((BATCH, F_R, F_M, D), BF16)
    ospec = pl.BlockSpec((None, F_R, F_TC, D), lambda b, i: (b, 0, i, 0))
    out_specs, out_shape = [ospec, ospec], [out, out]
    norm_specs, norm_args = [], []
    if norm is not None:
        norm_specs = [pl.BlockSpec((1, D), lambda b, i: (0, 0)), mod(0), mod(1)]
        norm_args = [norm[0], norm[1], norm[1]]
        out_specs.append(pl.BlockSpec((None, F_R, None, F_TC, D), lambda b, i: (b, 0, i, 0, 0)))
        out_shape.append(jax.ShapeDtypeStruct((BATCH, F_R, nt, F_TC, D), BF16))
    res = pl.pallas_call(
        functools.partial(_fnet_chan_dif_kernel, prenormed=norm is None),
        grid=(BATCH, nt),
        in_specs=[xspec(0), xspec(1), xspec(2), xspec(3)] + norm_specs + [
            pl.BlockSpec((F_GW, 2 * F_GW), lambda b, i: (0, 0))],
        out_specs=out_specs,
        out_shape=out_shape,
        compiler_params=_params(("parallel", "parallel"), 56),
        name="fnet_chan_dif",
    )(xl, xl, xl, xl, *norm_args, cs_c)
    return (res[0], res[1], xl) if norm is None else (res[0], res[1], res[2].reshape(N_LAT, D))


def _fnet_mix_kernel(x_ref, g_ref, sh_ref, sc_ref, c_ref, s_ref, p_ref, q_ref, wg_ref, o_ref, h_scr):
    _store_modnorm(x_ref, g_ref, sc_ref, sh_ref, h_scr)
    for c in range(D // TN):
        sl = slice(c * TN, (c + 1) * TN)
        y = jnp.dot(c_ref[...], p_ref[:, sl], preferred_element_type=F32)
        y = y - jnp.dot(s_ref[...], q_ref[:, sl], preferred_element_type=F32)
        gate = jnp.dot(h_scr[...], wg_ref[:, sl], preferred_element_type=F32)
        o_ref[:, sl] = (y * _silu(gate)).astype(BF16)


def _fnet_mix_lat_kernel(h_ref, c_ref, s_ref, p_ref, q_ref, wg_ref, o_ref, y_scr):
    for r in range(F_R):
        y = jnp.dot(c_ref[r], p_ref[r], preferred_element_type=F32)
        y = y - jnp.dot(s_ref[r], q_ref[r], preferred_element_type=F32)
        for c in range(F_SL):
            y_scr[c, pl.ds(r, F_M, stride=F_R), :] = y[:, c * LANES:(c + 1) * LANES]
    for t in range(SEQ // F_M):
        rows = slice(t * F_M, (t + 1) * F_M)
        gate = jnp.dot(h_ref[rows, :], wg_ref[...], preferred_element_type=F32)
        for c in range(F_SL):
            cols = slice(c * LANES, (c + 1) * LANES)
            o_ref[rows, cols] = (y_scr[c, rows, :] * _silu(gate[:, cols])).astype(BF16)


def _fnet_mix_lat(h, cr, sr, p, q, wg):
    tab = _resident((F_R, F_M, F_M), lambda b, n: (0, 0, 0))
    pq = pl.BlockSpec((None, F_R, F_M, TN), lambda b, n: (b, 0, 0, n))
    return pl.pallas_call(
        _fnet_mix_lat_kernel,
        grid=(BATCH, D // TN),
        in_specs=[pl.BlockSpec((SEQ, D), lambda b, n: (b, 0)), tab, tab, pq, pq,
                  pl.BlockSpec((None, D, TN), lambda b, n: (wg[1], 0, n))],
        out_specs=pl.BlockSpec((SEQ, TN), lambda b, n: (b, n)),
        out_shape=jax.ShapeDtypeStruct((N_LAT, D), BF16),
        scratch_shapes=[pltpu.VMEM((F_SL, SEQ, LANES), F32)],
        compiler_params=_params(("parallel", "arbitrary"), 56),
        name="fnet_mix_lat",
    )(h, cr, sr, p, q, wg[0])


def _fnet_mix_ctx(xc, row_blk0, g, mods, ct, st, a, b, wg):
    mod = lambda chunk: pl.BlockSpec((None, 1, D), lambda bi: (CTX_MOD_ROW, 0, chunk))
    tab = pl.BlockSpec((CTX, CTX), lambda bi: (0, 0))
    ab = pl.BlockSpec((CTX, D), lambda bi: (bi, 0))
    return pl.pallas_call(
        _fnet_mix_kernel,
        grid=(BATCH,),
        in_specs=[pl.BlockSpec((CTX, D), lambda bi: (row_blk0 + bi, 0)),
                  pl.BlockSpec((1, D), lambda bi: (0, 0)), mod(0), mod(1),
                  tab, tab, ab, ab, _layer_weight(wg[1])],
        out_specs=pl.BlockSpec((CTX, D), lambda bi: (bi, 0)),
        out_shape=jax.ShapeDtypeStruct((N_CTX, D), BF16),
        scratch_shapes=[pltpu.VMEM((CTX, D), BF16)],
        compiler_params=_params(("parallel",), 48),
        name="fnet_mix_ctx",
    )(xc, g, mods, mods, ct, st, a, b, wg[0])


def _residual_update(m_ref, w_ref, x_ref, gate_ref, o_ref, nxt):
    ss = 0.0
    for c in range(D // TW):
        sl = slice(c * TW, (c + 1) * TW)
        y = jnp.dot(m_ref[...], w_ref[:, sl], preferred_element_type=F32)
        o = x_ref[:, sl] + gate_ref[:, sl] * y
        o_ref[:, sl] = o
        if nxt is not None:
            ss = ss + jnp.sum(o * o, axis=-1, keepdims=True)
    if nxt is not None:
        gn_ref, shn_ref, scn_ref, h_ref = nxt
        rinv = lax.rsqrt(ss * (1.0 / D) + EPS)
        for c in range(D // TW):
            sl = slice(c * TW, (c + 1) * TW)
            gs = gn_ref[:, sl] * (1.0 + scn_ref[:, sl])
            h_ref[:, sl] = (o_ref[:, sl] * rinv * gs + shn_ref[:, sl]).astype(BF16)


def _outproj_kernel(m_ref, w_ref, x_ref, gate_ref, *rest, final):
    if final:
        fg_ref, o_ref = rest
        _residual_update(m_ref, w_ref, x_ref, gate_ref, o_ref, None)
        x = o_ref[...]
        ms = jnp.mean(x * x, axis=-1, keepdims=True)
        o_ref[...] = x * lax.rsqrt(ms + EPS) * fg_ref[...]
    else:
        gn_ref, shn_ref, scn_ref, o_ref, h_ref = rest
        _residual_update(m_ref, w_ref, x_ref, gate_ref, o_ref, (gn_ref, shn_ref, scn_ref, h_ref))


def _next_norm_specs():
    return [pl.BlockSpec((1, D), lambda r: (0, 0)),
            pl.BlockSpec((None, 1, D), lambda r: (_mod_row(r, TM), 0, 0)),
            pl.BlockSpec((None, 1, D), lambda r: (_mod_row(r, TM), 0, 1))]


def _outproj(mbuf, w, xs, mods, n_rows, final=None, nxt=None):
    nr = n_rows // TM
    tile = pl.BlockSpec((TM, D), lambda r: (r, 0))
    in_specs = [
        tile,
        _layer_weight(w[1]),
        tile,
        pl.BlockSpec((None, 1, D), lambda r: (_mod_row(r, TM), 0, 2)),
    ]
    args = [mbuf, w[0], xs, mods]
    if final is not None:
        in_specs.append(pl.BlockSpec((1, D), lambda r: (0, 0)))
        args.append(final)
        return pl.pallas_call(
            functools.partial(_outproj_kernel, final=True),
            grid=(n_rows // TM,),
            in_specs=in_specs,
            out_specs=tile,
            out_shape=jax.ShapeDtypeStruct((n_rows, D), F32),
            compiler_params=_params(("parallel",), 48),
            name="outproj_final",
        )(*args)
    g_next, mods_next = nxt
    return pl.pallas_call(
        functools.partial(_outproj_kernel, final=False),
        grid=(nr,),
        in_specs=in_specs + _next_norm_specs(),
        out_specs=[tile, tile],
        out_shape=[jax.ShapeDtypeStruct((N_ALL, D), F32), jax.ShapeDtypeStruct((N_ALL, D), BF16)],
        input_output_aliases={2: 0},
        compiler_params=_params(("parallel",), 48),
        name="outproj",
    )(*args, g_next, mods_next, mods_next)


def _outproj_join_kernel(ml_ref, mc_ref, w_ref, xl_ref, xc_ref, gate_ref,
                         gn_ref, shn_ref, scn_ref, o_ref, h_ref):
    is_lat = pl.program_id(0) < N_LAT // TM
    nxt = (gn_ref, shn_ref, scn_ref, h_ref)

    @pl.when(is_lat)
    def _():
        _residual_update(ml_ref, w_ref, xl_ref, gate_ref, o_ref, nxt)

    @pl.when(jnp.logical_not(is_lat))
    def _():
        _residual_update(mc_ref, w_ref, xc_ref, gate_ref, o_ref, nxt)


def _outproj_join(m_lat, m_ctx, w, x_lat, x_ctx, mods, nxt):
    nl = N_LAT // TM
    lat = pl.BlockSpec((TM, D), lambda r: (jnp.minimum(r, nl - 1), 0))
    cxt = pl.BlockSpec((TM, D), lambda r: (jnp.maximum(r - nl, 0), 0))
    tile = pl.BlockSpec((TM, D), lambda r: (r, 0))
    g_next, mods_next = nxt
    return pl.pallas_call(
        _outproj_join_kernel,
        grid=(N_ALL // TM,),
        in_specs=[lat, cxt, _layer_weight(w[1]), lat, cxt,
                  pl.BlockSpec((None, 1, D), lambda r: (_mod_row(r, TM), 0, 2))] + _next_norm_specs(),
        out_specs=[tile, tile],
        out_shape=[jax.ShapeDtypeStruct((N_ALL, D), F32), jax.ShapeDtypeStruct((N_ALL, D), BF16)],
        compiler_params=_params(("parallel",), 48),
        name="outproj_join",
    )(m_lat, m_ctx, w[0], x_lat, x_ctx, mods, g_next, mods_next, mods_next)


def _mlstm_qkv_kernel(h_ref, w_ref, wg_ref, bg_ref, out_ref, gt_ref):
    gt_ref[...] = jnp.dot(h_ref[...], wg_ref[...], preferred_element_type=F32) + bg_ref[...]
    for c in range(M_HALF // TW):
        sl = slice(c * TW, (c + 1) * TW)
        acc = jnp.dot(h_ref[...], w_ref[:, sl], preferred_element_type=F32)
        if c < M_QK // TW:
            acc = acc * (M_DQK ** -0.5)
        out_ref[:, sl] = acc.astype(BF16)


def _mlstm_oz_kernel(h_ref, wo_ref, wz_ref, out_ref):
    for c in range(M_V // TW):
        sl = slice(c * TW, (c + 1) * TW)
        o = jnp.dot(h_ref[...], wo_ref[:, sl], preferred_element_type=F32)
        out_ref[:, sl] = _sigmoid(o).astype(BF16)
    for c in range(M_V // TW):
        sl = slice(c * TW, (c + 1) * TW)
        z = jnp.dot(h_ref[...], wz_ref[:, sl], preferred_element_type=F32)
        out_ref[:, M_V + c * TW:M_V + (c + 1) * TW] = _silu(z).astype(BF16)


def _mlstm_proj(h, w_in, w_z, wg, bg):
    tile = pl.BlockSpec((TP, D), lambda r: (r, 0))
    wide = pl.BlockSpec((TP, M_HALF), lambda r: (r, 0))
    qkv, gates = pl.pallas_call(
        _mlstm_qkv_kernel,
        grid=(N_ALL // TP,),
        in_specs=[tile, _resident((D, M_HALF), lambda r: (0, 0)),
                  pl.BlockSpec((D, M_GATES), lambda r: (0, 0)), pl.BlockSpec((1, M_GATES), lambda r: (0, 0))],
        out_specs=[wide, pl.BlockSpec((TP, M_GATES), lambda r: (r, 0))],
        out_shape=[jax.ShapeDtypeStruct((N_ALL, M_HALF), BF16), jax.ShapeDtypeStruct((N_ALL, M_GATES), F32)],
        compiler_params=_params(("parallel",), 48),
        name="mlstm_qkv",
    )(h, w_in, wg, bg)
    oz = pl.pallas_call(
        _mlstm_oz_kernel,
        grid=(N_ALL // TP,),
        in_specs=[tile, _resident((D, M_V), lambda r: (0, M_HALF // M_V)), _resident((D, M_V), lambda r: (0, 0))],
        out_specs=wide,
        out_shape=jax.ShapeDtypeStruct((N_ALL, M_HALF), BF16),
        compiler_params=_params(("parallel",), 48),
        name="mlstm_oz",
    )(h, w_in, w_z)
    return qkv, oz, gates


def _split3(x):
    hi = x.astype(BF16)
    r1 = x - hi.astype(F32)
    mid = r1.astype(BF16)
    lo = (r1 - mid.astype(F32)).astype(BF16)
    return hi, mid, lo


def _cummax_rows(x, reverse):
    rows = x.shape[0]
    row = lax.broadcasted_iota(jnp.int32, x.shape, 0)
    sh = 1
    while sh < rows:
        if reverse:
            x = jnp.where(row < rows - sh, jnp.maximum(x, pltpu.roll(x, rows - sh, 0)), x)
        else:
            x = jnp.where(row >= sh, jnp.maximum(x, pltpu.roll(x, sh, 0)), x)
        sh *= 2
    return x


def _lanes(col, width):
    return jnp.broadcast_to(col, (col.shape[0], width))


def _mlstm_scan_kernel(q_ref, k_ref, v_ref, gt_ref, *rest, reverse):
    if reverse:
        o_ref, cn_scr, m_scr = rest
    else:
        hb_ref, so_ref, sz_ref, hn_ref, o_ref, cn_scr, m_scr = rest

    @pl.when(pl.program_id(1) == 0)
    def _():
        cn_scr[...] = jnp.zeros_like(cn_scr)
        m_scr[...] = jnp.zeros_like(m_scr)

    L = M_L
    lane0 = M_HEADS if reverse else 0
    ig = gt_ref[:, :LANES]
    ls = _log_sigmoid(gt_ref[:, LANES:])
    row = lax.broadcasted_iota(jnp.int32, (L, L), 0)
    col = lax.broadcasted_iota(jnp.int32, (L, L), 1)
    order = (row <= col) if reverse else (row >= col)
    tri = jnp.where(order, 1.0, 0.0).astype(BF16)
    hi, mid, lo = _split3(ls)
    b = (jnp.dot(tri, hi, preferred_element_type=F32)
         + jnp.dot(tri, mid, preferred_element_type=F32)
         + jnp.dot(tri, lo, preferred_element_type=F32))
    end = 0 if reverse else L - 1

    m_prev = m_scr[...]
    r = ig - b
    inter = b + m_prev
    m_t = jnp.maximum(inter, b + _cummax_rows(r, reverse))
    a_all = jnp.exp(inter - m_t)
    u_all = b - m_t
    en_all = jnp.exp(-m_t)
    b_end = b[end:end + 1, :]
    gl = b_end - b + ig
    m_new = jnp.maximum(b_end + m_prev, jnp.max(gl, axis=0, keepdims=True))
    w_all = jnp.exp(gl - m_new)
    decay_all = jnp.exp(b_end + m_prev - m_new)
    r_t = r.T
    m_scr[...] = m_new

    ones_bf = jnp.ones((L, LANES), BF16)
    mean_dv = jnp.full((M_DV, LANES), 1.0 / M_DV, BF16)
    nt = (((1,), (1,)), ((), ()))
    tn = (((0,), (0,)), ((), ()))

    def stage_a(h):
        l = lane0 + h
        q = q_ref[:, h * M_DQK:(h + 1) * M_DQK]
        k = k_ref[:, h * M_DQK:(h + 1) * M_DQK]
        qk = lax.dot_general(q, k, nt, preferred_element_type=F32)
        qcn = jnp.dot(q, cn_scr[h].astype(BF16), preferred_element_type=F32)
        p = jnp.exp(jnp.where(order, u_all[:, l:l + 1] + r_t[l:l + 1, :], -jnp.inf))
        return qk, qcn, p

    def stage_b(h, qk, p):
        s = qk * p
        s_hi = s.astype(BF16)
        s_lo = (s - s_hi.astype(F32)).astype(BF16)
        vo = jnp.concatenate([v_ref[:, h * M_DV:(h + 1) * M_DV], ones_bf], axis=1)
        sv = jnp.dot(s_hi, vo, preferred_element_type=F32)
        return sv, jnp.dot(s_lo, ones_bf, preferred_element_type=F32)

    def stage_c(h, qcn, sv, rs_lo):
        l = lane0 + h
        a = _lanes(a_all[:, l:l + 1], LANES)
        den = a * qcn[:, M_DV:] + (sv[:, M_DV:] + rs_lo)
        inv = 1.0 / jnp.maximum(jnp.abs(den), _lanes(en_all[:, l:l + 1], LANES))
        cols = [slice(h * M_DV + c0, h * M_DV + c0 + LANES) for c0 in range(0, M_DV, LANES)]
        hid = [(a * qcn[:, c0:c0 + LANES] + sv[:, c0:c0 + LANES]) * inv for c0 in range(0, M_DV, LANES)]
        if reverse:
            for sl, hv in zip(cols, hid):
                o_ref[:, sl] = hv
            return
        ys = [so_ref[:, sl].astype(F32) * (hv + hb_ref[:, sl]) for sl, hv in zip(cols, hid)]
        sq = jnp.concatenate([(y * y).astype(BF16) for y in ys], axis=1)
        ms = jnp.dot(sq, mean_dv, preferred_element_type=F32)
        scale = lax.rsqrt(ms + EPS)
        for sl, y in zip(cols, ys):
            o_ref[:, sl] = (y * scale * hn_ref[:, sl] * sz_ref[:, sl].astype(F32)).astype(BF16)

    def stage_d(h):
        l = lane0 + h
        k = k_ref[:, h * M_DQK:(h + 1) * M_DQK]
        kw = (k.astype(F32) * _lanes(w_all[:, l:l + 1], M_DQK)).astype(BF16)
        vo = jnp.concatenate([v_ref[:, h * M_DV:(h + 1) * M_DV], ones_bf], axis=1)
        upd = lax.dot_general(kw, vo, tn, preferred_element_type=F32)
        cn_scr[h] = decay_all[:, l:l + 1] * cn_scr[h] + upd

    sa = {}
    sb = {}
    for step in range(M_HEADS + 2):
        if step < M_HEADS:
            sa[step] = stage_a(step)
        if 1 <= step <= M_HEADS:
            h = step - 1
            sb[h] = stage_b(h, sa[h][0], sa[h][2])
        if step >= 2:
            h = step - 2
            stage_c(h, sa[h][1], *sb[h])
            stage_d(h)


def _mlstm_scan(qkv, gates, oz=None, hb=None, hn=None):
    reverse = hb is None
    nlc = SEQ // M_L
    ncc = CTX // M_L
    ctx_blk0 = N_LAT // M_L

    def blk(b, i):
        lat = (nlc + ncc - 1 - i) if reverse else (i - ncc)
        cxt = (ncc - 1 - i) if reverse else i
        return jnp.where(i < ncc, ctx_blk0 + b * ncc + cxt, b * nlc + lat)

    wide = lambda cb: pl.BlockSpec((M_L, M_V), lambda b, i: (blk(b, i), cb))
    in_specs = [
        pl.BlockSpec((M_L, M_QK), lambda b, i: (blk(b, i), 0)),
        pl.BlockSpec((M_L, M_QK), lambda b, i: (blk(b, i), 1)),
        wide(1),
        pl.BlockSpec((M_L, M_GATES), lambda b, i: (blk(b, i), 0)),
    ]
    args = [qkv, qkv, qkv, gates]
    if not reverse:
        in_specs += [wide(0), wide(0), wide(1), pl.BlockSpec((1, M_V), lambda b, i: (0, 0))]
        args += [hb, oz, oz, hn]
    return pl.pallas_call(
        functools.partial(_mlstm_scan_kernel, reverse=reverse),
        grid=(BATCH, nlc + ncc),
        in_specs=in_specs,
        out_specs=wide(0),
        out_shape=jax.ShapeDtypeStruct((N_ALL, M_V), F32 if reverse else BF16),
        scratch_shapes=[
            pltpu.VMEM((M_HEADS, M_DQK, M_DV + LANES), F32),
            pltpu.VMEM((1, LANES), F32),
        ],
        compiler_params=_params(("parallel", "arbitrary"), 48),
        name="mlstm_scan_bwd" if reverse else "mlstm_scan_fwd",
    )(*args)


A_QKW = A_Q + A_KVW


def _attn_qk_kernel(h_ref, w_ref, gain_ref, cos_ref, sin_next_ref, sin_prev_ref, q_ref, k_ref):
    mean_mat = jnp.full((A_HD, A_HD), 1.0 / A_HD, BF16)
    cos = cos_ref[...]
    sin_next = sin_next_ref[...]
    sin_prev = sin_prev_ref[...]
    for c in range(A_QKW // TN):
        acc = jnp.dot(h_ref[...], w_ref[:, c * TN:(c + 1) * TN], preferred_element_type=F32)
        for hh in range(TN // A_HD):
            lo = c * TN + hh * A_HD
            a = acc[:, hh * A_HD:(hh + 1) * A_HD]
            ms = jnp.dot((a * a).astype(BF16), mean_mat, preferred_element_type=F32)
            a = a * lax.rsqrt(ms + EPS) * gain_ref[:, lo:lo + A_HD]
            a = (a * cos + pltpu.roll(a, A_HD - 1, 1) * sin_next + pltpu.roll(a, 1, 1) * sin_prev).astype(BF16)
            if lo < A_Q:
                q_ref[:, lo:lo + A_HD] = a
            else:
                k_ref[:, lo - A_Q:lo - A_Q + A_HD] = a


def _attn_qk(h, w, gain, rope):
    lat_tiles = N_LAT // TP
    rope_blk = lambda r: (jnp.where(r < lat_tiles, r % (SEQ // TP), SEQ // TP), 0)
    return pl.pallas_call(
        _attn_qk_kernel,
        grid=(N_ALL // TP,),
        in_specs=[
            pl.BlockSpec((TP, D), lambda r: (r, 0)),
            _resident((D, A_QKW), lambda r: (0, 0)),
            pl.BlockSpec((1, A_QKW), lambda r: (0, 0)),
            pl.BlockSpec((TP, A_HD), rope_blk),
            pl.BlockSpec((TP, A_HD), rope_blk),
            pl.BlockSpec((TP, A_HD), rope_blk),
        ],
        out_specs=[
            pl.BlockSpec((TP, A_Q), lambda r: (r, 0)),
            pl.BlockSpec((TP, A_KVW), lambda r: (r, 0)),
        ],
        out_shape=[
            jax.ShapeDtypeStruct((N_ALL, A_Q), BF16),
            jax.ShapeDtypeStruct((N_ALL, A_KVW), BF16),
        ],
        compiler_params=_params(("parallel",), 48),
        name="attn_qk",
    )(h, w, gain, *rope)


def _attn_vz_kernel(h_ref, w_ref, v_ref, zg_ref):
    v_ref[...] = jnp.dot(h_ref[...], w_ref[:, :A_KVW], preferred_element_type=F32).astype(BF16)
    for c in range(A_Q // TW):
        lo = A_KVW + c * TW
        z = jnp.dot(h_ref[...], w_ref[:, lo:lo + TW], preferred_element_type=F32)
        zg_ref[:, c * TW:(c + 1) * TW] = _silu(z).astype(BF16)


def _attn_vz(h, w):
    return pl.pallas_call(
        _attn_vz_kernel,
        grid=(N_ALL // TP,),
        in_specs=[
            pl.BlockSpec((TP, D), lambda r: (r, 0)),
            _resident((D, A_KVW + A_Q), lambda r: (0, 1)),
        ],
        out_specs=[
            pl.BlockSpec((TP, A_KVW), lambda r: (r, 0)),
            pl.BlockSpec((TP, A_Q), lambda r: (r, 0)),
        ],
        out_shape=[
            jax.ShapeDtypeStruct((N_ALL, A_KVW), BF16),
            jax.ShapeDtypeStruct((N_ALL, A_Q), BF16),
        ],
        compiler_params=_params(("parallel",), 48),
        name="attn_vz",
    )(h, w)


A_LAT_TILES = SEQ // A_TQ
A_GPS = A_KV
A_SW = A_GPS * A_GW
A_KW = A_GPS * A_HD
A_KEYS = CTX + SEQ


def _attn_kernel(q_ref, kl_ref, kc_ref, vl_ref, vc_ref, zg_ref, o_ref, k_scr, vo_scr):
    t = pl.program_id(2)
    nt = (((1,), (1,)), ((), ()))
    heads = A_SW // A_HD
    hpg = A_GW // A_HD

    @pl.when(t == 0)
    def _():
        for g in range(A_GPS):
            cols = slice(g * A_HD, (g + 1) * A_HD)
            k_scr[g, :CTX, :] = kc_ref[:, cols]
            k_scr[g, CTX:, :] = kl_ref[:, cols]
            vo_scr[g, :CTX, :A_HD] = vc_ref[:, cols]
            vo_scr[g, CTX:, :A_HD] = vl_ref[:, cols]
            vo_scr[g, :, A_HD:] = jnp.ones((A_KEYS, A_HD), BF16)

    def run(nkeys):
        def scores(hh):
            q = q_ref[:, hh * A_HD:(hh + 1) * A_HD]
            return lax.dot_general(q, k_scr[hh // hpg, :nkeys, :], nt, preferred_element_type=F32)

        def probs(s):
            return jnp.exp(s - jnp.max(s, axis=-1, keepdims=True)).astype(BF16)

        def output(hh, p):
            sl = slice(hh * A_HD, (hh + 1) * A_HD)
            acc = jnp.dot(p, vo_scr[hh // hpg, :nkeys, :], preferred_element_type=F32)
            o_ref[:, sl] = (acc[:, :A_HD] / acc[:, A_HD:] * zg_ref[:, sl].astype(F32)).astype(BF16)

        s, p = {}, {}
        for step in range(heads + 2):
            if step < heads:
                s[step] = scores(step)
            if 1 <= step <= heads:
                p[step - 1] = probs(s[step - 1])
            if step >= 2:
                output(step - 2, p[step - 2])

    @pl.when(t < A_LAT_TILES)
    def _():
        run(A_KEYS)

    @pl.when(t >= A_LAT_TILES)
    def _():
        run(CTX)


def _attn(q, k, v, zg):
    ctx_blk0 = N_LAT // A_TQ
    qrow = lambda b, g, t: jnp.where(t < A_LAT_TILES, b * A_LAT_TILES + t, ctx_blk0 + b)
    return pl.pallas_call(
        _attn_kernel,
        grid=(BATCH, A_KV // A_GPS, A_LAT_TILES + 1),
        in_specs=[
            pl.BlockSpec((A_TQ, A_SW), lambda b, g, t: (qrow(b, g, t), g)),
            pl.BlockSpec((SEQ, A_KW), lambda b, g, t: (b, g)),
            pl.BlockSpec((CTX, A_KW), lambda b, g, t: (N_LAT // CTX + b, g)),
            pl.BlockSpec((SEQ, A_KW), lambda b, g, t: (b, g)),
            pl.BlockSpec((CTX, A_KW), lambda b, g, t: (N_LAT // CTX + b, g)),
            pl.BlockSpec((A_TQ, A_SW), lambda b, g, t: (qrow(b, g, t), g)),
        ],
        out_specs=pl.BlockSpec((A_TQ, A_SW), lambda b, g, t: (qrow(b, g, t), g)),
        out_shape=jax.ShapeDtypeStruct((N_ALL, A_Q), BF16),
        scratch_shapes=[pltpu.VMEM((A_GPS, A_KEYS, A_HD), BF16), pltpu.VMEM((A_GPS, A_KEYS, 2 * A_HD), BF16)],
        compiler_params=_params(("parallel", "parallel", "arbitrary"), 48),
        name="attn",
    )(q, k, k, v, v, zg)


def _dft_tables(n, scale):
    idx = jnp.arange(n, dtype=jnp.int32)
    ang = ((idx[:, None] * idx[None, :]) % n).astype(F32) * (2.0 * math.pi / n)
    return (jnp.cos(ang) * scale).astype(BF16), (jnp.sin(ang) * scale).astype(BF16)


def _dif_tables():
    jt = jnp.arange(F_M, dtype=jnp.int32)
    r = jnp.arange(F_R, dtype=jnp.int32)
    k = ((F_R * jt[None, :, None] + r[:, None, None]) * jt[None, None, :]) % SEQ
    ang = k.astype(F32) * (2.0 * math.pi / SEQ)
    return (jnp.cos(ang) * SEQ ** -0.5).astype(BF16), (jnp.sin(ang) * SEQ ** -0.5).astype(BF16)


def _rope_tables():
    t = jnp.arange(SEQ)
    freqs = ROPE_THETA ** (-jnp.arange(0, A_HD // 2, 2, dtype=F32) / (A_HD // 2))
    ang = jnp.concatenate([(t // GRID_W).astype(F32)[:, None] * freqs,
                           (t % GRID_W).astype(F32)[:, None] * freqs], axis=-1)
    cos = jnp.repeat(jnp.cos(ang), 2, axis=-1)
    sin = jnp.repeat(jnp.sin(ang), 2, axis=-1)
    even = (jnp.arange(A_HD) % 2 == 0)[None, :]
    pad = lambda a, fill: jnp.concatenate([a, jnp.full((TP, A_HD), fill, F32)], axis=0)
    return pad(cos, 1.0), pad(jnp.where(even, -sin, 0.0), 0.0), pad(jnp.where(even, 0.0, sin), 0.0)


def kernel(x, c, ctx, c_ctx, ada_w, ada_b, norm_g, fnet_w_gate, fnet_w_out, mlstm_w_in, mlstm_b_gate,
           mlstm_hn, mlstm_w_out, attn_w_in, attn_qn, attn_kn, attn_w_out, final_g):
    cc = jnp.concatenate([c, c_ctx[None, :], jnp.zeros((MOD_ROWS - BATCH - 1, D), F32)], axis=0)
    mods_all = _modvec(cc, ada_w, ada_b).reshape(DEPTH, MOD_ROWS, 1, 3 * D)

    cc_c, sc_c = _dft_tables(F_GW, F_GW ** -0.5)
    cs_chan = jnp.concatenate([cc_c, sc_c], axis=1)
    cr_lat, sr_lat = _dif_tables()
    ct_ctx, st_ctx = _dft_tables(CTX, CTX ** -0.5)
    fnet_wg_bf = fnet_w_gate.astype(BF16)
    fnet_wo_bf = fnet_w_out.astype(BF16)

    xs = None
    h = None
    for i in range(DEPTH):
        kind, j = i % 3, i // 3
        last = i == DEPTH - 1
        mods = mods_all[i]
        g = norm_g[i].reshape(1, D)
        n_rows = N_LAT if last else N_ALL
        nxt = None if last else (norm_g[i + 1].reshape(1, D), mods_all[i + 1])
        if kind == 0:
            assert i == 0 or last
            wg = (fnet_wg_bf, j)
            w_out = (fnet_wo_bf, j)
            x_lat = x.reshape(N_LAT, D) if i == 0 else xs
            p, q, h_lat = _fnet_chan_dif(*((x_lat, (g, mods)) if i == 0 else (h, None)), cs_chan)
            m_lat = _fnet_mix_lat(h_lat, cr_lat, sr_lat, p, q, wg)
            if i == 0:
                x_ctx = ctx.reshape(N_CTX, D)
                a, b = _fnet_chan_ctx(x_ctx, 0, g, mods, cs_chan)
                m_ctx = _fnet_mix_ctx(x_ctx, 0, g, mods, ct_ctx, st_ctx, a, b, wg)
                xs, h = _outproj_join(m_lat, m_ctx, w_out, x_lat, x_ctx, mods, nxt)
            else:
                out = _outproj(m_lat, w_out, xs, mods, n_rows, final=final_g.reshape(1, D))
            continue
        if kind == 1:
            w_in = mlstm_w_in[j].astype(BF16)
            n_main = 2 * M_QK + 2 * M_V
            gate_tiles = lambda gcols: jnp.pad(
                jnp.concatenate([gcols[..., 0:8], gcols[..., 16:24]], axis=-1),
                [(0, 0)] * (gcols.ndim - 1) + [(0, LANES - 2 * M_HEADS)])
            gate_layout = lambda gcols: jnp.concatenate(
                [gate_tiles(gcols), gate_tiles(gcols[..., M_HEADS:])], axis=-1)
            w_gates = gate_layout(w_in[:, n_main:n_main + 4 * M_HEADS])
            b_gates = gate_layout(mlstm_b_gate[j][None, :])
            qkv, oz, gates = _mlstm_proj(h, w_in, w_in[:, n_main + 4 * M_HEADS:], w_gates, b_gates)
            hb = _mlstm_scan(qkv, gates)
            mbuf = _mlstm_scan(qkv, gates, oz, hb, mlstm_hn[j].reshape(1, M_V))
            w_out = (mlstm_w_out.astype(BF16), j)
        else:
            w_in = attn_w_in[j].astype(BF16)
            qn = attn_qn[j] * (A_HD ** -0.5)
            gain = jnp.concatenate([jnp.tile(qn, A_HEADS), jnp.tile(attn_kn[j], A_KV)]).reshape(1, A_QKW)
            q, k = _attn_qk(h, w_in, gain, _rope_tables())
            v, zg = _attn_vz(h, w_in)
            mbuf = _attn(q, k, v, zg)
            w_out = (attn_w_out.astype(BF16), j)
        assert not last
        xs, h = _outproj(mbuf, w_out, xs, mods, n_rows, nxt=nxt)

    return out.reshape(BATCH, SEQ, D)
```
